```python
import math
import jax, jax.numpy as jnp
from jax import lax
import numpy as np

D_MODEL = 1024
BATCH = 16
SEQ = 256
DEPTH = 2
DEC_BATCH = 8
DEC_SEQ = 4096
PAST_LEN = 256

GRID_W = 64
EPS = 1e-6
N_HEADS = 16
N_KV_HEADS = 4
HEAD_DIM = 64
KV_REP = N_HEADS // N_KV_HEADS
ATTN_WIDTH = N_HEADS * HEAD_DIM
KV_WIDTH = N_KV_HEADS * HEAD_DIM
ROPE_AXIS_DIM = HEAD_DIM // 2
ROPE_THETA = 10000.0
Q_BLOCK = 128
D_INNER = 2 * D_MODEL
SSM_HEAD_DIM = 64
SSM_HEADS = D_INNER // SSM_HEAD_DIM
SSM_GROUPS = 4
HEADS_PER_GROUP = SSM_HEADS // SSM_GROUPS
D_STATE = 128
GN = SSM_GROUPS * D_STATE
CONV_K = 3
CONV_DIM = D_INNER + 2 * GN
SSD_CHUNK = 128
N_IN = ATTN_WIDTH + 2 * KV_WIDTH + D_INNER + CONV_DIM + 2 * SSM_HEADS + 2 * D_MODEL
IN_SPLITS = (
    ATTN_WIDTH,
    ATTN_WIDTH + KV_WIDTH,
    ATTN_WIDTH + 2 * KV_WIDTH,
    ATTN_WIDTH + 2 * KV_WIDTH + D_INNER,
    ATTN_WIDTH + 2 * KV_WIDTH + D_INNER + CONV_DIM,
    ATTN_WIDTH + 2 * KV_WIDTH + D_INNER + CONV_DIM + 2 * SSM_HEADS,
    ATTN_WIDTH + 2 * KV_WIDTH + D_INNER + CONV_DIM + 2 * SSM_HEADS + D_MODEL,
)
N_EXPERTS = 16
N_EXPERT_GROUPS = 4
EXPERTS_PER_GROUP = N_EXPERTS // N_EXPERT_GROUPS
TOP_K = 2
D_FF_EXPERT = 512
N_MOD = 6

kernel_name = 'hybrid_gqa_ssd_moe_diffusion_step'


def rmsnorm(x, w):
    xf = x.astype(jnp.float32)
    y = xf * lax.rsqrt(jnp.mean(xf * xf, axis=-1, keepdims=True) + EPS)
    return (y * w.astype(jnp.float32)).astype(x.dtype)


def adaln(cond, p):
    m = jax.nn.silu(cond) @ p['w_mod'] + p['b_mod']
    return jnp.split(m[..., None, :], N_MOD, axis=-1)


def axial_rope(n_tokens):
    rows = n_tokens // GRID_W
    row = jnp.repeat(jnp.arange(rows, dtype=jnp.float32), GRID_W)
    col = jnp.tile(jnp.arange(GRID_W, dtype=jnp.float32), rows)
    inv = 1.0 / (ROPE_THETA ** (jnp.arange(0, ROPE_AXIS_DIM, 2, dtype=jnp.float32) / ROPE_AXIS_DIM))
    ar = row[:, None] * inv
    ac = col[:, None] * inv
    ang = jnp.concatenate([ar, ar, ac, ac], axis=-1)
    return jnp.cos(ang), jnp.sin(ang)


def apply_rope(x, cos, sin):
    x4 = x.reshape(x.shape[:-1] + (4, HEAD_DIM // 4))
    rot = jnp.stack([-x4[..., 1, :], x4[..., 0, :], -x4[..., 3, :], x4[..., 2, :]], axis=-2).reshape(x.shape)
    cos = cos[None, :, None, :].astype(x.dtype)
    sin = sin[None, :, None, :].astype(x.dtype)
    return x * cos + rot * sin


def attend_blocked(q, k, v):
    b, lq = q.shape[:2]
    nb = lq // Q_BLOCK
    qb = q.reshape(b, nb, Q_BLOCK, N_KV_HEADS, KV_REP, HEAD_DIM).transpose(1, 0, 2, 3, 4, 5)
    scale = HEAD_DIM ** -0.5

    def one_block(qblk):
        s = jnp.einsum('bqgrd,bkgd->bgrqk', qblk, k).astype(jnp.float32) * scale
        pr = jax.nn.softmax(s, axis=-1).astype(v.dtype)
        return jnp.einsum('bgrqk,bkgd->bqgrd', pr, v)

    o = lax.map(one_block, qb)
    return o.transpose(1, 0, 2, 3, 4, 5).reshape(b, lq, ATTN_WIDTH)


def centred_dwconv(x, w, bias):
    l = x.shape[1]
    pad = CONV_K // 2
    xp = jnp.pad(x, ((0, 0), (pad, pad), (0, 0)))
    y = bias + xp[:, 0:l] * w[0]
    for j in range(1, CONV_K):
        y = y + xp[:, j:j + l] * w[j]
    return y


def ssd_chunked(x, dt, a_coef, bm, cm, s0):
    b, l = x.shape[:2]
    nc = l // SSD_CHUNK

    def chunks(t):
        return t.reshape((b, nc, SSD_CHUNK) + t.shape[2:]).swapaxes(0, 1)

    tri = jnp.tril(jnp.ones((SSD_CHUNK, SSD_CHUNK), dtype=bool))[None, :, :, None]

    def step(s, inp):
        xc, dtc, bc, cc = inp
        bh = jnp.repeat(bc, HEADS_PER_GROUP, axis=2)
        ch = jnp.repeat(cc, HEADS_PER_GROUP, axis=2)
        acs = jnp.cumsum(dtc * a_coef, axis=1)
        seg = acs[:, :, None, :] - acs[:, None, :, :]
        decay = jnp.exp(jnp.where(tri, seg, -jnp.inf))
        w_ij = jnp.einsum('bihn,bjhn->bijh', ch, bh) * decay * dtc[:, None, :, :]
        y = (jnp.einsum('bijh,bjhp->bihp', w_ij, xc)
             + jnp.exp(acs)[..., None] * jnp.einsum('bihn,bhpn->bihp', ch, s))
        last = acs[:, -1]
        wj = jnp.exp(last[:, None] - acs) * dtc
        s_new = jnp.exp(last)[:, :, None, None] * s + jnp.einsum('bjh,bjhn,bjhp->bhpn', wj, bh, xc)
        return s_new, y

    s_fin, ys = lax.scan(step, s0, (chunks(x), chunks(dt), chunks(bm), chunks(cm)))
    return ys.swapaxes(0, 1).reshape(x.shape), s_fin


def ssm_branch(z, xbc, dt_raw, s_init, p):
    b, l = xbc.shape[:2]
    f32 = jnp.float32
    xbc = jax.nn.silu(centred_dwconv(xbc, p['conv_w'], p['conv_b'])).astype(f32)
    xs = xbc[..., :D_INNER].reshape(b, l, SSM_HEADS, SSM_HEAD_DIM)
    bm = xbc[..., D_INNER:D_INNER + GN].reshape(b, l, SSM_GROUPS, D_STATE)
    cm = xbc[..., D_INNER + GN:].reshape(b, l, SSM_GROUPS, D_STATE)
    dt = jax.nn.softplus(dt_raw.astype(f32).reshape(b, l, 2, SSM_HEADS) + p['dt_bias'].astype(f32))
    a = -jnp.exp(p['a_log'].astype(f32))
    s_init = s_init.astype(f32)
    flip = lambda t: jnp.flip(t, axis=1)
    y_f, s_f = ssd_chunked(xs, dt[:, :, 0], a[0], bm, cm, s_init[:, 0])
    y_b, s_b = ssd_chunked(flip(xs), flip(dt[:, :, 1]), a[1], flip(bm), flip(cm), s_init[:, 1])
    y = y_f + flip(y_b) + p['ssm_d'].astype(f32)[:, None] * xs
    y = y.reshape(b, l, D_INNER) * jax.nn.silu(z.astype(f32))
    yg = y.reshape(b, l, SSM_GROUPS, D_INNER // SSM_GROUPS)
    yg = yg * lax.rsqrt(jnp.mean(yg * yg, axis=-1, keepdims=True) + EPS)
    y = yg.reshape(b, l, D_INNER) * p['ssm_norm_w'].astype(f32)
    out = y.astype(z.dtype) @ p['w_ssm_out']
    return out, jnp.stack([s_f, s_b], axis=1)


def project(h, p):
    b, l = h.shape[:2]
    q, k, v, z, xbc, dt_raw, g_a, g_s = jnp.split(h @ p['w_in'], IN_SPLITS, axis=-1)
    q = rmsnorm(q.reshape(b, l, N_HEADS, HEAD_DIM), p['q_norm_w'])
    k = rmsnorm(k.reshape(b, l, N_KV_HEADS, HEAD_DIM), p['k_norm_w'])
    v = v.reshape(b, l, N_KV_HEADS, HEAD_DIM)
    return q, k, v, z, xbc, dt_raw, g_a, g_s


def merge_branches(attn, ssm_o, g_a, g_s, p):
    merged = jax.nn.sigmoid(g_a) * (attn @ p['w_attn_out']) + jax.nn.sigmoid(g_s) * ssm_o
    return merged @ p['w_out']


def context_mixer(h, p):
    b = h.shape[0]
    q, k, v, z, xbc, dt_raw, g_a, g_s = project(h, p)
    attn = attend_blocked(q, k, v)
    s0 = jnp.zeros((b, 2, SSM_HEADS, SSM_HEAD_DIM, D_STATE), jnp.float32)
    ssm_o, s_fin = ssm_branch(z, xbc, dt_raw, s0, p)
    return merge_branches(attn, ssm_o, g_a, g_s, p), k, v, s_fin.astype(h.dtype)


def latent_mixer(h, ck, cv, cs, p):
    l = h.shape[1]
    q, k, v, z, xbc, dt_raw, g_a, g_s = project(h, p)
    cos, sin = axial_rope(l)
    q = apply_rope(q, cos, sin)
    k = apply_rope(k, cos, sin)
    k_all = jnp.concatenate([ck.astype(k.dtype), k], axis=1)
    v_all = jnp.concatenate([cv.astype(v.dtype), v], axis=1)
    attn = attend_blocked(q, k_all, v_all)
    ssm_o, _ = ssm_branch(z, xbc, dt_raw, cs, p)
    return merge_branches(attn, ssm_o, g_a, g_s, p)


def moe(h, w_router, router_bias, p):
    b, l, d = h.shape
    t = h.reshape(b * l, d)
    scores = jax.nn.sigmoid((t @ w_router).astype(jnp.float32))
    sel = scores + router_bias.astype(jnp.float32)
    grp = sel.reshape(-1, N_EXPERT_GROUPS, EXPERTS_PER_GROUP)
    grp_score = lax.top_k(grp, TOP_K)[0].sum(-1)
    best = jnp.argmax(grp_score, axis=-1)
    in_group = (jnp.arange(N_EXPERTS) // EXPERTS_PER_GROUP)[None, :] == best[:, None]
    _, idx = lax.top_k(jnp.where(in_group, sel, -jnp.inf), TOP_K)
    gv = jnp.take_along_axis(scores, idx, axis=-1)
    gv = gv / jnp.sum(gv, axis=-1, keepdims=True)
    combine = jnp.einsum('tk,tke->te', gv, jax.nn.one_hot(idx, N_EXPERTS, dtype=jnp.float32)).astype(h.dtype)
    out = jnp.zeros_like(t)
    for e in range(N_EXPERTS):
        he = jax.nn.silu(t @ p['w_exp_gate'][e]) * (t @ p['w_exp_up'][e])
        out = out + combine[:, e:e + 1] * (he @ p['w_exp_down'][e])
    return out.reshape(b, l, d)


def setup_inputs(seed: int = 0) -> dict:
    key = jax.random.key(seed)
    ks = jax.random.split(key, 32)
    f32 = jnp.float32

    def nrm(k, shape, scale):
        return jax.random.normal(k, shape, f32) * scale

    def gain(k, shape):
        return 1.0 + 0.02 * jax.random.normal(k, shape, f32)

    dt0 = jnp.exp(jax.random.uniform(ks[13], (DEPTH, 2, SSM_HEADS), f32, math.log(1e-3), math.log(1e-1)))
    return {
        'x_prompt': nrm(ks[0], (BATCH, SEQ, D_MODEL), 1.0),
        'x_sample': nrm(ks[1], (DEC_BATCH, DEC_SEQ, D_MODEL), 1.0),
        'cache_k': nrm(ks[2], (DEC_BATCH, DEPTH, PAST_LEN, N_KV_HEADS, HEAD_DIM), 1.0),
        'cache_v': nrm(ks[3], (DEC_BATCH, DEPTH, PAST_LEN, N_KV_HEADS, HEAD_DIM), 1.0),
        'state_ssm': nrm(ks[4], (DEC_BATCH, DEPTH, 2, SSM_HEADS, SSM_HEAD_DIM, D_STATE), 0.5),
        'c': nrm(ks[5], (DEC_BATCH, D_MODEL), 1.0),
        'c_ctx': nrm(ks[6], (D_MODEL,), 1.0),
        'norm1_w': gain(ks[7], (DEPTH, D_MODEL)),
        'norm2_w': gain(ks[8], (DEPTH, D_MODEL)),
        'w_mod': nrm(ks[9], (DEPTH, D_MODEL, N_MOD * D_MODEL), 0.5 * D_MODEL ** -0.5),
        'b_mod': nrm(ks[10], (DEPTH, N_MOD * D_MODEL), 0.02),
        'w_in': nrm(ks[11], (DEPTH, D_MODEL, N_IN), D_MODEL ** -0.5),
        'q_norm_w': gain(ks[12], (DEPTH, HEAD_DIM)),
        'k_norm_w': gain(ks[14], (DEPTH, HEAD_DIM)),
        'conv_w': nrm(ks[15], (DEPTH, CONV_K, CONV_DIM), CONV_K ** -0.5),
        'conv_b': nrm(ks[16], (DEPTH, CONV_DIM), 0.02),
        'a_log': jnp.log(jax.random.uniform(ks[17], (DEPTH, 2, SSM_HEADS), f32, 1.0, 16.0)),
        'dt_bias': dt0 + jnp.log(-jnp.expm1(-dt0)),
        'ssm_d': 1.0 + 0.1 * jax.random.normal(ks[18], (DEPTH, SSM_HEADS), f32),
        'ssm_norm_w': gain(ks[19], (DEPTH, D_INNER)),
        'w_attn_out': nrm(ks[20], (DEPTH, ATTN_WIDTH, D_MODEL), ATTN_WIDTH ** -0.5),
        'w_ssm_out': nrm(ks[21], (DEPTH, D_INNER, D_MODEL), D_INNER ** -0.5),
        'w_out': nrm(ks[22], (DEPTH, D_MODEL, D_MODEL), D_MODEL ** -0.5),
        'w_router': nrm(ks[23], (D_MODEL, N_EXPERTS), D_MODEL ** -0.5),
        'router_bias': nrm(ks[24], (N_EXPERTS,), 0.01),
        'w_exp_gate': nrm(ks[25], (DEPTH, N_EXPERTS, D_MODEL, D_FF_EXPERT), D_MODEL ** -0.5),
        'w_exp_up': nrm(ks[26], (DEPTH, N_EXPERTS, D_MODEL, D_FF_EXPERT), D_MODEL ** -0.5),
        'w_exp_down': nrm(ks[27], (DEPTH, N_EXPERTS, D_FF_EXPERT, D_MODEL), D_FF_EXPERT ** -0.5),
        'final_norm_w': gain(ks[28], (D_MODEL,)),
    }


def reference(x_prompt, x_sample, cache_k, cache_v, state_ssm, c, c_ctx, norm1_w, norm2_w, w_mod, b_mod,
              w_in, q_norm_w, k_norm_w, conv_w, conv_b, a_log, dt_bias, ssm_d, ssm_norm_w, w_attn_out,
              w_ssm_out, w_out, w_router, router_bias, w_exp_gate, w_exp_up, w_exp_down, final_norm_w):
    xp = x_prompt
    xs = x_sample
    new_k, new_v, new_s = [], [], []
    for l in range(DEPTH):
        p = {
            'w_mod': w_mod[l], 'b_mod': b_mod[l], 'w_in': w_in[l],
            'q_norm_w': q_norm_w[l], 'k_norm_w': k_norm_w[l],
            'conv_w': conv_w[l], 'conv_b': conv_b[l], 'a_log': a_log[l], 'dt_bias': dt_bias[l],
            'ssm_d': ssm_d[l], 'ssm_norm_w': ssm_norm_w[l],
            'w_attn_out': w_attn_out[l], 'w_ssm_out': w_ssm_out[l], 'w_out': w_out[l],
            'w_exp_gate': w_exp_gate[l], 'w_exp_up': w_exp_up[l], 'w_exp_down': w_exp_down[l],
        }
        sh1, sc1, g1, sh2, sc2, g2 = adaln(c_ctx, p)
        h = rmsnorm(xp, norm1_w[l]) * (1 + sc1) + sh1
        out, k_ctx, v_ctx, s_ctx = context_mixer(h, p)
        xp = xp + g1 * out
        h = rmsnorm(xp, norm2_w[l]) * (1 + sc2) + sh2
        xp = xp + g2 * moe(h, w_router, router_bias, p)
        new_k.append(k_ctx)
        new_v.append(v_ctx)
        new_s.append(s_ctx)
        sh1, sc1, g1, sh2, sc2, g2 = adaln(c, p)
        h = rmsnorm(xs, norm1_w[l]) * (1 + sc1) + sh1
        out = latent_mixer(h, cache_k[:, l], cache_v[:, l], state_ssm[:, l], p)
        xs = xs + g1 * out
        h = rmsnorm(xs, norm2_w[l]) * (1 + sc2) + sh2
        xs = xs + g2 * moe(h, w_router, router_bias, p)
    y_prompt = rmsnorm(xp, final_norm_w)
    y_sample = rmsnorm(xs, final_norm_w)
    new_cache_k = jnp.stack(new_k, axis=1)
    new_cache_v = jnp.stack(new_v, axis=1)
    new_state_ssm = jnp.stack(new_s, axis=1)
    return (y_prompt, y_sample, new_cache_k, new_cache_v, new_state_ssm)
```

```python
import functools

import jax
import jax.numpy as jnp
from jax import lax
from jax.experimental import pallas as pl
from jax.experimental.pallas import tpu as pltpu

F32 = jnp.float32
BF16 = jnp.bfloat16

EPS = 1e-6
D_MODEL = 1024
N_HEADS = 16
N_KV_HEADS = 4
HEAD_DIM = 64
ATTN_WIDTH = N_HEADS * HEAD_DIM
KV_WIDTH = N_KV_HEADS * HEAD_DIM
GRID_W = 64
ROPE_AXIS_DIM = HEAD_DIM // 2
ROPE_THETA = 10000.0
D_INNER = 2 * D_MODEL
SSM_HEAD_DIM = 64
SSM_HEADS = D_INNER // SSM_HEAD_DIM
SSM_GROUPS = 4
HEADS_PER_GROUP = SSM_HEADS // SSM_GROUPS
D_STATE = 128
GN = SSM_GROUPS * D_STATE
CONV_DIM = D_INNER + 2 * GN
SSD_CHUNK = 128
N_EXPERTS = 16
N_EXPERT_GROUPS = 4
EXPERTS_PER_GROUP = N_EXPERTS // N_EXPERT_GROUPS
D_FF_EXPERT = 512
N_MOD = 6
MOD_ROWS = 8

LANES = 128
SUBLANES = 8
VMEM_LIMIT = 56 * 1024 * 1024

TM = 256
TQ = 256
TK = 256
TE = 256
COND_ROWS = 16

Q_OFF, K_OFF, V_OFF = 0, ATTN_WIDTH, ATTN_WIDTH + KV_WIDTH
Z_OFF = ATTN_WIDTH + 2 * KV_WIDTH
XBC_OFF = Z_OFF + D_INNER
DT_OFF = XBC_OFF + CONV_DIM
G_OFF = DT_OFF + 2 * SSM_HEADS
N_IN = G_OFF + 2 * D_MODEL


def _cparams(sem):
    return pltpu.CompilerParams(dimension_semantics=sem, vmem_limit_bytes=VMEM_LIMIT)


def _const_spec(shape):
    nd = len(shape)
    return pl.BlockSpec(shape, lambda *_: (0,) * nd, pipeline_mode=pl.Buffered(1))


def _dot(a, b):
    return jnp.dot(a, b, preferred_element_type=F32)


def _dot_nt(a, b):
    return lax.dot_general(a, b, (((1,), (1,)), ((), ())), preferred_element_type=F32)


def _dot_tn(a, b):
    return lax.dot_general(a, b, (((0,), (0,)), ((), ())), preferred_element_type=F32)


def _split3(a):
    a1 = a.astype(BF16)
    r = a - a1.astype(F32)
    a2 = r.astype(BF16)
    a3 = (r - a2.astype(F32)).astype(BF16)
    return a1, a2, a3


def _dot_exact_lhs(m_bf16, a):
    a1, a2, a3 = _split3(a)
    return _dot(m_bf16, a1) + _dot(m_bf16, a2) + _dot(m_bf16, a3)


def _dot_exact_rhs(a, m_bf16):
    a1, a2, a3 = _split3(a)
    return _dot(a1, m_bf16) + _dot(a2, m_bf16) + _dot(a3, m_bf16)


def _modnorm(x, w, sc, sh):
    ms = jnp.mean(x * x, axis=-1, keepdims=True)
    return (x * lax.rsqrt(ms + EPS) * w) * (1.0 + sc) + sh


def _sigmoid(x):
    return 1.0 / (1.0 + jnp.exp(-x))


def _silu(x):
    return x * _sigmoid(x)


def _adaln_kernel(c_ref, w_ref, b_ref, o_ref):
    cs = _silu(c_ref[...])
    o_ref[0] = jnp.dot(cs, w_ref[0], preferred_element_type=F32,
                       precision=lax.Precision.HIGHEST) + b_ref[0]


def _adaln(cond, w_mod, b_mod):
    depth = w_mod.shape[0]
    nb = N_MOD
    return pl.pallas_call(
        _adaln_kernel,
        grid=(depth, nb),
        in_specs=[pl.BlockSpec((COND_ROWS, D_MODEL), lambda l, j: (0, 0)),
                  pl.BlockSpec((1, D_MODEL, D_MODEL), lambda l, j: (l, 0, j)),
                  pl.BlockSpec((1, 1, D_MODEL), lambda l, j: (l, 0, j))],
        out_specs=pl.BlockSpec((1, COND_ROWS, D_MODEL), lambda l, j: (l, 0, j)),
        out_shape=jax.ShapeDtypeStruct((depth, COND_ROWS, N_MOD * D_MODEL), F32),
        compiler_params=_cparams(("arbitrary", "arbitrary")),
        name="adaln",
    )(cond, w_mod, b_mod.reshape(depth, 1, N_MOD * D_MODEL))


def _in_proj_kernel(mrow_ref, rope_ref, first_ref, last_ref,
                    x_ref, xp_ref, xn_ref, mod_ref, n1w_ref,
                    wqkv_ref, wz_ref, wxbc_ref, wdt_ref, wg_ref,
                    qnw_ref, knw_ref, cos_ref, sin_ref, convw_ref, convb_ref, dtb_ref,
                    q_ref, k_ref, v_ref, z_ref, xs_ref, bc_ref, dt_ref, g_ref):
    del mrow_ref, rope_ref
    i = pl.program_id(0)
    sh1 = mod_ref[0, 0:1, :]
    sc1 = mod_ref[0, 1:2, :]
    nw = n1w_ref[...]
    h = _modnorm(x_ref[...], nw, sc1, sh1).astype(BF16)
    hp = _modnorm(xp_ref[...], nw, sc1, sh1).astype(BF16)
    hn = _modnorm(xn_ref[...], nw, sc1, sh1).astype(BF16)

    r = lax.broadcasted_iota(jnp.int32, (2 * LANES, 2 * LANES), 0) // HEAD_DIM
    c = lax.broadcasted_iota(jnp.int32, (2 * LANES, 2 * LANES), 1) // HEAD_DIM
    same_head = jnp.where(r == c, 1.0, 0.0).astype(BF16)
    lane = lax.broadcasted_iota(jnp.int32, (TM, LANES), 1)
    first_half = (lane % (HEAD_DIM // 2)) < (HEAD_DIM // 4)

    def head_norm_rope(t, w, scale):
        sq = t * t
        hi = sq.astype(BF16)
        lo = (sq - hi.astype(F32)).astype(BF16)
        ssum = _dot(hi, same_head) + _dot(lo, same_head)
        tn = t * lax.rsqrt(ssum * (1.0 / HEAD_DIM) + EPS) * w
        outs = []
        for s in range(2):
            a = tn[:, s * LANES:(s + 1) * LANES]
            rot = jnp.where(first_half, -pltpu.roll(a, LANES - HEAD_DIM // 4, 1),
                            pltpu.roll(a, HEAD_DIM // 4, 1))
            outs.append((a * cos_ref[...] + rot * sin_ref[...]) * scale)
        return outs

    qw = qnw_ref[...]
    kw = knw_ref[...]
    for cb in range(ATTN_WIDTH // (2 * LANES)):
        t = _dot(h, wqkv_ref[:, cb * 2 * LANES:(cb + 1) * 2 * LANES])
        a, b = head_norm_rope(t, qw, HEAD_DIM ** -0.5)
        q_ref[:, cb * 2 * LANES:cb * 2 * LANES + LANES] = a.astype(BF16)
        q_ref[:, cb * 2 * LANES + LANES:(cb + 1) * 2 * LANES] = b.astype(BF16)
    t = _dot(h, wqkv_ref[:, K_OFF:K_OFF + KV_WIDTH])
    a, b = head_norm_rope(t, kw, 1.0)
    k_ref[:, 0:LANES] = a
    k_ref[:, LANES:2 * LANES] = b
    v_ref[...] = _dot(h, wqkv_ref[:, V_OFF:V_OFF + KV_WIDTH])

    cw = 512
    for cb in range(D_INNER // cw):
        z_ref[:, cb * cw:(cb + 1) * cw] = _dot(h, wz_ref[:, cb * cw:(cb + 1) * cw]).astype(BF16)
    for cb in range(2 * D_MODEL // cw):
        g_ref[:, cb * cw:(cb + 1) * cw] = _sigmoid(
            _dot(h, wg_ref[:, cb * cw:(cb + 1) * cw])).astype(BF16)

    dtr = _dot(h, wdt_ref[...])[:, 0:2 * SSM_HEADS] + dtb_ref[...]
    dt_ref[...] = jnp.maximum(dtr, 0.0) + jnp.log(1.0 + jnp.exp(-jnp.abs(dtr)))

    keep_prev = jnp.where(first_ref[i] == 1, 0.0, 1.0)
    keep_next = jnp.where(last_ref[i] == 1, 0.0, 1.0)
    rows = lax.broadcasted_iota(jnp.int32, (TM, cw), 0)
    for cb in range(CONV_DIM // cw):
        sl = slice(cb * cw, (cb + 1) * cw)
        pre = _dot(h, wxbc_ref[:, sl])
        prev_row = _dot(hp, wxbc_ref[:, sl])[SUBLANES - 1:SUBLANES, :] * keep_prev
        next_row = _dot(hn, wxbc_ref[:, sl])[0:1, :] * keep_next
        up = jnp.where(rows == 0, prev_row, pltpu.roll(pre, 1, 0))
        down = jnp.where(rows == TM - 1, next_row, pltpu.roll(pre, TM - 1, 0))
        y = (convb_ref[:, sl] + up * convw_ref[0:1, sl] + pre * convw_ref[1:2, sl]
             + down * convw_ref[2:3, sl])
        y = _silu(y).astype(BF16)
        if cb < D_INNER // cw:
            xs_ref[:, sl] = y
        else:
            o = cb * cw - D_INNER
            bc_ref[:, o:o + cw] = y


def _in_proj(x, mod, meta, lw):
    t = x.shape[0]
    nblk = t // TM
    nhalo = t // SUBLANES
    per_halo = TM // SUBLANES

    def row_blk(width):
        return pl.BlockSpec((TM, width), lambda i, *_: (i, 0))

    grid_spec = pltpu.PrefetchScalarGridSpec(
        num_scalar_prefetch=4,
        grid=(nblk,),
        in_specs=[
            row_blk(D_MODEL),
            pl.BlockSpec((SUBLANES, D_MODEL),
                         lambda i, *_: (jnp.maximum(i * per_halo - 1, 0), 0)),
            pl.BlockSpec((SUBLANES, D_MODEL),
                         lambda i, *_: (jnp.minimum((i + 1) * per_halo, nhalo - 1), 0)),
            pl.BlockSpec((1, MOD_ROWS, D_MODEL), lambda i, mrow, *_: (mrow[i], 0, 0)),
            _const_spec((1, D_MODEL)),
            _const_spec((D_MODEL, ATTN_WIDTH + 2 * KV_WIDTH)),
            _const_spec((D_MODEL, D_INNER)),
            _const_spec((D_MODEL, CONV_DIM)),
            _const_spec((D_MODEL, LANES)),
            _const_spec((D_MODEL, 2 * D_MODEL)),
            _const_spec((1, 2 * LANES)),
            _const_spec((1, 2 * LANES)),
            pl.BlockSpec((TM, LANES), lambda i, mrow, rope, *_: (rope[i], 0)),
            pl.BlockSpec((TM, LANES), lambda i, mrow, rope, *_: (rope[i], 0)),
            _const_spec((3, CONV_DIM)),
            _const_spec((1, CONV_DIM)),
            _const_spec((1, 2 * SSM_HEADS)),
        ],
        out_specs=[row_blk(ATTN_WIDTH), row_blk(KV_WIDTH), row_blk(KV_WIDTH), row_blk(D_INNER),
                   row_blk(D_INNER), row_blk(2 * GN), row_blk(2 * SSM_HEADS), row_blk(2 * D_MODEL)],
    )
    out_shape = [
        jax.ShapeDtypeStruct((t, ATTN_WIDTH), BF16),
        jax.ShapeDtypeStruct((t, KV_WIDTH), F32),
        jax.ShapeDtypeStruct((t, KV_WIDTH), F32),
        jax.ShapeDtypeStruct((t, D_INNER), BF16),
        jax.ShapeDtypeStruct((t, D_INNER), BF16),
        jax.ShapeDtypeStruct((t, 2 * GN), BF16),
        jax.ShapeDtypeStruct((t, 2 * SSM_HEADS), F32),
        jax.ShapeDtypeStruct((t, 2 * D_MODEL), BF16),
    ]
    return pl.pallas_call(
        _in_proj_kernel, grid_spec=grid_spec, out_shape=out_shape,
        compiler_params=_cparams(("arbitrary",)), name="in_proj",
    )(meta["mrow"], meta["rope_blk"], meta["first"], meta["last"],
      x, x, x, mod, lw["norm1_w"], lw["wqkv"], lw["wz"], lw["wxbc"], lw["wdt"], lw["wg"],
      lw["q_norm_w"], lw["k_norm_w"], meta["cos"], meta["sin"],
      lw["conv_w"], lw["conv_b"], lw["dt_bias"])


def _attn_kernel(q_ref, k_ref, vt_ref, o_ref, *, n_kblk):
    nq = 4 * TQ
    zeros_half = jnp.zeros((HEAD_DIM, TQ), F32)
    for hh in range(2):
        cols = []
        for jj in range(2):
            off = 2 * LANES * hh + LANES * jj
            qt = q_ref[:, off:off + LANES].astype(F32).T
            for s in range(2):
                head = qt[s * HEAD_DIM:(s + 1) * HEAD_DIM, :]
                parts = [head, zeros_half] if hh == 0 else [zeros_half, head]
                cols.append(jnp.concatenate(parts, axis=0))
        qT = jnp.concatenate(cols, axis=1).astype(BF16)

        def body(kb, carry):
            m, l, acc = carry
            kblk = k_ref[0, kb]
            vT = vt_ref[0, hh, kb]
            s = _dot(kblk, qT)
            m_new = jnp.maximum(m, jnp.max(s, axis=0, keepdims=True))
            alpha = jnp.exp(m - m_new)
            p = jnp.exp(s - m_new)
            l = alpha * l + jnp.sum(p, axis=0, keepdims=True)
            acc = alpha * acc + _dot(vT, p.astype(BF16))
            return m_new, l, acc

        m0 = jnp.full((1, nq), -jnp.inf, F32)
        l0 = jnp.zeros((1, nq), F32)
        a0 = jnp.zeros((HEAD_DIM, nq), F32)
        m, l, acc = lax.fori_loop(0, n_kblk, body, (m0, l0, a0))
        oT = acc / l
        for jj in range(2):
            pair = jnp.concatenate([oT[:, (2 * jj) * TQ:(2 * jj + 1) * TQ],
                                    oT[:, (2 * jj + 1) * TQ:(2 * jj + 2) * TQ]], axis=0)
            off = 2 * LANES * hh + LANES * jj
            o_ref[:, off:off + LANES] = pair.T.astype(BF16)


def _attention(q, k_blk, vt_blk, nseq, seq_len):
    n_kblk = k_blk.shape[2]
    nqb = seq_len // TQ
    return pl.pallas_call(
        functools.partial(_attn_kernel, n_kblk=n_kblk),
        grid=(nseq, 2, nqb),
        in_specs=[
            pl.BlockSpec((TQ, 4 * LANES), lambda b, p, i: (b * nqb + i, p)),
            pl.BlockSpec((1, None, n_kblk, TK, LANES), lambda b, p, i: (b, p, 0, 0, 0)),
            pl.BlockSpec((1, None, 2, n_kblk, HEAD_DIM, TK), lambda b, p, i: (b, p, 0, 0, 0, 0)),
        ],
        out_specs=pl.BlockSpec((TQ, 4 * LANES), lambda b, p, i: (b * nqb + i, p)),
        out_shape=jax.ShapeDtypeStruct(q.shape, BF16),
        compiler_params=_cparams(("arbitrary", "arbitrary", "arbitrary")),
        name="attention",
    )(q, k_blk, vt_blk)


def _kv_blocks(k, v, nseq):
    lk = k.shape[1]
    n = lk // TK
    kb = k.astype(BF16).reshape(nseq, n, TK, 2, LANES).transpose(0, 3, 1, 2, 4)
    vb = v.astype(BF16).reshape(nseq, n, TK, 2, 2, HEAD_DIM).transpose(0, 3, 4, 1, 5, 2)
    return kb, vb


def _ssd_direction(x_ref, bc_ref, dt_ref, dtt_ref, a_row, a_col, st_ref, y_ref, d, expand):
    q = SSD_CHUNK
    row = lax.broadcasted_iota(jnp.int32, (q, q), 0)
    col = lax.broadcasted_iota(jnp.int32, (q, q), 1)
    lower = row >= col
    causal = lower if d == 0 else (row <= col)
    tri = jnp.where(causal, 1.0, 0.0).astype(BF16)
    tri_t = jnp.where(causal, 0.0, 1.0)
    tri_t = jnp.where(row == col, 1.0, tri_t).astype(BF16)

    hs = slice(d * SSM_HEADS, (d + 1) * SSM_HEADS)
    dt = dt_ref[:, hs]
    dtt = dtt_ref[hs, :]
    a = dt * a_row
    at = dtt * a_col
    acs = _dot_exact_lhs(tri, a)
    acst = _dot_exact_rhs(at, tri_t)
    edge = q - 1 if d == 0 else 0
    last = acs[edge:edge + 1, :]
    wj = jnp.exp(last - acs) * dt
    wj_x = _dot(wj.astype(BF16), expand)
    eacs_x = _dot_exact_rhs(jnp.exp(acs), expand)
    elast_x = _dot_exact_rhs(jnp.exp(last), expand)

    lane = lax.broadcasted_iota(jnp.int32, (q, LANES), 1)
    lo_half = lane < SSM_HEAD_DIM
    for g in range(SSM_GROUPS):
        bg = bc_ref[:, g * D_STATE:(g + 1) * D_STATE]
        cg = bc_ref[:, GN + g * D_STATE:GN + (g + 1) * D_STATE]
        gmat = _dot_nt(cg, bg)
        gsl = slice(g * HEADS_PER_GROUP * SSM_HEAD_DIM, (g + 1) * HEADS_PER_GROUP * SSM_HEAD_DIM)
        st = st_ref[d, :, gsl]
        y_inter = _dot(cg, st.astype(BF16)) * eacs_x[:, gsl]
        xg = x_ref[:, gsl]
        for hp in range(HEADS_PER_GROUP // 2):
            xpair = xg[:, hp * LANES:(hp + 1) * LANES]
            ypair = y_inter[:, hp * LANES:(hp + 1) * LANES]
            for s in range(2):
                hd = g * HEADS_PER_GROUP + 2 * hp + s
                seg = acs[:, hd:hd + 1] - acst[hd:hd + 1, :]
                w = gmat * jnp.exp(jnp.where(causal, seg, -jnp.inf)) * dtt[hd:hd + 1, :]
                keep = lo_half if s == 0 else jnp.logical_not(lo_half)
                xm = jnp.where(keep, xpair, jnp.zeros_like(xpair))
                ypair = ypair + _dot(w.astype(BF16), xm)
            lo = g * HEADS_PER_GROUP * SSM_HEAD_DIM + hp * LANES
            y_ref[:, lo:lo + LANES] = ypair.astype(y_ref.dtype)
        xw = (xg.astype(F32) * wj_x[:, gsl]).astype(BF16)
        st_ref[d, :, gsl] = st * elast_x[:, gsl] + _dot_tn(bg, xw)


def _ssd_kernel(*refs, has_init, has_final):
    it = iter(refs)
    xf_ref, bcf_ref, dtf_ref, dttf_ref = next(it), next(it), next(it), next(it)
    xb_ref, bcb_ref, dtb_ref, dttb_ref = next(it), next(it), next(it), next(it)
    arow_ref, acol_ref, exp_ref = next(it), next(it), next(it)
    s0_ref = next(it) if has_init else None
    yf_ref, yb_ref = next(it), next(it)
    sfin_ref = next(it) if has_final else None
    st_ref = next(it)
    c = pl.program_id(1)

    @pl.when(c == 0)
    def _():
        if has_init:
            st_ref[...] = s0_ref[0]
        else:
            st_ref[...] = jnp.zeros(st_ref.shape, F32)

    expand = exp_ref[...]
    _ssd_direction(xf_ref, bcf_ref, dtf_ref, dttf_ref, arow_ref[0:1, :], acol_ref[:, 0:1],
                   st_ref, yf_ref, 0, expand)
    _ssd_direction(xb_ref, bcb_ref, dtb_ref, dttb_ref, arow_ref[1:2, :], acol_ref[:, 1:2],
                   st_ref, yb_ref, 1, expand)

    if has_final:
        @pl.when(c == pl.num_programs(1) - 1)
        def _():
            sfin_ref[0] = st_ref[...]


def _ssd(xs, bc, dt, dtt, a_row, a_col, expand, s0, nseq, seq_len, has_final):
    t = xs.shape[0]
    nc = seq_len // SSD_CHUNK
    has_init = s0 is not None
    fwd = lambda b, c: (b * nc + c, 0)
    bwd = lambda b, c: (b * nc + nc - 1 - c, 0)
    fwd_t = lambda b, c: (0, b * nc + c)
    bwd_t = lambda b, c: (0, b * nc + nc - 1 - c)

    def specs(idx, idx_t):
        return [pl.BlockSpec((SSD_CHUNK, D_INNER), idx),
                pl.BlockSpec((SSD_CHUNK, 2 * GN), idx),
                pl.BlockSpec((SSD_CHUNK, 2 * SSM_HEADS), idx),
                pl.BlockSpec((2 * SSM_HEADS, SSD_CHUNK), idx_t)]

    in_specs = specs(fwd, fwd_t) + specs(bwd, bwd_t) + [
        pl.BlockSpec((2, SSM_HEADS), lambda b, c: (0, 0)),
        pl.BlockSpec((SSM_HEADS, 2), lambda b, c: (0, 0)),
        pl.BlockSpec((SSM_HEADS, D_INNER), lambda b, c: (0, 0)),
    ]
    args = [xs, bc, dt, dtt, xs, bc, dt, dtt, a_row, a_col, expand]
    st_shape = (2, D_STATE, D_INNER)
    if has_init:
        in_specs.append(pl.BlockSpec((1,) + st_shape, lambda b, c: (b, 0, 0, 0)))
        args.append(s0)
    out_specs = [pl.BlockSpec((SSD_CHUNK, D_INNER), fwd), pl.BlockSpec((SSD_CHUNK, D_INNER), bwd)]
    out_shape = [jax.ShapeDtypeStruct((t, D_INNER), BF16), jax.ShapeDtypeStruct((t, D_INNER), BF16)]
    if has_final:
        out_specs.append(pl.BlockSpec((1,) + st_shape, lambda b, c: (b, 0, 0, 0)))
        out_shape.append(jax.ShapeDtypeStruct((nseq,) + st_shape, F32))
    return pl.pallas_call(
        functools.partial(_ssd_kernel, has_init=has_init, has_final=has_final),
        grid=(nseq, nc), in_specs=in_specs, out_specs=out_specs, out_shape=out_shape,
        scratch_shapes=[pltpu.VMEM(st_shape, F32)],
        compiler_params=_cparams(("arbitrary", "arbitrary")),
        name="ssd",
    )(*args)


def _route(sel, scores):
    neg = -jnp.inf
    rows = [sel[e:e + 1, :] for e in range(N_EXPERTS)]
    srow = [scores[e:e + 1, :] for e in range(N_EXPERTS)]
    best_score = None
    best = None
    for g in range(N_EXPERT_GROUPS):
        v = rows[g * EXPERTS_PER_GROUP:(g + 1) * EXPERTS_PER_GROUP]
        top2 = None
        for a in range(EXPERTS_PER_GROUP):
            for b in range(a + 1, EXPERTS_PER_GROUP):
                s = v[a] + v[b]
                top2 = s if top2 is None else jnp.maximum(top2, s)
        if g == 0:
            best_score, best = top2, jnp.zeros(top2.shape, jnp.int32)
        else:
            better = top2 > best_score
            best = jnp.where(better, g, best)
            best_score = jnp.where(better, top2, best_score)

    def pick(vals):
        out = vals[0]
        for g in range(1, N_EXPERT_GROUPS):
            out = jnp.where(best == g, vals[g], out)
        return out

    gsel = [pick([rows[g * EXPERTS_PER_GROUP + j] for g in range(N_EXPERT_GROUPS)])
            for j in range(EXPERTS_PER_GROUP)]
    gsc = [pick([srow[g * EXPERTS_PER_GROUP + j] for g in range(N_EXPERT_GROUPS)])
           for j in range(EXPERTS_PER_GROUP)]

    def argmax_first(vals):
        bi = jnp.zeros(vals[0].shape, jnp.int32)
        bv = vals[0]
        for j in range(1, len(vals)):
            better = vals[j] > bv
            bi = jnp.where(better, j, bi)
            bv = jnp.where(better, vals[j], bv)
        return bi

    i1 = argmax_first(gsel)
    i2 = argmax_first([jnp.where(i1 == j, neg, gsel[j]) for j in range(EXPERTS_PER_GROUP)])

    def take(vals, idx):
        out = vals[0]
        for j in range(1, len(vals)):
            out = jnp.where(idx == j, vals[j], out)
        return out

    g1 = take(gsc, i1)
    g2 = take(gsc, i2)
    tot = g1 + g2
    idx = jnp.concatenate([best * EXPERTS_PER_GROUP + i1, best * EXPERTS_PER_GROUP + i2], axis=0)
    gate = jnp.concatenate([g1 / tot, g2 / tot], axis=0)
    return idx, gate


def _merge_kernel(mrow_ref, x_ref, attn_ref, yf_ref, yb_ref, xs_ref, z_ref, g_ref, mod_ref,
                  wa_ref, ws_ref, wo_ref, dvec_ref, snw_ref, n2w_ref, wrt_ref, rb_ref,
                  x1_ref, h2_ref, idx_ref, gate_ref):
    del mrow_ref
    gw = D_INNER // SSM_GROUPS
    ssm_o = None
    for g in range(SSM_GROUPS):
        sl = slice(g * gw, (g + 1) * gw)
        xs = xs_ref[:, sl].astype(F32)
        y = yf_ref[:, sl].astype(F32) + yb_ref[:, sl].astype(F32) + dvec_ref[:, sl] * xs
        y = y * _silu(z_ref[:, sl].astype(F32))
        ms = jnp.mean(y * y, axis=-1, keepdims=True)
        yn = (y * lax.rsqrt(ms + EPS) * snw_ref[:, sl]).astype(BF16)
        part = _dot(yn, ws_ref[sl, :])
        ssm_o = part if ssm_o is None else ssm_o + part
    attn_o = _dot(attn_ref[...], wa_ref[...])
    merged = (g_ref[:, 0:D_MODEL].astype(F32) * attn_o
              + g_ref[:, D_MODEL:2 * D_MODEL].astype(F32) * ssm_o)
    out = _dot(merged.astype(BF16), wo_ref[...])
    x1 = x_ref[...] + mod_ref[0, 2:3, :] * out
    x1_ref[...] = x1
    h2 = _modnorm(x1, n2w_ref[...], mod_ref[0, 4:5, :], mod_ref[0, 3:4, :])
    h2_ref[...] = h2.astype(BF16)
    logits = lax.dot_general(wrt_ref[...], h2, (((1,), (1,)), ((), ())),
                             preferred_element_type=F32, precision=lax.Precision.HIGHEST)
    scores = _sigmoid(logits)
    idx, gate = _route(scores + rb_ref[...], scores)
    idx_ref[...] = idx
    gate_ref[...] = gate


def _merge(x, attn, yf, yb, xs, z, g, mod, meta, lw, wrt, rb):
    t = x.shape[0]
    nblk = t // TM

    def row_blk(width):
        return pl.BlockSpec((TM, width), lambda i, *_: (i, 0))

    grid_spec = pltpu.PrefetchScalarGridSpec(
        num_scalar_prefetch=1,
        grid=(nblk,),
        in_specs=[row_blk(D_MODEL), row_blk(ATTN_WIDTH), row_blk(D_INNER), row_blk(D_INNER),
                  row_blk(D_INNER), row_blk(D_INNER), row_blk(2 * D_MODEL),
                  pl.BlockSpec((1, MOD_ROWS, D_MODEL), lambda i, mrow: (mrow[i], 0, 0)),
                  _const_spec((ATTN_WIDTH, D_MODEL)), _const_spec((D_INNER, D_MODEL)),
                  _const_spec((D_MODEL, D_MODEL)), _const_spec((1, D_INNER)),
                  _const_spec((1, D_INNER)), _const_spec((1, D_MODEL)),
                  _const_spec((N_EXPERTS, D_MODEL)), _const_spec((N_EXPERTS, 1))],
        out_specs=[row_blk(D_MODEL), row_blk(D_MODEL),
                   pl.BlockSpec((2, TM), lambda i, *_: (0, i)),
                   pl.BlockSpec((2, TM), lambda i, *_: (0, i))],
    )
    out_shape = [jax.ShapeDtypeStruct((t, D_MODEL), F32), jax.ShapeDtypeStruct((t, D_MODEL), BF16),
                 jax.ShapeDtypeStruct((2, t), jnp.int32), jax.ShapeDtypeStruct((2, t), F32)]
    return pl.pallas_call(
        _merge_kernel, grid_spec=grid_spec, out_shape=out_shape,
        compiler_params=_cparams(("arbitrary",)), name="merge",
    )(meta["mrow"], x, attn, yf, yb, xs, z, g, mod, lw["w_attn_out"], lw["w_ssm_out"], lw["w_out"],
      lw["ssm_d"], lw["ssm_norm_w"], lw["norm2_w"], wrt, rb)


def _expert_kernel(te_ref, nt_ref, x_ref, wg_ref, wu_ref, wd_ref, o_ref):
    del te_ref
    i = pl.program_id(0)

    @pl.when(i < nt_ref[0])
    def _():
        x = x_ref[...]
        hmid = _silu(_dot(x, wg_ref[0])) * _dot(x, wu_ref[0])
        o_ref[...] = _dot(hmid.astype(BF16), wd_ref[0])

    @pl.when(i >= nt_ref[0])
    def _():
        o_ref[...] = jnp.zeros(o_ref.shape, o_ref.dtype)


def _experts(x_sorted, tile_expert, n_tiles, wg, wu, wd):
    rows = x_sorted.shape[0]
    grid_spec = pltpu.PrefetchScalarGridSpec(
        num_scalar_prefetch=2,
        grid=(rows // TE,),
        in_specs=[pl.BlockSpec((TE, D_MODEL), lambda i, te, nt: (i, 0)),
                  pl.BlockSpec((1, D_MODEL, D_FF_EXPERT), lambda i, te, nt: (te[i], 0, 0)),
                  pl.BlockSpec((1, D_MODEL, D_FF_EXPERT), lambda i, te, nt: (te[i], 0, 0)),
                  pl.BlockSpec((1, D_FF_EXPERT, D_MODEL), lambda i, te, nt: (te[i], 0, 0))],
        out_specs=pl.BlockSpec((TE, D_MODEL), lambda i, te, nt: (i, 0)),
    )
    return pl.pallas_call(
        _expert_kernel, grid_spec=grid_spec,
        out_shape=jax.ShapeDtypeStruct((rows, D_MODEL), F32),
        compiler_params=_cparams(("arbitrary",)), name="experts",
    )(tile_expert, n_tiles, x_sorted, wg, wu, wd)


def _combine_kernel(mrow_ref, x1_ref, y0_ref, y1_ref, gate_ref, mod_ref, fnw_ref, o_ref, *, final):
    del mrow_ref
    moe = gate_ref[:, 0:1] * y0_ref[...] + gate_ref[:, 1:2] * y1_ref[...]
    x2 = x1_ref[...] + mod_ref[0, 5:6, :] * moe
    if final:
        ms = jnp.mean(x2 * x2, axis=-1, keepdims=True)
        x2 = x2 * lax.rsqrt(ms + EPS) * fnw_ref[...]
    o_ref[...] = x2


def _combine(x1, y0, y1, gate_t, mod, meta, fnw, final):
    t = x1.shape[0]

    def row_blk(width):
        return pl.BlockSpec((TM, width), lambda i, *_: (i, 0))

    grid_spec = pltpu.PrefetchScalarGridSpec(
        num_scalar_prefetch=1,
        grid=(t // TM,),
        in_specs=[row_blk(D_MODEL), row_blk(D_MODEL), row_blk(D_MODEL), row_blk(2),
                  pl.BlockSpec((1, MOD_ROWS, D_MODEL), lambda i, mrow: (mrow[i], 0, 0)),
                  _const_spec((1, D_MODEL))],
        out_specs=row_blk(D_MODEL),
    )
    return pl.pallas_call(
        functools.partial(_combine_kernel, final=final), grid_spec=grid_spec,
        out_shape=jax.ShapeDtypeStruct((t, D_MODEL), F32),
        compiler_params=_cparams(("arbitrary",)), name="combine",
    )(meta["mrow"], x1, y0, y1, gate_t, mod, fnw)


def _moe(h2, idx, gate, lw):
    t = h2.shape[0]
    n_assign = 2 * t
    rows = n_assign + N_EXPERTS * TE
    e_flat = idx.reshape(n_assign)
    onehot = (e_flat[:, None] == jnp.arange(N_EXPERTS, dtype=jnp.int32)[None, :]).astype(jnp.int32)
    counts = jnp.sum(onehot, axis=0)
    padded = ((counts + TE - 1) // TE) * TE
    pad_end = jnp.cumsum(padded)
    pad_off = pad_end - padded
    off = jnp.cumsum(counts) - counts
    rank = jnp.take_along_axis(jnp.cumsum(onehot, axis=0), e_flat[:, None], axis=1)[:, 0] - 1
    pos = pad_off[e_flat] + rank
    order = jnp.argsort(e_flat, stable=True).astype(jnp.int32)
    p = jnp.arange(rows, dtype=jnp.int32)
    ep = jnp.minimum(jnp.searchsorted(pad_end, p, side="right"), N_EXPERTS - 1).astype(jnp.int32)
    r = p - pad_off[ep]
    valid = r < counts[ep]
    src = jnp.where(valid, order[jnp.clip(off[ep] + r, 0, n_assign - 1)] % t, 0)
    n_tiles = (pad_end[-1] // TE).astype(jnp.int32).reshape(1)
    tile_start = jnp.arange(rows // TE, dtype=jnp.int32) * TE
    tile_expert = jnp.minimum(jnp.searchsorted(pad_end, tile_start, side="right"),
                              N_EXPERTS - 1).astype(jnp.int32)
    last_used = tile_expert[jnp.maximum(n_tiles[0] - 1, 0)]
    tile_expert = jnp.where(tile_start < pad_end[-1], tile_expert, last_used)
    x_sorted = jnp.take(h2, src, axis=0)
    y_sorted = _experts(x_sorted, tile_expert, n_tiles, lw["w_exp_gate"], lw["w_exp_up"],
                        lw["w_exp_down"])
    pos2 = pos.reshape(2, t)
    return jnp.take(y_sorted, pos2[0], axis=0), jnp.take(y_sorted, pos2[1], axis=0)


def _rope_tables(n_tokens):
    rows = n_tokens // GRID_W
    row = jnp.repeat(jnp.arange(rows, dtype=F32), GRID_W)
    col = jnp.tile(jnp.arange(GRID_W, dtype=F32), rows)
    inv = 1.0 / (ROPE_THETA ** (jnp.arange(0, ROPE_AXIS_DIM, 2, dtype=F32) / ROPE_AXIS_DIM))
    ar = row[:, None] * inv
    ac = col[:, None] * inv
    ang = jnp.concatenate([ar, ar, ac, ac], axis=-1)
    cos, sin = jnp.cos(ang), jnp.sin(ang)
    quarter = (jnp.arange(HEAD_DIM) % (HEAD_DIM // 2)) < (HEAD_DIM // 4)
    del quarter
    cos = jnp.concatenate([jnp.ones((TM, HEAD_DIM), F32), cos], axis=0)
    sin = jnp.concatenate([jnp.zeros((TM, HEAD_DIM), F32), sin], axis=0)
    return jnp.tile(cos, (1, 2)), jnp.tile(sin, (1, 2))


def _block_meta(n_ctx, ctx_len, n_lat, lat_len):
    ctx_blocks = n_ctx * ctx_len // TM
    lat_blocks = n_lat * lat_len // TM
    per_ctx = ctx_len // TM
    per_lat = lat_len // TM
    bi = jnp.arange(ctx_blocks + lat_blocks, dtype=jnp.int32)
    is_ctx = bi < ctx_blocks
    lat_i = bi - ctx_blocks
    mrow = jnp.where(is_ctx, 0, 1 + lat_i // per_lat).astype(jnp.int32)
    rope_blk = jnp.where(is_ctx, 0, 1 + lat_i % per_lat).astype(jnp.int32)
    first = jnp.where(is_ctx, bi % per_ctx == 0, lat_i % per_lat == 0).astype(jnp.int32)
    last = jnp.where(is_ctx, bi % per_ctx == per_ctx - 1, lat_i % per_lat == per_lat - 1).astype(jnp.int32)
    cos, sin = _rope_tables(lat_len)
    return {"mrow": mrow, "rope_blk": rope_blk, "first": first, "last": last, "cos": cos, "sin": sin}


def kernel(x_prompt, x_sample, cache_k, cache_v, state_ssm, c, c_ctx, norm1_w, norm2_w, w_mod, b_mod,
           w_in, q_norm_w, k_norm_w, conv_w, conv_b, a_log, dt_bias, ssm_d, ssm_norm_w, w_attn_out,
           w_ssm_out, w_out, w_router, router_bias, w_exp_gate, w_exp_up, w_exp_down, final_norm_w):
    n_ctx, ctx_len, _ = x_prompt.shape
    n_lat, lat_len, _ = x_sample.shape
    depth = w_in.shape[0]
    t_ctx = n_ctx * ctx_len
    t_lat = n_lat * lat_len
    assert ctx_len % TM == 0 and lat_len % TM == 0 and 1 + n_lat <= COND_ROWS
    assert cache_k.shape[2] % TK == 0 and lat_len % GRID_W == 0

    meta = _block_meta(n_ctx, ctx_len, n_lat, lat_len)
    x = jnp.concatenate([x_prompt.reshape(t_ctx, D_MODEL), x_sample.reshape(t_lat, D_MODEL)], axis=0)

    cond = jnp.zeros((COND_ROWS, D_MODEL), F32).at[0].set(c_ctx).at[1:1 + n_lat].set(c)
    mod_all = _adaln(cond, w_mod, b_mod).reshape(depth, COND_ROWS, N_MOD, D_MODEL)
    mod_all = jnp.pad(mod_all, ((0, 0), (0, 0), (0, MOD_ROWS - N_MOD), (0, 0)))

    expand = jnp.repeat(jnp.eye(SSM_HEADS, dtype=BF16), SSM_HEAD_DIM, axis=1)
    wrt = w_router.T
    rb = router_bias.reshape(N_EXPERTS, 1)
    fnw = final_norm_w.reshape(1, D_MODEL)

    new_k, new_v, new_s = [], [], []
    for l in range(depth):
        wl = w_in[l]
        lw = {
            "norm1_w": norm1_w[l].reshape(1, D_MODEL),
            "norm2_w": norm2_w[l].reshape(1, D_MODEL),
            "wqkv": wl[:, :Z_OFF].astype(BF16),
            "wz": wl[:, Z_OFF:XBC_OFF].astype(BF16),
            "wxbc": wl[:, XBC_OFF:DT_OFF].astype(BF16),
            "wdt": jnp.pad(wl[:, DT_OFF:G_OFF], ((0, 0), (0, LANES - 2 * SSM_HEADS))).astype(BF16),
            "wg": wl[:, G_OFF:].astype(BF16),
            "q_norm_w": jnp.tile(q_norm_w[l], 4).reshape(1, 2 * LANES),
            "k_norm_w": jnp.tile(k_norm_w[l], 4).reshape(1, 2 * LANES),
            "conv_w": conv_w[l],
            "conv_b": conv_b[l].reshape(1, CONV_DIM),
            "dt_bias": dt_bias[l].reshape(1, 2 * SSM_HEADS),
            "ssm_d": jnp.repeat(ssm_d[l], SSM_HEAD_DIM).reshape(1, D_INNER),
            "ssm_norm_w": ssm_norm_w[l].reshape(1, D_INNER),
            "w_attn_out": w_attn_out[l].astype(BF16),
            "w_ssm_out": w_ssm_out[l].astype(BF16),
            "w_out": w_out[l].astype(BF16),
            "w_exp_gate": w_exp_gate[l].astype(BF16),
            "w_exp_up": w_exp_up[l].astype(BF16),
            "w_exp_down": w_exp_down[l].astype(BF16),
        }
        mod = mod_all[l]
        q, k, v, z, xs, bc, dt, g = _in_proj(x, mod, meta, lw)

        k_ctx = k[:t_ctx].reshape(n_ctx, ctx_len, KV_WIDTH)
        v_ctx = v[:t_ctx].reshape(n_ctx, ctx_len, KV_WIDTH)
        kb, vb = _kv_blocks(k_ctx, v_ctx, n_ctx)
        attn_ctx = _attention(q[:t_ctx], kb, vb, n_ctx, ctx_len)
        past = cache_k.shape[2]
        k_lat = jnp.concatenate([cache_k[:, l].reshape(n_lat, past, KV_WIDTH),
                                 k[t_ctx:].reshape(n_lat, lat_len, KV_WIDTH)], axis=1)
        v_lat = jnp.concatenate([cache_v[:, l].reshape(n_lat, past, KV_WIDTH),
                                 v[t_ctx:].reshape(n_lat, lat_len, KV_WIDTH)], axis=1)
        kb, vb = _kv_blocks(k_lat, v_lat, n_lat)
        attn_lat = _attention(q[t_ctx:], kb, vb, n_lat, lat_len)
        attn = jnp.concatenate([attn_ctx, attn_lat], axis=0)

        a_neg = -jnp.exp(a_log[l])
        dtt = dt.T
        yf_c, yb_c, s_ctx = _ssd(xs[:t_ctx], bc[:t_ctx], dt[:t_ctx], dtt[:, :t_ctx], a_neg, a_neg.T,
                                 expand, None, n_ctx, ctx_len, True)
        s0 = state_ssm[:, l].transpose(0, 1, 4, 2, 3).reshape(n_lat, 2, D_STATE, D_INNER)
        yf_l, yb_l = _ssd(xs[t_ctx:], bc[t_ctx:], dt[t_ctx:], dtt[:, t_ctx:], a_neg, a_neg.T,
                          expand, s0, n_lat, lat_len, False)
        yf = jnp.concatenate([yf_c, yf_l], axis=0)
        yb = jnp.concatenate([yb_c, yb_l], axis=0)

        x1, h2, idx, gate = _merge(x, attn, yf, yb, xs, z, g, mod, meta, lw, wrt, rb)
        y0, y1 = _moe(h2, idx, gate, lw)
        x = _combine(x1, y0, y1, gate.T, mod, meta, fnw, l == depth - 1)

        new_k.append(k_ctx.reshape(n_ctx, ctx_len, N_KV_HEADS, HEAD_DIM))
        new_v.append(v_ctx.reshape(n_ctx, ctx_len, N_KV_HEADS, HEAD_DIM))
        new_s.append(s_ctx.reshape(n_ctx, 2, D_STATE, SSM_HEADS, SSM_HEAD_DIM).transpose(0, 1, 3, 4, 2))

    y_prompt = x[:t_ctx].reshape(n_ctx, ctx_len, D_MODEL)
    y_sample = x[t_ctx:].reshape(n_lat, lat_len, D_MODEL)
    return (y_prompt, y_sample, jnp.stack(new_k, axis=1), jnp.stack(new_v, axis=1),
            jnp.stack(new_s, axis=1))
```

```python
import functools

import jax
import jax.numpy as jnp
from jax import lax
from jax.experimental import pallas as pl
from jax.experimental.pallas import tpu as pltpu

F32 = jnp.float32
BF16 = jnp.bfloat16

EPS = 1e-6
D_MODEL = 1024
N_HEADS = 16
N_KV_HEADS = 4
HEAD_DIM = 64
ATTN_WIDTH = N_HEADS * HEAD_DIM
KV_WIDTH = N_KV_HEADS * HEAD_DIM
GRID_W = 64
ROPE_AXIS_DIM = HEAD_DIM // 2
ROPE_THETA = 10000.0
D_INNER = 2 * D_MODEL
SSM_HEAD_DIM = 64
SSM_HEADS = D_INNER // SSM_HEAD_DIM
SSM_GROUPS = 4
HEADS_PER_GROUP = SSM_HEADS // SSM_GROUPS
D_STATE = 128
GN = SSM_GROUPS * D_STATE
CONV_DIM = D_INNER + 2 * GN
SSD_CHUNK = 128
N_EXPERTS = 16
N_EXPERT_GROUPS = 4
EXPERTS_PER_GROUP = N_EXPERTS // N_EXPERT_GROUPS
D_FF_EXPERT = 512
N_MOD = 6
MOD_ROWS = 8

LANES = 128
SUBLANES = 8
VMEM_LIMIT = 56 * 1024 * 1024

TM = 256
TQ = 256
TK = 256
TE = 256
COND_ROWS = 16
V_ROWS = HEAD_DIM + SUBLANES
LOG2E = 1.4426950408889634
Q_SCALE = HEAD_DIM ** -0.5 * LOG2E

Q_OFF, K_OFF, V_OFF = 0, ATTN_WIDTH, ATTN_WIDTH + KV_WIDTH
Z_OFF = ATTN_WIDTH + 2 * KV_WIDTH
XBC_OFF = Z_OFF + D_INNER
DT_OFF = XBC_OFF + CONV_DIM
G_OFF = DT_OFF + 2 * SSM_HEADS
N_IN = G_OFF + 2 * D_MODEL


def _cparams(sem):
    return pltpu.CompilerParams(dimension_semantics=sem, vmem_limit_bytes=VMEM_LIMIT)


def _const_spec(shape):
    nd = len(shape)
    return pl.BlockSpec(shape, lambda *_: (0,) * nd, pipeline_mode=pl.Buffered(1))


def _dot(a, b):
    return jnp.dot(a, b, preferred_element_type=F32)


def _dot_nt(a, b):
    return lax.dot_general(a, b, (((1,), (1,)), ((), ())), preferred_element_type=F32)


def _dot_tn(a, b):
    return lax.dot_general(a, b, (((0,), (0,)), ((), ())), preferred_element_type=F32)


def _split3(a):
    a1 = a.astype(BF16)
    r = a - a1.astype(F32)
    a2 = r.astype(BF16)
    a3 = (r - a2.astype(F32)).astype(BF16)
    return a1, a2, a3


def _dot_exact_lhs(m_bf16, a):
    a1, a2, a3 = _split3(a)
    return _dot(m_bf16, a1) + _dot(m_bf16, a2) + _dot(m_bf16, a3)


def _dot_exact_rhs(a, m_bf16):
    a1, a2, a3 = _split3(a)
    return _dot(a1, m_bf16) + _dot(a2, m_bf16) + _dot(a3, m_bf16)


def _modnorm(x, w, sc, sh):
    ms = jnp.mean(x * x, axis=-1, keepdims=True)
    return (x * lax.rsqrt(ms + EPS) * w) * (1.0 + sc) + sh


def _sigmoid(x):
    return 1.0 / (1.0 + jnp.exp(-x))


def _silu(x):
    return x * _sigmoid(x)


def _adaln_kernel(c_ref, w_ref, b_ref, o_ref):
    cs = _silu(c_ref[...])
    o_ref[0] = jnp.dot(cs, w_ref[0], preferred_element_type=F32,
                       precision=lax.Precision.HIGHEST) + b_ref[0]


def _adaln(cond, w_mod, b_mod):
    depth = w_mod.shape[0]
    nb = N_MOD
    return pl.pallas_call(
        _adaln_kernel,
        grid=(depth, nb),
        in_specs=[pl.BlockSpec((COND_ROWS, D_MODEL), lambda l, j: (0, 0)),
                  pl.BlockSpec((1, D_MODEL, D_MODEL), lambda l, j: (l, 0, j)),
                  pl.BlockSpec((1, 1, D_MODEL), lambda l, j: (l, 0, j))],
        out_specs=pl.BlockSpec((1, COND_ROWS, D_MODEL), lambda l, j: (l, 0, j)),
        out_shape=jax.ShapeDtypeStruct((depth, COND_ROWS, N_MOD * D_MODEL), F32),
        compiler_params=_cparams(("arbitrary", "arbitrary")),
        name="adaln",
    )(cond, w_mod, b_mod.reshape(depth, 1, N_MOD * D_MODEL))


def _in_proj_kernel(mrow_ref, rope_ref, first_ref, last_ref,
                    x_ref, xp_ref, xn_ref, mod_ref, n1w_ref,
                    wqkv_ref, wz_ref, wxbc_ref, wdt_ref, wg_ref,
                    qnw_ref, knw_ref, cos_ref, sin_ref, convw_ref, convb_ref, dtb_ref,
                    q_ref, k_ref, v_ref, z_ref, xs_ref, bc_ref, dt_ref, g_ref):
    del mrow_ref, rope_ref
    i = pl.program_id(0)
    sh1 = mod_ref[0, 0:1, :]
    sc1 = mod_ref[0, 1:2, :]
    nw = n1w_ref[...]
    h = _modnorm(x_ref[...], nw, sc1, sh1).astype(BF16)
    hp = _modnorm(xp_ref[...], nw, sc1, sh1).astype(BF16)
    hn = _modnorm(xn_ref[...], nw, sc1, sh1).astype(BF16)

    r = lax.broadcasted_iota(jnp.int32, (2 * LANES, 2 * LANES), 0) // HEAD_DIM
    c = lax.broadcasted_iota(jnp.int32, (2 * LANES, 2 * LANES), 1) // HEAD_DIM
    same_head = jnp.where(r == c, 1.0, 0.0).astype(BF16)
    lane = lax.broadcasted_iota(jnp.int32, (TM, LANES), 1)
    first_half = (lane % (HEAD_DIM // 2)) < (HEAD_DIM // 4)

    def head_norm_rope(t, w, scale):
        sq = t * t
        hi = sq.astype(BF16)
        lo = (sq - hi.astype(F32)).astype(BF16)
        ssum = _dot(hi, same_head) + _dot(lo, same_head)
        tn = t * lax.rsqrt(ssum * (1.0 / HEAD_DIM) + EPS) * w
        outs = []
        for s in range(2):
            a = tn[:, s * LANES:(s + 1) * LANES]
            rot = jnp.where(first_half, -pltpu.roll(a, LANES - HEAD_DIM // 4, 1),
                            pltpu.roll(a, HEAD_DIM // 4, 1))
            outs.append((a * cos_ref[...] + rot * sin_ref[...]) * scale)
        return outs

    qw = qnw_ref[...]
    kw = knw_ref[...]
    for cb in range(ATTN_WIDTH // (2 * LANES)):
        t = _dot(h, wqkv_ref[:, cb * 2 * LANES:(cb + 1) * 2 * LANES])
        a, b = head_norm_rope(t, qw, Q_SCALE)
        q_ref[:, cb * 2 * LANES:cb * 2 * LANES + LANES] = a.astype(BF16)
        q_ref[:, cb * 2 * LANES + LANES:(cb + 1) * 2 * LANES] = b.astype(BF16)
    t = _dot(h, wqkv_ref[:, K_OFF:K_OFF + KV_WIDTH])
    a, b = head_norm_rope(t, kw, 1.0)
    k_ref[:, 0:LANES] = a
    k_ref[:, LANES:2 * LANES] = b
    v_ref[...] = _dot(h, wqkv_ref[:, V_OFF:V_OFF + KV_WIDTH])

    cw = 512
    for cb in range(D_INNER // cw):
        z_ref[:, cb * cw:(cb + 1) * cw] = _dot(h, wz_ref[:, cb * cw:(cb + 1) * cw]).astype(BF16)
    for cb in range(2 * D_MODEL // cw):
        g_ref[:, cb * cw:(cb + 1) * cw] = _sigmoid(
            _dot(h, wg_ref[:, cb * cw:(cb + 1) * cw])).astype(BF16)

    dtr = _dot(h, wdt_ref[...]) + dtb_ref[...]
    dt_ref[...] = jnp.maximum(dtr, 0.0) + jnp.log(1.0 + jnp.exp(-jnp.abs(dtr)))

    keep_prev = jnp.where(first_ref[i] == 1, 0.0, 1.0)
    keep_next = jnp.where(last_ref[i] == 1, 0.0, 1.0)
    rows = lax.broadcasted_iota(jnp.int32, (TM, cw), 0)
    for cb in range(CONV_DIM // cw):
        sl = slice(cb * cw, (cb + 1) * cw)
        pre = _dot(h, wxbc_ref[:, sl])
        prev_row = _dot(hp, wxbc_ref[:, sl])[SUBLANES - 1:SUBLANES, :] * keep_prev
        next_row = _dot(hn, wxbc_ref[:, sl])[0:1, :] * keep_next
        up = jnp.where(rows == 0, prev_row, pltpu.roll(pre, 1, 0))
        down = jnp.where(rows == TM - 1, next_row, pltpu.roll(pre, TM - 1, 0))
        y = (convb_ref[:, sl] + up * convw_ref[0:1, sl] + pre * convw_ref[1:2, sl]
             + down * convw_ref[2:3, sl])
        y = _silu(y).astype(BF16)
        if cb < D_INNER // cw:
            xs_ref[:, sl] = y
        else:
            o = cb * cw - D_INNER
            bc_ref[:, o:o + cw] = y


def _in_proj(x, mod, meta, lw):
    t = x.shape[0]
    nblk = t // TM
    nhalo = t // SUBLANES
    per_halo = TM // SUBLANES

    def row_blk(width):
        return pl.BlockSpec((TM, width), lambda i, *_: (i, 0))

    grid_spec = pltpu.PrefetchScalarGridSpec(
        num_scalar_prefetch=4,
        grid=(nblk,),
        in_specs=[
            row_blk(D_MODEL),
            pl.BlockSpec((SUBLANES, D_MODEL),
                         lambda i, *_: (jnp.maximum(i * per_halo - 1, 0), 0)),
            pl.BlockSpec((SUBLANES, D_MODEL),
                         lambda i, *_: (jnp.minimum((i + 1) * per_halo, nhalo - 1), 0)),
            pl.BlockSpec((1, MOD_ROWS, D_MODEL), lambda i, mrow, *_: (mrow[i], 0, 0)),
            _const_spec((1, D_MODEL)),
            _const_spec((D_MODEL, ATTN_WIDTH + 2 * KV_WIDTH)),
            _const_spec((D_MODEL, D_INNER)),
            _const_spec((D_MODEL, CONV_DIM)),
            _const_spec((D_MODEL, LANES)),
            _const_spec((D_MODEL, 2 * D_MODEL)),
            _const_spec((1, 2 * LANES)),
            _const_spec((1, 2 * LANES)),
            pl.BlockSpec((TM, LANES), lambda i, mrow, rope, *_: (rope[i], 0)),
            pl.BlockSpec((TM, LANES), lambda i, mrow, rope, *_: (rope[i], 0)),
            _const_spec((3, CONV_DIM)),
            _const_spec((1, CONV_DIM)),
            _const_spec((1, LANES)),
        ],
        out_specs=[row_blk(ATTN_WIDTH), row_blk(KV_WIDTH), row_blk(KV_WIDTH), row_blk(D_INNER),
                   row_blk(D_INNER), row_blk(2 * GN), row_blk(LANES), row_blk(2 * D_MODEL)],
    )
    out_shape = [
        jax.ShapeDtypeStruct((t, ATTN_WIDTH), BF16),
        jax.ShapeDtypeStruct((t, KV_WIDTH), F32),
        jax.ShapeDtypeStruct((t, KV_WIDTH), F32),
        jax.ShapeDtypeStruct((t, D_INNER), BF16),
        jax.ShapeDtypeStruct((t, D_INNER), BF16),
        jax.ShapeDtypeStruct((t, 2 * GN), BF16),
        jax.ShapeDtypeStruct((t, LANES), F32),
        jax.ShapeDtypeStruct((t, 2 * D_MODEL), BF16),
    ]
    return pl.pallas_call(
        _in_proj_kernel, grid_spec=grid_spec, out_shape=out_shape,
        compiler_params=_cparams(("arbitrary",)), name="in_proj",
    )(meta["mrow"], meta["rope_blk"], meta["first"], meta["last"],
      x, x, x, mod, lw["norm1_w"], lw["wqkv"], lw["wz"], lw["wxbc"], lw["wdt"], lw["wg"],
      lw["q_norm_w"], lw["k_norm_w"], meta["cos"], meta["sin"],
      lw["conv_w"], lw["conv_b"], lw["dt_bias"])


def _attn_kernel(q_ref, k_ref, vt_ref, o_ref, qt_ref, m_ref, acc_ref, sa_ref, mxa_ref, sb_ref, mxb_ref,
                 *, n_kblk):
    nq = 4 * TQ
    zeros_half = jnp.zeros((HEAD_DIM, TQ), F32)
    for hh in range(2):
        cols = []
        for jj in range(2):
            off = 2 * LANES * hh + LANES * jj
            qt = q_ref[:, off:off + LANES].astype(F32).T
            for s in range(2):
                head = qt[s * HEAD_DIM:(s + 1) * HEAD_DIM, :]
                parts = [head, zeros_half] if hh == 0 else [zeros_half, head]
                cols.append(jnp.concatenate(parts, axis=0))
        qt_ref[hh] = jnp.concatenate(cols, axis=1).astype(BF16)
        m_ref[hh] = jnp.full((1, nq), -jnp.inf, F32)
        acc_ref[hh] = jnp.zeros((V_ROWS, nq), F32)

    def scores(kb, s_ref, mx_ref):
        kblk = k_ref[0, kb]
        for hh in range(2):
            s = _dot(kblk, qt_ref[hh])
            s_ref[hh] = s
            mx_ref[hh] = jnp.max(s, axis=0, keepdims=True)

    def consume(kb, s_ref, mx_ref):
        for hh in range(2):
            m = m_ref[hh]
            m_new = jnp.maximum(m, mx_ref[hh])
            alpha = jnp.exp2(m - m_new)
            p = jnp.exp2(s_ref[hh] - m_new).astype(BF16)
            acc_ref[hh] = alpha * acc_ref[hh] + _dot(vt_ref[0, hh, kb], p)
            m_ref[hh] = m_new

    scores(0, sa_ref, mxa_ref)

    def body(j, carry):
        scores(2 * j + 1, sb_ref, mxb_ref)
        consume(2 * j, sa_ref, mxa_ref)
        scores(2 * j + 2, sa_ref, mxa_ref)
        consume(2 * j + 1, sb_ref, mxb_ref)
        return carry

    lax.fori_loop(0, (n_kblk - 1) // 2, body, 0)
    if n_kblk % 2 == 1:
        consume(n_kblk - 1, sa_ref, mxa_ref)
    else:
        scores(n_kblk - 1, sb_ref, mxb_ref)
        consume(n_kblk - 2, sa_ref, mxa_ref)
        consume(n_kblk - 1, sb_ref, mxb_ref)

    for hh in range(2):
        oT = acc_ref[hh]
        oT = oT[0:HEAD_DIM, :] / oT[HEAD_DIM:HEAD_DIM + 1, :]
        for jj in range(2):
            pair = jnp.concatenate([oT[:, (2 * jj) * TQ:(2 * jj + 1) * TQ],
                                    oT[:, (2 * jj + 1) * TQ:(2 * jj + 2) * TQ]], axis=0)
            off = 2 * LANES * hh + LANES * jj
            o_ref[:, off:off + LANES] = pair.T.astype(BF16)


def _attention(q, k_blk, vt_blk, nseq, seq_len):
    n_kblk = k_blk.shape[2]
    nqb = seq_len // TQ
    nq = 4 * TQ
    return pl.pallas_call(
        functools.partial(_attn_kernel, n_kblk=n_kblk),
        grid=(nseq, 2, nqb),
        in_specs=[
            pl.BlockSpec((TQ, 4 * LANES), lambda b, p, i: (b * nqb + i, p)),
            pl.BlockSpec((1, None, n_kblk, TK, LANES), lambda b, p, i: (b, p, 0, 0, 0)),
            pl.BlockSpec((1, None, 2, n_kblk, V_ROWS, TK), lambda b, p, i: (b, p, 0, 0, 0, 0)),
        ],
        out_specs=pl.BlockSpec((TQ, 4 * LANES), lambda b, p, i: (b * nqb + i, p)),
        out_shape=jax.ShapeDtypeStruct(q.shape, BF16),
        scratch_shapes=[pltpu.VMEM((2, LANES, nq), BF16),
                        pltpu.VMEM((2, 1, nq), F32),
                        pltpu.VMEM((2, V_ROWS, nq), F32),
                        pltpu.VMEM((2, TK, nq), F32),
                        pltpu.VMEM((2, 1, nq), F32),
                        pltpu.VMEM((2, TK, nq), F32),
                        pltpu.VMEM((2, 1, nq), F32)],
        compiler_params=_cparams(("arbitrary", "arbitrary", "arbitrary")),
        name="attention",
    )(q, k_blk, vt_blk)


def _kv_blocks(k, v, nseq):
    lk = k.shape[1]
    n = lk // TK
    kb = k.astype(BF16).reshape(nseq, n, TK, 2, LANES).transpose(0, 3, 1, 2, 4)
    vb = v.astype(BF16).reshape(nseq, n, TK, 2, 2, HEAD_DIM).transpose(0, 3, 4, 1, 5, 2)
    ones = jnp.ones(vb.shape[:4] + (1, TK), BF16)
    zeros = jnp.zeros(vb.shape[:4] + (V_ROWS - HEAD_DIM - 1, TK), BF16)
    return kb, jnp.concatenate([vb, ones, zeros], axis=4)


def _ssd_direction(x_ref, bc_ref, dt_ref, a_lanes, expand, st_ref, y_ref, d):
    q = SSD_CHUNK
    row = lax.broadcasted_iota(jnp.int32, (q, q), 0)
    col = lax.broadcasted_iota(jnp.int32, (q, q), 1)
    causal = (row >= col) if d == 0 else (row <= col)
    tril = jnp.where(row >= col, 1.0, 0.0).astype(BF16)

    dt = dt_ref[...]
    a = dt * a_lanes
    prefix = _dot_exact_lhs(tril, a)
    if d == 0:
        cs = prefix
        last = cs[q - 1:q, :]
    else:
        cs = prefix[q - 1:q, :] - prefix + a
        last = cs[0:1, :]
    cs2 = cs * LOG2E
    cst2 = (cs2 - jnp.log2(dt)).T
    wj = jnp.exp(last - cs) * dt
    wj_x = _dot(wj.astype(BF16), expand)
    elast = jnp.exp(last)

    lane = lax.broadcasted_iota(jnp.int32, (q, LANES), 1)
    lo_half = lane < SSM_HEAD_DIM
    lo_half_row = lo_half[0:1, :]
    for g in range(SSM_GROUPS):
        bg = bc_ref[:, g * D_STATE:(g + 1) * D_STATE]
        cg = bc_ref[:, GN + g * D_STATE:GN + (g + 1) * D_STATE]
        gmat = _dot_nt(cg, bg)
        gsl = slice(g * HEADS_PER_GROUP * SSM_HEAD_DIM, (g + 1) * HEADS_PER_GROUP * SSM_HEAD_DIM)
        st = st_ref[d, :, gsl]
        y_inter = _dot(cg, st.astype(BF16))
        xg = x_ref[:, gsl]
        el_parts = []
        for hp in range(HEADS_PER_GROUP // 2):
            xpair = xg[:, hp * LANES:(hp + 1) * LANES]
            l0 = d * SSM_HEADS + g * HEADS_PER_GROUP + 2 * hp
            ws, es = [], []
            for s in range(2):
                hl = l0 + s
                csb = jnp.broadcast_to(cs2[:, hl:hl + 1], (q, q))
                seg = csb - cst2[hl:hl + 1, :]
                ws.append(gmat * jnp.exp2(jnp.where(causal, seg, -jnp.inf)))
                es.append(jnp.exp2(csb))
            wcat = jnp.concatenate(ws, axis=1).astype(BF16)
            zero = jnp.zeros_like(xpair)
            xm = jnp.concatenate([jnp.where(lo_half, xpair, zero), jnp.where(lo_half, zero, xpair)],
                                 axis=0)
            e_pair = jnp.where(lo_half, es[0], es[1])
            ypair = y_inter[:, hp * LANES:(hp + 1) * LANES] * e_pair + _dot(wcat, xm)
            lo = g * HEADS_PER_GROUP * SSM_HEAD_DIM + hp * LANES
            y_ref[:, lo:lo + LANES] = ypair.astype(y_ref.dtype)
            el_parts.append(jnp.where(lo_half_row, elast[:, l0:l0 + 1], elast[:, l0 + 1:l0 + 2]))
        xw = (xg.astype(F32) * wj_x[:, gsl]).astype(BF16)
        el = jnp.concatenate(el_parts, axis=1)
        st_ref[d, :, gsl] = st * el + _dot_tn(bg, xw)


def _ssd_kernel(fblk_ref, bblk_ref, first_ref, last_ref, init_ref, s0i_ref, sfi_ref,
                xf_ref, bcf_ref, dtf_ref, xb_ref, bcb_ref, dtb_ref, a_ref, exp_ref, s0_ref,
                yf_ref, yb_ref, sfin_ref, st_ref):
    del fblk_ref, bblk_ref, s0i_ref, sfi_ref
    s = pl.program_id(0)

    @pl.when(jnp.logical_and(first_ref[s] == 1, init_ref[s] == 1))
    def _():
        st_ref[...] = s0_ref[0]

    @pl.when(jnp.logical_and(first_ref[s] == 1, init_ref[s] == 0))
    def _():
        st_ref[...] = jnp.zeros(st_ref.shape, F32)

    a_lanes = a_ref[...]
    _ssd_direction(xf_ref, bcf_ref, dtf_ref, a_lanes, exp_ref[0], st_ref, yf_ref, 0)
    _ssd_direction(xb_ref, bcb_ref, dtb_ref, a_lanes, exp_ref[1], st_ref, yb_ref, 1)

    @pl.when(jnp.logical_and(last_ref[s] == 1, init_ref[s] == 0))
    def _():
        sfin_ref[0] = st_ref[...]


def _ssd(xs, bc, dt, a_lanes, expand, s0, meta, n_ctx):
    t = xs.shape[0]
    n_steps = t // SSD_CHUNK
    st_shape = (2, D_STATE, D_INNER)
    fwd = lambda s, fblk, bblk, *_: (fblk[s], 0)
    bwd = lambda s, fblk, bblk, *_: (bblk[s], 0)

    def specs(idx):
        return [pl.BlockSpec((SSD_CHUNK, D_INNER), idx),
                pl.BlockSpec((SSD_CHUNK, 2 * GN), idx),
                pl.BlockSpec((SSD_CHUNK, LANES), idx)]

    grid_spec = pltpu.PrefetchScalarGridSpec(
        num_scalar_prefetch=7,
        grid=(n_steps,),
        in_specs=specs(fwd) + specs(bwd) + [
            pl.BlockSpec((1, LANES), lambda s, *_: (0, 0)),
            pl.BlockSpec((2, LANES, D_INNER), lambda s, *_: (0, 0, 0)),
            pl.BlockSpec((1,) + st_shape, lambda s, f, b, fi, la, ini, s0i, sfi: (s0i[s], 0, 0, 0)),
        ],
        out_specs=[pl.BlockSpec((SSD_CHUNK, D_INNER), fwd), pl.BlockSpec((SSD_CHUNK, D_INNER), bwd),
                   pl.BlockSpec((1,) + st_shape, lambda s, f, b, fi, la, ini, s0i, sfi: (sfi[s], 0, 0, 0))],
        scratch_shapes=[pltpu.VMEM(st_shape, F32)],
    )
    out_shape = [jax.ShapeDtypeStruct((t, D_INNER), BF16), jax.ShapeDtypeStruct((t, D_INNER), BF16),
                 jax.ShapeDtypeStruct((n_ctx,) + st_shape, F32)]
    return pl.pallas_call(
        _ssd_kernel, grid_spec=grid_spec, out_shape=out_shape,
        compiler_params=_cparams(("arbitrary",)), name="ssd",
    )(meta["ssd_fblk"], meta["ssd_bblk"], meta["ssd_first"], meta["ssd_last"], meta["ssd_init"],
      meta["ssd_s0i"], meta["ssd_sfi"], xs, bc, dt, xs, bc, dt, a_lanes, expand, s0)


def _route(sel, scores):
    neg = -jnp.inf
    rows = [sel[e:e + 1, :] for e in range(N_EXPERTS)]
    srow = [scores[e:e + 1, :] for e in range(N_EXPERTS)]
    best_score = None
    best = None
    for g in range(N_EXPERT_GROUPS):
        v = rows[g * EXPERTS_PER_GROUP:(g + 1) * EXPERTS_PER_GROUP]
        top2 = None
        for a in range(EXPERTS_PER_GROUP):
            for b in range(a + 1, EXPERTS_PER_GROUP):
                s = v[a] + v[b]
                top2 = s if top2 is None else jnp.maximum(top2, s)
        if g == 0:
            best_score, best = top2, jnp.zeros(top2.shape, jnp.int32)
        else:
            better = top2 > best_score
            best = jnp.where(better, g, best)
            best_score = jnp.where(better, top2, best_score)

    def pick(vals):
        out = vals[0]
        for g in range(1, N_EXPERT_GROUPS):
            out = jnp.where(best == g, vals[g], out)
        return out

    gsel = [pick([rows[g * EXPERTS_PER_GROUP + j] for g in range(N_EXPERT_GROUPS)])
            for j in range(EXPERTS_PER_GROUP)]
    gsc = [pick([srow[g * EXPERTS_PER_GROUP + j] for g in range(N_EXPERT_GROUPS)])
           for j in range(EXPERTS_PER_GROUP)]

    def argmax_first(vals):
        bi = jnp.zeros(vals[0].shape, jnp.int32)
        bv = vals[0]
        for j in range(1, len(vals)):
            better = vals[j] > bv
            bi = jnp.where(better, j, bi)
            bv = jnp.where(better, vals[j], bv)
        return bi

    i1 = argmax_first(gsel)
    i2 = argmax_first([jnp.where(i1 == j, neg, gsel[j]) for j in range(EXPERTS_PER_GROUP)])

    def take(vals, idx):
        out = vals[0]
        for j in range(1, len(vals)):
            out = jnp.where(idx == j, vals[j], out)
        return out

    g1 = take(gsc, i1)
    g2 = take(gsc, i2)
    tot = g1 + g2
    idx = jnp.concatenate([best * EXPERTS_PER_GROUP + i1, best * EXPERTS_PER_GROUP + i2], axis=0)
    gate = jnp.concatenate([g1 / tot, g2 / tot], axis=0)
    return idx, gate


def _merge_kernel(mrow_ref, x_ref, attn_ref, yf_ref, yb_ref, xs_ref, z_ref, g_ref, mod_ref,
                  wa_ref, ws_ref, wo_ref, dvec_ref, snw_ref, n2w_ref, wrt_ref, rb_ref,
                  x1_ref, h2_ref, idx_ref, gate_ref):
    del mrow_ref
    gw = D_INNER // SSM_GROUPS
    ssm_o = None
    for g in range(SSM_GROUPS):
        sl = slice(g * gw, (g + 1) * gw)
        xs = xs_ref[:, sl].astype(F32)
        y = yf_ref[:, sl].astype(F32) + yb_ref[:, sl].astype(F32) + dvec_ref[:, sl] * xs
        y = y * _silu(z_ref[:, sl].astype(F32))
        ms = jnp.mean(y * y, axis=-1, keepdims=True)
        yn = (y * lax.rsqrt(ms + EPS) * snw_ref[:, sl]).astype(BF16)
        part = _dot(yn, ws_ref[sl, :])
        ssm_o = part if ssm_o is None else ssm_o + part
    attn_o = _dot(attn_ref[...], wa_ref[...])
    merged = (g_ref[:, 0:D_MODEL].astype(F32) * attn_o
              + g_ref[:, D_MODEL:2 * D_MODEL].astype(F32) * ssm_o)
    out = _dot(merged.astype(BF16), wo_ref[...])
    x1 = x_ref[...] + mod_ref[0, 2:3, :] * out
    x1_ref[...] = x1
    h2 = _modnorm(x1, n2w_ref[...], mod_ref[0, 4:5, :], mod_ref[0, 3:4, :])
    h2_ref[...] = h2.astype(BF16)
    logits = lax.dot_general(wrt_ref[...], h2, (((1,), (1,)), ((), ())),
                             preferred_element_type=F32, precision=lax.Precision.HIGHEST)
    scores = _sigmoid(logits)
    idx, gate = _route(scores + rb_ref[...], scores)
    idx_ref[...] = idx
    gate_ref[...] = gate


def _merge(x, attn, yf, yb, xs, z, g, mod, meta, lw, wrt, rb):
    t = x.shape[0]
    nblk = t // TM

    def row_blk(width):
        return pl.BlockSpec((TM, width), lambda i, *_: (i, 0))

    grid_spec = pltpu.PrefetchScalarGridSpec(
        num_scalar_prefetch=1,
        grid=(nblk,),
        in_specs=[row_blk(D_MODEL), row_blk(ATTN_WIDTH), row_blk(D_INNER), row_blk(D_INNER),
                  row_blk(D_INNER), row_blk(D_INNER), row_blk(2 * D_MODEL),
                  pl.BlockSpec((1, MOD_ROWS, D_MODEL), lambda i, mrow: (mrow[i], 0, 0)),
                  _const_spec((ATTN_WIDTH, D_MODEL)), _const_spec((D_INNER, D_MODEL)),
                  _const_spec((D_MODEL, D_MODEL)), _const_spec((1, D_INNER)),
                  _const_spec((1, D_INNER)), _const_spec((1, D_MODEL)),
                  _const_spec((N_EXPERTS, D_MODEL)), _const_spec((N_EXPERTS, 1))],
        out_specs=[row_blk(D_MODEL), row_blk(D_MODEL),
                   pl.BlockSpec((2, TM), lambda i, *_: (0, i)),
                   pl.BlockSpec((2, TM), lambda i, *_: (0, i))],
    )
    out_shape = [jax.ShapeDtypeStruct((t, D_MODEL), F32), jax.ShapeDtypeStruct((t, D_MODEL), BF16),
                 jax.ShapeDtypeStruct((2, t), jnp.int32), jax.ShapeDtypeStruct((2, t), F32)]
    return pl.pallas_call(
        _merge_kernel, grid_spec=grid_spec, out_shape=out_shape,
        compiler_params=_cparams(("arbitrary",)), name="merge",
    )(meta["mrow"], x, attn, yf, yb, xs, z, g, mod, lw["w_attn_out"], lw["w_ssm_out"], lw["w_out"],
      lw["ssm_d"], lw["ssm_norm_w"], lw["norm2_w"], wrt, rb)


def _expert_kernel(te_ref, nt_ref, x_ref, wg_ref, wu_ref, wd_ref, o_ref):
    del te_ref
    i = pl.program_id(0)

    @pl.when(i < nt_ref[0])
    def _():
        x = x_ref[...]
        hmid = _silu(_dot(x, wg_ref[0])) * _dot(x, wu_ref[0])
        o_ref[...] = _dot(hmid.astype(BF16), wd_ref[0])

    @pl.when(i >= nt_ref[0])
    def _():
        o_ref[...] = jnp.zeros(o_ref.shape, o_ref.dtype)


def _experts(x_sorted, tile_expert, n_tiles, wg, wu, wd):
    rows = x_sorted.shape[0]
    grid_spec = pltpu.PrefetchScalarGridSpec(
        num_scalar_prefetch=2,
        grid=(rows // TE,),
        in_specs=[pl.BlockSpec((TE, D_MODEL), lambda i, te, nt: (i, 0)),
                  pl.BlockSpec((1, D_MODEL, D_FF_EXPERT), lambda i, te, nt: (te[i], 0, 0)),
                  pl.BlockSpec((1, D_MODEL, D_FF_EXPERT), lambda i, te, nt: (te[i], 0, 0)),
                  pl.BlockSpec((1, D_FF_EXPERT, D_MODEL), lambda i, te, nt: (te[i], 0, 0))],
        out_specs=pl.BlockSpec((TE, D_MODEL), lambda i, te, nt: (i, 0)),
    )
    return pl.pallas_call(
        _expert_kernel, grid_spec=grid_spec,
        out_shape=jax.ShapeDtypeStruct((rows, D_MODEL), F32),
        compiler_params=_cparams(("arbitrary",)), name="experts",
    )(tile_expert, n_tiles, x_sorted, wg, wu, wd)


def _combine_kernel(mrow_ref, x1_ref, y0_ref, y1_ref, gate_ref, mod_ref, fnw_ref, o_ref, *, final):
    del mrow_ref
    moe = gate_ref[:, 0:1] * y0_ref[...] + gate_ref[:, 1:2] * y1_ref[...]
    x2 = x1_ref[...] + mod_ref[0, 5:6, :] * moe
    if final:
        ms = jnp.mean(x2 * x2, axis=-1, keepdims=True)
        x2 = x2 * lax.rsqrt(ms + EPS) * fnw_ref[...]
    o_ref[...] = x2


def _combine(x1, y0, y1, gate_t, mod, meta, fnw, final):
    t = x1.shape[0]

    def row_blk(width):
        return pl.BlockSpec((TM, width), lambda i, *_: (i, 0))

    grid_spec = pltpu.PrefetchScalarGridSpec(
        num_scalar_prefetch=1,
        grid=(t // TM,),
        in_specs=[row_blk(D_MODEL), row_blk(D_MODEL), row_blk(D_MODEL), row_blk(2),
                  pl.BlockSpec((1, MOD_ROWS, D_MODEL), lambda i, mrow: (mrow[i], 0, 0)),
                  _const_spec((1, D_MODEL))],
        out_specs=row_blk(D_MODEL),
    )
    return pl.pallas_call(
        functools.partial(_combine_kernel, final=final), grid_spec=grid_spec,
        out_shape=jax.ShapeDtypeStruct((t, D_MODEL), F32),
        compiler_params=_cparams(("arbitrary",)), name="combine",
    )(meta["mrow"], x1, y0, y1, gate_t, mod, fnw)


def _moe(h2, idx, gate, lw):
    t = h2.shape[0]
    n_assign = 2 * t
    rows = n_assign + N_EXPERTS * TE
    e_flat = idx.reshape(n_assign)
    onehot = (e_flat[:, None] == jnp.arange(N_EXPERTS, dtype=jnp.int32)[None, :]).astype(jnp.int32)
    counts = jnp.sum(onehot, axis=0)
    padded = ((counts + TE - 1) // TE) * TE
    pad_end = jnp.cumsum(padded)
    pad_off = pad_end - padded
    off = jnp.cumsum(counts) - counts
    rank = jnp.take_along_axis(jnp.cumsum(onehot, axis=0), e_flat[:, None], axis=1)[:, 0] - 1
    pos = pad_off[e_flat] + rank
    order = jnp.argsort(e_flat, stable=True).astype(jnp.int32)
    p = jnp.arange(rows, dtype=jnp.int32)
    ep = jnp.minimum(jnp.searchsorted(pad_end, p, side="right"), N_EXPERTS - 1).astype(jnp.int32)
    r = p - pad_off[ep]
    valid = r < counts[ep]
    src = jnp.where(valid, order[jnp.clip(off[ep] + r, 0, n_assign - 1)] % t, 0)
    n_tiles = (pad_end[-1] // TE).astype(jnp.int32).reshape(1)
    tile_start = jnp.arange(rows // TE, dtype=jnp.int32) * TE
    tile_expert = jnp.minimum(jnp.searchsorted(pad_end, tile_start, side="right"),
                              N_EXPERTS - 1).astype(jnp.int32)
    last_used = tile_expert[jnp.maximum(n_tiles[0] - 1, 0)]
    tile_expert = jnp.where(tile_start < pad_end[-1], tile_expert, last_used)
    x_sorted = jnp.take(h2, src, axis=0)
    y_sorted = _experts(x_sorted, tile_expert, n_tiles, lw["w_exp_gate"], lw["w_exp_up"],
                        lw["w_exp_down"])
    pos2 = pos.reshape(2, t)
    return jnp.take(y_sorted, pos2[0], axis=0), jnp.take(y_sorted, pos2[1], axis=0)


def _rope_tables(n_tokens):
    rows = n_tokens // GRID_W
    row = jnp.repeat(jnp.arange(rows, dtype=F32), GRID_W)
    col = jnp.tile(jnp.arange(GRID_W, dtype=F32), rows)
    inv = 1.0 / (ROPE_THETA ** (jnp.arange(0, ROPE_AXIS_DIM, 2, dtype=F32) / ROPE_AXIS_DIM))
    ar = row[:, None] * inv
    ac = col[:, None] * inv
    ang = jnp.concatenate([ar, ar, ac, ac], axis=-1)
    cos, sin = jnp.cos(ang), jnp.sin(ang)
    cos = jnp.concatenate([jnp.ones((TM, HEAD_DIM), F32), cos], axis=0)
    sin = jnp.concatenate([jnp.zeros((TM, HEAD_DIM), F32), sin], axis=0)
    return jnp.tile(cos, (1, 2)), jnp.tile(sin, (1, 2))


def _block_meta(n_ctx, ctx_len, n_lat, lat_len):
    ctx_blocks = n_ctx * ctx_len // TM
    lat_blocks = n_lat * lat_len // TM
    per_ctx = ctx_len // TM
    per_lat = lat_len // TM
    bi = jnp.arange(ctx_blocks + lat_blocks, dtype=jnp.int32)
    is_ctx = bi < ctx_blocks
    lat_i = bi - ctx_blocks
    mrow = jnp.where(is_ctx, 0, 1 + lat_i // per_lat).astype(jnp.int32)
    rope_blk = jnp.where(is_ctx, 0, 1 + lat_i % per_lat).astype(jnp.int32)
    first = jnp.where(is_ctx, bi % per_ctx == 0, lat_i % per_lat == 0).astype(jnp.int32)
    last = jnp.where(is_ctx, bi % per_ctx == per_ctx - 1, lat_i % per_lat == per_lat - 1).astype(jnp.int32)
    cos, sin = _rope_tables(lat_len)
    meta = {"mrow": mrow, "rope_blk": rope_blk, "first": first, "last": last, "cos": cos, "sin": sin}

    nc_ctx, nc_lat = ctx_len // SSD_CHUNK, lat_len // SSD_CHUNK
    n_ctx_steps = n_ctx * nc_ctx
    si = jnp.arange(n_ctx_steps + n_lat * nc_lat, dtype=jnp.int32)
    s_ctx = si < n_ctx_steps
    li = si - n_ctx_steps
    seq = jnp.where(s_ctx, si // nc_ctx, li // nc_lat)
    ch = jnp.where(s_ctx, si % nc_ctx, li % nc_lat)
    nc = jnp.where(s_ctx, nc_ctx, nc_lat)
    base = jnp.where(s_ctx, seq * nc_ctx, n_ctx_steps + seq * nc_lat)
    meta.update({
        "ssd_fblk": (base + ch).astype(jnp.int32),
        "ssd_bblk": (base + nc - 1 - ch).astype(jnp.int32),
        "ssd_first": (ch == 0).astype(jnp.int32),
        "ssd_last": (ch == nc - 1).astype(jnp.int32),
        "ssd_init": jnp.where(s_ctx, 0, 1).astype(jnp.int32),
        "ssd_s0i": jnp.where(s_ctx, 0, seq).astype(jnp.int32),
        "ssd_sfi": jnp.where(s_ctx, seq, n_ctx - 1).astype(jnp.int32),
    })
    return meta


def kernel(x_prompt, x_sample, cache_k, cache_v, state_ssm, c, c_ctx, norm1_w, norm2_w, w_mod, b_mod,
           w_in, q_norm_w, k_norm_w, conv_w, conv_b, a_log, dt_bias, ssm_d, ssm_norm_w, w_attn_out,
           w_ssm_out, w_out, w_router, router_bias, w_exp_gate, w_exp_up, w_exp_down, final_norm_w):
    n_ctx, ctx_len, _ = x_prompt.shape
    n_lat, lat_len, _ = x_sample.shape
    depth = w_in.shape[0]
    t_ctx = n_ctx * ctx_len
    t_lat = n_lat * lat_len
    assert ctx_len % TM == 0 and lat_len % TM == 0 and 1 + n_lat <= COND_ROWS
    assert cache_k.shape[2] % TK == 0 and lat_len % GRID_W == 0

    meta = _block_meta(n_ctx, ctx_len, n_lat, lat_len)
    x = jnp.concatenate([x_prompt.reshape(t_ctx, D_MODEL), x_sample.reshape(t_lat, D_MODEL)], axis=0)

    cond = jnp.zeros((COND_ROWS, D_MODEL), F32).at[0].set(c_ctx).at[1:1 + n_lat].set(c)
    mod_all = _adaln(cond, w_mod, b_mod).reshape(depth, COND_ROWS, N_MOD, D_MODEL)
    mod_all = jnp.pad(mod_all, ((0, 0), (0, 0), (0, MOD_ROWS - N_MOD), (0, 0)))

    head_of_lane = jnp.arange(D_INNER, dtype=jnp.int32) // SSM_HEAD_DIM
    lane_id = jnp.arange(LANES, dtype=jnp.int32)
    expand = jnp.stack([(lane_id[:, None] == head_of_lane[None, :] + d * SSM_HEADS) for d in range(2)]
                       ).astype(BF16)
    wrt = w_router.T
    rb = router_bias.reshape(N_EXPERTS, 1)
    fnw = final_norm_w.reshape(1, D_MODEL)

    new_k, new_v, new_s = [], [], []
    for l in range(depth):
        wl = w_in[l]
        lw = {
            "norm1_w": norm1_w[l].reshape(1, D_MODEL),
            "norm2_w": norm2_w[l].reshape(1, D_MODEL),
            "wqkv": wl[:, :Z_OFF].astype(BF16),
            "wz": wl[:, Z_OFF:XBC_OFF].astype(BF16),
            "wxbc": wl[:, XBC_OFF:DT_OFF].astype(BF16),
            "wdt": jnp.pad(wl[:, DT_OFF:G_OFF], ((0, 0), (0, LANES - 2 * SSM_HEADS))).astype(BF16),
            "wg": wl[:, G_OFF:].astype(BF16),
            "q_norm_w": jnp.tile(q_norm_w[l], 4).reshape(1, 2 * LANES),
            "k_norm_w": jnp.tile(k_norm_w[l], 4).reshape(1, 2 * LANES),
            "conv_w": conv_w[l],
            "conv_b": conv_b[l].reshape(1, CONV_DIM),
            "dt_bias": jnp.pad(dt_bias[l].reshape(1, 2 * SSM_HEADS), ((0, 0), (0, LANES - 2 * SSM_HEADS))),
            "ssm_d": jnp.repeat(ssm_d[l], SSM_HEAD_DIM).reshape(1, D_INNER),
            "ssm_norm_w": ssm_norm_w[l].reshape(1, D_INNER),
            "w_attn_out": w_attn_out[l].astype(BF16),
            "w_ssm_out": w_ssm_out[l].astype(BF16),
            "w_out": w_out[l].astype(BF16),
            "w_exp_gate": w_exp_gate[l].astype(BF16),
            "w_exp_up": w_exp_up[l].astype(BF16),
            "w_exp_down": w_exp_down[l].astype(BF16),
        }
        mod = mod_all[l]
        q, k, v, z, xs, bc, dt, g = _in_proj(x, mod, meta, lw)

        k_ctx = k[:t_ctx].reshape(n_ctx, ctx_len, KV_WIDTH)
        v_ctx = v[:t_ctx].reshape(n_ctx, ctx_len, KV_WIDTH)
        kb, vb = _kv_blocks(k_ctx, v_ctx, n_ctx)
        attn_ctx = _attention(q[:t_ctx], kb, vb, n_ctx, ctx_len)
        past = cache_k.shape[2]
        k_lat = jnp.concatenate([cache_k[:, l].reshape(n_lat, past, KV_WIDTH),
                                 k[t_ctx:].reshape(n_lat, lat_len, KV_WIDTH)], axis=1)
        v_lat = jnp.concatenate([cache_v[:, l].reshape(n_lat, past, KV_WIDTH),
                                 v[t_ctx:].reshape(n_lat, lat_len, KV_WIDTH)], axis=1)
        kb, vb = _kv_blocks(k_lat, v_lat, n_lat)
        attn_lat = _attention(q[t_ctx:], kb, vb, n_lat, lat_len)
        attn = jnp.concatenate([attn_ctx, attn_lat], axis=0)

        a_lanes = jnp.pad(-jnp.exp(a_log[l]).reshape(1, 2 * SSM_HEADS),
                          ((0, 0), (0, LANES - 2 * SSM_HEADS)))
        s0 = state_ssm[:, l].transpose(0, 1, 4, 2, 3).reshape(n_lat, 2, D_STATE, D_INNER)
        yf, yb, s_ctx = _ssd(xs, bc, dt, a_lanes, expand, s0, meta, n_ctx)

        x1, h2, idx, gate = _merge(x, attn, yf, yb, xs, z, g, mod, meta, lw, wrt, rb)
        y0, y1 = _moe(h2, idx, gate, lw)
        x = _combine(x1, y0, y1, gate.T, mod, meta, fnw, l == depth - 1)

        new_k.append(k_ctx.reshape(n_ctx, ctx_len, N_KV_HEADS, HEAD_DIM))
        new_v.append(v_ctx.reshape(n_ctx, ctx_len, N_KV_HEADS, HEAD_DIM))
        new_s.append(s_ctx.reshape(n_ctx, 2, D_STATE, SSM_HEADS, SSM_HEAD_DIM).transpose(0, 1, 3, 4, 2))

    y_prompt = x[:t_ctx].reshape(n_ctx, ctx_len, D_MODEL)
    y_sample = x[t_ctx:].reshape(n_lat, lat_len, D_MODEL)
    return (y_prompt, y_sample, jnp.stack(new_k, axis=1), jnp.stack(new_v, axis=1),
            jnp.stack(new_s, axis=1))
```

```python
import functools

import jax
import jax.numpy as jnp
from jax import lax
from jax.experimental import pallas as pl
from jax.experimental.pallas import tpu as pltpu

F32 = jnp.float32
BF16 = jnp.bfloat16

EPS = 1e-6
D_MODEL = 1024
N_HEADS = 16
N_KV_HEADS = 4
HEAD_DIM = 64
ATTN_WIDTH = N_HEADS * HEAD_DIM
KV_WIDTH = N_KV_HEADS * HEAD_DIM
GRID_W = 64
ROPE_AXIS_DIM = HEAD_DIM // 2
ROPE_THETA = 10000.0
D_INNER = 2 * D_MODEL
SSM_HEAD_DIM = 64
SSM_HEADS = D_INNER // SSM_HEAD_DIM
SSM_GROUPS = 4
HEADS_PER_GROUP = SSM_HEADS // SSM_GROUPS
D_STATE = 128
GN = SSM_GROUPS * D_STATE
CONV_DIM = D_INNER + 2 * GN
SSD_CHUNK = 128
N_EXPERTS = 16
N_EXPERT_GROUPS = 4
EXPERTS_PER_GROUP = N_EXPERTS // N_EXPERT_GROUPS
D_FF_EXPERT = 512
N_MOD = 6
MOD_ROWS = 8

LANES = 128
SUBLANES = 8
VMEM_LIMIT = 56 * 1024 * 1024

TM = 512
TQ = 512
TK = 256
TE = 256
CONV_CHUNK = 1024
COND_ROWS = 16
V_ROWS = HEAD_DIM + SUBLANES
LOG2E = 1.4426950408889634
Q_SCALE = HEAD_DIM ** -0.5 * LOG2E

Q_OFF, K_OFF, V_OFF = 0, ATTN_WIDTH, ATTN_WIDTH + KV_WIDTH
Z_OFF = ATTN_WIDTH + 2 * KV_WIDTH
XBC_OFF = Z_OFF + D_INNER
DT_OFF = XBC_OFF + CONV_DIM
G_OFF = DT_OFF + 2 * SSM_HEADS
N_IN = G_OFF + 2 * D_MODEL


def _cparams(sem):
    return pltpu.CompilerParams(dimension_semantics=sem, vmem_limit_bytes=VMEM_LIMIT)


def _const_spec(shape):
    nd = len(shape)
    return pl.BlockSpec(shape, lambda *_: (0,) * nd, pipeline_mode=pl.Buffered(1))


def _dot(a, b):
    return jnp.dot(a, b, preferred_element_type=F32)


def _dot_nt(a, b):
    return lax.dot_general(a, b, (((1,), (1,)), ((), ())), preferred_element_type=F32)


def _dot_tn(a, b):
    return lax.dot_general(a, b, (((0,), (0,)), ((), ())), preferred_element_type=F32)


def _split3(a):
    a1 = a.astype(BF16)
    r = a - a1.astype(F32)
    a2 = r.astype(BF16)
    a3 = (r - a2.astype(F32)).astype(BF16)
    return a1, a2, a3


def _dot_exact_lhs(m_bf16, a):
    a1, a2, a3 = _split3(a)
    return _dot(m_bf16, a1) + _dot(m_bf16, a2) + _dot(m_bf16, a3)


def _modnorm(x, w, sc, sh):
    ms = jnp.mean(x * x, axis=-1, keepdims=True)
    return (x * lax.rsqrt(ms + EPS) * w) * (1.0 + sc) + sh


def _sigmoid(x):
    return 1.0 / (1.0 + jnp.exp(-x))


def _silu(x):
    return x * _sigmoid(x)


def _adaln_kernel(c_ref, w_ref, b_ref, o_ref):
    cs = _silu(c_ref[...])
    o_ref[0] = jnp.dot(cs, w_ref[0], preferred_element_type=F32,
                       precision=lax.Precision.HIGHEST) + b_ref[0]


def _adaln(cond, w_mod, b_mod):
    depth = w_mod.shape[0]
    nb = N_MOD
    return pl.pallas_call(
        _adaln_kernel,
        grid=(depth, nb),
        in_specs=[pl.BlockSpec((COND_ROWS, D_MODEL), lambda l, j: (0, 0)),
                  pl.BlockSpec((1, D_MODEL, D_MODEL), lambda l, j: (l, 0, j)),
                  pl.BlockSpec((1, 1, D_MODEL), lambda l, j: (l, 0, j))],
        out_specs=pl.BlockSpec((1, COND_ROWS, D_MODEL), lambda l, j: (l, 0, j)),
        out_shape=jax.ShapeDtypeStruct((depth, COND_ROWS, N_MOD * D_MODEL), F32),
        compiler_params=_cparams(("arbitrary", "arbitrary")),
        name="adaln",
    )(cond, w_mod, b_mod.reshape(depth, 1, N_MOD * D_MODEL))


def _in_proj_kernel(mrow_ref, rope_ref, ctxi_ref,
                    x_ref, xp_ref, xn_ref, mod_ref, n1w_ref,
                    wqkv_ref, wz_ref, wxbc_ref, wdt_ref, wg_ref,
                    qnw_ref, knw_ref, cos_ref, sin_ref, kprev_ref, knext_ref,
                    convw_ref, convb_ref, dtb_ref,
                    q_ref, kb_ref, vt_ref, kc_ref, vc_ref, z_ref, xs_ref, bc_ref, dt_ref, g_ref,
                    *, n_ctx_blocks):
    del mrow_ref, rope_ref, ctxi_ref
    i = pl.program_id(0)
    is_ctx = i < n_ctx_blocks
    sh1 = mod_ref[0, 0:1, :]
    sc1 = mod_ref[0, 1:2, :]
    nw = n1w_ref[...]
    h = _modnorm(x_ref[...], nw, sc1, sh1).astype(BF16)
    hp = _modnorm(xp_ref[...], nw, sc1, sh1).astype(BF16)
    hn = _modnorm(xn_ref[...], nw, sc1, sh1).astype(BF16)

    r = lax.broadcasted_iota(jnp.int32, (2 * LANES, 2 * LANES), 0) // HEAD_DIM
    c = lax.broadcasted_iota(jnp.int32, (2 * LANES, 2 * LANES), 1) // HEAD_DIM
    same_head = jnp.where(r == c, 1.0, 0.0).astype(BF16)
    lane = lax.broadcasted_iota(jnp.int32, (TM, LANES), 1)
    first_half = (lane % (HEAD_DIM // 2)) < (HEAD_DIM // 4)

    def head_norm_rope(t, w, scale):
        sq = t * t
        hi = sq.astype(BF16)
        lo = (sq - hi.astype(F32)).astype(BF16)
        ssum = _dot(hi, same_head) + _dot(lo, same_head)
        tn = t * lax.rsqrt(ssum * (1.0 / HEAD_DIM) + EPS) * w
        outs = []
        for s in range(2):
            a = tn[:, s * LANES:(s + 1) * LANES]
            rot = jnp.where(first_half, -pltpu.roll(a, LANES - HEAD_DIM // 4, 1),
                            pltpu.roll(a, HEAD_DIM // 4, 1))
            outs.append((a * cos_ref[...] + rot * sin_ref[...]) * scale)
        return outs

    qw = qnw_ref[...]
    kw = knw_ref[...]
    ones_rows = jnp.where(lax.broadcasted_iota(jnp.int32, (V_ROWS - HEAD_DIM, TK), 0) == 0,
                          1.0, 0.0).astype(BF16)
    cw = CONV_CHUNK
    rows = lax.broadcasted_iota(jnp.int32, (TM, cw), 0)
    kprev = jnp.concatenate([kprev_ref[...]] * (cw // LANES), axis=1)
    knext = jnp.concatenate([knext_ref[...]] * (cw // LANES), axis=1)

    def q_stage(cb):
        def mm():
            return _dot(h, wqkv_ref[:, cb * 2 * LANES:(cb + 1) * 2 * LANES])

        def fin(t):
            a, b = head_norm_rope(t, qw, Q_SCALE)
            q_ref[:, cb * 2 * LANES:cb * 2 * LANES + LANES] = a.astype(BF16)
            q_ref[:, cb * 2 * LANES + LANES:(cb + 1) * 2 * LANES] = b.astype(BF16)
        return mm, fin

    def k_fin(t):
        kpairs = head_norm_rope(t, kw, 1.0)
        for rb in range(TM // TK):
            for pr in range(2):
                kb_ref[pr, rb] = kpairs[pr][rb * TK:(rb + 1) * TK, :].astype(BF16)

        @pl.when(is_ctx)
        def _():
            kc_ref[:, 0:LANES] = kpairs[0]
            kc_ref[:, LANES:2 * LANES] = kpairs[1]

    def v_fin(v):
        for rb in range(TM // TK):
            vT = v[rb * TK:(rb + 1) * TK, :].T
            for hd in range(N_KV_HEADS):
                vt_ref[hd // 2, hd % 2, rb, 0:HEAD_DIM, :] = (
                    vT[hd * HEAD_DIM:(hd + 1) * HEAD_DIM, :].astype(BF16))
                vt_ref[hd // 2, hd % 2, rb, HEAD_DIM:V_ROWS, :] = ones_rows

        @pl.when(is_ctx)
        def _():
            vc_ref[...] = v

    def z_fin(t):
        z_ref[...] = t.astype(BF16)

    def g_fin(t):
        g_ref[...] = _sigmoid(t).astype(BF16)

    def dt_fin(t):
        dtr = t + dtb_ref[...]
        dt_ref[...] = jnp.maximum(dtr, 0.0) + jnp.log(1.0 + jnp.exp(-jnp.abs(dtr)))

    def conv_stage(cb):
        sl = slice(cb * cw, (cb + 1) * cw)

        def mm():
            return (_dot(h, wxbc_ref[:, sl]), _dot(hp, wxbc_ref[:, sl]), _dot(hn, wxbc_ref[:, sl]))

        def fin(res):
            pre, pp, pn = res
            prev_row = pp[SUBLANES - 1:SUBLANES, :]
            next_row = pn[0:1, :]
            up = jnp.where(rows == 0, prev_row, pltpu.roll(pre, 1, 0)) * kprev
            down = jnp.where(rows == TM - 1, next_row, pltpu.roll(pre, TM - 1, 0)) * knext
            y = (convb_ref[:, sl] + up * convw_ref[0:1, sl] + pre * convw_ref[1:2, sl]
                 + down * convw_ref[2:3, sl])
            y = _silu(y).astype(BF16)
            lo = cb * cw
            if lo < D_INNER:
                xs_ref[:, lo:lo + cw] = y
            else:
                bc_ref[:, lo - D_INNER:lo - D_INNER + cw] = y
        return mm, fin

    stages = [q_stage(cb) for cb in range(ATTN_WIDTH // (2 * LANES))]
    stages.append((lambda: _dot(h, wqkv_ref[:, K_OFF:K_OFF + KV_WIDTH]), k_fin))
    stages.append((lambda: _dot(h, wqkv_ref[:, V_OFF:V_OFF + KV_WIDTH]), v_fin))
    stages.append((lambda: _dot(h, wdt_ref[...]), dt_fin))
    stages += [conv_stage(cb) for cb in range(CONV_DIM // cw)]
    stages.append((lambda: _dot(h, wz_ref[...]), z_fin))
    stages.append((lambda: _dot(h, wg_ref[...]), g_fin))
    pending = None
    for mm, fin in stages:
        res = mm()
        if pending is not None:
            pending[0](pending[1])
        pending = (fin, res)
    pending[0](pending[1])


def _in_proj(x, mod, meta, lw, t_ctx):
    t = x.shape[0]
    nblk = t // TM
    nhalo = t // SUBLANES
    per_halo = TM // SUBLANES
    n_ctx_blocks = t_ctx // TM
    nkb = TM // TK

    def row_blk(width):
        return pl.BlockSpec((TM, width), lambda i, *_: (i, 0))

    ctx_blk = pl.BlockSpec((TM, KV_WIDTH), lambda i, mrow, rope, ctxi: (ctxi[i], 0))
    grid_spec = pltpu.PrefetchScalarGridSpec(
        num_scalar_prefetch=3,
        grid=(nblk,),
        in_specs=[
            row_blk(D_MODEL),
            pl.BlockSpec((SUBLANES, D_MODEL),
                         lambda i, *_: (jnp.maximum(i * per_halo - 1, 0), 0)),
            pl.BlockSpec((SUBLANES, D_MODEL),
                         lambda i, *_: (jnp.minimum((i + 1) * per_halo, nhalo - 1), 0)),
            pl.BlockSpec((1, MOD_ROWS, D_MODEL), lambda i, mrow, *_: (mrow[i], 0, 0)),
            _const_spec((1, D_MODEL)),
            _const_spec((D_MODEL, ATTN_WIDTH + 2 * KV_WIDTH)),
            _const_spec((D_MODEL, D_INNER)),
            _const_spec((D_MODEL, CONV_DIM)),
            _const_spec((D_MODEL, LANES)),
            _const_spec((D_MODEL, 2 * D_MODEL)),
            _const_spec((1, 2 * LANES)),
            _const_spec((1, 2 * LANES)),
            pl.BlockSpec((TM, LANES), lambda i, mrow, rope, *_: (rope[i], 0)),
            pl.BlockSpec((TM, LANES), lambda i, mrow, rope, *_: (rope[i], 0)),
            row_blk(LANES),
            row_blk(LANES),
            _const_spec((3, CONV_DIM)),
            _const_spec((1, CONV_DIM)),
            _const_spec((1, LANES)),
        ],
        out_specs=[row_blk(ATTN_WIDTH),
                   pl.BlockSpec((2, nkb, TK, LANES), lambda i, *_: (0, i, 0, 0)),
                   pl.BlockSpec((2, 2, nkb, V_ROWS, TK), lambda i, *_: (0, 0, i, 0, 0)),
                   ctx_blk, ctx_blk,
                   row_blk(D_INNER), row_blk(D_INNER), row_blk(2 * GN), row_blk(LANES),
                   row_blk(2 * D_MODEL)],
    )
    out_shape = [
        jax.ShapeDtypeStruct((t, ATTN_WIDTH), BF16),
        jax.ShapeDtypeStruct((2, t // TK, TK, LANES), BF16),
        jax.ShapeDtypeStruct((2, 2, t // TK, V_ROWS, TK), BF16),
        jax.ShapeDtypeStruct((t_ctx, KV_WIDTH), F32),
        jax.ShapeDtypeStruct((t_ctx, KV_WIDTH), F32),
        jax.ShapeDtypeStruct((t, D_INNER), BF16),
        jax.ShapeDtypeStruct((t, D_INNER), BF16),
        jax.ShapeDtypeStruct((t, 2 * GN), BF16),
        jax.ShapeDtypeStruct((t, LANES), F32),
        jax.ShapeDtypeStruct((t, 2 * D_MODEL), BF16),
    ]
    return pl.pallas_call(
        functools.partial(_in_proj_kernel, n_ctx_blocks=n_ctx_blocks),
        grid_spec=grid_spec, out_shape=out_shape,
        compiler_params=_cparams(("arbitrary",)), name="in_proj",
    )(meta["mrow"], meta["rope_blk"], meta["ctx_blk"],
      x, x, x, mod, lw["norm1_w"], lw["wqkv"], lw["wz"], lw["wxbc"], lw["wdt"], lw["wg"],
      lw["q_norm_w"], lw["k_norm_w"], meta["cos"], meta["sin"], meta["keep_prev"], meta["keep_next"],
      lw["conv_w"], lw["conv_b"], lw["dt_bias"])


def _attn_kernel(*refs, n_new, n_cache, tq):
    q_ref, kn_ref, vn_ref = refs[0:3]
    if n_cache:
        kc_ref, vc_ref = refs[3:5]
    o_ref, qt_ref, m_ref, acc_ref, sa_ref, mxa_ref, sb_ref, mxb_ref = refs[-8:]
    nq = 4 * tq
    zeros_half = jnp.zeros((HEAD_DIM, tq), F32)
    for hh in range(2):
        cols = []
        for jj in range(2):
            off = 2 * LANES * hh + LANES * jj
            qt = q_ref[:, off:off + LANES].astype(F32).T
            for s in range(2):
                head = qt[s * HEAD_DIM:(s + 1) * HEAD_DIM, :]
                parts = [head, zeros_half] if hh == 0 else [zeros_half, head]
                cols.append(jnp.concatenate(parts, axis=0))
        qt_ref[hh] = jnp.concatenate(cols, axis=1).astype(BF16)
        m_ref[hh] = jnp.full((1, nq), -jnp.inf, F32)
        acc_ref[hh] = jnp.zeros((V_ROWS, nq), F32)

    def scores(kblk, s_ref, mx_ref):
        for hh in range(2):
            s = _dot(kblk, qt_ref[hh])
            s_ref[hh] = s
            mx_ref[hh] = jnp.max(s, axis=0, keepdims=True)

    def consume(vt_of, s_ref, mx_ref):
        for hh in range(2):
            m = m_ref[hh]
            m_new = jnp.maximum(m, mx_ref[hh])
            alpha = jnp.exp2(m - m_new)
            p = jnp.exp2(s_ref[hh] - m_new).astype(BF16)
            acc_ref[hh] = alpha * acc_ref[hh] + _dot(vt_of(hh), p)
            m_ref[hh] = m_new

    bufs = ((sa_ref, mxa_ref), (sb_ref, mxb_ref))
    scores(kn_ref[0], *bufs[0])

    def body(j, carry):
        scores(kn_ref[2 * j + 1], *bufs[1])
        consume(lambda hh: vn_ref[hh, 2 * j], *bufs[0])
        scores(kn_ref[2 * j + 2], *bufs[0])
        consume(lambda hh: vn_ref[hh, 2 * j + 1], *bufs[1])
        return carry

    n_loop = (n_new - 1) // 2
    lax.fori_loop(0, n_loop, body, 0)
    rest = [(kn_ref, vn_ref, b) for b in range(2 * n_loop, n_new)]
    if n_cache:
        rest += [(kc_ref, vc_ref, b) for b in range(n_cache)]
    for r, (_, v_src, b) in enumerate(rest):
        if r + 1 < len(rest):
            k_nxt, _, b_nxt = rest[r + 1]
            scores(k_nxt[b_nxt], *bufs[(r + 1) % 2])
        consume(lambda hh, v_src=v_src, b=b: v_src[hh, b], *bufs[r % 2])

    for hh in range(2):
        oT = acc_ref[hh]
        oT = oT[0:HEAD_DIM, :] / oT[HEAD_DIM:HEAD_DIM + 1, :]
        for jj in range(2):
            pair = jnp.concatenate([oT[:, (2 * jj) * tq:(2 * jj + 1) * tq],
                                    oT[:, (2 * jj + 1) * tq:(2 * jj + 2) * tq]], axis=0)
            off = 2 * LANES * hh + LANES * jj
            o_ref[:, off:off + LANES] = pair.T.astype(BF16)


def _attention(q, kn, vn, cache, prev_out, nseq, seq_len, row0, tq):
    n_new = seq_len // TK
    nqb = seq_len // tq
    nq = 4 * tq
    q0 = row0 // tq
    kv0 = row0 // (n_new * TK)
    assert row0 % tq == 0 and row0 % (n_new * TK) == 0
    qspec = pl.BlockSpec((tq, 4 * LANES), lambda b, p, i: (q0 + b * nqb + i, p))
    in_specs = [
        qspec,
        pl.BlockSpec((None, n_new, TK, LANES), lambda b, p, i: (p, kv0 + b, 0, 0)),
        pl.BlockSpec((None, 2, n_new, V_ROWS, TK), lambda b, p, i: (p, 0, kv0 + b, 0, 0)),
    ]
    args = [q, kn, vn]
    n_cache = 0
    if cache is not None:
        kc, vc = cache
        n_cache = kc.shape[1] // nseq
        in_specs += [pl.BlockSpec((None, n_cache, TK, LANES), lambda b, p, i: (p, b, 0, 0)),
                     pl.BlockSpec((None, 2, n_cache, V_ROWS, TK), lambda b, p, i: (p, 0, b, 0, 0))]
        args += [kc, vc]
    aliases = {}
    kern = functools.partial(_attn_kernel, n_new=n_new, n_cache=n_cache, tq=tq)
    if prev_out is not None:
        n_in = len(args)
        in_specs.append(pl.BlockSpec(memory_space=pl.ANY))
        aliases = {n_in: 0}
        args.append(prev_out)
        inner = kern
        kern = lambda *refs: inner(*refs[:n_in], *refs[n_in + 1:])
    return pl.pallas_call(
        kern,
        grid=(nseq, 2, nqb),
        in_specs=in_specs,
        out_specs=qspec,
        out_shape=jax.ShapeDtypeStruct(q.shape, BF16),
        scratch_shapes=[pltpu.VMEM((2, LANES, nq), BF16),
                        pltpu.VMEM((2, 1, nq), F32),
                        pltpu.VMEM((2, V_ROWS, nq), F32),
                        pltpu.VMEM((2, TK, nq), F32),
                        pltpu.VMEM((2, 1, nq), F32),
                        pltpu.VMEM((2, TK, nq), F32),
                        pltpu.VMEM((2, 1, nq), F32)],
        input_output_aliases=aliases,
        compiler_params=_cparams(("arbitrary", "arbitrary", "arbitrary")),
        name="attention",
    )(*args)


def _cache_blocks(ck, cv):
    nseq, past = ck.shape[0], ck.shape[1]
    n = past // TK
    kb = ck.astype(BF16).reshape(nseq * n, TK, 2, LANES).transpose(2, 0, 1, 3)
    vb = cv.astype(BF16).reshape(nseq * n, TK, 2, 2, HEAD_DIM).transpose(2, 3, 0, 4, 1)
    ones = jnp.ones(vb.shape[:3] + (1, TK), BF16)
    zeros = jnp.zeros(vb.shape[:3] + (V_ROWS - HEAD_DIM - 1, TK), BF16)
    return kb, jnp.concatenate([vb, ones, zeros], axis=3)


def _ssd_direction(x_ref, bc_ref, dt_ref, a_lanes, expand, st_ref, y_ref, d):
    q = SSD_CHUNK
    row = lax.broadcasted_iota(jnp.int32, (q, q), 0)
    col = lax.broadcasted_iota(jnp.int32, (q, q), 1)
    causal = (row >= col) if d == 0 else (row <= col)
    tril = jnp.where(row >= col, 1.0, 0.0).astype(BF16)

    dt = dt_ref[...]
    a = dt * a_lanes
    prefix = _dot_exact_lhs(tril, a)
    if d == 0:
        cs = prefix
        last = cs[q - 1:q, :]
    else:
        cs = prefix[q - 1:q, :] - prefix + a
        last = cs[0:1, :]
    cs2 = cs * LOG2E
    cst2 = (cs2 - jnp.log2(dt)).T
    wj = jnp.exp(last - cs) * dt
    wj_x = _dot(wj.astype(BF16), expand)
    elast = jnp.exp(last)

    lane = lax.broadcasted_iota(jnp.int32, (q, LANES), 1)
    lo_half = lane < SSM_HEAD_DIM
    lo_half_row = lo_half[0:1, :]
    for g in range(SSM_GROUPS):
        bg = bc_ref[:, g * D_STATE:(g + 1) * D_STATE]
        cg = bc_ref[:, GN + g * D_STATE:GN + (g + 1) * D_STATE]
        gmat = _dot_nt(cg, bg)
        gsl = slice(g * HEADS_PER_GROUP * SSM_HEAD_DIM, (g + 1) * HEADS_PER_GROUP * SSM_HEAD_DIM)
        st = st_ref[d, :, gsl]
        y_inter = _dot(cg, st.astype(BF16))
        xg = x_ref[:, gsl]
        el_parts = []
        for hp in range(HEADS_PER_GROUP // 2):
            xpair = xg[:, hp * LANES:(hp + 1) * LANES]
            l0 = d * SSM_HEADS + g * HEADS_PER_GROUP + 2 * hp
            ws, es = [], []
            for s in range(2):
                hl = l0 + s
                csb = jnp.broadcast_to(cs2[:, hl:hl + 1], (q, q))
                seg = csb - cst2[hl:hl + 1, :]
                ws.append(gmat * jnp.exp2(jnp.where(causal, seg, -jnp.inf)))
                es.append(jnp.exp2(csb))
            wcat = jnp.concatenate(ws, axis=1).astype(BF16)
            zero = jnp.zeros_like(xpair)
            xm = jnp.concatenate([jnp.where(lo_half, xpair, zero), jnp.where(lo_half, zero, xpair)],
                                 axis=0)
            e_pair = jnp.where(lo_half, es[0], es[1])
            ypair = y_inter[:, hp * LANES:(hp + 1) * LANES] * e_pair + _dot(wcat, xm)
            lo = g * HEADS_PER_GROUP * SSM_HEAD_DIM + hp * LANES
            y_ref[:, lo:lo + LANES] = ypair.astype(y_ref.dtype)
            el_parts.append(jnp.where(lo_half_row, elast[:, l0:l0 + 1], elast[:, l0 + 1:l0 + 2]))
        xw = (xg.astype(F32) * wj_x[:, gsl]).astype(BF16)
        el = jnp.concatenate(el_parts, axis=1)
        st_ref[d, :, gsl] = st * el + _dot_tn(bg, xw)


def _ssd_kernel(fblk_ref, bblk_ref, first_ref, last_ref, init_ref, s0i_ref, sfi_ref,
                xf_ref, bcf_ref, dtf_ref, xb_ref, bcb_ref, dtb_ref, a_ref, exp_ref, s0_ref,
                yf_ref, yb_ref, sfin_ref, st_ref):
    del fblk_ref, bblk_ref, s0i_ref, sfi_ref
    s = pl.program_id(0)

    @pl.when(jnp.logical_and(first_ref[s] == 1, init_ref[s] == 1))
    def _():
        st_ref[...] = s0_ref[0]

    @pl.when(jnp.logical_and(first_ref[s] == 1, init_ref[s] == 0))
    def _():
        st_ref[...] = jnp.zeros(st_ref.shape, F32)

    a_lanes = a_ref[...]
    _ssd_direction(xf_ref, bcf_ref, dtf_ref, a_lanes, exp_ref[0], st_ref, yf_ref, 0)
    _ssd_direction(xb_ref, bcb_ref, dtb_ref, a_lanes, exp_ref[1], st_ref, yb_ref, 1)

    @pl.when(jnp.logical_and(last_ref[s] == 1, init_ref[s] == 0))
    def _():
        sfin_ref[0] = st_ref[...]


def _ssd(xs, bc, dt, a_lanes, expand, s0, meta, n_ctx):
    t = xs.shape[0]
    n_steps = t // SSD_CHUNK
    st_shape = (2, D_STATE, D_INNER)
    fwd = lambda s, fblk, bblk, *_: (fblk[s], 0)
    bwd = lambda s, fblk, bblk, *_: (bblk[s], 0)

    def specs(idx):
        return [pl.BlockSpec((SSD_CHUNK, D_INNER), idx),
                pl.BlockSpec((SSD_CHUNK, 2 * GN), idx),
                pl.BlockSpec((SSD_CHUNK, LANES), idx)]

    grid_spec = pltpu.PrefetchScalarGridSpec(
        num_scalar_prefetch=7,
        grid=(n_steps,),
        in_specs=specs(fwd) + specs(bwd) + [
            pl.BlockSpec((1, LANES), lambda s, *_: (0, 0)),
            pl.BlockSpec((2, LANES, D_INNER), lambda s, *_: (0, 0, 0)),
            pl.BlockSpec((1,) + st_shape, lambda s, f, b, fi, la, ini, s0i, sfi: (s0i[s], 0, 0, 0)),
        ],
        out_specs=[pl.BlockSpec((SSD_CHUNK, D_INNER), fwd), pl.BlockSpec((SSD_CHUNK, D_INNER), bwd),
                   pl.BlockSpec((1,) + st_shape, lambda s, f, b, fi, la, ini, s0i, sfi: (sfi[s], 0, 0, 0))],
        scratch_shapes=[pltpu.VMEM(st_shape, F32)],
    )
    out_shape = [jax.ShapeDtypeStruct((t, D_INNER), BF16), jax.ShapeDtypeStruct((t, D_INNER), BF16),
                 jax.ShapeDtypeStruct((n_ctx,) + st_shape, F32)]
    return pl.pallas_call(
        _ssd_kernel, grid_spec=grid_spec, out_shape=out_shape,
        compiler_params=_cparams(("arbitrary",)), name="ssd",
    )(meta["ssd_fblk"], meta["ssd_bblk"], meta["ssd_first"], meta["ssd_last"], meta["ssd_init"],
      meta["ssd_s0i"], meta["ssd_sfi"], xs, bc, dt, xs, bc, dt, a_lanes, expand, s0)


def _route(sel, scores):
    neg = -jnp.inf
    rows = [sel[e:e + 1, :] for e in range(N_EXPERTS)]
    srow = [scores[e:e + 1, :] for e in range(N_EXPERTS)]
    best_score = None
    best = None
    for g in range(N_EXPERT_GROUPS):
        v = rows[g * EXPERTS_PER_GROUP:(g + 1) * EXPERTS_PER_GROUP]
        top2 = None
        for a in range(EXPERTS_PER_GROUP):
            for b in range(a + 1, EXPERTS_PER_GROUP):
                s = v[a] + v[b]
                top2 = s if top2 is None else jnp.maximum(top2, s)
        if g == 0:
            best_score, best = top2, jnp.zeros(top2.shape, jnp.int32)
        else:
            better = top2 > best_score
            best = jnp.where(better, g, best)
            best_score = jnp.where(better, top2, best_score)

    def pick(vals):
        out = vals[0]
        for g in range(1, N_EXPERT_GROUPS):
            out = jnp.where(best == g, vals[g], out)
        return out

    gsel = [pick([rows[g * EXPERTS_PER_GROUP + j] for g in range(N_EXPERT_GROUPS)])
            for j in range(EXPERTS_PER_GROUP)]
    gsc = [pick([srow[g * EXPERTS_PER_GROUP + j] for g in range(N_EXPERT_GROUPS)])
           for j in range(EXPERTS_PER_GROUP)]

    def argmax_first(vals):
        bi = jnp.zeros(vals[0].shape, jnp.int32)
        bv = vals[0]
        for j in range(1, len(vals)):
            better = vals[j] > bv
            bi = jnp.where(better, j, bi)
            bv = jnp.where(better, vals[j], bv)
        return bi

    i1 = argmax_first(gsel)
    i2 = argmax_first([jnp.where(i1 == j, neg, gsel[j]) for j in range(EXPERTS_PER_GROUP)])

    def take(vals, idx):
        out = vals[0]
        for j in range(1, len(vals)):
            out = jnp.where(idx == j, vals[j], out)
        return out

    g1 = take(gsc, i1)
    g2 = take(gsc, i2)
    tot = g1 + g2
    idx = jnp.concatenate([best * EXPERTS_PER_GROUP + i1, best * EXPERTS_PER_GROUP + i2], axis=0)
    gate = jnp.concatenate([g1 / tot, g2 / tot], axis=0)
    return idx, gate


def _merge_kernel(mrow_ref, x_ref, attn_ref, yf_ref, yb_ref, xs_ref, z_ref, g_ref, mod_ref,
                  wa_ref, ws_ref, wo_ref, dvec_ref, snw_ref, n2w_ref, wrt_ref, rb_ref,
                  x1_ref, h2_ref, idx_ref, gate_ref):
    del mrow_ref
    gw = D_INNER // SSM_GROUPS
    ssm_o = None
    for g in range(SSM_GROUPS):
        sl = slice(g * gw, (g + 1) * gw)
        xs = xs_ref[:, sl].astype(F32)
        y = yf_ref[:, sl].astype(F32) + yb_ref[:, sl].astype(F32) + dvec_ref[:, sl] * xs
        y = y * _silu(z_ref[:, sl].astype(F32))
        ms = jnp.mean(y * y, axis=-1, keepdims=True)
        yn = (y * lax.rsqrt(ms + EPS) * snw_ref[:, sl]).astype(BF16)
        part = _dot(yn, ws_ref[sl, :])
        ssm_o = part if ssm_o is None else ssm_o + part
    attn_o = _dot(attn_ref[...], wa_ref[...])
    merged = (g_ref[:, 0:D_MODEL].astype(F32) * attn_o
              + g_ref[:, D_MODEL:2 * D_MODEL].astype(F32) * ssm_o)
    out = _dot(merged.astype(BF16), wo_ref[...])
    x1 = x_ref[...] + mod_ref[0, 2:3, :] * out
    x1_ref[...] = x1
    h2 = _modnorm(x1, n2w_ref[...], mod_ref[0, 4:5, :], mod_ref[0, 3:4, :])
    h2_ref[...] = h2.astype(BF16)
    logits = lax.dot_general(wrt_ref[...], h2, (((1,), (1,)), ((), ())),
                             preferred_element_type=F32, precision=lax.Precision.HIGHEST)
    scores = _sigmoid(logits)
    idx, gate = _route(scores + rb_ref[...], scores)
    idx_ref[...] = idx
    gate_ref[...] = gate


def _merge(x, attn, yf, yb, xs, z, g, mod, meta, lw, wrt, rb):
    t = x.shape[0]
    nblk = t // TM

    def row_blk(width):
        return pl.BlockSpec((TM, width), lambda i, *_: (i, 0))

    grid_spec = pltpu.PrefetchScalarGridSpec(
        num_scalar_prefetch=1,
        grid=(nblk,),
        in_specs=[row_blk(D_MODEL), row_blk(ATTN_WIDTH), row_blk(D_INNER), row_blk(D_INNER),
                  row_blk(D_INNER), row_blk(D_INNER), row_blk(2 * D_MODEL),
                  pl.BlockSpec((1, MOD_ROWS, D_MODEL), lambda i, mrow: (mrow[i], 0, 0)),
                  _const_spec((ATTN_WIDTH, D_MODEL)), _const_spec((D_INNER, D_MODEL)),
                  _const_spec((D_MODEL, D_MODEL)), _const_spec((1, D_INNER)),
                  _const_spec((1, D_INNER)), _const_spec((1, D_MODEL)),
                  _const_spec((N_EXPERTS, D_MODEL)), _const_spec((N_EXPERTS, 1))],
        out_specs=[row_blk(D_MODEL), row_blk(D_MODEL),
                   pl.BlockSpec((2, TM), lambda i, *_: (0, i)),
                   pl.BlockSpec((2, TM), lambda i, *_: (0, i))],
    )
    out_shape = [jax.ShapeDtypeStruct((t, D_MODEL), F32), jax.ShapeDtypeStruct((t, D_MODEL), BF16),
                 jax.ShapeDtypeStruct((2, t), jnp.int32), jax.ShapeDtypeStruct((2, t), F32)]
    return pl.pallas_call(
        _merge_kernel, grid_spec=grid_spec, out_shape=out_shape,
        compiler_params=_cparams(("arbitrary",)), name="merge",
    )(meta["mrow"], x, attn, yf, yb, xs, z, g, mod, lw["w_attn_out"], lw["w_ssm_out"], lw["w_out"],
      lw["ssm_d"], lw["ssm_norm_w"], lw["norm2_w"], wrt, rb)


def _expert_kernel(te_ref, nt_ref, chg_ref, x_ref, wg_ref, wu_ref, wd_ref, o_ref,
                   wgs_ref, wus_ref, wds_ref):
    del te_ref
    i = pl.program_id(0)

    @pl.when(chg_ref[i] == 1)
    def _():
        wgs_ref[...] = wg_ref[0].astype(BF16)
        wus_ref[...] = wu_ref[0].astype(BF16)
        wds_ref[...] = wd_ref[0].astype(BF16)

    @pl.when(i < nt_ref[0])
    def _():
        x = x_ref[...]
        hmid = _silu(_dot(x, wgs_ref[...])) * _dot(x, wus_ref[...])
        o_ref[...] = _dot(hmid.astype(BF16), wds_ref[...])

    @pl.when(i >= nt_ref[0])
    def _():
        o_ref[...] = jnp.zeros(o_ref.shape, o_ref.dtype)


def _experts(x_sorted, tile_expert, n_tiles, wg, wu, wd):
    rows = x_sorted.shape[0]
    changed = jnp.concatenate([jnp.ones((1,), jnp.int32),
                               (tile_expert[1:] != tile_expert[:-1]).astype(jnp.int32)])
    grid_spec = pltpu.PrefetchScalarGridSpec(
        num_scalar_prefetch=3,
        grid=(rows // TE,),
        in_specs=[pl.BlockSpec((TE, D_MODEL), lambda i, te, nt, chg: (i, 0)),
                  pl.BlockSpec((1, D_MODEL, D_FF_EXPERT), lambda i, te, nt, chg: (te[i], 0, 0)),
                  pl.BlockSpec((1, D_MODEL, D_FF_EXPERT), lambda i, te, nt, chg: (te[i], 0, 0)),
                  pl.BlockSpec((1, D_FF_EXPERT, D_MODEL), lambda i, te, nt, chg: (te[i], 0, 0))],
        out_specs=pl.BlockSpec((TE, D_MODEL), lambda i, te, nt, chg: (i, 0)),
        scratch_shapes=[pltpu.VMEM((D_MODEL, D_FF_EXPERT), BF16),
                        pltpu.VMEM((D_MODEL, D_FF_EXPERT), BF16),
                        pltpu.VMEM((D_FF_EXPERT, D_MODEL), BF16)],
    )
    return pl.pallas_call(
        _expert_kernel, grid_spec=grid_spec,
        out_shape=jax.ShapeDtypeStruct((rows, D_MODEL), F32),
        compiler_params=_cparams(("arbitrary",)), name="experts",
    )(tile_expert, n_tiles, changed, x_sorted, wg, wu, wd)


def _combine_kernel(mrow_ref, x1_ref, y0_ref, y1_ref, gate_ref, mod_ref, fnw_ref, o_ref, *, final):
    del mrow_ref
    moe = gate_ref[:, 0:1] * y0_ref[...] + gate_ref[:, 1:2] * y1_ref[...]
    x2 = x1_ref[...] + mod_ref[0, 5:6, :] * moe
    if final:
        ms = jnp.mean(x2 * x2, axis=-1, keepdims=True)
        x2 = x2 * lax.rsqrt(ms + EPS) * fnw_ref[...]
    o_ref[...] = x2


def _combine(x1, y0, y1, gate_t, mod, meta, fnw, final):
    t = x1.shape[0]

    def row_blk(width):
        return pl.BlockSpec((TM, width), lambda i, *_: (i, 0))

    grid_spec = pltpu.PrefetchScalarGridSpec(
        num_scalar_prefetch=1,
        grid=(t // TM,),
        in_specs=[row_blk(D_MODEL), row_blk(D_MODEL), row_blk(D_MODEL), row_blk(2),
                  pl.BlockSpec((1, MOD_ROWS, D_MODEL), lambda i, mrow: (mrow[i], 0, 0)),
                  _const_spec((1, D_MODEL))],
        out_specs=row_blk(D_MODEL),
    )
    return pl.pallas_call(
        functools.partial(_combine_kernel, final=final), grid_spec=grid_spec,
        out_shape=jax.ShapeDtypeStruct((t, D_MODEL), F32),
        compiler_params=_cparams(("arbitrary",)), name="combine",
    )(meta["mrow"], x1, y0, y1, gate_t, mod, fnw)


def _moe(h2, idx, gate, lw):
    t = h2.shape[0]
    n_assign = 2 * t
    rows = n_assign + N_EXPERTS * TE
    e_flat = idx.reshape(n_assign)
    onehot = (e_flat[:, None] == jnp.arange(N_EXPERTS, dtype=jnp.int32)[None, :]).astype(jnp.int32)
    counts = jnp.sum(onehot, axis=0)
    padded = ((counts + TE - 1) // TE) * TE
    pad_end = jnp.cumsum(padded)
    pad_off = pad_end - padded
    off = jnp.cumsum(counts) - counts
    rank = jnp.take_along_axis(jnp.cumsum(onehot, axis=0), e_flat[:, None], axis=1)[:, 0] - 1
    pos = pad_off[e_flat] + rank
    order = jnp.argsort(e_flat, stable=True).astype(jnp.int32)
    p = jnp.arange(rows, dtype=jnp.int32)
    ep = jnp.minimum(jnp.searchsorted(pad_end, p, side="right"), N_EXPERTS - 1).astype(jnp.int32)
    r = p - pad_off[ep]
    valid = r < counts[ep]
    src = jnp.where(valid, order[jnp.clip(off[ep] + r, 0, n_assign - 1)] % t, 0)
    n_tiles = (pad_end[-1] // TE).astype(jnp.int32).reshape(1)
    tile_start = jnp.arange(rows // TE, dtype=jnp.int32) * TE
    tile_expert = jnp.minimum(jnp.searchsorted(pad_end, tile_start, side="right"),
                              N_EXPERTS - 1).astype(jnp.int32)
    last_used = tile_expert[jnp.maximum(n_tiles[0] - 1, 0)]
    tile_expert = jnp.where(tile_start < pad_end[-1], tile_expert, last_used)
    x_sorted = jnp.take(h2, src, axis=0)
    y_sorted = _experts(x_sorted, tile_expert, n_tiles, lw["w_exp_gate"], lw["w_exp_up"],
                        lw["w_exp_down"])
    pos2 = pos.reshape(2, t)
    return jnp.take(y_sorted, pos2[0], axis=0), jnp.take(y_sorted, pos2[1], axis=0)


def _rope_tables(n_tokens):
    rows = n_tokens // GRID_W
    row = jnp.repeat(jnp.arange(rows, dtype=F32), GRID_W)
    col = jnp.tile(jnp.arange(GRID_W, dtype=F32), rows)
    inv = 1.0 / (ROPE_THETA ** (jnp.arange(0, ROPE_AXIS_DIM, 2, dtype=F32) / ROPE_AXIS_DIM))
    ar = row[:, None] * inv
    ac = col[:, None] * inv
    ang = jnp.concatenate([ar, ar, ac, ac], axis=-1)
    cos, sin = jnp.cos(ang), jnp.sin(ang)
    cos = jnp.concatenate([jnp.ones((TM, HEAD_DIM), F32), cos], axis=0)
    sin = jnp.concatenate([jnp.zeros((TM, HEAD_DIM), F32), sin], axis=0)
    return jnp.tile(cos, (1, 2)), jnp.tile(sin, (1, 2))


def _block_meta(n_ctx, ctx_len, n_lat, lat_len):
    ctx_blocks = n_ctx * ctx_len // TM
    lat_blocks = n_lat * lat_len // TM
    per_lat = lat_len // TM
    bi = jnp.arange(ctx_blocks + lat_blocks, dtype=jnp.int32)
    is_ctx = bi < ctx_blocks
    lat_i = bi - ctx_blocks
    mrow = jnp.where(is_ctx, 0, 1 + lat_i // per_lat).astype(jnp.int32)
    rope_blk = jnp.where(is_ctx, 0, 1 + lat_i % per_lat).astype(jnp.int32)
    cos, sin = _rope_tables(lat_len)
    ti = jnp.arange(n_ctx * ctx_len + n_lat * lat_len, dtype=jnp.int32)
    pos = jnp.where(ti < n_ctx * ctx_len, ti % ctx_len, (ti - n_ctx * ctx_len) % lat_len)
    slen = jnp.where(ti < n_ctx * ctx_len, ctx_len, lat_len)
    keep_prev = jnp.broadcast_to((pos != 0).astype(F32)[:, None], (ti.shape[0], LANES))
    keep_next = jnp.broadcast_to((pos != slen - 1).astype(F32)[:, None], (ti.shape[0], LANES))
    meta = {"mrow": mrow, "rope_blk": rope_blk, "cos": cos, "sin": sin,
            "ctx_blk": jnp.minimum(bi, ctx_blocks - 1).astype(jnp.int32),
            "keep_prev": keep_prev, "keep_next": keep_next}

    nc_ctx, nc_lat = ctx_len // SSD_CHUNK, lat_len // SSD_CHUNK
    n_ctx_steps = n_ctx * nc_ctx
    si = jnp.arange(n_ctx_steps + n_lat * nc_lat, dtype=jnp.int32)
    s_ctx = si < n_ctx_steps
    li = si - n_ctx_steps
    seq = jnp.where(s_ctx, si // nc_ctx, li // nc_lat)
    ch = jnp.where(s_ctx, si % nc_ctx, li % nc_lat)
    nc = jnp.where(s_ctx, nc_ctx, nc_lat)
    base = jnp.where(s_ctx, seq * nc_ctx, n_ctx_steps + seq * nc_lat)
    meta.update({
        "ssd_fblk": (base + ch).astype(jnp.int32),
        "ssd_bblk": (base + nc - 1 - ch).astype(jnp.int32),
        "ssd_first": (ch == 0).astype(jnp.int32),
        "ssd_last": (ch == nc - 1).astype(jnp.int32),
        "ssd_init": jnp.where(s_ctx, 0, 1).astype(jnp.int32),
        "ssd_s0i": jnp.where(s_ctx, 0, seq).astype(jnp.int32),
        "ssd_sfi": jnp.where(s_ctx, seq, n_ctx - 1).astype(jnp.int32),
    })
    return meta


def kernel(x_prompt, x_sample, cache_k, cache_v, state_ssm, c, c_ctx, norm1_w, norm2_w, w_mod, b_mod,
           w_in, q_norm_w, k_norm_w, conv_w, conv_b, a_log, dt_bias, ssm_d, ssm_norm_w, w_attn_out,
           w_ssm_out, w_out, w_router, router_bias, w_exp_gate, w_exp_up, w_exp_down, final_norm_w):
    n_ctx, ctx_len, _ = x_prompt.shape
    n_lat, lat_len, _ = x_sample.shape
    depth = w_in.shape[0]
    t_ctx = n_ctx * ctx_len
    t_lat = n_lat * lat_len
    assert t_ctx % TM == 0 and lat_len % TM == 0 and 1 + n_lat <= COND_ROWS
    assert ctx_len % TK == 0 and cache_k.shape[2] % TK == 0 and lat_len % GRID_W == 0

    meta = _block_meta(n_ctx, ctx_len, n_lat, lat_len)
    x = jnp.concatenate([x_prompt.reshape(t_ctx, D_MODEL), x_sample.reshape(t_lat, D_MODEL)], axis=0)

    cond = jnp.zeros((COND_ROWS, D_MODEL), F32).at[0].set(c_ctx).at[1:1 + n_lat].set(c)
    mod_all = _adaln(cond, w_mod, b_mod).reshape(depth, COND_ROWS, N_MOD, D_MODEL)
    mod_all = jnp.pad(mod_all, ((0, 0), (0, 0), (0, MOD_ROWS - N_MOD), (0, 0)))

    head_of_lane = jnp.arange(D_INNER, dtype=jnp.int32) // SSM_HEAD_DIM
    lane_id = jnp.arange(LANES, dtype=jnp.int32)
    expand = jnp.stack([(lane_id[:, None] == head_of_lane[None, :] + d * SSM_HEADS) for d in range(2)]
                       ).astype(BF16)
    wrt = w_router.T
    rb = router_bias.reshape(N_EXPERTS, 1)
    fnw = final_norm_w.reshape(1, D_MODEL)

    new_k, new_v, new_s = [], [], []
    for l in range(depth):
        wl = w_in[l]
        lw = {
            "norm1_w": norm1_w[l].reshape(1, D_MODEL),
            "norm2_w": norm2_w[l].reshape(1, D_MODEL),
            "wqkv": wl[:, :Z_OFF].astype(BF16),
            "wz": wl[:, Z_OFF:XBC_OFF].astype(BF16),
            "wxbc": wl[:, XBC_OFF:DT_OFF].astype(BF16),
            "wdt": jnp.pad(wl[:, DT_OFF:G_OFF], ((0, 0), (0, LANES - 2 * SSM_HEADS))).astype(BF16),
            "wg": wl[:, G_OFF:].astype(BF16),
            "q_norm_w": jnp.tile(q_norm_w[l], 4).reshape(1, 2 * LANES),
            "k_norm_w": jnp.tile(k_norm_w[l], 4).reshape(1, 2 * LANES),
            "conv_w": conv_w[l],
            "conv_b": conv_b[l].reshape(1, CONV_DIM),
            "dt_bias": jnp.pad(dt_bias[l].reshape(1, 2 * SSM_HEADS), ((0, 0), (0, LANES - 2 * SSM_HEADS))),
            "ssm_d": jnp.repeat(ssm_d[l], SSM_HEAD_DIM).reshape(1, D_INNER),
            "ssm_norm_w": ssm_norm_w[l].reshape(1, D_INNER),
            "w_attn_out": w_attn_out[l].astype(BF16),
            "w_ssm_out": w_ssm_out[l].astype(BF16),
            "w_out": w_out[l].astype(BF16),
            "w_exp_gate": w_exp_gate[l],
            "w_exp_up": w_exp_up[l],
            "w_exp_down": w_exp_down[l],
        }
        mod = mod_all[l]
        q, kn, vn, k_ctx, v_ctx, z, xs, bc, dt, g = _in_proj(x, mod, meta, lw, t_ctx)

        attn = _attention(q, kn, vn, None, None, n_ctx, ctx_len, 0, min(TQ, ctx_len))
        attn = _attention(q, kn, vn, _cache_blocks(cache_k[:, l], cache_v[:, l]), attn,
                          n_lat, lat_len, t_ctx, min(TQ, lat_len))

        a_lanes = jnp.pad(-jnp.exp(a_log[l]).reshape(1, 2 * SSM_HEADS),
                          ((0, 0), (0, LANES - 2 * SSM_HEADS)))
        s0 = state_ssm[:, l].transpose(0, 1, 4, 2, 3).reshape(n_lat, 2, D_STATE, D_INNER)
        yf, yb, s_ctx = _ssd(xs, bc, dt, a_lanes, expand, s0, meta, n_ctx)

        x1, h2, idx, gate = _merge(x, attn, yf, yb, xs, z, g, mod, meta, lw, wrt, rb)
        y0, y1 = _moe(h2, idx, gate, lw)
        x = _combine(x1, y0, y1, gate.T, mod, meta, fnw, l == depth - 1)

        new_k.append(k_ctx.reshape(n_ctx, ctx_len, N_KV_HEADS, HEAD_DIM))
        new_v.append(v_ctx.reshape(n_ctx, ctx_len, N_KV_HEADS, HEAD_DIM))
        new_s.append(s_ctx.reshape(n_ctx, 2, D_STATE, SSM_HEADS, SSM_HEAD_DIM).transpose(0, 1, 3, 4, 2))

    y_prompt = x[:t_ctx].reshape(n_ctx, ctx_len, D_MODEL)
    y_sample = x[t_ctx:].reshape(n_lat, lat_len, D_MODEL)
    return (y_prompt, y_sample, jnp.stack(new_k, axis=1), jnp.stack(new_v, axis=1),
            jnp.stack(new_s, axis=1))
```

```python
import functools

import jax
import jax.numpy as jnp
from jax import lax
from jax.experimental import pallas as pl
from jax.experimental.pallas import tpu as pltpu

F32 = jnp.float32
BF16 = jnp.bfloat16

EPS = 1e-6
D_MODEL = 1024
N_HEADS = 16
N_KV_HEADS = 4
HEAD_DIM = 64
ATTN_WIDTH = N_HEADS * HEAD_DIM
KV_WIDTH = N_KV_HEADS * HEAD_DIM
GRID_W = 64
ROPE_AXIS_DIM = HEAD_DIM // 2
ROPE_THETA = 10000.0
D_INNER = 2 * D_MODEL
SSM_HEAD_DIM = 64
SSM_HEADS = D_INNER // SSM_HEAD_DIM
SSM_GROUPS = 4
HEADS_PER_GROUP = SSM_HEADS // SSM_GROUPS
D_STATE = 128
GN = SSM_GROUPS * D_STATE
CONV_DIM = D_INNER + 2 * GN
SSD_CHUNK = 128
N_EXPERTS = 16
N_EXPERT_GROUPS = 4
EXPERTS_PER_GROUP = N_EXPERTS // N_EXPERT_GROUPS
D_FF_EXPERT = 512
N_MOD = 6
MOD_ROWS = 8

LANES = 128
SUBLANES = 8
VMEM_LIMIT = 56 * 1024 * 1024

TM = 512
TQ = 512
TK = 256
TE = 256
CONV_CHUNK = 1024
COND_ROWS = 16
V_ROWS = HEAD_DIM + SUBLANES
LOG2E = 1.4426950408889634
Q_SCALE = HEAD_DIM ** -0.5 * LOG2E

Q_OFF, K_OFF, V_OFF = 0, ATTN_WIDTH, ATTN_WIDTH + KV_WIDTH
Z_OFF = ATTN_WIDTH + 2 * KV_WIDTH
XBC_OFF = Z_OFF + D_INNER
DT_OFF = XBC_OFF + CONV_DIM
G_OFF = DT_OFF + 2 * SSM_HEADS
N_IN = G_OFF + 2 * D_MODEL


def _cparams(sem):
    return pltpu.CompilerParams(dimension_semantics=sem, vmem_limit_bytes=VMEM_LIMIT)


def _const_spec(shape):
    nd = len(shape)
    return pl.BlockSpec(shape, lambda *_: (0,) * nd, pipeline_mode=pl.Buffered(1))


def _dot(a, b):
    return jnp.dot(a, b, preferred_element_type=F32)


def _dot_nt(a, b):
    return lax.dot_general(a, b, (((1,), (1,)), ((), ())), preferred_element_type=F32)


def _dot_tn(a, b):
    return lax.dot_general(a, b, (((0,), (0,)), ((), ())), preferred_element_type=F32)


def _split3(a):
    a1 = a.astype(BF16)
    r = a - a1.astype(F32)
    a2 = r.astype(BF16)
    a3 = (r - a2.astype(F32)).astype(BF16)
    return a1, a2, a3


def _dot_exact_lhs(m_bf16, a):
    a1, a2, a3 = _split3(a)
    return _dot(m_bf16, a1) + _dot(m_bf16, a2) + _dot(m_bf16, a3)


def _modnorm(x, w, sc, sh):
    ms = jnp.mean(x * x, axis=-1, keepdims=True)
    return (x * lax.rsqrt(ms + EPS) * w) * (1.0 + sc) + sh


def _sigmoid(x):
    return 1.0 / (1.0 + jnp.exp(-x))


def _silu(x):
    return x * _sigmoid(x)


def _adaln_kernel(c_ref, w_ref, b_ref, o_ref):
    cs = _silu(c_ref[...])
    o_ref[0] = jnp.dot(cs, w_ref[0], preferred_element_type=F32,
                       precision=lax.Precision.HIGHEST) + b_ref[0]


def _adaln(cond, w_mod, b_mod):
    depth = w_mod.shape[0]
    nb = N_MOD
    return pl.pallas_call(
        _adaln_kernel,
        grid=(depth, nb),
        in_specs=[pl.BlockSpec((COND_ROWS, D_MODEL), lambda l, j: (0, 0)),
                  pl.BlockSpec((1, D_MODEL, D_MODEL), lambda l, j: (l, 0, j)),
                  pl.BlockSpec((1, 1, D_MODEL), lambda l, j: (l, 0, j))],
        out_specs=pl.BlockSpec((1, COND_ROWS, D_MODEL), lambda l, j: (l, 0, j)),
        out_shape=jax.ShapeDtypeStruct((depth, COND_ROWS, N_MOD * D_MODEL), F32),
        compiler_params=_cparams(("arbitrary", "arbitrary")),
        name="adaln",
    )(cond, w_mod, b_mod.reshape(depth, 1, N_MOD * D_MODEL))


def _in_proj_kernel(mrow_ref, rope_ref, ctxi_ref,
                    x_ref, xp_ref, xn_ref, mod_ref, n1w_ref,
                    wqkv_ref, wz_ref, wxbc_ref, wdt_ref, wg_ref,
                    qnw_ref, knw_ref, cos_ref, sin_ref, kprev_ref, knext_ref,
                    convw_ref, convb_ref, dtb_ref,
                    q_ref, kb_ref, vt_ref, kc_ref, vc_ref, z_ref, xs_ref, bc_ref, dt_ref, g_ref,
                    *, n_ctx_blocks):
    del mrow_ref, rope_ref, ctxi_ref
    i = pl.program_id(0)
    is_ctx = i < n_ctx_blocks
    sh1 = mod_ref[0, 0:1, :]
    sc1 = mod_ref[0, 1:2, :]
    nw = n1w_ref[...]
    h = _modnorm(x_ref[...], nw, sc1, sh1).astype(BF16)
    hp = _modnorm(xp_ref[...], nw, sc1, sh1).astype(BF16)
    hn = _modnorm(xn_ref[...], nw, sc1, sh1).astype(BF16)

    r = lax.broadcasted_iota(jnp.int32, (2 * LANES, 2 * LANES), 0) // HEAD_DIM
    c = lax.broadcasted_iota(jnp.int32, (2 * LANES, 2 * LANES), 1) // HEAD_DIM
    same_head = jnp.where(r == c, 1.0, 0.0).astype(BF16)
    lane = lax.broadcasted_iota(jnp.int32, (TM, LANES), 1)
    first_half = (lane % (HEAD_DIM // 2)) < (HEAD_DIM // 4)

    def head_norm_rope(t, w, scale):
        sq = t * t
        hi = sq.astype(BF16)
        lo = (sq - hi.astype(F32)).astype(BF16)
        ssum = _dot(hi, same_head) + _dot(lo, same_head)
        tn = t * lax.rsqrt(ssum * (1.0 / HEAD_DIM) + EPS) * w
        outs = []
        for s in range(2):
            a = tn[:, s * LANES:(s + 1) * LANES]
            rot = jnp.where(first_half, -pltpu.roll(a, LANES - HEAD_DIM // 4, 1),
                            pltpu.roll(a, HEAD_DIM // 4, 1))
            outs.append((a * cos_ref[...] + rot * sin_ref[...]) * scale)
        return outs

    qw = qnw_ref[...]
    kw = knw_ref[...]
    ones_rows = jnp.where(lax.broadcasted_iota(jnp.int32, (V_ROWS - HEAD_DIM, TK), 0) == 0,
                          1.0, 0.0).astype(BF16)
    cw = CONV_CHUNK
    rows = lax.broadcasted_iota(jnp.int32, (TM, cw), 0)
    kprev = jnp.concatenate([kprev_ref[...]] * (cw // LANES), axis=1)
    knext = jnp.concatenate([knext_ref[...]] * (cw // LANES), axis=1)

    def q_stage(cb):
        def mm():
            return _dot(h, wqkv_ref[:, cb * 2 * LANES:(cb + 1) * 2 * LANES])

        def fin(t):
            a, b = head_norm_rope(t, qw, Q_SCALE)
            q_ref[:, cb * 2 * LANES:cb * 2 * LANES + LANES] = a.astype(BF16)
            q_ref[:, cb * 2 * LANES + LANES:(cb + 1) * 2 * LANES] = b.astype(BF16)
        return mm, fin

    def k_fin(t):
        kpairs = head_norm_rope(t, kw, 1.0)
        for rb in range(TM // TK):
            for pr in range(2):
                kb_ref[pr, rb] = kpairs[pr][rb * TK:(rb + 1) * TK, :].astype(BF16)

        @pl.when(is_ctx)
        def _():
            kc_ref[:, 0:LANES] = kpairs[0]
            kc_ref[:, LANES:2 * LANES] = kpairs[1]

    def v_fin(v):
        for rb in range(TM // TK):
            vT = v[rb * TK:(rb + 1) * TK, :].T
            for hd in range(N_KV_HEADS):
                vt_ref[hd // 2, hd % 2, rb, 0:HEAD_DIM, :] = (
                    vT[hd * HEAD_DIM:(hd + 1) * HEAD_DIM, :].astype(BF16))
                vt_ref[hd // 2, hd % 2, rb, HEAD_DIM:V_ROWS, :] = ones_rows

        @pl.when(is_ctx)
        def _():
            vc_ref[...] = v

    def z_fin(t):
        z_ref[...] = t.astype(BF16)

    def g_fin(t):
        g_ref[...] = _sigmoid(t).astype(BF16)

    def dt_fin(t):
        dtr = t + dtb_ref[...]
        dt_ref[...] = jnp.maximum(dtr, 0.0) + jnp.log(1.0 + jnp.exp(-jnp.abs(dtr)))

    def conv_stage(cb):
        sl = slice(cb * cw, (cb + 1) * cw)

        def mm():
            return (_dot(h, wxbc_ref[:, sl]), _dot(hp, wxbc_ref[:, sl]), _dot(hn, wxbc_ref[:, sl]))

        def fin(res):
            pre, pp, pn = res
            prev_row = pp[SUBLANES - 1:SUBLANES, :]
            next_row = pn[0:1, :]
            up = jnp.where(rows == 0, prev_row, pltpu.roll(pre, 1, 0)) * kprev
            down = jnp.where(rows == TM - 1, next_row, pltpu.roll(pre, TM - 1, 0)) * knext
            y = (convb_ref[:, sl] + up * convw_ref[0:1, sl] + pre * convw_ref[1:2, sl]
                 + down * convw_ref[2:3, sl])
            y = _silu(y).astype(BF16)
            lo = cb * cw
            if lo < D_INNER:
                xs_ref[:, lo:lo + cw] = y
            else:
                bc_ref[:, lo - D_INNER:lo - D_INNER + cw] = y
        return mm, fin

    stages = [q_stage(cb) for cb in range(ATTN_WIDTH // (2 * LANES))]
    stages.append((lambda: _dot(h, wqkv_ref[:, K_OFF:K_OFF + KV_WIDTH]), k_fin))
    stages.append((lambda: _dot(h, wqkv_ref[:, V_OFF:V_OFF + KV_WIDTH]), v_fin))
    stages.append((lambda: _dot(h, wdt_ref[...]), dt_fin))
    stages += [conv_stage(cb) for cb in range(CONV_DIM // cw)]
    stages.append((lambda: _dot(h, wz_ref[...]), z_fin))
    stages.append((lambda: _dot(h, wg_ref[...]), g_fin))
    pending = None
    for mm, fin in stages:
        res = mm()
        if pending is not None:
            pending[0](pending[1])
        pending = (fin, res)
    pending[0](pending[1])


def _in_proj(x, mod, meta, lw, t_ctx):
    t = x.shape[0]
    nblk = t // TM
    nhalo = t // SUBLANES
    per_halo = TM // SUBLANES
    n_ctx_blocks = t_ctx // TM
    nkb = TM // TK

    def row_blk(width):
        return pl.BlockSpec((TM, width), lambda i, *_: (i, 0))

    ctx_blk = pl.BlockSpec((TM, KV_WIDTH), lambda i, mrow, rope, ctxi: (ctxi[i], 0))
    grid_spec = pltpu.PrefetchScalarGridSpec(
        num_scalar_prefetch=3,
        grid=(nblk,),
        in_specs=[
            row_blk(D_MODEL),
            pl.BlockSpec((SUBLANES, D_MODEL),
                         lambda i, *_: (jnp.maximum(i * per_halo - 1, 0), 0)),
            pl.BlockSpec((SUBLANES, D_MODEL),
                         lambda i, *_: (jnp.minimum((i + 1) * per_halo, nhalo - 1), 0)),
            pl.BlockSpec((1, MOD_ROWS, D_MODEL), lambda i, mrow, *_: (mrow[i], 0, 0)),
            _const_spec((1, D_MODEL)),
            _const_spec((D_MODEL, ATTN_WIDTH + 2 * KV_WIDTH)),
            _const_spec((D_MODEL, D_INNER)),
            _const_spec((D_MODEL, CONV_DIM)),
            _const_spec((D_MODEL, LANES)),
            _const_spec((D_MODEL, 2 * D_MODEL)),
            _const_spec((1, 2 * LANES)),
            _const_spec((1, 2 * LANES)),
            pl.BlockSpec((TM, LANES), lambda i, mrow, rope, *_: (rope[i], 0)),
            pl.BlockSpec((TM, LANES), lambda i, mrow, rope, *_: (rope[i], 0)),
            row_blk(LANES),
            row_blk(LANES),
            _const_spec((3, CONV_DIM)),
            _const_spec((1, CONV_DIM)),
            _const_spec((1, LANES)),
        ],
        out_specs=[row_blk(ATTN_WIDTH),
                   pl.BlockSpec((2, nkb, TK, LANES), lambda i, *_: (0, i, 0, 0)),
                   pl.BlockSpec((2, 2, nkb, V_ROWS, TK), lambda i, *_: (0, 0, i, 0, 0)),
                   ctx_blk, ctx_blk,
                   row_blk(D_INNER), row_blk(D_INNER), row_blk(2 * GN), row_blk(LANES),
                   row_blk(2 * D_MODEL)],
    )
    out_shape = [
        jax.ShapeDtypeStruct((t, ATTN_WIDTH), BF16),
        jax.ShapeDtypeStruct((2, t // TK, TK, LANES), BF16),
        jax.ShapeDtypeStruct((2, 2, t // TK, V_ROWS, TK), BF16),
        jax.ShapeDtypeStruct((t_ctx, KV_WIDTH), F32),
        jax.ShapeDtypeStruct((t_ctx, KV_WIDTH), F32),
        jax.ShapeDtypeStruct((t, D_INNER), BF16),
        jax.ShapeDtypeStruct((t, D_INNER), BF16),
        jax.ShapeDtypeStruct((t, 2 * GN), BF16),
        jax.ShapeDtypeStruct((t, LANES), F32),
        jax.ShapeDtypeStruct((t, 2 * D_MODEL), BF16),
    ]
    return pl.pallas_call(
        functools.partial(_in_proj_kernel, n_ctx_blocks=n_ctx_blocks),
        grid_spec=grid_spec, out_shape=out_shape,
        compiler_params=_cparams(("arbitrary",)), name="in_proj",
    )(meta["mrow"], meta["rope_blk"], meta["ctx_blk"],
      x, x, x, mod, lw["norm1_w"], lw["wqkv"], lw["wz"], lw["wxbc"], lw["wdt"], lw["wg"],
      lw["q_norm_w"], lw["k_norm_w"], meta["cos"], meta["sin"], meta["keep_prev"], meta["keep_next"],
      lw["conv_w"], lw["conv_b"], lw["dt_bias"])


def _attn_kernel(*refs, n_new, n_cache, tq):
    q_ref, kn_ref, vn_ref = refs[0:3]
    if n_cache:
        kc_ref, vc_ref = refs[3:5]
    o_ref, qt_ref, m_ref, acc_ref, sa_ref, mxa_ref, sb_ref, mxb_ref = refs[-8:]
    nq = 4 * tq
    zeros_half = jnp.zeros((HEAD_DIM, tq), F32)
    for hh in range(2):
        cols = []
        for jj in range(2):
            off = 2 * LANES * hh + LANES * jj
            qt = q_ref[:, off:off + LANES].astype(F32).T
            for s in range(2):
                head = qt[s * HEAD_DIM:(s + 1) * HEAD_DIM, :]
                parts = [head, zeros_half] if hh == 0 else [zeros_half, head]
                cols.append(jnp.concatenate(parts, axis=0))
        qt_ref[hh] = jnp.concatenate(cols, axis=1).astype(BF16)
        m_ref[hh] = jnp.full((1, nq), -jnp.inf, F32)
        acc_ref[hh] = jnp.zeros((V_ROWS, nq), F32)

    def scores(kblk, s_ref, mx_ref):
        for hh in range(2):
            s = _dot(kblk, qt_ref[hh])
            s_ref[hh] = s
            mx_ref[hh] = jnp.max(s, axis=0, keepdims=True)

    def consume(vt_of, s_ref, mx_ref):
        for hh in range(2):
            m = m_ref[hh]
            m_new = jnp.maximum(m, mx_ref[hh])
            alpha = jnp.exp2(m - m_new)
            p = jnp.exp2(s_ref[hh] - m_new).astype(BF16)
            acc_ref[hh] = alpha * acc_ref[hh] + _dot(vt_of(hh), p)
            m_ref[hh] = m_new

    bufs = ((sa_ref, mxa_ref), (sb_ref, mxb_ref))
    scores(kn_ref[0], *bufs[0])

    def body(j, carry):
        scores(kn_ref[2 * j + 1], *bufs[1])
        consume(lambda hh: vn_ref[hh, 2 * j], *bufs[0])
        scores(kn_ref[2 * j + 2], *bufs[0])
        consume(lambda hh: vn_ref[hh, 2 * j + 1], *bufs[1])
        return carry

    n_loop = (n_new - 1) // 2
    lax.fori_loop(0, n_loop, body, 0)
    rest = [(kn_ref, vn_ref, b) for b in range(2 * n_loop, n_new)]
    if n_cache:
        rest += [(kc_ref, vc_ref, b) for b in range(n_cache)]
    for r, (_, v_src, b) in enumerate(rest):
        if r + 1 < len(rest):
            k_nxt, _, b_nxt = rest[r + 1]
            scores(k_nxt[b_nxt], *bufs[(r + 1) % 2])
        consume(lambda hh, v_src=v_src, b=b: v_src[hh, b], *bufs[r % 2])

    for hh in range(2):
        oT = acc_ref[hh]
        oT = oT[0:HEAD_DIM, :] / oT[HEAD_DIM:HEAD_DIM + 1, :]
        for jj in range(2):
            pair = jnp.concatenate([oT[:, (2 * jj) * tq:(2 * jj + 1) * tq],
                                    oT[:, (2 * jj + 1) * tq:(2 * jj + 2) * tq]], axis=0)
            off = 2 * LANES * hh + LANES * jj
            o_ref[:, off:off + LANES] = pair.T.astype(BF16)


def _attention(q, kn, vn, cache, prev_out, nseq, seq_len, row0, tq):
    n_new = seq_len // TK
    nqb = seq_len // tq
    nq = 4 * tq
    q0 = row0 // tq
    kv0 = row0 // (n_new * TK)
    assert row0 % tq == 0 and row0 % (n_new * TK) == 0
    qspec = pl.BlockSpec((tq, 4 * LANES), lambda b, p, i: (q0 + b * nqb + i, p))
    in_specs = [
        qspec,
        pl.BlockSpec((None, n_new, TK, LANES), lambda b, p, i: (p, kv0 + b, 0, 0)),
        pl.BlockSpec((None, 2, n_new, V_ROWS, TK), lambda b, p, i: (p, 0, kv0 + b, 0, 0)),
    ]
    args = [q, kn, vn]
    n_cache = 0
    if cache is not None:
        kc, vc = cache
        n_cache = kc.shape[1] // nseq
        in_specs += [pl.BlockSpec((None, n_cache, TK, LANES), lambda b, p, i: (p, b, 0, 0)),
                     pl.BlockSpec((None, 2, n_cache, V_ROWS, TK), lambda b, p, i: (p, 0, b, 0, 0))]
        args += [kc, vc]
    aliases = {}
    kern = functools.partial(_attn_kernel, n_new=n_new, n_cache=n_cache, tq=tq)
    if prev_out is not None:
        n_in = len(args)
        in_specs.append(pl.BlockSpec(memory_space=pl.ANY))
        aliases = {n_in: 0}
        args.append(prev_out)
        inner = kern
        kern = lambda *refs: inner(*refs[:n_in], *refs[n_in + 1:])
    return pl.pallas_call(
        kern,
        grid=(nseq, 2, nqb),
        in_specs=in_specs,
        out_specs=qspec,
        out_shape=jax.ShapeDtypeStruct(q.shape, BF16),
        scratch_shapes=[pltpu.VMEM((2, LANES, nq), BF16),
                        pltpu.VMEM((2, 1, nq), F32),
                        pltpu.VMEM((2, V_ROWS, nq), F32),
                        pltpu.VMEM((2, TK, nq), F32),
                        pltpu.VMEM((2, 1, nq), F32),
                        pltpu.VMEM((2, TK, nq), F32),
                        pltpu.VMEM((2, 1, nq), F32)],
        input_output_aliases=aliases,
        compiler_params=_cparams(("arbitrary", "arbitrary", "arbitrary")),
        name="attention",
    )(*args)


def _cache_blocks(ck, cv):
    nseq, past = ck.shape[0], ck.shape[1]
    n = past // TK
    kb = ck.astype(BF16).reshape(nseq * n, TK, 2, LANES).transpose(2, 0, 1, 3)
    vb = cv.astype(BF16).reshape(nseq * n, TK, 2, 2, HEAD_DIM).transpose(2, 3, 0, 4, 1)
    ones = jnp.ones(vb.shape[:3] + (1, TK), BF16)
    zeros = jnp.zeros(vb.shape[:3] + (V_ROWS - HEAD_DIM - 1, TK), BF16)
    return kb, jnp.concatenate([vb, ones, zeros], axis=3)


def _ssd_direction(x_ref, bc_ref, dt_ref, a_lanes, expand, st_ref, y_ref, d):
    q = SSD_CHUNK
    row = lax.broadcasted_iota(jnp.int32, (q, q), 0)
    col = lax.broadcasted_iota(jnp.int32, (q, q), 1)
    causal = (row >= col) if d == 0 else (row <= col)
    tril = jnp.where(row >= col, 1.0, 0.0).astype(BF16)

    dt = dt_ref[...]
    a = dt * a_lanes
    prefix = _dot_exact_lhs(tril, a)
    if d == 0:
        cs = prefix
        last = cs[q - 1:q, :]
    else:
        cs = prefix[q - 1:q, :] - prefix + a
        last = cs[0:1, :]
    cs2 = cs * LOG2E
    cst2 = (cs2 - jnp.log2(dt)).T
    wj = jnp.exp(last - cs) * dt
    wj_x = _dot(wj.astype(BF16), expand)
    elast = jnp.exp(last)

    lane = lax.broadcasted_iota(jnp.int32, (q, LANES), 1)
    lo_half = lane < SSM_HEAD_DIM
    lo_half_row = lo_half[0:1, :]
    for g in range(SSM_GROUPS):
        bg = bc_ref[:, g * D_STATE:(g + 1) * D_STATE]
        cg = bc_ref[:, GN + g * D_STATE:GN + (g + 1) * D_STATE]
        gmat = _dot_nt(cg, bg)
        gsl = slice(g * HEADS_PER_GROUP * SSM_HEAD_DIM, (g + 1) * HEADS_PER_GROUP * SSM_HEAD_DIM)
        st = st_ref[d, :, gsl]
        y_inter = _dot(cg, st.astype(BF16))
        xg = x_ref[:, gsl]
        el_parts = []
        for hp in range(HEADS_PER_GROUP // 2):
            xpair = xg[:, hp * LANES:(hp + 1) * LANES]
            l0 = d * SSM_HEADS + g * HEADS_PER_GROUP + 2 * hp
            ws, es = [], []
            for s in range(2):
                hl = l0 + s
                csb = jnp.broadcast_to(cs2[:, hl:hl + 1], (q, q))
                seg = csb - cst2[hl:hl + 1, :]
                ws.append(gmat * jnp.exp2(jnp.where(causal, seg, -jnp.inf)))
                es.append(jnp.exp2(csb))
            wcat = jnp.concatenate(ws, axis=1).astype(BF16)
            zero = jnp.zeros_like(xpair)
            xm = jnp.concatenate([jnp.where(lo_half, xpair, zero), jnp.where(lo_half, zero, xpair)],
                                 axis=0)
            e_pair = jnp.where(lo_half, es[0], es[1])
            ypair = y_inter[:, hp * LANES:(hp + 1) * LANES] * e_pair + _dot(wcat, xm)
            lo = g * HEADS_PER_GROUP * SSM_HEAD_DIM + hp * LANES
            y_ref[:, lo:lo + LANES] = ypair.astype(y_ref.dtype)
            el_parts.append(jnp.where(lo_half_row, elast[:, l0:l0 + 1], elast[:, l0 + 1:l0 + 2]))
        xw = (xg.astype(F32) * wj_x[:, gsl]).astype(BF16)
        el = jnp.concatenate(el_parts, axis=1)
        st_ref[d, :, gsl] = st * el + _dot_tn(bg, xw)


def _ssd_kernel(fblk_ref, bblk_ref, first_ref, last_ref, init_ref, s0i_ref, sfi_ref,
                xf_ref, bcf_ref, dtf_ref, xb_ref, bcb_ref, dtb_ref, a_ref, exp_ref, s0_ref,
                yf_ref, yb_ref, sfin_ref, st_ref):
    del fblk_ref, bblk_ref, s0i_ref, sfi_ref
    s = pl.program_id(0)

    @pl.when(jnp.logical_and(first_ref[s] == 1, init_ref[s] == 1))
    def _():
        st_ref[...] = s0_ref[0]

    @pl.when(jnp.logical_and(first_ref[s] == 1, init_ref[s] == 0))
    def _():
        st_ref[...] = jnp.zeros(st_ref.shape, F32)

    a_lanes = a_ref[...]
    _ssd_direction(xf_ref, bcf_ref, dtf_ref, a_lanes, exp_ref[0], st_ref, yf_ref, 0)
    _ssd_direction(xb_ref, bcb_ref, dtb_ref, a_lanes, exp_ref[1], st_ref, yb_ref, 1)

    @pl.when(jnp.logical_and(last_ref[s] == 1, init_ref[s] == 0))
    def _():
        sfin_ref[0] = st_ref[...]


def _ssd(xs, bc, dt, a_lanes, expand, s0, meta, n_ctx):
    t = xs.shape[0]
    n_steps = t // SSD_CHUNK
    st_shape = (2, D_STATE, D_INNER)
    fwd = lambda s, fblk, bblk, *_: (fblk[s], 0)
    bwd = lambda s, fblk, bblk, *_: (bblk[s], 0)

    def specs(idx):
        return [pl.BlockSpec((SSD_CHUNK, D_INNER), idx),
                pl.BlockSpec((SSD_CHUNK, 2 * GN), idx),
                pl.BlockSpec((SSD_CHUNK, LANES), idx)]

    grid_spec = pltpu.PrefetchScalarGridSpec(
        num_scalar_prefetch=7,
        grid=(n_steps,),
        in_specs=specs(fwd) + specs(bwd) + [
            pl.BlockSpec((1, LANES), lambda s, *_: (0, 0)),
            pl.BlockSpec((2, LANES, D_INNER), lambda s, *_: (0, 0, 0)),
            pl.BlockSpec((1,) + st_shape, lambda s, f, b, fi, la, ini, s0i, sfi: (s0i[s], 0, 0, 0)),
        ],
        out_specs=[pl.BlockSpec((SSD_CHUNK, D_INNER), fwd), pl.BlockSpec((SSD_CHUNK, D_INNER), bwd),
                   pl.BlockSpec((1,) + st_shape, lambda s, f, b, fi, la, ini, s0i, sfi: (sfi[s], 0, 0, 0))],
        scratch_shapes=[pltpu.VMEM(st_shape, F32)],
    )
    out_shape = [jax.ShapeDtypeStruct((t, D_INNER), BF16), jax.ShapeDtypeStruct((t, D_INNER), BF16),
                 jax.ShapeDtypeStruct((n_ctx,) + st_shape, F32)]
    return pl.pallas_call(
        _ssd_kernel, grid_spec=grid_spec, out_shape=out_shape,
        compiler_params=_cparams(("arbitrary",)), name="ssd",
    )(meta["ssd_fblk"], meta["ssd_bblk"], meta["ssd_first"], meta["ssd_last"], meta["ssd_init"],
      meta["ssd_s0i"], meta["ssd_sfi"], xs, bc, dt, xs, bc, dt, a_lanes, expand, s0)


def _route(sel, scores):
    neg = -jnp.inf
    rows = [sel[e:e + 1, :] for e in range(N_EXPERTS)]
    srow = [scores[e:e + 1, :] for e in range(N_EXPERTS)]
    best_score = None
    best = None
    for g in range(N_EXPERT_GROUPS):
        v = rows[g * EXPERTS_PER_GROUP:(g + 1) * EXPERTS_PER_GROUP]
        top2 = None
        for a in range(EXPERTS_PER_GROUP):
            for b in range(a + 1, EXPERTS_PER_GROUP):
                s = v[a] + v[b]
                top2 = s if top2 is None else jnp.maximum(top2, s)
        if g == 0:
            best_score, best = top2, jnp.zeros(top2.shape, jnp.int32)
        else:
            better = top2 > best_score
            best = jnp.where(better, g, best)
            best_score = jnp.where(better, top2, best_score)

    def pick(vals):
        out = vals[0]
        for g in range(1, N_EXPERT_GROUPS):
            out = jnp.where(best == g, vals[g], out)
        return out

    gsel = [pick([rows[g * EXPERTS_PER_GROUP + j] for g in range(N_EXPERT_GROUPS)])
            for j in range(EXPERTS_PER_GROUP)]
    gsc = [pick([srow[g * EXPERTS_PER_GROUP + j] for g in range(N_EXPERT_GROUPS)])
           for j in range(EXPERTS_PER_GROUP)]

    def argmax_first(vals):
        bi = jnp.zeros(vals[0].shape, jnp.int32)
        bv = vals[0]
        for j in range(1, len(vals)):
            better = vals[j] > bv
            bi = jnp.where(better, j, bi)
            bv = jnp.where(better, vals[j], bv)
        return bi

    i1 = argmax_first(gsel)
    i2 = argmax_first([jnp.where(i1 == j, neg, gsel[j]) for j in range(EXPERTS_PER_GROUP)])

    def take(vals, idx):
        out = vals[0]
        for j in range(1, len(vals)):
            out = jnp.where(idx == j, vals[j], out)
        return out

    g1 = take(gsc, i1)
    g2 = take(gsc, i2)
    tot = g1 + g2
    idx = jnp.concatenate([best * EXPERTS_PER_GROUP + i1, best * EXPERTS_PER_GROUP + i2], axis=0)
    gate = jnp.concatenate([g1 / tot, g2 / tot], axis=0)
    return idx, gate


def _merge_kernel(mrow_ref, x_ref, attn_ref, yf_ref, yb_ref, xs_ref, z_ref, g_ref, mod_ref,
                  wa_ref, ws_ref, wo_ref, dvec_ref, snw_ref, n2w_ref, wrt_ref, rb_ref,
                  x1_ref, h2_ref, idx_ref, gate_ref):
    del mrow_ref
    gw = D_INNER // SSM_GROUPS
    ssm_o = None
    for g in range(SSM_GROUPS):
        sl = slice(g * gw, (g + 1) * gw)
        xs = xs_ref[:, sl].astype(F32)
        y = yf_ref[:, sl].astype(F32) + yb_ref[:, sl].astype(F32) + dvec_ref[:, sl] * xs
        y = y * _silu(z_ref[:, sl].astype(F32))
        ms = jnp.mean(y * y, axis=-1, keepdims=True)
        yn = (y * lax.rsqrt(ms + EPS) * snw_ref[:, sl]).astype(BF16)
        part = _dot(yn, ws_ref[sl, :])
        ssm_o = part if ssm_o is None else ssm_o + part
    attn_o = _dot(attn_ref[...], wa_ref[...])
    merged = (g_ref[:, 0:D_MODEL].astype(F32) * attn_o
              + g_ref[:, D_MODEL:2 * D_MODEL].astype(F32) * ssm_o)
    out = _dot(merged.astype(BF16), wo_ref[...])
    x1 = x_ref[...] + mod_ref[0, 2:3, :] * out
    x1_ref[...] = x1
    h2 = _modnorm(x1, n2w_ref[...], mod_ref[0, 4:5, :], mod_ref[0, 3:4, :])
    h2_ref[...] = h2.astype(BF16)
    logits = lax.dot_general(wrt_ref[...], h2, (((1,), (1,)), ((), ())),
                             preferred_element_type=F32, precision=lax.Precision.HIGHEST)
    scores = _sigmoid(logits)
    idx, gate = _route(scores + rb_ref[...], scores)
    idx_ref[...] = idx
    gate_ref[...] = gate


def _merge(x, attn, yf, yb, xs, z, g, mod, meta, lw, wrt, rb):
    t = x.shape[0]
    nblk = t // TM

    def row_blk(width):
        return pl.BlockSpec((TM, width), lambda i, *_: (i, 0))

    grid_spec = pltpu.PrefetchScalarGridSpec(
        num_scalar_prefetch=1,
        grid=(nblk,),
        in_specs=[row_blk(D_MODEL), row_blk(ATTN_WIDTH), row_blk(D_INNER), row_blk(D_INNER),
                  row_blk(D_INNER), row_blk(D_INNER), row_blk(2 * D_MODEL),
                  pl.BlockSpec((1, MOD_ROWS, D_MODEL), lambda i, mrow: (mrow[i], 0, 0)),
                  _const_spec((ATTN_WIDTH, D_MODEL)), _const_spec((D_INNER, D_MODEL)),
                  _const_spec((D_MODEL, D_MODEL)), _const_spec((1, D_INNER)),
                  _const_spec((1, D_INNER)), _const_spec((1, D_MODEL)),
                  _const_spec((N_EXPERTS, D_MODEL)), _const_spec((N_EXPERTS, 1))],
        out_specs=[row_blk(D_MODEL), row_blk(D_MODEL),
                   pl.BlockSpec((2, TM), lambda i, *_: (0, i)),
                   pl.BlockSpec((2, TM), lambda i, *_: (0, i))],
    )
    out_shape = [jax.ShapeDtypeStruct((t, D_MODEL), F32), jax.ShapeDtypeStruct((t, D_MODEL), BF16),
                 jax.ShapeDtypeStruct((2, t), jnp.int32), jax.ShapeDtypeStruct((2, t), F32)]
    return pl.pallas_call(
        _merge_kernel, grid_spec=grid_spec, out_shape=out_shape,
        compiler_params=_cparams(("arbitrary",)), name="merge",
    )(meta["mrow"], x, attn, yf, yb, xs, z, g, mod, lw["w_attn_out"], lw["w_ssm_out"], lw["w_out"],
      lw["ssm_d"], lw["ssm_norm_w"], lw["norm2_w"], wrt, rb)


def _expert_kernel(te_ref, nt_ref, chg_ref, x_ref, wg_ref, wu_ref, wd_ref, o_ref,
                   wgs_ref, wus_ref, wds_ref):
    del te_ref
    i = pl.program_id(0)

    @pl.when(chg_ref[i] == 1)
    def _():
        wgs_ref[...] = wg_ref[0].astype(BF16)
        wus_ref[...] = wu_ref[0].astype(BF16)
        wds_ref[...] = wd_ref[0].astype(BF16)

    @pl.when(i < nt_ref[0])
    def _():
        x = x_ref[...]
        hmid = _silu(_dot(x, wgs_ref[...])) * _dot(x, wus_ref[...])
        o_ref[...] = _dot(hmid.astype(BF16), wds_ref[...]).astype(o_ref.dtype)

    @pl.when(i >= nt_ref[0])
    def _():
        o_ref[...] = jnp.zeros(o_ref.shape, o_ref.dtype)


def _experts(x_sorted, tile_expert, n_tiles, wg, wu, wd, layer):
    rows = x_sorted.shape[0]
    changed = jnp.concatenate([jnp.ones((1,), jnp.int32),
                               (tile_expert[1:] != tile_expert[:-1]).astype(jnp.int32)])
    grid_spec = pltpu.PrefetchScalarGridSpec(
        num_scalar_prefetch=3,
        grid=(rows // TE,),
        in_specs=[pl.BlockSpec((TE, D_MODEL), lambda i, te, nt, chg: (i, 0)),
                  pl.BlockSpec((None, 1, D_MODEL, D_FF_EXPERT), lambda i, te, nt, chg: (layer, te[i], 0, 0)),
                  pl.BlockSpec((None, 1, D_MODEL, D_FF_EXPERT), lambda i, te, nt, chg: (layer, te[i], 0, 0)),
                  pl.BlockSpec((None, 1, D_FF_EXPERT, D_MODEL), lambda i, te, nt, chg: (layer, te[i], 0, 0))],
        out_specs=pl.BlockSpec((TE, D_MODEL), lambda i, te, nt, chg: (i, 0)),
        scratch_shapes=[pltpu.VMEM((D_MODEL, D_FF_EXPERT), BF16),
                        pltpu.VMEM((D_MODEL, D_FF_EXPERT), BF16),
                        pltpu.VMEM((D_FF_EXPERT, D_MODEL), BF16)],
    )
    return pl.pallas_call(
        _expert_kernel, grid_spec=grid_spec,
        out_shape=jax.ShapeDtypeStruct((rows, D_MODEL), BF16),
        compiler_params=_cparams(("arbitrary",)), name="experts",
    )(tile_expert, n_tiles, changed, x_sorted, wg, wu, wd)


def _combine_kernel(mrow_ref, x1_ref, y0_ref, y1_ref, gate_ref, mod_ref, fnw_ref, *out_refs,
                    n_ctx_blocks):
    del mrow_ref
    moe = (gate_ref[:, 0:1] * y0_ref[...].astype(F32) + gate_ref[:, 1:2] * y1_ref[...].astype(F32))
    x2 = x1_ref[...] + mod_ref[0, 5:6, :] * moe
    if n_ctx_blocks is None:
        out_refs[0][...] = x2
        return
    ms = jnp.mean(x2 * x2, axis=-1, keepdims=True)
    y = x2 * lax.rsqrt(ms + EPS) * fnw_ref[...]
    is_ctx = pl.program_id(0) < n_ctx_blocks

    @pl.when(is_ctx)
    def _():
        out_refs[0][...] = y

    @pl.when(jnp.logical_not(is_ctx))
    def _():
        out_refs[1][...] = y


def _combine(x1, y0, y1, gate_t, mod, meta, fnw, t_ctx):
    t = x1.shape[0]

    def row_blk(width):
        return pl.BlockSpec((TM, width), lambda i, *_: (i, 0))

    if t_ctx is None:
        n_ctx_blocks = None
        out_specs = row_blk(D_MODEL)
        out_shape = jax.ShapeDtypeStruct((t, D_MODEL), F32)
    else:
        n_ctx_blocks = t_ctx // TM
        out_specs = [pl.BlockSpec((TM, D_MODEL), lambda i, *_: (jnp.minimum(i, n_ctx_blocks - 1), 0)),
                     pl.BlockSpec((TM, D_MODEL), lambda i, *_: (jnp.maximum(i - n_ctx_blocks, 0), 0))]
        out_shape = [jax.ShapeDtypeStruct((t_ctx, D_MODEL), F32),
                     jax.ShapeDtypeStruct((t - t_ctx, D_MODEL), F32)]
    grid_spec = pltpu.PrefetchScalarGridSpec(
        num_scalar_prefetch=1,
        grid=(t // TM,),
        in_specs=[row_blk(D_MODEL), row_blk(D_MODEL), row_blk(D_MODEL), row_blk(2),
                  pl.BlockSpec((1, MOD_ROWS, D_MODEL), lambda i, mrow: (mrow[i], 0, 0)),
                  _const_spec((1, D_MODEL))],
        out_specs=out_specs,
    )
    return pl.pallas_call(
        functools.partial(_combine_kernel, n_ctx_blocks=n_ctx_blocks), grid_spec=grid_spec,
        out_shape=out_shape,
        compiler_params=_cparams(("arbitrary",)), name="combine",
    )(meta["mrow"], x1, y0, y1, gate_t, mod, fnw)


def _moe(h2, idx, gate, w_exp, layer):
    t = h2.shape[0]
    n_assign = 2 * t
    rows = n_assign + N_EXPERTS * TE
    e_flat = idx.reshape(n_assign)
    onehot = (e_flat[:, None] == jnp.arange(N_EXPERTS, dtype=jnp.int32)[None, :]).astype(jnp.int32)
    counts = jnp.sum(onehot, axis=0)
    padded = ((counts + TE - 1) // TE) * TE
    pad_end = jnp.cumsum(padded)
    pad_off = pad_end - padded
    off = jnp.cumsum(counts) - counts
    rank = jnp.take_along_axis(jnp.cumsum(onehot, axis=0), e_flat[:, None], axis=1)[:, 0] - 1
    pos = pad_off[e_flat] + rank
    order = jnp.argsort(e_flat, stable=True).astype(jnp.int32)
    p = jnp.arange(rows, dtype=jnp.int32)
    ep = jnp.minimum(jnp.searchsorted(pad_end, p, side="right"), N_EXPERTS - 1).astype(jnp.int32)
    r = p - pad_off[ep]
    valid = r < counts[ep]
    src = jnp.where(valid, order[jnp.clip(off[ep] + r, 0, n_assign - 1)] % t, 0)
    n_tiles = (pad_end[-1] // TE).astype(jnp.int32).reshape(1)
    tile_start = jnp.arange(rows // TE, dtype=jnp.int32) * TE
    tile_expert = jnp.minimum(jnp.searchsorted(pad_end, tile_start, side="right"),
                              N_EXPERTS - 1).astype(jnp.int32)
    last_used = tile_expert[jnp.maximum(n_tiles[0] - 1, 0)]
    tile_expert = jnp.where(tile_start < pad_end[-1], tile_expert, last_used)
    x_sorted = h2.at[src].get(mode="promise_in_bounds")
    y_sorted = _experts(x_sorted, tile_expert, n_tiles, *w_exp, layer)
    pos2 = pos.reshape(2, t)
    return (y_sorted.at[pos2[0]].get(mode="promise_in_bounds"),
            y_sorted.at[pos2[1]].get(mode="promise_in_bounds"))


def _rope_tables(n_tokens):
    rows = n_tokens // GRID_W
    row = jnp.repeat(jnp.arange(rows, dtype=F32), GRID_W)
    col = jnp.tile(jnp.arange(GRID_W, dtype=F32), rows)
    inv = 1.0 / (ROPE_THETA ** (jnp.arange(0, ROPE_AXIS_DIM, 2, dtype=F32) / ROPE_AXIS_DIM))
    ar = row[:, None] * inv
    ac = col[:, None] * inv
    ang = jnp.concatenate([ar, ar, ac, ac], axis=-1)
    cos, sin = jnp.cos(ang), jnp.sin(ang)
    cos = jnp.concatenate([jnp.ones((TM, HEAD_DIM), F32), cos], axis=0)
    sin = jnp.concatenate([jnp.zeros((TM, HEAD_DIM), F32), sin], axis=0)
    return jnp.tile(cos, (1, 2)), jnp.tile(sin, (1, 2))


def _block_meta(n_ctx, ctx_len, n_lat, lat_len):
    ctx_blocks = n_ctx * ctx_len // TM
    lat_blocks = n_lat * lat_len // TM
    per_lat = lat_len // TM
    bi = jnp.arange(ctx_blocks + lat_blocks, dtype=jnp.int32)
    is_ctx = bi < ctx_blocks
    lat_i = bi - ctx_blocks
    mrow = jnp.where(is_ctx, 0, 1 + lat_i // per_lat).astype(jnp.int32)
    rope_blk = jnp.where(is_ctx, 0, 1 + lat_i % per_lat).astype(jnp.int32)
    cos, sin = _rope_tables(lat_len)
    ti = jnp.arange(n_ctx * ctx_len + n_lat * lat_len, dtype=jnp.int32)
    pos = jnp.where(ti < n_ctx * ctx_len, ti % ctx_len, (ti - n_ctx * ctx_len) % lat_len)
    slen = jnp.where(ti < n_ctx * ctx_len, ctx_len, lat_len)
    keep_prev = jnp.broadcast_to((pos != 0).astype(F32)[:, None], (ti.shape[0], LANES))
    keep_next = jnp.broadcast_to((pos != slen - 1).astype(F32)[:, None], (ti.shape[0], LANES))
    meta = {"mrow": mrow, "rope_blk": rope_blk, "cos": cos, "sin": sin,
            "ctx_blk": jnp.minimum(bi, ctx_blocks - 1).astype(jnp.int32),
            "keep_prev": keep_prev, "keep_next": keep_next}

    nc_ctx, nc_lat = ctx_len // SSD_CHUNK, lat_len // SSD_CHUNK
    n_ctx_steps = n_ctx * nc_ctx
    si = jnp.arange(n_ctx_steps + n_lat * nc_lat, dtype=jnp.int32)
    s_ctx = si < n_ctx_steps
    li = si - n_ctx_steps
    seq = jnp.where(s_ctx, si // nc_ctx, li // nc_lat)
    ch = jnp.where(s_ctx, si % nc_ctx, li % nc_lat)
    nc = jnp.where(s_ctx, nc_ctx, nc_lat)
    base = jnp.where(s_ctx, seq * nc_ctx, n_ctx_steps + seq * nc_lat)
    meta.update({
        "ssd_fblk": (base + ch).astype(jnp.int32),
        "ssd_bblk": (base + nc - 1 - ch).astype(jnp.int32),
        "ssd_first": (ch == 0).astype(jnp.int32),
        "ssd_last": (ch == nc - 1).astype(jnp.int32),
        "ssd_init": jnp.where(s_ctx, 0, 1).astype(jnp.int32),
        "ssd_s0i": jnp.where(s_ctx, 0, seq).astype(jnp.int32),
        "ssd_sfi": jnp.where(s_ctx, seq, n_ctx - 1).astype(jnp.int32),
    })
    return meta


def kernel(x_prompt, x_sample, cache_k, cache_v, state_ssm, c, c_ctx, norm1_w, norm2_w, w_mod, b_mod,
           w_in, q_norm_w, k_norm_w, conv_w, conv_b, a_log, dt_bias, ssm_d, ssm_norm_w, w_attn_out,
           w_ssm_out, w_out, w_router, router_bias, w_exp_gate, w_exp_up, w_exp_down, final_norm_w):
    n_ctx, ctx_len, _ = x_prompt.shape
    n_lat, lat_len, _ = x_sample.shape
    depth = w_in.shape[0]
    t_ctx = n_ctx * ctx_len
    t_lat = n_lat * lat_len
    assert t_ctx % TM == 0 and lat_len % TM == 0 and 1 + n_lat <= COND_ROWS
    assert ctx_len % TK == 0 and cache_k.shape[2] % TK == 0 and lat_len % GRID_W == 0

    meta = _block_meta(n_ctx, ctx_len, n_lat, lat_len)
    x = jnp.concatenate([x_prompt.reshape(t_ctx, D_MODEL), x_sample.reshape(t_lat, D_MODEL)], axis=0)

    cond = jnp.zeros((COND_ROWS, D_MODEL), F32).at[0].set(c_ctx).at[1:1 + n_lat].set(c)
    mod_all = _adaln(cond, w_mod, b_mod).reshape(depth, COND_ROWS, N_MOD, D_MODEL)
    mod_all = jnp.pad(mod_all, ((0, 0), (0, 0), (0, MOD_ROWS - N_MOD), (0, 0)))

    head_of_lane = jnp.arange(D_INNER, dtype=jnp.int32) // SSM_HEAD_DIM
    lane_id = jnp.arange(LANES, dtype=jnp.int32)
    expand = jnp.stack([(lane_id[:, None] == head_of_lane[None, :] + d * SSM_HEADS) for d in range(2)]
                       ).astype(BF16)
    wrt = w_router.T
    rb = router_bias.reshape(N_EXPERTS, 1)
    fnw = final_norm_w.reshape(1, D_MODEL)

    new_k, new_v, new_s = [], [], []
    for l in range(depth):
        wl = w_in[l]
        lw = {
            "norm1_w": norm1_w[l].reshape(1, D_MODEL),
            "norm2_w": norm2_w[l].reshape(1, D_MODEL),
            "wqkv": wl[:, :Z_OFF].astype(BF16),
            "wz": wl[:, Z_OFF:XBC_OFF].astype(BF16),
            "wxbc": wl[:, XBC_OFF:DT_OFF].astype(BF16),
            "wdt": jnp.pad(wl[:, DT_OFF:G_OFF], ((0, 0), (0, LANES - 2 * SSM_HEADS))).astype(BF16),
            "wg": wl[:, G_OFF:].astype(BF16),
            "q_norm_w": jnp.tile(q_norm_w[l], 4).reshape(1, 2 * LANES),
            "k_norm_w": jnp.tile(k_norm_w[l], 4).reshape(1, 2 * LANES),
            "conv_w": conv_w[l],
            "conv_b": conv_b[l].reshape(1, CONV_DIM),
            "dt_bias": jnp.pad(dt_bias[l].reshape(1, 2 * SSM_HEADS), ((0, 0), (0, LANES - 2 * SSM_HEADS))),
            "ssm_d": jnp.repeat(ssm_d[l], SSM_HEAD_DIM).reshape(1, D_INNER),
            "ssm_norm_w": ssm_norm_w[l].reshape(1, D_INNER),
            "w_attn_out": w_attn_out[l].astype(BF16),
            "w_ssm_out": w_ssm_out[l].astype(BF16),
            "w_out": w_out[l].astype(BF16),
        }
        mod = mod_all[l]
        q, kn, vn, k_ctx, v_ctx, z, xs, bc, dt, g = _in_proj(x, mod, meta, lw, t_ctx)

        attn = _attention(q, kn, vn, None, None, n_ctx, ctx_len, 0, min(TQ, ctx_len))
        attn = _attention(q, kn, vn, _cache_blocks(cache_k[:, l], cache_v[:, l]), attn,
                          n_lat, lat_len, t_ctx, min(TQ, lat_len))

        a_lanes = jnp.pad(-jnp.exp(a_log[l]).reshape(1, 2 * SSM_HEADS),
                          ((0, 0), (0, LANES - 2 * SSM_HEADS)))
        s0 = state_ssm[:, l].transpose(0, 1, 4, 2, 3).reshape(n_lat, 2, D_STATE, D_INNER)
        yf, yb, s_ctx = _ssd(xs, bc, dt, a_lanes, expand, s0, meta, n_ctx)

        x1, h2, idx, gate = _merge(x, attn, yf, yb, xs, z, g, mod, meta, lw, wrt, rb)
        y0, y1 = _moe(h2, idx, gate, (w_exp_gate, w_exp_up, w_exp_down), l)
        if l < depth - 1:
            x = _combine(x1, y0, y1, gate.T, mod, meta, fnw, None)
        else:
            y_prompt, y_sample = _combine(x1, y0, y1, gate.T, mod, meta, fnw, t_ctx)

        new_k.append(k_ctx.reshape(n_ctx, ctx_len, N_KV_HEADS, HEAD_DIM))
        new_v.append(v_ctx.reshape(n_ctx, ctx_len, N_KV_HEADS, HEAD_DIM))
        new_s.append(s_ctx.reshape(n_ctx, 2, D_STATE, SSM_HEADS, SSM_HEAD_DIM).transpose(0, 1, 3, 4, 2))

    y_prompt = y_prompt.reshape(n_ctx, ctx_len, D_MODEL)
    y_sample = y_sample.reshape(n_lat, lat_len, D_MODEL)
    return (y_prompt, y_sample, jnp.stack(new_k, axis=1), jnp.stack(new_v, axis=1),
            jnp.stack(new_s, axis=1))
```

```python
import functools

import jax
import jax.numpy as jnp
from jax import lax
from jax.experimental import pallas as pl
from jax.experimental.pallas import tpu as pltpu

F32 = jnp.float32
BF16 = jnp.bfloat16

EPS = 1e-6
D_MODEL = 1024
N_HEADS = 16
N_KV_HEADS = 4
HEAD_DIM = 64
ATTN_WIDTH = N_HEADS * HEAD_DIM
KV_WIDTH = N_KV_HEADS * HEAD_DIM
GRID_W = 64
ROPE_AXIS_DIM = HEAD_DIM // 2
ROPE_THETA = 10000.0
D_INNER = 2 * D_MODEL
SSM_HEAD_DIM = 64
SSM_HEADS = D_INNER // SSM_HEAD_DIM
SSM_GROUPS = 4
HEADS_PER_GROUP = SSM_HEADS // SSM_GROUPS
D_STATE = 128
GN = SSM_GROUPS * D_STATE
CONV_DIM = D_INNER + 2 * GN
SSD_CHUNK = 128
N_EXPERTS = 16
N_EXPERT_GROUPS = 4
EXPERTS_PER_GROUP = N_EXPERTS // N_EXPERT_GROUPS
D_FF_EXPERT = 512
N_MOD = 6
MOD_ROWS = 8

LANES = 128
SUBLANES = 8
VMEM_LIMIT = 56 * 1024 * 1024

TM = 512
TQ = 512
TK = 256
TE = 256
RANK_BLOCK = 512
CONV_CHUNK = 1024
COND_ROWS = 16
V_ROWS = HEAD_DIM + SUBLANES
LOG2E = 1.4426950408889634
Q_SCALE = HEAD_DIM ** -0.5 * LOG2E

Q_OFF, K_OFF, V_OFF = 0, ATTN_WIDTH, ATTN_WIDTH + KV_WIDTH
Z_OFF = ATTN_WIDTH + 2 * KV_WIDTH
XBC_OFF = Z_OFF + D_INNER
DT_OFF = XBC_OFF + CONV_DIM
G_OFF = DT_OFF + 2 * SSM_HEADS
N_IN = G_OFF + 2 * D_MODEL


def _cparams(sem):
    return pltpu.CompilerParams(dimension_semantics=sem, vmem_limit_bytes=VMEM_LIMIT)


def _const_spec(shape):
    nd = len(shape)
    return pl.BlockSpec(shape, lambda *_: (0,) * nd, pipeline_mode=pl.Buffered(1))


def _dot(a, b):
    return jnp.dot(a, b, preferred_element_type=F32)


def _dot_nt(a, b):
    return lax.dot_general(a, b, (((1,), (1,)), ((), ())), preferred_element_type=F32)


def _dot_tn(a, b):
    return lax.dot_general(a, b, (((0,), (0,)), ((), ())), preferred_element_type=F32)


def _split3(a):
    a1 = a.astype(BF16)
    r = a - a1.astype(F32)
    a2 = r.astype(BF16)
    a3 = (r - a2.astype(F32)).astype(BF16)
    return a1, a2, a3


def _dot_exact_lhs(m_bf16, a):
    a1, a2, a3 = _split3(a)
    return _dot(m_bf16, a1) + _dot(m_bf16, a2) + _dot(m_bf16, a3)


def _modnorm(x, w, sc, sh):
    ms = jnp.mean(x * x, axis=-1, keepdims=True)
    return (x * lax.rsqrt(ms + EPS) * w) * (1.0 + sc) + sh


def _sigmoid(x):
    return 1.0 / (1.0 + jnp.exp(-x))


def _silu(x):
    return x * _sigmoid(x)


def _adaln_kernel(c_ref, w_ref, b_ref, o_ref):
    cs = _silu(c_ref[...])
    o_ref[0] = jnp.dot(cs, w_ref[0], preferred_element_type=F32,
                       precision=lax.Precision.HIGHEST) + b_ref[0]


def _adaln(cond, w_mod, b_mod):
    depth = w_mod.shape[0]
    nb = N_MOD
    return pl.pallas_call(
        _adaln_kernel,
        grid=(depth, nb),
        in_specs=[pl.BlockSpec((COND_ROWS, D_MODEL), lambda l, j: (0, 0)),
                  pl.BlockSpec((1, D_MODEL, D_MODEL), lambda l, j: (l, 0, j)),
                  pl.BlockSpec((1, 1, D_MODEL), lambda l, j: (l, 0, j))],
        out_specs=pl.BlockSpec((1, COND_ROWS, D_MODEL), lambda l, j: (l, 0, j)),
        out_shape=jax.ShapeDtypeStruct((depth, COND_ROWS, N_MOD * D_MODEL), F32),
        compiler_params=_cparams(("arbitrary", "arbitrary")),
        name="adaln",
    )(cond, w_mod, b_mod.reshape(depth, 1, N_MOD * D_MODEL))


def _in_proj_kernel(mrow_ref, rope_ref, ctxi_ref,
                    x_ref, xp_ref, xn_ref, mod_ref, n1w_ref,
                    wqkv_ref, wz_ref, wxbc_ref, wdt_ref, wg_ref,
                    qnw_ref, knw_ref, cos_ref, sin_ref, kprev_ref, knext_ref,
                    convw_ref, convb_ref, dtb_ref,
                    *rest, n_ctx_blocks, ctx_len):
    q_ref, kb_ref, vt_ref, kc_ref, vc_ref, z_ref, xs_ref, bc_ref, dt_ref, g_ref = rest[-10:]
    del mrow_ref, rope_ref, ctxi_ref
    i = pl.program_id(0)
    is_ctx = i < n_ctx_blocks
    sh1 = mod_ref[0, 0:1, :]
    sc1 = mod_ref[0, 1:2, :]
    nw = n1w_ref[...]
    h = _modnorm(x_ref[...], nw, sc1, sh1).astype(BF16)
    hp = _modnorm(xp_ref[...], nw, sc1, sh1).astype(BF16)
    hn = _modnorm(xn_ref[...], nw, sc1, sh1).astype(BF16)

    r = lax.broadcasted_iota(jnp.int32, (2 * LANES, 2 * LANES), 0) // HEAD_DIM
    c = lax.broadcasted_iota(jnp.int32, (2 * LANES, 2 * LANES), 1) // HEAD_DIM
    same_head = jnp.where(r == c, 1.0, 0.0).astype(BF16)
    lane = lax.broadcasted_iota(jnp.int32, (TM, LANES), 1)
    first_half = (lane % (HEAD_DIM // 2)) < (HEAD_DIM // 4)

    def head_norm_rope(t, w, scale):
        sq = t * t
        hi = sq.astype(BF16)
        lo = (sq - hi.astype(F32)).astype(BF16)
        ssum = _dot(hi, same_head) + _dot(lo, same_head)
        tn = t * lax.rsqrt(ssum * (1.0 / HEAD_DIM) + EPS) * w
        outs = []
        for s in range(2):
            a = tn[:, s * LANES:(s + 1) * LANES]
            rot = jnp.where(first_half, -pltpu.roll(a, LANES - HEAD_DIM // 4, 1),
                            pltpu.roll(a, HEAD_DIM // 4, 1))
            outs.append((a * cos_ref[...] + rot * sin_ref[...]) * scale)
        return outs

    qw = qnw_ref[...]
    kw = knw_ref[...]
    ones_rows = jnp.where(lax.broadcasted_iota(jnp.int32, (V_ROWS - HEAD_DIM, TK), 0) == 0,
                          1.0, 0.0).astype(BF16)
    cw = CONV_CHUNK
    rows = lax.broadcasted_iota(jnp.int32, (TM, cw), 0)
    kprev = jnp.concatenate([kprev_ref[...]] * (cw // LANES), axis=1)
    knext = jnp.concatenate([knext_ref[...]] * (cw // LANES), axis=1)

    def q_stage(cb):
        def mm():
            return _dot(h, wqkv_ref[:, cb * 2 * LANES:(cb + 1) * 2 * LANES])

        def fin(t):
            a, b = head_norm_rope(t, qw, Q_SCALE)
            q_ref[:, cb * 2 * LANES:cb * 2 * LANES + LANES] = a.astype(BF16)
            q_ref[:, cb * 2 * LANES + LANES:(cb + 1) * 2 * LANES] = b.astype(BF16)
        return mm, fin

    def k_fin(t):
        kpairs = head_norm_rope(t, kw, 1.0)
        for rb in range(TM // TK):
            for pr in range(2):
                kb_ref[pr, rb] = kpairs[pr][rb * TK:(rb + 1) * TK, :].astype(BF16)

        @pl.when(is_ctx)
        def _():
            for sq in range(TM // ctx_len):
                for pr in range(2):
                    kc_ref[sq, :, pr * LANES:(pr + 1) * LANES] = kpairs[pr][sq * ctx_len:(sq + 1) * ctx_len, :]

    def v_fin(v):
        for rb in range(TM // TK):
            vT = v[rb * TK:(rb + 1) * TK, :].T
            for hd in range(N_KV_HEADS):
                vt_ref[hd // 2, hd % 2, rb, 0:HEAD_DIM, :] = (
                    vT[hd * HEAD_DIM:(hd + 1) * HEAD_DIM, :].astype(BF16))
                vt_ref[hd // 2, hd % 2, rb, HEAD_DIM:V_ROWS, :] = ones_rows

        @pl.when(is_ctx)
        def _():
            for sq in range(TM // ctx_len):
                vc_ref[sq] = v[sq * ctx_len:(sq + 1) * ctx_len, :]

    def z_fin(t):
        z_ref[...] = t.astype(BF16)

    def g_fin(t):
        g_ref[...] = _sigmoid(t).astype(BF16)

    def dt_fin(t):
        dtr = t + dtb_ref[...]
        dt_ref[...] = jnp.maximum(dtr, 0.0) + jnp.log(1.0 + jnp.exp(-jnp.abs(dtr)))

    def conv_stage(cb):
        sl = slice(cb * cw, (cb + 1) * cw)

        def mm():
            return (_dot(h, wxbc_ref[:, sl]), _dot(hp, wxbc_ref[:, sl]), _dot(hn, wxbc_ref[:, sl]))

        def fin(res):
            pre, pp, pn = res
            prev_row = pp[SUBLANES - 1:SUBLANES, :]
            next_row = pn[0:1, :]
            up = jnp.where(rows == 0, prev_row, pltpu.roll(pre, 1, 0)) * kprev
            down = jnp.where(rows == TM - 1, next_row, pltpu.roll(pre, TM - 1, 0)) * knext
            y = (convb_ref[:, sl] + up * convw_ref[0:1, sl] + pre * convw_ref[1:2, sl]
                 + down * convw_ref[2:3, sl])
            y = _silu(y).astype(BF16)
            lo = cb * cw
            if lo < D_INNER:
                xs_ref[:, lo:lo + cw] = y
            else:
                bc_ref[:, lo - D_INNER:lo - D_INNER + cw] = y
        return mm, fin

    stages = [q_stage(cb) for cb in range(ATTN_WIDTH // (2 * LANES))]
    stages.append((lambda: _dot(h, wqkv_ref[:, K_OFF:K_OFF + KV_WIDTH]), k_fin))
    stages.append((lambda: _dot(h, wqkv_ref[:, V_OFF:V_OFF + KV_WIDTH]), v_fin))
    stages.append((lambda: _dot(h, wdt_ref[...]), dt_fin))
    stages += [conv_stage(cb) for cb in range(CONV_DIM // cw)]
    stages.append((lambda: _dot(h, wz_ref[...]), z_fin))
    stages.append((lambda: _dot(h, wg_ref[...]), g_fin))
    pending = None
    for mm, fin in stages:
        res = mm()
        if pending is not None:
            pending[0](pending[1])
        pending = (fin, res)
    pending[0](pending[1])


def _in_proj(x, mod, meta, lw, t_ctx, ctx_len, layer, depth, prev_kv):
    t = x.shape[0]
    nblk = t // TM
    nhalo = t // SUBLANES
    per_halo = TM // SUBLANES
    n_ctx_blocks = t_ctx // TM
    nkb = TM // TK

    def row_blk(width):
        return pl.BlockSpec((TM, width), lambda i, *_: (i, 0))

    spb = TM // ctx_len
    ctx_blk = pl.BlockSpec((spb, None, ctx_len, KV_WIDTH),
                           lambda i, mrow, rope, ctxi: (ctxi[i], layer, 0, 0))
    in_specs = [
        row_blk(D_MODEL),
            pl.BlockSpec((SUBLANES, D_MODEL),
                         lambda i, *_: (jnp.maximum(i * per_halo - 1, 0), 0)),
            pl.BlockSpec((SUBLANES, D_MODEL),
                         lambda i, *_: (jnp.minimum((i + 1) * per_halo, nhalo - 1), 0)),
            pl.BlockSpec((1, MOD_ROWS, D_MODEL), lambda i, mrow, *_: (mrow[i], 0, 0)),
            _const_spec((1, D_MODEL)),
            _const_spec((D_MODEL, ATTN_WIDTH + 2 * KV_WIDTH)),
            _const_spec((D_MODEL, D_INNER)),
            _const_spec((D_MODEL, CONV_DIM)),
            _const_spec((D_MODEL, LANES)),
            _const_spec((D_MODEL, 2 * D_MODEL)),
            _const_spec((1, 2 * LANES)),
            _const_spec((1, 2 * LANES)),
            pl.BlockSpec((TM, LANES), lambda i, mrow, rope, *_: (rope[i], 0)),
            pl.BlockSpec((TM, LANES), lambda i, mrow, rope, *_: (rope[i], 0)),
            row_blk(LANES),
            row_blk(LANES),
            _const_spec((3, CONV_DIM)),
            _const_spec((1, CONV_DIM)),
            _const_spec((1, LANES)),
    ]
    out_specs = [row_blk(ATTN_WIDTH),
                   pl.BlockSpec((2, nkb, TK, LANES), lambda i, *_: (0, i, 0, 0)),
                   pl.BlockSpec((2, 2, nkb, V_ROWS, TK), lambda i, *_: (0, 0, i, 0, 0)),
                   ctx_blk, ctx_blk,
                   row_blk(D_INNER), row_blk(D_INNER), row_blk(2 * GN), row_blk(LANES),
                   row_blk(2 * D_MODEL)]
    out_shape = [
        jax.ShapeDtypeStruct((t, ATTN_WIDTH), BF16),
        jax.ShapeDtypeStruct((2, t // TK, TK, LANES), BF16),
        jax.ShapeDtypeStruct((2, 2, t // TK, V_ROWS, TK), BF16),
        jax.ShapeDtypeStruct((t_ctx // ctx_len, depth, ctx_len, KV_WIDTH), F32),
        jax.ShapeDtypeStruct((t_ctx // ctx_len, depth, ctx_len, KV_WIDTH), F32),
        jax.ShapeDtypeStruct((t, D_INNER), BF16),
        jax.ShapeDtypeStruct((t, D_INNER), BF16),
        jax.ShapeDtypeStruct((t, 2 * GN), BF16),
        jax.ShapeDtypeStruct((t, LANES), F32),
        jax.ShapeDtypeStruct((t, 2 * D_MODEL), BF16),
    ]
    args = [meta["mrow"], meta["rope_blk"], meta["ctx_blk"],
            x, x, x, mod, lw["norm1_w"], lw["wqkv"], lw["wz"], lw["wxbc"], lw["wdt"], lw["wg"],
            lw["q_norm_w"], lw["k_norm_w"], meta["cos"], meta["sin"], meta["keep_prev"],
            meta["keep_next"], lw["conv_w"], lw["conv_b"], lw["dt_bias"]]
    aliases = {}
    if prev_kv is not None:
        in_specs += [pl.BlockSpec(memory_space=pl.ANY)] * 2
        aliases = {len(args): 3, len(args) + 1: 4}
        args += list(prev_kv)
    grid_spec = pltpu.PrefetchScalarGridSpec(num_scalar_prefetch=3, grid=(nblk,), in_specs=in_specs,
                                             out_specs=out_specs)
    return pl.pallas_call(
        functools.partial(_in_proj_kernel, n_ctx_blocks=n_ctx_blocks, ctx_len=ctx_len),
        grid_spec=grid_spec, out_shape=out_shape, input_output_aliases=aliases,
        compiler_params=_cparams(("arbitrary",)), name="in_proj",
    )(*args)


def _attn_kernel(*refs, n_new, n_cache, tq):
    q_ref, kn_ref, vn_ref = refs[0:3]
    if n_cache:
        kc_ref, vc_ref = refs[3:5]
    o_ref, qt_ref, m_ref, acc_ref, sa_ref, mxa_ref, sb_ref, mxb_ref = refs[-8:]
    nq = 4 * tq
    zeros_half = jnp.zeros((HEAD_DIM, tq), F32)
    for hh in range(2):
        cols = []
        for jj in range(2):
            off = 2 * LANES * hh + LANES * jj
            qt = q_ref[:, off:off + LANES].astype(F32).T
            for s in range(2):
                head = qt[s * HEAD_DIM:(s + 1) * HEAD_DIM, :]
                parts = [head, zeros_half] if hh == 0 else [zeros_half, head]
                cols.append(jnp.concatenate(parts, axis=0))
        qt_ref[hh] = jnp.concatenate(cols, axis=1).astype(BF16)
        m_ref[hh] = jnp.full((1, nq), -jnp.inf, F32)
        acc_ref[hh] = jnp.zeros((V_ROWS, nq), F32)

    def scores(kblk, s_ref, mx_ref):
        for hh in range(2):
            s = _dot(kblk, qt_ref[hh])
            s_ref[hh] = s
            mx_ref[hh] = jnp.max(s, axis=0, keepdims=True)

    def consume(vt_of, s_ref, mx_ref):
        for hh in range(2):
            m = m_ref[hh]
            m_new = jnp.maximum(m, mx_ref[hh])
            alpha = jnp.exp2(m - m_new)
            p = jnp.exp2(s_ref[hh] - m_new).astype(BF16)
            acc_ref[hh] = alpha * acc_ref[hh] + _dot(vt_of(hh), p)
            m_ref[hh] = m_new

    bufs = ((sa_ref, mxa_ref), (sb_ref, mxb_ref))
    scores(kn_ref[0], *bufs[0])

    def body(j, carry):
        scores(kn_ref[2 * j + 1], *bufs[1])
        consume(lambda hh: vn_ref[hh, 2 * j], *bufs[0])
        scores(kn_ref[2 * j + 2], *bufs[0])
        consume(lambda hh: vn_ref[hh, 2 * j + 1], *bufs[1])
        return carry

    n_loop = (n_new - 1) // 2
    lax.fori_loop(0, n_loop, body, 0)
    rest = [(kn_ref, vn_ref, b) for b in range(2 * n_loop, n_new)]
    if n_cache:
        rest += [(kc_ref, vc_ref, b) for b in range(n_cache)]
    for r, (_, v_src, b) in enumerate(rest):
        if r + 1 < len(rest):
            k_nxt, _, b_nxt = rest[r + 1]
            scores(k_nxt[b_nxt], *bufs[(r + 1) % 2])
        consume(lambda hh, v_src=v_src, b=b: v_src[hh, b], *bufs[r % 2])

    for hh in range(2):
        oT = acc_ref[hh]
        oT = oT[0:HEAD_DIM, :] / oT[HEAD_DIM:HEAD_DIM + 1, :]
        for jj in range(2):
            pair = jnp.concatenate([oT[:, (2 * jj) * tq:(2 * jj + 1) * tq],
                                    oT[:, (2 * jj + 1) * tq:(2 * jj + 2) * tq]], axis=0)
            off = 2 * LANES * hh + LANES * jj
            o_ref[:, off:off + LANES] = pair.T.astype(BF16)


def _attention(q, kn, vn, cache, prev_out, nseq, seq_len, row0, tq):
    n_new = seq_len // TK
    nqb = seq_len // tq
    nq = 4 * tq
    q0 = row0 // tq
    kv0 = row0 // (n_new * TK)
    assert row0 % tq == 0 and row0 % (n_new * TK) == 0
    qspec = pl.BlockSpec((tq, 4 * LANES), lambda b, p, i: (q0 + b * nqb + i, p))
    in_specs = [
        qspec,
        pl.BlockSpec((None, n_new, TK, LANES), lambda b, p, i: (p, kv0 + b, 0, 0)),
        pl.BlockSpec((None, 2, n_new, V_ROWS, TK), lambda b, p, i: (p, 0, kv0 + b, 0, 0)),
    ]
    args = [q, kn, vn]
    n_cache = 0
    if cache is not None:
        kc, vc = cache
        n_cache = kc.shape[1] // nseq
        in_specs += [pl.BlockSpec((None, n_cache, TK, LANES), lambda b, p, i: (p, b, 0, 0)),
                     pl.BlockSpec((None, 2, n_cache, V_ROWS, TK), lambda b, p, i: (p, 0, b, 0, 0))]
        args += [kc, vc]
    aliases = {}
    kern = functools.partial(_attn_kernel, n_new=n_new, n_cache=n_cache, tq=tq)
    if prev_out is not None:
        n_in = len(args)
        in_specs.append(pl.BlockSpec(memory_space=pl.ANY))
        aliases = {n_in: 0}
        args.append(prev_out)
        inner = kern
        kern = lambda *refs: inner(*refs[:n_in], *refs[n_in + 1:])
    return pl.pallas_call(
        kern,
        grid=(nseq, 2, nqb),
        in_specs=in_specs,
        out_specs=qspec,
        out_shape=jax.ShapeDtypeStruct(q.shape, BF16),
        scratch_shapes=[pltpu.VMEM((2, LANES, nq), BF16),
                        pltpu.VMEM((2, 1, nq), F32),
                        pltpu.VMEM((2, V_ROWS, nq), F32),
                        pltpu.VMEM((2, TK, nq), F32),
                        pltpu.VMEM((2, 1, nq), F32),
                        pltpu.VMEM((2, TK, nq), F32),
                        pltpu.VMEM((2, 1, nq), F32)],
        input_output_aliases=aliases,
        compiler_params=_cparams(("arbitrary", "arbitrary", "arbitrary")),
        name="attention",
    )(*args)


def _cache_blocks(ck, cv):
    nseq, past = ck.shape[0], ck.shape[1]
    n = past // TK
    kb = ck.astype(BF16).reshape(nseq * n, TK, 2, LANES).transpose(2, 0, 1, 3)
    vb = cv.astype(BF16).reshape(nseq * n, TK, 2, 2, HEAD_DIM).transpose(2, 3, 0, 4, 1)
    ones = jnp.ones(vb.shape[:3] + (1, TK), BF16)
    zeros = jnp.zeros(vb.shape[:3] + (V_ROWS - HEAD_DIM - 1, TK), BF16)
    return kb, jnp.concatenate([vb, ones, zeros], axis=3)


def _ssd_direction(x_ref, bc_ref, dt_ref, a_lanes, expand, st_ref, y_ref, d):
    q = SSD_CHUNK
    row = lax.broadcasted_iota(jnp.int32, (q, q), 0)
    col = lax.broadcasted_iota(jnp.int32, (q, q), 1)
    causal = (row >= col) if d == 0 else (row <= col)
    tril = jnp.where(row >= col, 1.0, 0.0).astype(BF16)

    dt = dt_ref[...]
    a = dt * a_lanes
    prefix = _dot_exact_lhs(tril, a)
    if d == 0:
        cs = prefix
        last = cs[q - 1:q, :]
    else:
        cs = prefix[q - 1:q, :] - prefix + a
        last = cs[0:1, :]
    cs2 = cs * LOG2E
    cst2 = (cs2 - jnp.log2(dt)).T
    wj = jnp.exp(last - cs) * dt
    wj_x = _dot(wj.astype(BF16), expand)
    elast = jnp.exp(last)

    lane = lax.broadcasted_iota(jnp.int32, (q, LANES), 1)
    lo_half = lane < SSM_HEAD_DIM
    lo_half_row = lo_half[0:1, :]
    for g in range(SSM_GROUPS):
        bg = bc_ref[:, g * D_STATE:(g + 1) * D_STATE]
        cg = bc_ref[:, GN + g * D_STATE:GN + (g + 1) * D_STATE]
        gmat = _dot_nt(cg, bg)
        gsl = slice(g * HEADS_PER_GROUP * SSM_HEAD_DIM, (g + 1) * HEADS_PER_GROUP * SSM_HEAD_DIM)
        st = st_ref[d, :, gsl]
        y_inter = _dot(cg, st.astype(BF16))
        xg = x_ref[:, gsl]
        el_parts = []
        for hp in range(HEADS_PER_GROUP // 2):
            xpair = xg[:, hp * LANES:(hp + 1) * LANES]
            l0 = d * SSM_HEADS + g * HEADS_PER_GROUP + 2 * hp
            ws, es = [], []
            for s in range(2):
                hl = l0 + s
                csb = jnp.broadcast_to(cs2[:, hl:hl + 1], (q, q))
                seg = csb - cst2[hl:hl + 1, :]
                ws.append(gmat * jnp.exp2(jnp.where(causal, seg, -jnp.inf)))
                es.append(jnp.exp2(csb))
            wcat = jnp.concatenate(ws, axis=1).astype(BF16)
            zero = jnp.zeros_like(xpair)
            xm = jnp.concatenate([jnp.where(lo_half, xpair, zero), jnp.where(lo_half, zero, xpair)],
                                 axis=0)
            e_pair = jnp.where(lo_half, es[0], es[1])
            ypair = y_inter[:, hp * LANES:(hp + 1) * LANES] * e_pair + _dot(wcat, xm)
            lo = g * HEADS_PER_GROUP * SSM_HEAD_DIM + hp * LANES
            y_ref[:, lo:lo + LANES] = ypair.astype(y_ref.dtype)
            el_parts.append(jnp.where(lo_half_row, elast[:, l0:l0 + 1], elast[:, l0 + 1:l0 + 2]))
        xw = (xg.astype(F32) * wj_x[:, gsl]).astype(BF16)
        el = jnp.concatenate(el_parts, axis=1)
        st_ref[d, :, gsl] = st * el + _dot_tn(bg, xw)


def _ssd_kernel(fblk_ref, bblk_ref, first_ref, last_ref, init_ref, s0i_ref, sfi_ref,
                xf_ref, bcf_ref, dtf_ref, xb_ref, bcb_ref, dtb_ref, a_ref, exp_ref, s0_ref, *rest):
    yf_ref, yb_ref, sfin_ref, st_ref = rest[-4:]
    del fblk_ref, bblk_ref, s0i_ref, sfi_ref
    s = pl.program_id(0)

    @pl.when(jnp.logical_and(first_ref[s] == 1, init_ref[s] == 1))
    def _():
        st_ref[...] = s0_ref[0]

    @pl.when(jnp.logical_and(first_ref[s] == 1, init_ref[s] == 0))
    def _():
        st_ref[...] = jnp.zeros(st_ref.shape, F32)

    a_lanes = a_ref[...]
    _ssd_direction(xf_ref, bcf_ref, dtf_ref, a_lanes, exp_ref[0], st_ref, yf_ref, 0)
    _ssd_direction(xb_ref, bcb_ref, dtb_ref, a_lanes, exp_ref[1], st_ref, yb_ref, 1)

    @pl.when(jnp.logical_and(last_ref[s] == 1, init_ref[s] == 0))
    def _():
        sfin_ref[0] = st_ref[...]


def _ssd(xs, bc, dt, a_lanes, expand, s0, meta, n_ctx, layer, depth, prev_sfin):
    t = xs.shape[0]
    n_steps = t // SSD_CHUNK
    st_shape = (2, D_STATE, D_INNER)
    fwd = lambda s, fblk, bblk, *_: (fblk[s], 0)
    bwd = lambda s, fblk, bblk, *_: (bblk[s], 0)

    def specs(idx):
        return [pl.BlockSpec((SSD_CHUNK, D_INNER), idx),
                pl.BlockSpec((SSD_CHUNK, 2 * GN), idx),
                pl.BlockSpec((SSD_CHUNK, LANES), idx)]

    in_specs = specs(fwd) + specs(bwd) + [
        pl.BlockSpec((1, LANES), lambda s, *_: (0, 0)),
        pl.BlockSpec((2, LANES, D_INNER), lambda s, *_: (0, 0, 0)),
        pl.BlockSpec((1,) + st_shape, lambda s, f, b, fi, la, ini, s0i, sfi: (s0i[s], 0, 0, 0)),
    ]
    args = [meta["ssd_fblk"], meta["ssd_bblk"], meta["ssd_first"], meta["ssd_last"], meta["ssd_init"],
            meta["ssd_s0i"], meta["ssd_sfi"], xs, bc, dt, xs, bc, dt, a_lanes, expand, s0]
    aliases = {}
    if prev_sfin is not None:
        in_specs.append(pl.BlockSpec(memory_space=pl.ANY))
        aliases = {len(args): 2}
        args.append(prev_sfin)
    grid_spec = pltpu.PrefetchScalarGridSpec(
        num_scalar_prefetch=7,
        grid=(n_steps,),
        in_specs=in_specs,
        out_specs=[pl.BlockSpec((SSD_CHUNK, D_INNER), fwd), pl.BlockSpec((SSD_CHUNK, D_INNER), bwd),
                   pl.BlockSpec((1, None) + st_shape,
                                lambda s, f, b, fi, la, ini, s0i, sfi: (sfi[s], layer, 0, 0, 0))],
        scratch_shapes=[pltpu.VMEM(st_shape, F32)],
    )
    out_shape = [jax.ShapeDtypeStruct((t, D_INNER), BF16), jax.ShapeDtypeStruct((t, D_INNER), BF16),
                 jax.ShapeDtypeStruct((n_ctx, depth) + st_shape, F32)]
    return pl.pallas_call(
        _ssd_kernel, grid_spec=grid_spec, out_shape=out_shape, input_output_aliases=aliases,
        compiler_params=_cparams(("arbitrary",)), name="ssd",
    )(*args)


def _route(sel, scores):
    neg = -jnp.inf
    rows = [sel[e:e + 1, :] for e in range(N_EXPERTS)]
    srow = [scores[e:e + 1, :] for e in range(N_EXPERTS)]
    best_score = None
    best = None
    for g in range(N_EXPERT_GROUPS):
        v = rows[g * EXPERTS_PER_GROUP:(g + 1) * EXPERTS_PER_GROUP]
        top2 = None
        for a in range(EXPERTS_PER_GROUP):
            for b in range(a + 1, EXPERTS_PER_GROUP):
                s = v[a] + v[b]
                top2 = s if top2 is None else jnp.maximum(top2, s)
        if g == 0:
            best_score, best = top2, jnp.zeros(top2.shape, jnp.int32)
        else:
            better = top2 > best_score
            best = jnp.where(better, g, best)
            best_score = jnp.where(better, top2, best_score)

    def pick(vals):
        out = vals[0]
        for g in range(1, N_EXPERT_GROUPS):
            out = jnp.where(best == g, vals[g], out)
        return out

    gsel = [pick([rows[g * EXPERTS_PER_GROUP + j] for g in range(N_EXPERT_GROUPS)])
            for j in range(EXPERTS_PER_GROUP)]
    gsc = [pick([srow[g * EXPERTS_PER_GROUP + j] for g in range(N_EXPERT_GROUPS)])
           for j in range(EXPERTS_PER_GROUP)]

    def argmax_first(vals):
        bi = jnp.zeros(vals[0].shape, jnp.int32)
        bv = vals[0]
        for j in range(1, len(vals)):
            better = vals[j] > bv
            bi = jnp.where(better, j, bi)
            bv = jnp.where(better, vals[j], bv)
        return bi

    i1 = argmax_first(gsel)
    i2 = argmax_first([jnp.where(i1 == j, neg, gsel[j]) for j in range(EXPERTS_PER_GROUP)])

    def take(vals, idx):
        out = vals[0]
        for j in range(1, len(vals)):
            out = jnp.where(idx == j, vals[j], out)
        return out

    g1 = take(gsc, i1)
    g2 = take(gsc, i2)
    tot = g1 + g2
    idx = jnp.concatenate([best * EXPERTS_PER_GROUP + i1, best * EXPERTS_PER_GROUP + i2], axis=0)
    gate = jnp.concatenate([g1 / tot, g2 / tot], axis=0)
    return idx, gate


def _merge_kernel(mrow_ref, x_ref, attn_ref, yf_ref, yb_ref, xs_ref, z_ref, g_ref, mod_ref,
                  wa_ref, ws_ref, wo_ref, dvec_ref, snw_ref, n2w_ref, wrt_ref, rb_ref,
                  x1_ref, h2_ref, idx_ref, gate_ref):
    del mrow_ref
    gw = D_INNER // SSM_GROUPS
    ssm_o = None
    for g in range(SSM_GROUPS):
        sl = slice(g * gw, (g + 1) * gw)
        xs = xs_ref[:, sl].astype(F32)
        y = yf_ref[:, sl].astype(F32) + yb_ref[:, sl].astype(F32) + dvec_ref[:, sl] * xs
        y = y * _silu(z_ref[:, sl].astype(F32))
        ms = jnp.mean(y * y, axis=-1, keepdims=True)
        yn = (y * lax.rsqrt(ms + EPS) * snw_ref[:, sl]).astype(BF16)
        part = _dot(yn, ws_ref[sl, :])
        ssm_o = part if ssm_o is None else ssm_o + part
    attn_o = _dot(attn_ref[...], wa_ref[...])
    merged = (g_ref[:, 0:D_MODEL].astype(F32) * attn_o
              + g_ref[:, D_MODEL:2 * D_MODEL].astype(F32) * ssm_o)
    out = _dot(merged.astype(BF16), wo_ref[...])
    x1 = x_ref[...] + mod_ref[0, 2:3, :] * out
    x1_ref[...] = x1
    h2 = _modnorm(x1, n2w_ref[...], mod_ref[0, 4:5, :], mod_ref[0, 3:4, :])
    h2_ref[...] = h2.astype(BF16)
    logits = lax.dot_general(wrt_ref[...], h2, (((1,), (1,)), ((), ())),
                             preferred_element_type=F32, precision=lax.Precision.HIGHEST)
    scores = _sigmoid(logits)
    idx, gate = _route(scores + rb_ref[...], scores)
    idx_ref[...] = idx
    gate_ref[...] = gate


def _merge(x, attn, yf, yb, xs, z, g, mod, meta, lw, wrt, rb):
    t = x.shape[0]
    nblk = t // TM

    def row_blk(width):
        return pl.BlockSpec((TM, width), lambda i, *_: (i, 0))

    grid_spec = pltpu.PrefetchScalarGridSpec(
        num_scalar_prefetch=1,
        grid=(nblk,),
        in_specs=[row_blk(D_MODEL), row_blk(ATTN_WIDTH), row_blk(D_INNER), row_blk(D_INNER),
                  row_blk(D_INNER), row_blk(D_INNER), row_blk(2 * D_MODEL),
                  pl.BlockSpec((1, MOD_ROWS, D_MODEL), lambda i, mrow: (mrow[i], 0, 0)),
                  _const_spec((ATTN_WIDTH, D_MODEL)), _const_spec((D_INNER, D_MODEL)),
                  _const_spec((D_MODEL, D_MODEL)), _const_spec((1, D_INNER)),
                  _const_spec((1, D_INNER)), _const_spec((1, D_MODEL)),
                  _const_spec((N_EXPERTS, D_MODEL)), _const_spec((N_EXPERTS, 1))],
        out_specs=[row_blk(D_MODEL), row_blk(D_MODEL),
                   pl.BlockSpec((2, TM), lambda i, *_: (0, i)),
                   pl.BlockSpec((2, TM), lambda i, *_: (0, i))],
    )
    out_shape = [jax.ShapeDtypeStruct((t, D_MODEL), F32), jax.ShapeDtypeStruct((t, D_MODEL), BF16),
                 jax.ShapeDtypeStruct((2, t), jnp.int32), jax.ShapeDtypeStruct((2, t), F32)]
    return pl.pallas_call(
        _merge_kernel, grid_spec=grid_spec, out_shape=out_shape,
        compiler_params=_cparams(("arbitrary",)), name="merge",
    )(meta["mrow"], x, attn, yf, yb, xs, z, g, mod, lw["w_attn_out"], lw["w_ssm_out"], lw["w_out"],
      lw["ssm_d"], lw["ssm_norm_w"], lw["norm2_w"], wrt, rb)


def _expert_kernel(te_ref, nt_ref, chg_ref, x_ref, wg_ref, wu_ref, wd_ref, o_ref,
                   wgs_ref, wus_ref, wds_ref):
    del te_ref
    i = pl.program_id(0)

    @pl.when(chg_ref[i] == 1)
    def _():
        wgs_ref[...] = wg_ref[0].astype(BF16)
        wus_ref[...] = wu_ref[0].astype(BF16)
        wds_ref[...] = wd_ref[0].astype(BF16)

    @pl.when(i < nt_ref[0])
    def _():
        x = x_ref[...]
        hmid = _silu(_dot(x, wgs_ref[...])) * _dot(x, wus_ref[...])
        o_ref[...] = _dot(hmid.astype(BF16), wds_ref[...]).astype(o_ref.dtype)

    @pl.when(i >= nt_ref[0])
    def _():
        o_ref[...] = jnp.zeros(o_ref.shape, o_ref.dtype)


def _experts(x_sorted, tile_expert, n_tiles, wg, wu, wd, layer):
    rows = x_sorted.shape[0]
    changed = jnp.concatenate([jnp.ones((1,), jnp.int32),
                               (tile_expert[1:] != tile_expert[:-1]).astype(jnp.int32)])
    grid_spec = pltpu.PrefetchScalarGridSpec(
        num_scalar_prefetch=3,
        grid=(rows // TE,),
        in_specs=[pl.BlockSpec((TE, D_MODEL), lambda i, te, nt, chg: (i, 0)),
                  pl.BlockSpec((None, 1, D_MODEL, D_FF_EXPERT), lambda i, te, nt, chg: (layer, te[i], 0, 0)),
                  pl.BlockSpec((None, 1, D_MODEL, D_FF_EXPERT), lambda i, te, nt, chg: (layer, te[i], 0, 0)),
                  pl.BlockSpec((None, 1, D_FF_EXPERT, D_MODEL), lambda i, te, nt, chg: (layer, te[i], 0, 0))],
        out_specs=pl.BlockSpec((TE, D_MODEL), lambda i, te, nt, chg: (i, 0)),
        scratch_shapes=[pltpu.VMEM((D_MODEL, D_FF_EXPERT), BF16),
                        pltpu.VMEM((D_MODEL, D_FF_EXPERT), BF16),
                        pltpu.VMEM((D_FF_EXPERT, D_MODEL), BF16)],
    )
    return pl.pallas_call(
        _expert_kernel, grid_spec=grid_spec,
        out_shape=jax.ShapeDtypeStruct((rows, D_MODEL), BF16),
        compiler_params=_cparams(("arbitrary",)), name="experts",
    )(tile_expert, n_tiles, changed, x_sorted, wg, wu, wd)


def _combine_kernel(mrow_ref, x1_ref, y0_ref, y1_ref, gate_ref, mod_ref, fnw_ref, *out_refs,
                    n_ctx_blocks):
    del mrow_ref
    moe = (gate_ref[:, 0:1] * y0_ref[...].astype(F32) + gate_ref[:, 1:2] * y1_ref[...].astype(F32))
    x2 = x1_ref[...] + mod_ref[0, 5:6, :] * moe
    if n_ctx_blocks is None:
        out_refs[0][...] = x2
        return
    ms = jnp.mean(x2 * x2, axis=-1, keepdims=True)
    y = x2 * lax.rsqrt(ms + EPS) * fnw_ref[...]
    is_ctx = pl.program_id(0) < n_ctx_blocks

    @pl.when(is_ctx)
    def _():
        out_refs[0][...] = y

    @pl.when(jnp.logical_not(is_ctx))
    def _():
        out_refs[1][...] = y


def _combine(x1, y0, y1, gate_t, mod, meta, fnw, t_ctx):
    t = x1.shape[0]

    def row_blk(width):
        return pl.BlockSpec((TM, width), lambda i, *_: (i, 0))

    if t_ctx is None:
        n_ctx_blocks = None
        out_specs = row_blk(D_MODEL)
        out_shape = jax.ShapeDtypeStruct((t, D_MODEL), F32)
    else:
        n_ctx_blocks = t_ctx // TM
        out_specs = [pl.BlockSpec((TM, D_MODEL), lambda i, *_: (jnp.minimum(i, n_ctx_blocks - 1), 0)),
                     pl.BlockSpec((TM, D_MODEL), lambda i, *_: (jnp.maximum(i - n_ctx_blocks, 0), 0))]
        out_shape = [jax.ShapeDtypeStruct((t_ctx, D_MODEL), F32),
                     jax.ShapeDtypeStruct((t - t_ctx, D_MODEL), F32)]
    grid_spec = pltpu.PrefetchScalarGridSpec(
        num_scalar_prefetch=1,
        grid=(t // TM,),
        in_specs=[row_blk(D_MODEL), row_blk(D_MODEL), row_blk(D_MODEL), row_blk(2),
                  pl.BlockSpec((1, MOD_ROWS, D_MODEL), lambda i, mrow: (mrow[i], 0, 0)),
                  _const_spec((1, D_MODEL))],
        out_specs=out_specs,
    )
    return pl.pallas_call(
        functools.partial(_combine_kernel, n_ctx_blocks=n_ctx_blocks), grid_spec=grid_spec,
        out_shape=out_shape,
        compiler_params=_cparams(("arbitrary",)), name="combine",
    )(meta["mrow"], x1, y0, y1, gate_t, mod, fnw)


def _moe(h2, idx, gate, w_exp, layer):
    t = h2.shape[0]
    n_assign = 2 * t
    rows = n_assign + N_EXPERTS * TE
    e_flat = idx.reshape(n_assign)
    onehot = e_flat[:, None] == jnp.arange(N_EXPERTS, dtype=jnp.int32)[None, :]
    nb = n_assign // RANK_BLOCK
    oh = onehot.astype(BF16).reshape(nb, RANK_BLOCK, N_EXPERTS)
    tril = jnp.tril(jnp.ones((RANK_BLOCK, RANK_BLOCK), BF16))
    local = jnp.einsum("ij,bjk->bik", tril, oh, preferred_element_type=F32)
    bsum = local[:, -1, :]
    before = jnp.dot(jnp.tril(jnp.ones((nb, nb), F32), -1), bsum, precision=lax.Precision.HIGHEST)
    running = (local + before[:, None, :]).reshape(n_assign, N_EXPERTS)
    counts = (before[-1] + bsum[-1]).astype(jnp.int32)
    padded = ((counts + TE - 1) // TE) * TE
    pad_end = jnp.cumsum(padded)
    pad_off = pad_end - padded
    off = jnp.cumsum(counts) - counts
    pos = jnp.sum(jnp.where(onehot, running + (pad_off - 1).astype(F32)[None, :], 0.0), axis=1
                  ).astype(jnp.int32)
    order = jnp.argsort(e_flat, stable=True).astype(jnp.int32)
    p = jnp.arange(rows, dtype=jnp.int32)
    ep = jnp.minimum(jnp.searchsorted(pad_end, p, side="right"), N_EXPERTS - 1).astype(jnp.int32)
    r = p - pad_off[ep]
    valid = r < counts[ep]
    src = jnp.where(valid, order[jnp.clip(off[ep] + r, 0, n_assign - 1)] % t, p % t)
    n_tiles = (pad_end[-1] // TE).astype(jnp.int32).reshape(1)
    tile_start = jnp.arange(rows // TE, dtype=jnp.int32) * TE
    tile_expert = jnp.minimum(jnp.searchsorted(pad_end, tile_start, side="right"),
                              N_EXPERTS - 1).astype(jnp.int32)
    last_used = tile_expert[jnp.maximum(n_tiles[0] - 1, 0)]
    tile_expert = jnp.where(tile_start < pad_end[-1], tile_expert, last_used)
    x_sorted = h2.at[src].get(mode="promise_in_bounds")
    y_sorted = _experts(x_sorted, tile_expert, n_tiles, *w_exp, layer)
    pos2 = pos.reshape(2, t)
    return (y_sorted.at[pos2[0]].get(mode="promise_in_bounds"),
            y_sorted.at[pos2[1]].get(mode="promise_in_bounds"))


def _rope_tables(n_tokens):
    rows = n_tokens // GRID_W
    row = jnp.repeat(jnp.arange(rows, dtype=F32), GRID_W)
    col = jnp.tile(jnp.arange(GRID_W, dtype=F32), rows)
    inv = 1.0 / (ROPE_THETA ** (jnp.arange(0, ROPE_AXIS_DIM, 2, dtype=F32) / ROPE_AXIS_DIM))
    ar = row[:, None] * inv
    ac = col[:, None] * inv
    ang = jnp.concatenate([ar, ar, ac, ac], axis=-1)
    cos, sin = jnp.cos(ang), jnp.sin(ang)
    cos = jnp.concatenate([jnp.ones((TM, HEAD_DIM), F32), cos], axis=0)
    sin = jnp.concatenate([jnp.zeros((TM, HEAD_DIM), F32), sin], axis=0)
    return jnp.tile(cos, (1, 2)), jnp.tile(sin, (1, 2))


def _block_meta(n_ctx, ctx_len, n_lat, lat_len):
    ctx_blocks = n_ctx * ctx_len // TM
    lat_blocks = n_lat * lat_len // TM
    per_lat = lat_len // TM
    bi = jnp.arange(ctx_blocks + lat_blocks, dtype=jnp.int32)
    is_ctx = bi < ctx_blocks
    lat_i = bi - ctx_blocks
    mrow = jnp.where(is_ctx, 0, 1 + lat_i // per_lat).astype(jnp.int32)
    rope_blk = jnp.where(is_ctx, 0, 1 + lat_i % per_lat).astype(jnp.int32)
    cos, sin = _rope_tables(lat_len)
    ti = jnp.arange(n_ctx * ctx_len + n_lat * lat_len, dtype=jnp.int32)
    pos = jnp.where(ti < n_ctx * ctx_len, ti % ctx_len, (ti - n_ctx * ctx_len) % lat_len)
    slen = jnp.where(ti < n_ctx * ctx_len, ctx_len, lat_len)
    keep_prev = jnp.broadcast_to((pos != 0).astype(F32)[:, None], (ti.shape[0], LANES))
    keep_next = jnp.broadcast_to((pos != slen - 1).astype(F32)[:, None], (ti.shape[0], LANES))
    meta = {"mrow": mrow, "rope_blk": rope_blk, "cos": cos, "sin": sin,
            "ctx_blk": jnp.minimum(bi, ctx_blocks - 1).astype(jnp.int32),
            "keep_prev": keep_prev, "keep_next": keep_next}

    nc_ctx, nc_lat = ctx_len // SSD_CHUNK, lat_len // SSD_CHUNK
    n_ctx_steps = n_ctx * nc_ctx
    si = jnp.arange(n_ctx_steps + n_lat * nc_lat, dtype=jnp.int32)
    s_ctx = si < n_ctx_steps
    li = si - n_ctx_steps
    seq = jnp.where(s_ctx, si // nc_ctx, li // nc_lat)
    ch = jnp.where(s_ctx, si % nc_ctx, li % nc_lat)
    nc = jnp.where(s_ctx, nc_ctx, nc_lat)
    base = jnp.where(s_ctx, seq * nc_ctx, n_ctx_steps + seq * nc_lat)
    meta.update({
        "ssd_fblk": (base + ch).astype(jnp.int32),
        "ssd_bblk": (base + nc - 1 - ch).astype(jnp.int32),
        "ssd_first": (ch == 0).astype(jnp.int32),
        "ssd_last": (ch == nc - 1).astype(jnp.int32),
        "ssd_init": jnp.where(s_ctx, 0, 1).astype(jnp.int32),
        "ssd_s0i": jnp.where(s_ctx, 0, seq).astype(jnp.int32),
        "ssd_sfi": jnp.where(s_ctx, seq, n_ctx - 1).astype(jnp.int32),
    })
    return meta


def kernel(x_prompt, x_sample, cache_k, cache_v, state_ssm, c, c_ctx, norm1_w, norm2_w, w_mod, b_mod,
           w_in, q_norm_w, k_norm_w, conv_w, conv_b, a_log, dt_bias, ssm_d, ssm_norm_w, w_attn_out,
           w_ssm_out, w_out, w_router, router_bias, w_exp_gate, w_exp_up, w_exp_down, final_norm_w):
    n_ctx, ctx_len, _ = x_prompt.shape
    n_lat, lat_len, _ = x_sample.shape
    depth = w_in.shape[0]
    t_ctx = n_ctx * ctx_len
    t_lat = n_lat * lat_len
    assert t_ctx % TM == 0 and lat_len % TM == 0 and 1 + n_lat <= COND_ROWS and TM % ctx_len == 0
    assert ctx_len % TK == 0 and cache_k.shape[2] % TK == 0 and lat_len % GRID_W == 0

    meta = _block_meta(n_ctx, ctx_len, n_lat, lat_len)
    x = jnp.concatenate([x_prompt.reshape(t_ctx, D_MODEL), x_sample.reshape(t_lat, D_MODEL)], axis=0)

    cond = jnp.zeros((COND_ROWS, D_MODEL), F32).at[0].set(c_ctx).at[1:1 + n_lat].set(c)
    mod_all = _adaln(cond, w_mod, b_mod).reshape(depth, COND_ROWS, N_MOD, D_MODEL)
    mod_all = jnp.pad(mod_all, ((0, 0), (0, 0), (0, MOD_ROWS - N_MOD), (0, 0)))

    head_of_lane = jnp.arange(D_INNER, dtype=jnp.int32) // SSM_HEAD_DIM
    lane_id = jnp.arange(LANES, dtype=jnp.int32)
    expand = jnp.stack([(lane_id[:, None] == head_of_lane[None, :] + d * SSM_HEADS) for d in range(2)]
                       ).astype(BF16)
    wrt = w_router.T
    rb = router_bias.reshape(N_EXPERTS, 1)
    fnw = final_norm_w.reshape(1, D_MODEL)

    kv_ctx, s_ctx = None, None
    for l in range(depth):
        wl = w_in[l]
        lw = {
            "norm1_w": norm1_w[l].reshape(1, D_MODEL),
            "norm2_w": norm2_w[l].reshape(1, D_MODEL),
            "wqkv": wl[:, :Z_OFF].astype(BF16),
            "wz": wl[:, Z_OFF:XBC_OFF].astype(BF16),
            "wxbc": wl[:, XBC_OFF:DT_OFF].astype(BF16),
            "wdt": jnp.pad(wl[:, DT_OFF:G_OFF], ((0, 0), (0, LANES - 2 * SSM_HEADS))).astype(BF16),
            "wg": wl[:, G_OFF:].astype(BF16),
            "q_norm_w": jnp.tile(q_norm_w[l], 4).reshape(1, 2 * LANES),
            "k_norm_w": jnp.tile(k_norm_w[l], 4).reshape(1, 2 * LANES),
            "conv_w": conv_w[l],
            "conv_b": conv_b[l].reshape(1, CONV_DIM),
            "dt_bias": jnp.pad(dt_bias[l].reshape(1, 2 * SSM_HEADS), ((0, 0), (0, LANES - 2 * SSM_HEADS))),
            "ssm_d": jnp.repeat(ssm_d[l], SSM_HEAD_DIM).reshape(1, D_INNER),
            "ssm_norm_w": ssm_norm_w[l].reshape(1, D_INNER),
            "w_attn_out": w_attn_out[l].astype(BF16),
            "w_ssm_out": w_ssm_out[l].astype(BF16),
            "w_out": w_out[l].astype(BF16),
        }
        mod = mod_all[l]
        q, kn, vn, k_ctx, v_ctx, z, xs, bc, dt, g = _in_proj(x, mod, meta, lw, t_ctx, ctx_len, l, depth,
                                                             kv_ctx)
        kv_ctx = (k_ctx, v_ctx)

        attn = _attention(q, kn, vn, None, None, n_ctx, ctx_len, 0, min(TQ, ctx_len))
        attn = _attention(q, kn, vn, _cache_blocks(cache_k[:, l], cache_v[:, l]), attn,
                          n_lat, lat_len, t_ctx, min(TQ, lat_len))

        a_lanes = jnp.pad(-jnp.exp(a_log[l]).reshape(1, 2 * SSM_HEADS),
                          ((0, 0), (0, LANES - 2 * SSM_HEADS)))
        s0 = state_ssm[:, l].transpose(0, 1, 4, 2, 3).reshape(n_lat, 2, D_STATE, D_INNER)
        yf, yb, s_ctx = _ssd(xs, bc, dt, a_lanes, expand, s0, meta, n_ctx, l, depth, s_ctx)

        x1, h2, idx, gate = _merge(x, attn, yf, yb, xs, z, g, mod, meta, lw, wrt, rb)
        y0, y1 = _moe(h2, idx, gate, (w_exp_gate, w_exp_up, w_exp_down), l)
        if l < depth - 1:
            x = _combine(x1, y0, y1, gate.T, mod, meta, fnw, None)
        else:
            y_prompt, y_sample = _combine(x1, y0, y1, gate.T, mod, meta, fnw, t_ctx)

    y_prompt = y_prompt.reshape(n_ctx, ctx_len, D_MODEL)
    y_sample = y_sample.reshape(n_lat, lat_len, D_MODEL)
    new_k = kv_ctx[0].reshape(n_ctx, depth, ctx_len, N_KV_HEADS, HEAD_DIM)
    new_v = kv_ctx[1].reshape(n_ctx, depth, ctx_len, N_KV_HEADS, HEAD_DIM)
    new_s = s_ctx.reshape(n_ctx, depth, 2, D_STATE, SSM_HEADS, SSM_HEAD_DIM).transpose(0, 1, 2, 4, 5, 3)
    return (y_prompt, y_sample, new_k, new_v, new_s)
```

```python
import functools

import jax
import jax.numpy as jnp
from jax import lax
from jax.experimental import pallas as pl
from jax.experimental.pallas import tpu as pltpu

F32 = jnp.float32
BF16 = jnp.bfloat16

EPS = 1e-6
D_MODEL = 1024
N_HEADS = 16
N_KV_HEADS = 4
HEAD_DIM = 64
ATTN_WIDTH = N_HEADS * HEAD_DIM
KV_WIDTH = N_KV_HEADS * HEAD_DIM
GRID_W = 64
ROPE_AXIS_DIM = HEAD_DIM // 2
ROPE_THETA = 10000.0
D_INNER = 2 * D_MODEL
SSM_HEAD_DIM = 64
SSM_HEADS = D_INNER // SSM_HEAD_DIM
SSM_GROUPS = 4
HEADS_PER_GROUP = SSM_HEADS // SSM_GROUPS
D_STATE = 128
GN = SSM_GROUPS * D_STATE
CONV_DIM = D_INNER + 2 * GN
SSD_CHUNK = 128
N_EXPERTS = 16
N_EXPERT_GROUPS = 4
EXPERTS_PER_GROUP = N_EXPERTS // N_EXPERT_GROUPS
D_FF_EXPERT = 512
N_MOD = 6
MOD_ROWS = 8

LANES = 128
SUBLANES = 8
VMEM_LIMIT = 56 * 1024 * 1024

TM = 512
TQ = 512
TK = 256
TE = 256
RANK_BLOCK = 512
CONV_CHUNK = 1024
COND_ROWS = 16
V_ROWS = HEAD_DIM + SUBLANES
LOG2E = 1.4426950408889634
Q_SCALE = HEAD_DIM ** -0.5 * LOG2E

Q_OFF, K_OFF, V_OFF = 0, ATTN_WIDTH, ATTN_WIDTH + KV_WIDTH
Z_OFF = ATTN_WIDTH + 2 * KV_WIDTH
XBC_OFF = Z_OFF + D_INNER
DT_OFF = XBC_OFF + CONV_DIM
G_OFF = DT_OFF + 2 * SSM_HEADS
N_IN = G_OFF + 2 * D_MODEL


def _cparams(sem):
    return pltpu.CompilerParams(dimension_semantics=sem, vmem_limit_bytes=VMEM_LIMIT)


def _const_spec(shape):
    nd = len(shape)
    return pl.BlockSpec(shape, lambda *_: (0,) * nd, pipeline_mode=pl.Buffered(1))


def _dot(a, b):
    return jnp.dot(a, b, preferred_element_type=F32)


def _dot_nt(a, b):
    return lax.dot_general(a, b, (((1,), (1,)), ((), ())), preferred_element_type=F32)


def _dot_tn(a, b):
    return lax.dot_general(a, b, (((0,), (0,)), ((), ())), preferred_element_type=F32)


def _split3(a):
    a1 = a.astype(BF16)
    r = a - a1.astype(F32)
    a2 = r.astype(BF16)
    a3 = (r - a2.astype(F32)).astype(BF16)
    return a1, a2, a3


def _dot_exact_lhs(m_bf16, a):
    a1, a2, a3 = _split3(a)
    return _dot(m_bf16, a1) + _dot(m_bf16, a2) + _dot(m_bf16, a3)


def _modnorm(x, w, sc, sh):
    ms = jnp.mean(x * x, axis=-1, keepdims=True)
    return (x * lax.rsqrt(ms + EPS) * w) * (1.0 + sc) + sh


def _sigmoid(x):
    return 1.0 / (1.0 + jnp.exp(-x))


def _silu(x):
    return x * _sigmoid(x)


def _adaln_kernel(c_ref, w_ref, b_ref, o_ref):
    cs = _silu(c_ref[...])
    o_ref[0] = jnp.dot(cs, w_ref[0], preferred_element_type=F32,
                       precision=lax.Precision.HIGHEST) + b_ref[0]


def _adaln(cond, w_mod, b_mod):
    depth = w_mod.shape[0]
    nb = N_MOD
    return pl.pallas_call(
        _adaln_kernel,
        grid=(depth, nb),
        in_specs=[pl.BlockSpec((COND_ROWS, D_MODEL), lambda l, j: (0, 0)),
                  pl.BlockSpec((1, D_MODEL, D_MODEL), lambda l, j: (l, 0, j)),
                  pl.BlockSpec((1, 1, D_MODEL), lambda l, j: (l, 0, j))],
        out_specs=pl.BlockSpec((1, COND_ROWS, D_MODEL), lambda l, j: (l, 0, j)),
        out_shape=jax.ShapeDtypeStruct((depth, COND_ROWS, N_MOD * D_MODEL), F32),
        compiler_params=_cparams(("arbitrary", "arbitrary")),
        name="adaln",
    )(cond, w_mod, b_mod.reshape(depth, 1, N_MOD * D_MODEL))


def _in_proj_kernel(mrow_ref, rope_ref, ctxi_ref,
                    x_ref, xp_ref, xn_ref, mod_ref, n1w_ref,
                    wqkv_ref, wz_ref, wxbc_ref, wdt_ref, wg_ref,
                    qnw_ref, knw_ref, cos_ref, sin_ref, kprev_ref, knext_ref,
                    convw_ref, convb_ref, dtb_ref,
                    *rest, n_ctx_blocks, ctx_len):
    q_ref, kb_ref, vt_ref, kc_ref, vc_ref, z_ref, xs_ref, bc_ref, dt_ref, g_ref = rest[-10:]
    del mrow_ref, rope_ref, ctxi_ref
    i = pl.program_id(0)
    is_ctx = i < n_ctx_blocks
    sh1 = mod_ref[0, 0:1, :]
    sc1 = mod_ref[0, 1:2, :]
    nw = n1w_ref[...]
    h = _modnorm(x_ref[...], nw, sc1, sh1).astype(BF16)
    hp = _modnorm(xp_ref[...], nw, sc1, sh1).astype(BF16)
    hn = _modnorm(xn_ref[...], nw, sc1, sh1).astype(BF16)

    r = lax.broadcasted_iota(jnp.int32, (2 * LANES, 2 * LANES), 0) // HEAD_DIM
    c = lax.broadcasted_iota(jnp.int32, (2 * LANES, 2 * LANES), 1) // HEAD_DIM
    same_head = jnp.where(r == c, 1.0, 0.0).astype(BF16)
    lane = lax.broadcasted_iota(jnp.int32, (TM, LANES), 1)
    first_half = (lane % (HEAD_DIM // 2)) < (HEAD_DIM // 4)

    def head_norm_rope(t, w, scale):
        sq = t * t
        hi = sq.astype(BF16)
        lo = (sq - hi.astype(F32)).astype(BF16)
        ssum = _dot(hi, same_head) + _dot(lo, same_head)
        tn = t * lax.rsqrt(ssum * (1.0 / HEAD_DIM) + EPS) * w
        outs = []
        for s in range(2):
            a = tn[:, s * LANES:(s + 1) * LANES]
            rot = jnp.where(first_half, -pltpu.roll(a, LANES - HEAD_DIM // 4, 1),
                            pltpu.roll(a, HEAD_DIM // 4, 1))
            outs.append((a * cos_ref[...] + rot * sin_ref[...]) * scale)
        return outs

    qw = qnw_ref[...]
    kw = knw_ref[...]
    ones_rows = jnp.where(lax.broadcasted_iota(jnp.int32, (V_ROWS - HEAD_DIM, TK), 0) == 0,
                          1.0, 0.0).astype(BF16)
    cw = CONV_CHUNK
    rows = lax.broadcasted_iota(jnp.int32, (TM, cw), 0)
    kprev = jnp.concatenate([kprev_ref[...]] * (cw // LANES), axis=1)
    knext = jnp.concatenate([knext_ref[...]] * (cw // LANES), axis=1)

    def q_stage(cb):
        def mm():
            return _dot(h, wqkv_ref[:, cb * 2 * LANES:(cb + 1) * 2 * LANES])

        def fin(t):
            a, b = head_norm_rope(t, qw, Q_SCALE)
            q_ref[:, cb * 2 * LANES:cb * 2 * LANES + LANES] = a.astype(BF16)
            q_ref[:, cb * 2 * LANES + LANES:(cb + 1) * 2 * LANES] = b.astype(BF16)
        return mm, fin

    def k_fin(t):
        kpairs = head_norm_rope(t, kw, 1.0)
        for rb in range(TM // TK):
            for pr in range(2):
                kb_ref[pr, rb] = kpairs[pr][rb * TK:(rb + 1) * TK, :].astype(BF16)

        @pl.when(is_ctx)
        def _():
            for sq in range(TM // ctx_len):
                for pr in range(2):
                    kc_ref[sq, :, pr * LANES:(pr + 1) * LANES] = kpairs[pr][sq * ctx_len:(sq + 1) * ctx_len, :]

    def v_fin(v):
        for rb in range(TM // TK):
            vT = v[rb * TK:(rb + 1) * TK, :].T
            for hd in range(N_KV_HEADS):
                vt_ref[hd // 2, hd % 2, rb, 0:HEAD_DIM, :] = (
                    vT[hd * HEAD_DIM:(hd + 1) * HEAD_DIM, :].astype(BF16))
                vt_ref[hd // 2, hd % 2, rb, HEAD_DIM:V_ROWS, :] = ones_rows

        @pl.when(is_ctx)
        def _():
            for sq in range(TM // ctx_len):
                vc_ref[sq] = v[sq * ctx_len:(sq + 1) * ctx_len, :]

    def z_fin(t):
        z_ref[...] = t.astype(BF16)

    def g_fin(t):
        g_ref[...] = _sigmoid(t).astype(BF16)

    def dt_fin(t):
        dtr = t + dtb_ref[...]
        dt_ref[...] = jnp.maximum(dtr, 0.0) + jnp.log(1.0 + jnp.exp(-jnp.abs(dtr)))

    def conv_stage(cb):
        sl = slice(cb * cw, (cb + 1) * cw)

        def mm():
            return (_dot(h, wxbc_ref[:, sl]), _dot(hp, wxbc_ref[:, sl]), _dot(hn, wxbc_ref[:, sl]))

        def fin(res):
            pre, pp, pn = res
            prev_row = pp[SUBLANES - 1:SUBLANES, :]
            next_row = pn[0:1, :]
            up = jnp.where(rows == 0, prev_row, pltpu.roll(pre, 1, 0)) * kprev
            down = jnp.where(rows == TM - 1, next_row, pltpu.roll(pre, TM - 1, 0)) * knext
            y = (convb_ref[:, sl] + up * convw_ref[0:1, sl] + pre * convw_ref[1:2, sl]
                 + down * convw_ref[2:3, sl])
            y = _silu(y).astype(BF16)
            lo = cb * cw
            if lo < D_INNER:
                xs_ref[:, lo:lo + cw] = y
            else:
                bc_ref[:, lo - D_INNER:lo - D_INNER + cw] = y
        return mm, fin

    stages = [q_stage(cb) for cb in range(ATTN_WIDTH // (2 * LANES))]
    stages.append((lambda: _dot(h, wqkv_ref[:, K_OFF:K_OFF + KV_WIDTH]), k_fin))
    stages.append((lambda: _dot(h, wqkv_ref[:, V_OFF:V_OFF + KV_WIDTH]), v_fin))
    stages.append((lambda: _dot(h, wdt_ref[...]), dt_fin))
    stages += [conv_stage(cb) for cb in range(CONV_DIM // cw)]
    stages.append((lambda: _dot(h, wz_ref[...]), z_fin))
    stages.append((lambda: _dot(h, wg_ref[...]), g_fin))
    pending = None
    for mm, fin in stages:
        res = mm()
        if pending is not None:
            pending[0](pending[1])
        pending = (fin, res)
    pending[0](pending[1])


def _in_proj(x, mod, meta, lw, t_ctx, ctx_len, layer, depth, prev_kv):
    t = x.shape[0]
    nblk = t // TM
    nhalo = t // SUBLANES
    per_halo = TM // SUBLANES
    n_ctx_blocks = t_ctx // TM
    nkb = TM // TK

    def row_blk(width):
        return pl.BlockSpec((TM, width), lambda i, *_: (i, 0))

    spb = TM // ctx_len
    ctx_blk = pl.BlockSpec((spb, None, ctx_len, KV_WIDTH),
                           lambda i, mrow, rope, ctxi: (ctxi[i], layer, 0, 0))
    in_specs = [
        row_blk(D_MODEL),
            pl.BlockSpec((SUBLANES, D_MODEL),
                         lambda i, *_: (jnp.maximum(i * per_halo - 1, 0), 0)),
            pl.BlockSpec((SUBLANES, D_MODEL),
                         lambda i, *_: (jnp.minimum((i + 1) * per_halo, nhalo - 1), 0)),
            pl.BlockSpec((1, MOD_ROWS, D_MODEL), lambda i, mrow, *_: (mrow[i], 0, 0)),
            _const_spec((1, D_MODEL)),
            _const_spec((D_MODEL, ATTN_WIDTH + 2 * KV_WIDTH)),
            _const_spec((D_MODEL, D_INNER)),
            _const_spec((D_MODEL, CONV_DIM)),
            _const_spec((D_MODEL, LANES)),
            _const_spec((D_MODEL, 2 * D_MODEL)),
            _const_spec((1, 2 * LANES)),
            _const_spec((1, 2 * LANES)),
            pl.BlockSpec((TM, LANES), lambda i, mrow, rope, *_: (rope[i], 0)),
            pl.BlockSpec((TM, LANES), lambda i, mrow, rope, *_: (rope[i], 0)),
            row_blk(LANES),
            row_blk(LANES),
            _const_spec((3, CONV_DIM)),
            _const_spec((1, CONV_DIM)),
            _const_spec((1, LANES)),
    ]
    out_specs = [row_blk(ATTN_WIDTH),
                   pl.BlockSpec((2, nkb, TK, LANES), lambda i, *_: (0, i, 0, 0)),
                   pl.BlockSpec((2, 2, nkb, V_ROWS, TK), lambda i, *_: (0, 0, i, 0, 0)),
                   ctx_blk, ctx_blk,
                   row_blk(D_INNER), row_blk(D_INNER), row_blk(2 * GN), row_blk(LANES),
                   row_blk(2 * D_MODEL)]
    out_shape = [
        jax.ShapeDtypeStruct((t, ATTN_WIDTH), BF16),
        jax.ShapeDtypeStruct((2, t // TK, TK, LANES), BF16),
        jax.ShapeDtypeStruct((2, 2, t // TK, V_ROWS, TK), BF16),
        jax.ShapeDtypeStruct((t_ctx // ctx_len, depth, ctx_len, KV_WIDTH), F32),
        jax.ShapeDtypeStruct((t_ctx // ctx_len, depth, ctx_len, KV_WIDTH), F32),
        jax.ShapeDtypeStruct((t, D_INNER), BF16),
        jax.ShapeDtypeStruct((t, D_INNER), BF16),
        jax.ShapeDtypeStruct((t, 2 * GN), BF16),
        jax.ShapeDtypeStruct((t, LANES), F32),
        jax.ShapeDtypeStruct((t, 2 * D_MODEL), BF16),
    ]
    args = [meta["mrow"], meta["rope_blk"], meta["ctx_blk"],
            x, x, x, mod, lw["norm1_w"], lw["wqkv"], lw["wz"], lw["wxbc"], lw["wdt"], lw["wg"],
            lw["q_norm_w"], lw["k_norm_w"], meta["cos"], meta["sin"], meta["keep_prev"],
            meta["keep_next"], lw["conv_w"], lw["conv_b"], lw["dt_bias"]]
    aliases = {}
    if prev_kv is not None:
        in_specs += [pl.BlockSpec(memory_space=pl.ANY)] * 2
        aliases = {len(args): 3, len(args) + 1: 4}
        args += list(prev_kv)
    grid_spec = pltpu.PrefetchScalarGridSpec(num_scalar_prefetch=3, grid=(nblk,), in_specs=in_specs,
                                             out_specs=out_specs)
    return pl.pallas_call(
        functools.partial(_in_proj_kernel, n_ctx_blocks=n_ctx_blocks, ctx_len=ctx_len),
        grid_spec=grid_spec, out_shape=out_shape, input_output_aliases=aliases,
        compiler_params=_cparams(("arbitrary",)), name="in_proj",
    )(*args)


def _attn_kernel(*refs, n_new, n_cache, tq):
    q_ref, kn_ref, vn_ref = refs[0:3]
    if n_cache:
        kc_ref, vc_ref = refs[3:5]
    o_ref, qt_ref, m_ref, acc_ref, sa_ref, mxa_ref, sb_ref, mxb_ref = refs[-8:]
    nq = 4 * tq
    zeros_half = jnp.zeros((HEAD_DIM, tq), F32)
    for hh in range(2):
        cols = []
        for jj in range(2):
            off = 2 * LANES * hh + LANES * jj
            qt = q_ref[:, off:off + LANES].astype(F32).T
            for s in range(2):
                head = qt[s * HEAD_DIM:(s + 1) * HEAD_DIM, :]
                parts = [head, zeros_half] if hh == 0 else [zeros_half, head]
                cols.append(jnp.concatenate(parts, axis=0))
        qt_ref[hh] = jnp.concatenate(cols, axis=1).astype(BF16)
        m_ref[hh] = jnp.full((1, nq), -jnp.inf, F32)
        acc_ref[hh] = jnp.zeros((V_ROWS, nq), F32)

    def scores(kblk, s_ref, mx_ref):
        for hh in range(2):
            s = _dot(kblk, qt_ref[hh])
            s_ref[hh] = s
            mx_ref[hh] = jnp.max(s, axis=0, keepdims=True)

    def consume(vt_of, s_ref, mx_ref):
        for hh in range(2):
            m = m_ref[hh]
            m_new = jnp.maximum(m, mx_ref[hh])
            alpha = jnp.exp2(m - m_new)
            p = jnp.exp2(s_ref[hh] - m_new).astype(BF16)
            acc_ref[hh] = alpha * acc_ref[hh] + _dot(vt_of(hh), p)
            m_ref[hh] = m_new

    bufs = ((sa_ref, mxa_ref), (sb_ref, mxb_ref))
    scores(kn_ref[0], *bufs[0])

    def body(j, carry):
        scores(kn_ref[2 * j + 1], *bufs[1])
        consume(lambda hh: vn_ref[hh, 2 * j], *bufs[0])
        scores(kn_ref[2 * j + 2], *bufs[0])
        consume(lambda hh: vn_ref[hh, 2 * j + 1], *bufs[1])
        return carry

    n_loop = (n_new - 1) // 2
    lax.fori_loop(0, n_loop, body, 0)
    rest = [(kn_ref, vn_ref, b) for b in range(2 * n_loop, n_new)]
    if n_cache:
        rest += [(kc_ref, vc_ref, b) for b in range(n_cache)]
    for r, (_, v_src, b) in enumerate(rest):
        if r + 1 < len(rest):
            k_nxt, _, b_nxt = rest[r + 1]
            scores(k_nxt[b_nxt], *bufs[(r + 1) % 2])
        consume(lambda hh, v_src=v_src, b=b: v_src[hh, b], *bufs[r % 2])

    for hh in range(2):
        oT = acc_ref[hh]
        oT = oT[0:HEAD_DIM, :] / oT[HEAD_DIM:HEAD_DIM + 1, :]
        for jj in range(2):
            pair = jnp.concatenate([oT[:, (2 * jj) * tq:(2 * jj + 1) * tq],
                                    oT[:, (2 * jj + 1) * tq:(2 * jj + 2) * tq]], axis=0)
            off = 2 * LANES * hh + LANES * jj
            o_ref[:, off:off + LANES] = pair.T.astype(BF16)


def _attention(q, kn, vn, cache, prev_out, nseq, seq_len, row0, tq):
    n_new = seq_len // TK
    nqb = seq_len // tq
    nq = 4 * tq
    q0 = row0 // tq
    kv0 = row0 // (n_new * TK)
    assert row0 % tq == 0 and row0 % (n_new * TK) == 0
    qspec = pl.BlockSpec((tq, 4 * LANES), lambda b, p, i: (q0 + b * nqb + i, p))
    in_specs = [
        qspec,
        pl.BlockSpec((None, n_new, TK, LANES), lambda b, p, i: (p, kv0 + b, 0, 0)),
        pl.BlockSpec((None, 2, n_new, V_ROWS, TK), lambda b, p, i: (p, 0, kv0 + b, 0, 0)),
    ]
    args = [q, kn, vn]
    n_cache = 0
    if cache is not None:
        kc, vc = cache
        n_cache = kc.shape[1] // nseq
        in_specs += [pl.BlockSpec((None, n_cache, TK, LANES), lambda b, p, i: (p, b, 0, 0)),
                     pl.BlockSpec((None, 2, n_cache, V_ROWS, TK), lambda b, p, i: (p, 0, b, 0, 0))]
        args += [kc, vc]
    aliases = {}
    kern = functools.partial(_attn_kernel, n_new=n_new, n_cache=n_cache, tq=tq)
    if prev_out is not None:
        n_in = len(args)
        in_specs.append(pl.BlockSpec(memory_space=pl.ANY))
        aliases = {n_in: 0}
        args.append(prev_out)
        inner = kern
        kern = lambda *refs: inner(*refs[:n_in], *refs[n_in + 1:])
    return pl.pallas_call(
        kern,
        grid=(nseq, 2, nqb),
        in_specs=in_specs,
        out_specs=qspec,
        out_shape=jax.ShapeDtypeStruct(q.shape, BF16),
        scratch_shapes=[pltpu.VMEM((2, LANES, nq), BF16),
                        pltpu.VMEM((2, 1, nq), F32),
                        pltpu.VMEM((2, V_ROWS, nq), F32),
                        pltpu.VMEM((2, TK, nq), F32),
                        pltpu.VMEM((2, 1, nq), F32),
                        pltpu.VMEM((2, TK, nq), F32),
                        pltpu.VMEM((2, 1, nq), F32)],
        input_output_aliases=aliases,
        compiler_params=_cparams(("arbitrary", "arbitrary", "arbitrary")),
        name="attention",
    )(*args)


def _cache_blocks(ck, cv):
    nseq, past = ck.shape[0], ck.shape[1]
    n = past // TK
    kb = ck.astype(BF16).reshape(nseq * n, TK, 2, LANES).transpose(2, 0, 1, 3)
    vb = cv.astype(BF16).reshape(nseq * n, TK, 2, 2, HEAD_DIM).transpose(2, 3, 0, 4, 1)
    ones = jnp.ones(vb.shape[:3] + (1, TK), BF16)
    zeros = jnp.zeros(vb.shape[:3] + (V_ROWS - HEAD_DIM - 1, TK), BF16)
    return kb, jnp.concatenate([vb, ones, zeros], axis=3)


def _ssd_direction(x_ref, bc_ref, dt_ref, a_lanes, expand, st_ref, y_ref, d):
    q = SSD_CHUNK
    row = lax.broadcasted_iota(jnp.int32, (q, q), 0)
    col = lax.broadcasted_iota(jnp.int32, (q, q), 1)
    causal = (row >= col) if d == 0 else (row <= col)
    tril = jnp.where(row >= col, 1.0, 0.0).astype(BF16)

    dt = dt_ref[...]
    a = dt * a_lanes
    prefix = _dot_exact_lhs(tril, a)
    if d == 0:
        cs = prefix
        last = cs[q - 1:q, :]
    else:
        cs = prefix[q - 1:q, :] - prefix + a
        last = cs[0:1, :]
    cs2 = cs * LOG2E
    cst2 = (cs2 - jnp.log2(dt)).T
    wj = jnp.exp(last - cs) * dt
    wj_x = _dot(wj.astype(BF16), expand)
    elast = jnp.exp(last)

    lane = lax.broadcasted_iota(jnp.int32, (q, LANES), 1)
    lo_half = lane < SSM_HEAD_DIM
    lo_half_row = lo_half[0:1, :]
    for g in range(SSM_GROUPS):
        bg = bc_ref[:, g * D_STATE:(g + 1) * D_STATE]
        cg = bc_ref[:, GN + g * D_STATE:GN + (g + 1) * D_STATE]
        gmat = _dot_nt(cg, bg)
        gsl = slice(g * HEADS_PER_GROUP * SSM_HEAD_DIM, (g + 1) * HEADS_PER_GROUP * SSM_HEAD_DIM)
        st = st_ref[d, :, gsl]
        y_inter = _dot(cg, st.astype(BF16))
        xg = x_ref[:, gsl]
        el_parts = []
        for hp in range(HEADS_PER_GROUP // 2):
            xpair = xg[:, hp * LANES:(hp + 1) * LANES]
            l0 = d * SSM_HEADS + g * HEADS_PER_GROUP + 2 * hp
            ws, es = [], []
            for s in range(2):
                hl = l0 + s
                csb = jnp.broadcast_to(cs2[:, hl:hl + 1], (q, q))
                seg = csb - cst2[hl:hl + 1, :]
                ws.append(gmat * jnp.exp2(jnp.where(causal, seg, -jnp.inf)))
                es.append(jnp.exp2(csb))
            wcat = jnp.concatenate(ws, axis=1).astype(BF16)
            zero = jnp.zeros_like(xpair)
            xm = jnp.concatenate([jnp.where(lo_half, xpair, zero), jnp.where(lo_half, zero, xpair)],
                                 axis=0)
            e_pair = jnp.where(lo_half, es[0], es[1])
            ypair = y_inter[:, hp * LANES:(hp + 1) * LANES] * e_pair + _dot(wcat, xm)
            lo = g * HEADS_PER_GROUP * SSM_HEAD_DIM + hp * LANES
            y_ref[:, lo:lo + LANES] = ypair.astype(y_ref.dtype)
            el_parts.append(jnp.where(lo_half_row, elast[:, l0:l0 + 1], elast[:, l0 + 1:l0 + 2]))
        xw = (xg.astype(F32) * wj_x[:, gsl]).astype(BF16)
        el = jnp.concatenate(el_parts, axis=1)
        st_ref[d, :, gsl] = st * el + _dot_tn(bg, xw)


def _ssd_kernel(fblk_ref, bblk_ref, first_ref, last_ref, init_ref, s0i_ref, sfi_ref,
                xf_ref, bcf_ref, dtf_ref, xb_ref, bcb_ref, dtb_ref, a_ref, exp_ref, s0_ref, *rest):
    yf_ref, yb_ref, sfin_ref, st_ref = rest[-4:]
    del fblk_ref, bblk_ref, s0i_ref, sfi_ref
    s = pl.program_id(0)

    @pl.when(jnp.logical_and(first_ref[s] == 1, init_ref[s] == 1))
    def _():
        st_ref[...] = s0_ref[0]

    @pl.when(jnp.logical_and(first_ref[s] == 1, init_ref[s] == 0))
    def _():
        st_ref[...] = jnp.zeros(st_ref.shape, F32)

    a_lanes = a_ref[...]
    _ssd_direction(xf_ref, bcf_ref, dtf_ref, a_lanes, exp_ref[0], st_ref, yf_ref, 0)
    _ssd_direction(xb_ref, bcb_ref, dtb_ref, a_lanes, exp_ref[1], st_ref, yb_ref, 1)

    @pl.when(jnp.logical_and(last_ref[s] == 1, init_ref[s] == 0))
    def _():
        sfin_ref[0] = st_ref[...]


def _ssd(xs, bc, dt, a_lanes, expand, s0, meta, n_ctx, layer, depth, prev_sfin):
    t = xs.shape[0]
    n_steps = t // SSD_CHUNK
    st_shape = (2, D_STATE, D_INNER)
    fwd = lambda s, fblk, bblk, *_: (fblk[s], 0)
    bwd = lambda s, fblk, bblk, *_: (bblk[s], 0)

    def specs(idx):
        return [pl.BlockSpec((SSD_CHUNK, D_INNER), idx),
                pl.BlockSpec((SSD_CHUNK, 2 * GN), idx),
                pl.BlockSpec((SSD_CHUNK, LANES), idx)]

    in_specs = specs(fwd) + specs(bwd) + [
        pl.BlockSpec((1, LANES), lambda s, *_: (0, 0)),
        pl.BlockSpec((2, LANES, D_INNER), lambda s, *_: (0, 0, 0)),
        pl.BlockSpec((1,) + st_shape, lambda s, f, b, fi, la, ini, s0i, sfi: (s0i[s], 0, 0, 0)),
    ]
    args = [meta["ssd_fblk"], meta["ssd_bblk"], meta["ssd_first"], meta["ssd_last"], meta["ssd_init"],
            meta["ssd_s0i"], meta["ssd_sfi"], xs, bc, dt, xs, bc, dt, a_lanes, expand, s0]
    aliases = {}
    if prev_sfin is not None:
        in_specs.append(pl.BlockSpec(memory_space=pl.ANY))
        aliases = {len(args): 2}
        args.append(prev_sfin)
    grid_spec = pltpu.PrefetchScalarGridSpec(
        num_scalar_prefetch=7,
        grid=(n_steps,),
        in_specs=in_specs,
        out_specs=[pl.BlockSpec((SSD_CHUNK, D_INNER), fwd), pl.BlockSpec((SSD_CHUNK, D_INNER), bwd),
                   pl.BlockSpec((1, None) + st_shape,
                                lambda s, f, b, fi, la, ini, s0i, sfi: (sfi[s], layer, 0, 0, 0))],
        scratch_shapes=[pltpu.VMEM(st_shape, F32)],
    )
    out_shape = [jax.ShapeDtypeStruct((t, D_INNER), BF16), jax.ShapeDtypeStruct((t, D_INNER), BF16),
                 jax.ShapeDtypeStruct((n_ctx, depth) + st_shape, F32)]
    return pl.pallas_call(
        _ssd_kernel, grid_spec=grid_spec, out_shape=out_shape, input_output_aliases=aliases,
        compiler_params=_cparams(("arbitrary",)), name="ssd",
    )(*args)


def _route(sel, scores):
    neg = -jnp.inf
    rows = [sel[e:e + 1, :] for e in range(N_EXPERTS)]
    srow = [scores[e:e + 1, :] for e in range(N_EXPERTS)]
    best_score = None
    best = None
    for g in range(N_EXPERT_GROUPS):
        v = rows[g * EXPERTS_PER_GROUP:(g + 1) * EXPERTS_PER_GROUP]
        top2 = None
        for a in range(EXPERTS_PER_GROUP):
            for b in range(a + 1, EXPERTS_PER_GROUP):
                s = v[a] + v[b]
                top2 = s if top2 is None else jnp.maximum(top2, s)
        if g == 0:
            best_score, best = top2, jnp.zeros(top2.shape, jnp.int32)
        else:
            better = top2 > best_score
            best = jnp.where(better, g, best)
            best_score = jnp.where(better, top2, best_score)

    def pick(vals):
        out = vals[0]
        for g in range(1, N_EXPERT_GROUPS):
            out = jnp.where(best == g, vals[g], out)
        return out

    gsel = [pick([rows[g * EXPERTS_PER_GROUP + j] for g in range(N_EXPERT_GROUPS)])
            for j in range(EXPERTS_PER_GROUP)]
    gsc = [pick([srow[g * EXPERTS_PER_GROUP + j] for g in range(N_EXPERT_GROUPS)])
           for j in range(EXPERTS_PER_GROUP)]

    def argmax_first(vals):
        bi = jnp.zeros(vals[0].shape, jnp.int32)
        bv = vals[0]
        for j in range(1, len(vals)):
            better = vals[j] > bv
            bi = jnp.where(better, j, bi)
            bv = jnp.where(better, vals[j], bv)
        return bi

    i1 = argmax_first(gsel)
    i2 = argmax_first([jnp.where(i1 == j, neg, gsel[j]) for j in range(EXPERTS_PER_GROUP)])

    def take(vals, idx):
        out = vals[0]
        for j in range(1, len(vals)):
            out = jnp.where(idx == j, vals[j], out)
        return out

    g1 = take(gsc, i1)
    g2 = take(gsc, i2)
    tot = g1 + g2
    idx = jnp.concatenate([best * EXPERTS_PER_GROUP + i1, best * EXPERTS_PER_GROUP + i2], axis=0)
    gate = jnp.concatenate([g1 / tot, g2 / tot], axis=0)
    return idx, gate


def _merge_kernel(mrow_ref, x_ref, attn_ref, yf_ref, yb_ref, xs_ref, z_ref, g_ref, mod_ref,
                  wa_ref, ws_ref, wo_ref, dvec_ref, snw_ref, n2w_ref, wrt_ref, rb_ref,
                  x1_ref, h2_ref, idx_ref, gate_ref):
    del mrow_ref
    gw = D_INNER // SSM_GROUPS
    ssm_o = None
    for g in range(SSM_GROUPS):
        sl = slice(g * gw, (g + 1) * gw)
        xs = xs_ref[:, sl].astype(F32)
        y = yf_ref[:, sl].astype(F32) + yb_ref[:, sl].astype(F32) + dvec_ref[:, sl] * xs
        y = y * _silu(z_ref[:, sl].astype(F32))
        ms = jnp.mean(y * y, axis=-1, keepdims=True)
        yn = (y * lax.rsqrt(ms + EPS) * snw_ref[:, sl]).astype(BF16)
        part = _dot(yn, ws_ref[sl, :])
        ssm_o = part if ssm_o is None else ssm_o + part
    attn_o = _dot(attn_ref[...], wa_ref[...])
    merged = (g_ref[:, 0:D_MODEL].astype(F32) * attn_o
              + g_ref[:, D_MODEL:2 * D_MODEL].astype(F32) * ssm_o)
    out = _dot(merged.astype(BF16), wo_ref[...])
    x1 = x_ref[...] + mod_ref[0, 2:3, :] * out
    x1_ref[...] = x1
    h2 = _modnorm(x1, n2w_ref[...], mod_ref[0, 4:5, :], mod_ref[0, 3:4, :])
    h2_ref[...] = h2.astype(BF16)
    logits = lax.dot_general(wrt_ref[...], h2, (((1,), (1,)), ((), ())),
                             preferred_element_type=F32, precision=lax.Precision.HIGHEST)
    scores = _sigmoid(logits)
    idx, gate = _route(scores + rb_ref[...], scores)
    idx_ref[...] = idx
    gate_ref[...] = gate


def _merge(x, attn, yf, yb, xs, z, g, mod, meta, lw, wrt, rb):
    t = x.shape[0]
    nblk = t // TM

    def row_blk(width):
        return pl.BlockSpec((TM, width), lambda i, *_: (i, 0))

    grid_spec = pltpu.PrefetchScalarGridSpec(
        num_scalar_prefetch=1,
        grid=(nblk,),
        in_specs=[row_blk(D_MODEL), row_blk(ATTN_WIDTH), row_blk(D_INNER), row_blk(D_INNER),
                  row_blk(D_INNER), row_blk(D_INNER), row_blk(2 * D_MODEL),
                  pl.BlockSpec((1, MOD_ROWS, D_MODEL), lambda i, mrow: (mrow[i], 0, 0)),
                  _const_spec((ATTN_WIDTH, D_MODEL)), _const_spec((D_INNER, D_MODEL)),
                  _const_spec((D_MODEL, D_MODEL)), _const_spec((1, D_INNER)),
                  _const_spec((1, D_INNER)), _const_spec((1, D_MODEL)),
                  _const_spec((N_EXPERTS, D_MODEL)), _const_spec((N_EXPERTS, 1))],
        out_specs=[row_blk(D_MODEL), row_blk(D_MODEL),
                   pl.BlockSpec((2, TM), lambda i, *_: (0, i)),
                   pl.BlockSpec((2, TM), lambda i, *_: (0, i))],
    )
    out_shape = [jax.ShapeDtypeStruct((t, D_MODEL), F32), jax.ShapeDtypeStruct((t, D_MODEL), BF16),
                 jax.ShapeDtypeStruct((2, t), jnp.int32), jax.ShapeDtypeStruct((2, t), F32)]
    return pl.pallas_call(
        _merge_kernel, grid_spec=grid_spec, out_shape=out_shape,
        compiler_params=_cparams(("arbitrary",)), name="merge",
    )(meta["mrow"], x, attn, yf, yb, xs, z, g, mod, lw["w_attn_out"], lw["w_ssm_out"], lw["w_out"],
      lw["ssm_d"], lw["ssm_norm_w"], lw["norm2_w"], wrt, rb)


def _expert_kernel(te_ref, nt_ref, chg_ref, x_ref, wg_ref, wu_ref, wd_ref, o_ref,
                   wgs_ref, wus_ref, wds_ref):
    del te_ref
    i = pl.program_id(0)

    @pl.when(chg_ref[i] == 1)
    def _():
        wgs_ref[...] = wg_ref[0].astype(BF16)
        wus_ref[...] = wu_ref[0].astype(BF16)
        wds_ref[...] = wd_ref[0].astype(BF16)

    @pl.when(i < nt_ref[0])
    def _():
        x = x_ref[...]
        hmid = _silu(_dot(x, wgs_ref[...])) * _dot(x, wus_ref[...])
        o_ref[...] = _dot(hmid.astype(BF16), wds_ref[...]).astype(o_ref.dtype)

    @pl.when(i >= nt_ref[0])
    def _():
        o_ref[...] = jnp.zeros(o_ref.shape, o_ref.dtype)


def _experts(x_sorted, tile_expert, n_tiles, wg, wu, wd, layer):
    rows = x_sorted.shape[0]
    changed = jnp.concatenate([jnp.ones((1,), jnp.int32),
                               (tile_expert[1:] != tile_expert[:-1]).astype(jnp.int32)])
    grid_spec = pltpu.PrefetchScalarGridSpec(
        num_scalar_prefetch=3,
        grid=(rows // TE,),
        in_specs=[pl.BlockSpec((TE, D_MODEL), lambda i, te, nt, chg: (i, 0)),
                  pl.BlockSpec((None, 1, D_MODEL, D_FF_EXPERT), lambda i, te, nt, chg: (layer, te[i], 0, 0)),
                  pl.BlockSpec((None, 1, D_MODEL, D_FF_EXPERT), lambda i, te, nt, chg: (layer, te[i], 0, 0)),
                  pl.BlockSpec((None, 1, D_FF_EXPERT, D_MODEL), lambda i, te, nt, chg: (layer, te[i], 0, 0))],
        out_specs=pl.BlockSpec((TE, D_MODEL), lambda i, te, nt, chg: (i, 0)),
        scratch_shapes=[pltpu.VMEM((D_MODEL, D_FF_EXPERT), BF16),
                        pltpu.VMEM((D_MODEL, D_FF_EXPERT), BF16),
                        pltpu.VMEM((D_FF_EXPERT, D_MODEL), BF16)],
    )
    return pl.pallas_call(
        _expert_kernel, grid_spec=grid_spec,
        out_shape=jax.ShapeDtypeStruct((rows, D_MODEL), BF16),
        compiler_params=_cparams(("arbitrary",)), name="experts",
    )(tile_expert, n_tiles, changed, x_sorted, wg, wu, wd)


def _combine_kernel(mrow_ref, x1_ref, y0_ref, y1_ref, gate_ref, mod_ref, fnw_ref, *out_refs,
                    n_ctx_blocks):
    del mrow_ref
    moe = (gate_ref[:, 0:1] * y0_ref[...].astype(F32) + gate_ref[:, 1:2] * y1_ref[...].astype(F32))
    x2 = x1_ref[...] + mod_ref[0, 5:6, :] * moe
    if n_ctx_blocks is None:
        out_refs[0][...] = x2
        return
    ms = jnp.mean(x2 * x2, axis=-1, keepdims=True)
    y = x2 * lax.rsqrt(ms + EPS) * fnw_ref[...]
    is_ctx = pl.program_id(0) < n_ctx_blocks

    @pl.when(is_ctx)
    def _():
        out_refs[0][...] = y

    @pl.when(jnp.logical_not(is_ctx))
    def _():
        out_refs[1][...] = y


def _combine(x1, y0, y1, gate_t, mod, meta, fnw, t_ctx):
    t = x1.shape[0]

    def row_blk(width):
        return pl.BlockSpec((TM, width), lambda i, *_: (i, 0))

    if t_ctx is None:
        n_ctx_blocks = None
        out_specs = row_blk(D_MODEL)
        out_shape = jax.ShapeDtypeStruct((t, D_MODEL), F32)
    else:
        n_ctx_blocks = t_ctx // TM
        out_specs = [pl.BlockSpec((TM, D_MODEL), lambda i, *_: (jnp.minimum(i, n_ctx_blocks - 1), 0)),
                     pl.BlockSpec((TM, D_MODEL), lambda i, *_: (jnp.maximum(i - n_ctx_blocks, 0), 0))]
        out_shape = [jax.ShapeDtypeStruct((t_ctx, D_MODEL), F32),
                     jax.ShapeDtypeStruct((t - t_ctx, D_MODEL), F32)]
    grid_spec = pltpu.PrefetchScalarGridSpec(
        num_scalar_prefetch=1,
        grid=(t // TM,),
        in_specs=[row_blk(D_MODEL), row_blk(D_MODEL), row_blk(D_MODEL), row_blk(2),
                  pl.BlockSpec((1, MOD_ROWS, D_MODEL), lambda i, mrow: (mrow[i], 0, 0)),
                  _const_spec((1, D_MODEL))],
        out_specs=out_specs,
    )
    return pl.pallas_call(
        functools.partial(_combine_kernel, n_ctx_blocks=n_ctx_blocks), grid_spec=grid_spec,
        out_shape=out_shape,
        compiler_params=_cparams(("arbitrary",)), name="combine",
    )(meta["mrow"], x1, y0, y1, gate_t, mod, fnw)


def _moe(h2, idx, gate, w_exp, layer):
    t = h2.shape[0]
    n_assign = 2 * t
    rows = n_assign + N_EXPERTS * TE
    e_flat = idx.reshape(n_assign)
    onehot = e_flat[:, None] == jnp.arange(N_EXPERTS, dtype=jnp.int32)[None, :]
    nb = n_assign // RANK_BLOCK
    oh = onehot.astype(BF16).reshape(nb, RANK_BLOCK, N_EXPERTS)
    tril = jnp.tril(jnp.ones((RANK_BLOCK, RANK_BLOCK), BF16))
    local = jnp.einsum("ij,bjk->bik", tril, oh, preferred_element_type=F32)
    bsum = local[:, -1, :]
    before = jnp.dot(jnp.tril(jnp.ones((nb, nb), F32), -1), bsum, precision=lax.Precision.HIGHEST)
    running = (local + before[:, None, :]).reshape(n_assign, N_EXPERTS)
    counts = (before[-1] + bsum[-1]).astype(jnp.int32)
    padded = ((counts + TE - 1) // TE) * TE
    pad_end = jnp.cumsum(padded)
    pad_off = pad_end - padded
    off = jnp.cumsum(counts) - counts
    pos = jnp.sum(jnp.where(onehot, running + (pad_off - 1).astype(F32)[None, :], 0.0), axis=1
                  ).astype(jnp.int32)
    order = jnp.argsort(e_flat, stable=True).astype(jnp.int32)
    p = jnp.arange(rows, dtype=jnp.int32)
    ep = jnp.minimum(jnp.sum(p[:, None] >= pad_end[None, :], axis=1), N_EXPERTS - 1).astype(jnp.int32)
    r = p - pad_off[ep]
    valid = r < counts[ep]
    src = jnp.where(valid, order[jnp.clip(off[ep] + r, 0, n_assign - 1)] % t, p % t)
    n_tiles = (pad_end[-1] // TE).astype(jnp.int32).reshape(1)
    tile_start = jnp.arange(rows // TE, dtype=jnp.int32) * TE
    tile_expert = jnp.minimum(jnp.sum(tile_start[:, None] >= pad_end[None, :], axis=1),
                              N_EXPERTS - 1).astype(jnp.int32)
    last_used = tile_expert[jnp.maximum(n_tiles[0] - 1, 0)]
    tile_expert = jnp.where(tile_start < pad_end[-1], tile_expert, last_used)
    x_sorted = h2.at[src].get(mode="promise_in_bounds")
    y_sorted = _experts(x_sorted, tile_expert, n_tiles, *w_exp, layer)
    pos2 = pos.reshape(2, t)
    return (y_sorted.at[pos2[0]].get(mode="promise_in_bounds"),
            y_sorted.at[pos2[1]].get(mode="promise_in_bounds"))


def _rope_tables(n_tokens):
    rows = n_tokens // GRID_W
    row = jnp.repeat(jnp.arange(rows, dtype=F32), GRID_W)
    col = jnp.tile(jnp.arange(GRID_W, dtype=F32), rows)
    inv = 1.0 / (ROPE_THETA ** (jnp.arange(0, ROPE_AXIS_DIM, 2, dtype=F32) / ROPE_AXIS_DIM))
    ar = row[:, None] * inv
    ac = col[:, None] * inv
    ang = jnp.concatenate([ar, ar, ac, ac], axis=-1)
    cos, sin = jnp.cos(ang), jnp.sin(ang)
    cos = jnp.concatenate([jnp.ones((TM, HEAD_DIM), F32), cos], axis=0)
    sin = jnp.concatenate([jnp.zeros((TM, HEAD_DIM), F32), sin], axis=0)
    return jnp.tile(cos, (1, 2)), jnp.tile(sin, (1, 2))


def _block_meta(n_ctx, ctx_len, n_lat, lat_len):
    ctx_blocks = n_ctx * ctx_len // TM
    lat_blocks = n_lat * lat_len // TM
    per_lat = lat_len // TM
    bi = jnp.arange(ctx_blocks + lat_blocks, dtype=jnp.int32)
    is_ctx = bi < ctx_blocks
    lat_i = bi - ctx_blocks
    mrow = jnp.where(is_ctx, 0, 1 + lat_i // per_lat).astype(jnp.int32)
    rope_blk = jnp.where(is_ctx, 0, 1 + lat_i % per_lat).astype(jnp.int32)
    cos, sin = _rope_tables(lat_len)
    ti = jnp.arange(n_ctx * ctx_len + n_lat * lat_len, dtype=jnp.int32)
    pos = jnp.where(ti < n_ctx * ctx_len, ti % ctx_len, (ti - n_ctx * ctx_len) % lat_len)
    slen = jnp.where(ti < n_ctx * ctx_len, ctx_len, lat_len)
    keep_prev = jnp.broadcast_to((pos != 0).astype(F32)[:, None], (ti.shape[0], LANES))
    keep_next = jnp.broadcast_to((pos != slen - 1).astype(F32)[:, None], (ti.shape[0], LANES))
    meta = {"mrow": mrow, "rope_blk": rope_blk, "cos": cos, "sin": sin,
            "ctx_blk": jnp.minimum(bi, ctx_blocks - 1).astype(jnp.int32),
            "keep_prev": keep_prev, "keep_next": keep_next}

    nc_ctx, nc_lat = ctx_len // SSD_CHUNK, lat_len // SSD_CHUNK
    n_ctx_steps = n_ctx * nc_ctx
    si = jnp.arange(n_ctx_steps + n_lat * nc_lat, dtype=jnp.int32)
    s_ctx = si < n_ctx_steps
    li = si - n_ctx_steps
    seq = jnp.where(s_ctx, si // nc_ctx, li // nc_lat)
    ch = jnp.where(s_ctx, si % nc_ctx, li % nc_lat)
    nc = jnp.where(s_ctx, nc_ctx, nc_lat)
    base = jnp.where(s_ctx, seq * nc_ctx, n_ctx_steps + seq * nc_lat)
    meta.update({
        "ssd_fblk": (base + ch).astype(jnp.int32),
        "ssd_bblk": (base + nc - 1 - ch).astype(jnp.int32),
        "ssd_first": (ch == 0).astype(jnp.int32),
        "ssd_last": (ch == nc - 1).astype(jnp.int32),
        "ssd_init": jnp.where(s_ctx, 0, 1).astype(jnp.int32),
        "ssd_s0i": jnp.where(s_ctx, 0, seq).astype(jnp.int32),
        "ssd_sfi": jnp.where(s_ctx, seq, n_ctx - 1).astype(jnp.int32),
    })
    return meta


def kernel(x_prompt, x_sample, cache_k, cache_v, state_ssm, c, c_ctx, norm1_w, norm2_w, w_mod, b_mod,
           w_in, q_norm_w, k_norm_w, conv_w, conv_b, a_log, dt_bias, ssm_d, ssm_norm_w, w_attn_out,
           w_ssm_out, w_out, w_router, router_bias, w_exp_gate, w_exp_up, w_exp_down, final_norm_w):
    n_ctx, ctx_len, _ = x_prompt.shape
    n_lat, lat_len, _ = x_sample.shape
    depth = w_in.shape[0]
    t_ctx = n_ctx * ctx_len
    t_lat = n_lat * lat_len
    assert t_ctx % TM == 0 and lat_len % TM == 0 and 1 + n_lat <= COND_ROWS and TM % ctx_len == 0
    assert ctx_len % TK == 0 and cache_k.shape[2] % TK == 0 and lat_len % GRID_W == 0

    meta = _block_meta(n_ctx, ctx_len, n_lat, lat_len)
    x = jnp.concatenate([x_prompt.reshape(t_ctx, D_MODEL), x_sample.reshape(t_lat, D_MODEL)], axis=0)

    cond = jnp.zeros((COND_ROWS, D_MODEL), F32).at[0].set(c_ctx).at[1:1 + n_lat].set(c)
    mod_all = _adaln(cond, w_mod, b_mod).reshape(depth, COND_ROWS, N_MOD, D_MODEL)
    mod_all = jnp.pad(mod_all, ((0, 0), (0, 0), (0, MOD_ROWS - N_MOD), (0, 0)))

    head_of_lane = jnp.arange(D_INNER, dtype=jnp.int32) // SSM_HEAD_DIM
    lane_id = jnp.arange(LANES, dtype=jnp.int32)
    expand = jnp.stack([(lane_id[:, None] == head_of_lane[None, :] + d * SSM_HEADS) for d in range(2)]
                       ).astype(BF16)
    wrt = w_router.T
    rb = router_bias.reshape(N_EXPERTS, 1)
    fnw = final_norm_w.reshape(1, D_MODEL)

    kv_ctx, s_ctx = None, None
    for l in range(depth):
        wl = w_in[l]
        lw = {
            "norm1_w": norm1_w[l].reshape(1, D_MODEL),
            "norm2_w": norm2_w[l].reshape(1, D_MODEL),
            "wqkv": wl[:, :Z_OFF].astype(BF16),
            "wz": wl[:, Z_OFF:XBC_OFF].astype(BF16),
            "wxbc": wl[:, XBC_OFF:DT_OFF].astype(BF16),
            "wdt": jnp.pad(wl[:, DT_OFF:G_OFF], ((0, 0), (0, LANES - 2 * SSM_HEADS))).astype(BF16),
            "wg": wl[:, G_OFF:].astype(BF16),
            "q_norm_w": jnp.tile(q_norm_w[l], 4).reshape(1, 2 * LANES),
            "k_norm_w": jnp.tile(k_norm_w[l], 4).reshape(1, 2 * LANES),
            "conv_w": conv_w[l],
            "conv_b": conv_b[l].reshape(1, CONV_DIM),
            "dt_bias": jnp.pad(dt_bias[l].reshape(1, 2 * SSM_HEADS), ((0, 0), (0, LANES - 2 * SSM_HEADS))),
            "ssm_d": jnp.repeat(ssm_d[l], SSM_HEAD_DIM).reshape(1, D_INNER),
            "ssm_norm_w": ssm_norm_w[l].reshape(1, D_INNER),
            "w_attn_out": w_attn_out[l].astype(BF16),
            "w_ssm_out": w_ssm_out[l].astype(BF16),
            "w_out": w_out[l].astype(BF16),
        }
        mod = mod_all[l]
        q, kn, vn, k_ctx, v_ctx, z, xs, bc, dt, g = _in_proj(x, mod, meta, lw, t_ctx, ctx_len, l, depth,
                                                             kv_ctx)
        kv_ctx = (k_ctx, v_ctx)

        attn = _attention(q, kn, vn, None, None, n_ctx, ctx_len, 0, min(TQ, ctx_len))
        attn = _attention(q, kn, vn, _cache_blocks(cache_k[:, l], cache_v[:, l]), attn,
                          n_lat, lat_len, t_ctx, min(TQ, lat_len))

        a_lanes = jnp.pad(-jnp.exp(a_log[l]).reshape(1, 2 * SSM_HEADS),
                          ((0, 0), (0, LANES - 2 * SSM_HEADS)))
        s0 = state_ssm[:, l].transpose(0, 1, 4, 2, 3).reshape(n_lat, 2, D_STATE, D_INNER)
        yf, yb, s_ctx = _ssd(xs, bc, dt, a_lanes, expand, s0, meta, n_ctx, l, depth, s_ctx)

        x1, h2, idx, gate = _merge(x, attn, yf, yb, xs, z, g, mod, meta, lw, wrt, rb)
        y0, y1 = _moe(h2, idx, gate, (w_exp_gate, w_exp_up, w_exp_down), l)
        if l < depth - 1:
            x = _combine(x1, y0, y1, gate.T, mod, meta, fnw, None)
        else:
            y_prompt, y_sample = _combine(x1, y0, y1, gate.T, mod, meta, fnw, t_ctx)

    y_prompt = y_prompt.reshape(n_ctx, ctx_len, D_MODEL)
    y_sample = y_sample.reshape(n_lat, lat_len, D_MODEL)
    new_k = kv_ctx[0].reshape(n_ctx, depth, ctx_len, N_KV_HEADS, HEAD_DIM)
    new_v = kv_ctx[1].reshape(n_ctx, depth, ctx_len, N_KV_HEADS, HEAD_DIM)
    new_s = s_ctx.reshape(n_ctx, depth, 2, D_STATE, SSM_HEADS, SSM_HEAD_DIM).transpose(0, 1, 2, 4, 5, 3)
    return (y_prompt, y_sample, new_k, new_v, new_s)
```

```python
import functools

import jax
import jax.numpy as jnp
from jax import lax
from jax.experimental import pallas as pl
from jax.experimental.pallas import tpu as pltpu

F32 = jnp.float32
BF16 = jnp.bfloat16

EPS = 1e-6
D_MODEL = 1024
N_HEADS = 16
N_KV_HEADS = 4
HEAD_DIM = 64
ATTN_WIDTH = N_HEADS * HEAD_DIM
KV_WIDTH = N_KV_HEADS * HEAD_DIM
GRID_W = 64
ROPE_AXIS_DIM = HEAD_DIM // 2
ROPE_THETA = 10000.0
D_INNER = 2 * D_MODEL
SSM_HEAD_DIM = 64
SSM_HEADS = D_INNER // SSM_HEAD_DIM
SSM_GROUPS = 4
HEADS_PER_GROUP = SSM_HEADS // SSM_GROUPS
D_STATE = 128
GN = SSM_GROUPS * D_STATE
CONV_DIM = D_INNER + 2 * GN
SSD_CHUNK = 128
N_EXPERTS = 16
N_EXPERT_GROUPS = 4
EXPERTS_PER_GROUP = N_EXPERTS // N_EXPERT_GROUPS
D_FF_EXPERT = 512
N_MOD = 6
MOD_ROWS = 8

LANES = 128
SUBLANES = 8
VMEM_LIMIT = 56 * 1024 * 1024

TM = 512
TQ = 512
TK = 256
TE = 512
MERGE_CHAINS = 4
RANK_BLOCK = 512
CONV_CHUNK = 1024
COND_ROWS = 16
V_ROWS = HEAD_DIM + SUBLANES
LOG2E = 1.4426950408889634
Q_SCALE = HEAD_DIM ** -0.5 * LOG2E

Q_OFF, K_OFF, V_OFF = 0, ATTN_WIDTH, ATTN_WIDTH + KV_WIDTH
Z_OFF = ATTN_WIDTH + 2 * KV_WIDTH
XBC_OFF = Z_OFF + D_INNER
DT_OFF = XBC_OFF + CONV_DIM
G_OFF = DT_OFF + 2 * SSM_HEADS
N_IN = G_OFF + 2 * D_MODEL


def _cparams(sem):
    return pltpu.CompilerParams(dimension_semantics=sem, vmem_limit_bytes=VMEM_LIMIT)


def _const_spec(shape):
    nd = len(shape)
    return pl.BlockSpec(shape, lambda *_: (0,) * nd, pipeline_mode=pl.Buffered(1))


def _dot(a, b):
    return jnp.dot(a, b, preferred_element_type=F32)


def _dot_nt(a, b):
    return lax.dot_general(a, b, (((1,), (1,)), ((), ())), preferred_element_type=F32)


def _dot_tn(a, b):
    return lax.dot_general(a, b, (((0,), (0,)), ((), ())), preferred_element_type=F32)


def _split3(a):
    a1 = a.astype(BF16)
    r = a - a1.astype(F32)
    a2 = r.astype(BF16)
    a3 = (r - a2.astype(F32)).astype(BF16)
    return a1, a2, a3


def _dot_exact_lhs(m_bf16, a):
    a1, a2, a3 = _split3(a)
    return _dot(m_bf16, a1) + _dot(m_bf16, a2) + _dot(m_bf16, a3)


def _modnorm(x, w, sc, sh):
    ms = jnp.mean(x * x, axis=-1, keepdims=True)
    return x * lax.rsqrt(ms + EPS) * (w * (1.0 + sc)) + sh


def _sigmoid(x):
    return 1.0 / (1.0 + jnp.exp(-x))


def _silu(x):
    return x * _sigmoid(x)


def _adaln_kernel(c_ref, w_ref, b_ref, o_ref):
    cs = _silu(c_ref[...])
    o_ref[0] = jnp.dot(cs, w_ref[0], preferred_element_type=F32,
                       precision=lax.Precision.HIGHEST) + b_ref[0]


def _adaln(cond, w_mod, b_mod):
    depth = w_mod.shape[0]
    nb = N_MOD
    return pl.pallas_call(
        _adaln_kernel,
        grid=(depth, nb),
        in_specs=[pl.BlockSpec((COND_ROWS, D_MODEL), lambda l, j: (0, 0)),
                  pl.BlockSpec((1, D_MODEL, D_MODEL), lambda l, j: (l, 0, j)),
                  pl.BlockSpec((1, 1, D_MODEL), lambda l, j: (l, 0, j))],
        out_specs=pl.BlockSpec((1, COND_ROWS, D_MODEL), lambda l, j: (l, 0, j)),
        out_shape=jax.ShapeDtypeStruct((depth, COND_ROWS, N_MOD * D_MODEL), F32),
        compiler_params=_cparams(("arbitrary", "arbitrary")),
        name="adaln",
    )(cond, w_mod, b_mod.reshape(depth, 1, N_MOD * D_MODEL))


def _in_proj_kernel(mrow_ref, rope_ref, ctxi_ref,
                    x_ref, xp_ref, xn_ref, mod_ref, n1w_ref,
                    wqkv_ref, wz_ref, wxbc_ref, wdt_ref, wg_ref,
                    qnw_ref, knw_ref, cos_ref, sin_ref, kprev_ref, knext_ref,
                    convw_ref, convb_ref, dtb_ref,
                    *rest, n_ctx_blocks, ctx_len):
    q_ref, kb_ref, vt_ref, kc_ref, vc_ref, z_ref, xs_ref, bc_ref, dt_ref, g_ref = rest[-10:]
    del mrow_ref, rope_ref, ctxi_ref
    i = pl.program_id(0)
    is_ctx = i < n_ctx_blocks
    sh1 = mod_ref[0, 0:1, :]
    sc1 = mod_ref[0, 1:2, :]
    nw = n1w_ref[...]
    h = _modnorm(x_ref[...], nw, sc1, sh1).astype(BF16)
    hp = _modnorm(xp_ref[...], nw, sc1, sh1).astype(BF16)
    hn = _modnorm(xn_ref[...], nw, sc1, sh1).astype(BF16)

    r = lax.broadcasted_iota(jnp.int32, (2 * LANES, 2 * LANES), 0) // HEAD_DIM
    c = lax.broadcasted_iota(jnp.int32, (2 * LANES, 2 * LANES), 1) // HEAD_DIM
    same_head = jnp.where(r == c, 1.0, 0.0).astype(BF16)
    lane = lax.broadcasted_iota(jnp.int32, (TM, LANES), 1)
    first_half = (lane % (HEAD_DIM // 2)) < (HEAD_DIM // 4)

    def head_norm_rope(t, w, scale):
        sq = t * t
        hi = sq.astype(BF16)
        lo = (sq - hi.astype(F32)).astype(BF16)
        ssum = _dot(hi, same_head) + _dot(lo, same_head)
        tn = t * lax.rsqrt(ssum * (1.0 / HEAD_DIM) + EPS) * w
        outs = []
        for s in range(2):
            a = tn[:, s * LANES:(s + 1) * LANES]
            rot = jnp.where(first_half, -pltpu.roll(a, LANES - HEAD_DIM // 4, 1),
                            pltpu.roll(a, HEAD_DIM // 4, 1))
            outs.append((a * cos_ref[...] + rot * sin_ref[...]) * scale)
        return outs

    qw = qnw_ref[...]
    kw = knw_ref[...]
    ones_rows = jnp.where(lax.broadcasted_iota(jnp.int32, (V_ROWS - HEAD_DIM, TK), 0) == 0,
                          1.0, 0.0).astype(BF16)
    cw = CONV_CHUNK
    rows = lax.broadcasted_iota(jnp.int32, (TM, cw), 0)
    kprev = jnp.concatenate([kprev_ref[...]] * (cw // LANES), axis=1)
    knext = jnp.concatenate([knext_ref[...]] * (cw // LANES), axis=1)

    def q_stage(cb):
        def mm():
            return _dot(h, wqkv_ref[:, cb * 2 * LANES:(cb + 1) * 2 * LANES])

        def fin(t):
            a, b = head_norm_rope(t, qw, Q_SCALE)
            q_ref[:, cb * 2 * LANES:cb * 2 * LANES + LANES] = a.astype(BF16)
            q_ref[:, cb * 2 * LANES + LANES:(cb + 1) * 2 * LANES] = b.astype(BF16)
        return mm, fin

    def k_fin(t):
        kpairs = head_norm_rope(t, kw, 1.0)
        for rb in range(TM // TK):
            for pr in range(2):
                kb_ref[pr, rb] = kpairs[pr][rb * TK:(rb + 1) * TK, :].astype(BF16)

        @pl.when(is_ctx)
        def _():
            for sq in range(TM // ctx_len):
                for pr in range(2):
                    kc_ref[sq, :, pr * LANES:(pr + 1) * LANES] = kpairs[pr][sq * ctx_len:(sq + 1) * ctx_len, :]

    def v_fin(v):
        for rb in range(TM // TK):
            vT = v[rb * TK:(rb + 1) * TK, :].T
            for hd in range(N_KV_HEADS):
                vt_ref[hd // 2, hd % 2, rb, 0:HEAD_DIM, :] = (
                    vT[hd * HEAD_DIM:(hd + 1) * HEAD_DIM, :].astype(BF16))
                vt_ref[hd // 2, hd % 2, rb, HEAD_DIM:V_ROWS, :] = ones_rows

        @pl.when(is_ctx)
        def _():
            for sq in range(TM // ctx_len):
                vc_ref[sq] = v[sq * ctx_len:(sq + 1) * ctx_len, :]

    def z_fin(t):
        z_ref[...] = t.astype(BF16)

    def g_fin(t):
        g_ref[...] = _sigmoid(t).astype(BF16)

    def dt_fin(t):
        dtr = t + dtb_ref[...]
        dt_ref[...] = jnp.maximum(dtr, 0.0) + jnp.log(1.0 + jnp.exp(-jnp.abs(dtr)))

    def conv_stage(cb):
        sl = slice(cb * cw, (cb + 1) * cw)

        def mm():
            return (_dot(h, wxbc_ref[:, sl]), _dot(hp, wxbc_ref[:, sl]), _dot(hn, wxbc_ref[:, sl]))

        def fin(res):
            pre, pp, pn = res
            prev_row = pp[SUBLANES - 1:SUBLANES, :]
            next_row = pn[0:1, :]
            up = jnp.where(rows == 0, prev_row, pltpu.roll(pre, 1, 0)) * kprev
            down = jnp.where(rows == TM - 1, next_row, pltpu.roll(pre, TM - 1, 0)) * knext
            y = (convb_ref[:, sl] + up * convw_ref[0:1, sl] + pre * convw_ref[1:2, sl]
                 + down * convw_ref[2:3, sl])
            y = _silu(y).astype(BF16)
            lo = cb * cw
            if lo < D_INNER:
                xs_ref[:, lo:lo + cw] = y
            else:
                bc_ref[:, lo - D_INNER:lo - D_INNER + cw] = y
        return mm, fin

    stages = [q_stage(cb) for cb in range(ATTN_WIDTH // (2 * LANES))]
    stages.append((lambda: _dot(h, wqkv_ref[:, K_OFF:K_OFF + KV_WIDTH]), k_fin))
    stages.append((lambda: _dot(h, wqkv_ref[:, V_OFF:V_OFF + KV_WIDTH]), v_fin))
    stages.append((lambda: _dot(h, wdt_ref[...]), dt_fin))
    stages += [conv_stage(cb) for cb in range(CONV_DIM // cw)]
    stages.append((lambda: _dot(h, wz_ref[...]), z_fin))
    stages.append((lambda: _dot(h, wg_ref[...]), g_fin))
    pending = None
    for mm, fin in stages:
        res = mm()
        if pending is not None:
            pending[0](pending[1])
        pending = (fin, res)
    pending[0](pending[1])


def _in_proj(x, mod, meta, lw, t_ctx, ctx_len, layer, depth, prev_kv):
    t = x.shape[0]
    nblk = t // TM
    nhalo = t // SUBLANES
    per_halo = TM // SUBLANES
    n_ctx_blocks = t_ctx // TM
    nkb = TM // TK

    def row_blk(width):
        return pl.BlockSpec((TM, width), lambda i, *_: (i, 0))

    spb = TM // ctx_len
    ctx_blk = pl.BlockSpec((spb, None, ctx_len, KV_WIDTH),
                           lambda i, mrow, rope, ctxi: (ctxi[i], layer, 0, 0))
    in_specs = [
        row_blk(D_MODEL),
            pl.BlockSpec((SUBLANES, D_MODEL),
                         lambda i, *_: (jnp.maximum(i * per_halo - 1, 0), 0)),
            pl.BlockSpec((SUBLANES, D_MODEL),
                         lambda i, *_: (jnp.minimum((i + 1) * per_halo, nhalo - 1), 0)),
            pl.BlockSpec((1, MOD_ROWS, D_MODEL), lambda i, mrow, *_: (mrow[i], 0, 0)),
            _const_spec((1, D_MODEL)),
            _const_spec((D_MODEL, ATTN_WIDTH + 2 * KV_WIDTH)),
            _const_spec((D_MODEL, D_INNER)),
            _const_spec((D_MODEL, CONV_DIM)),
            _const_spec((D_MODEL, LANES)),
            _const_spec((D_MODEL, 2 * D_MODEL)),
            _const_spec((1, 2 * LANES)),
            _const_spec((1, 2 * LANES)),
            pl.BlockSpec((TM, LANES), lambda i, mrow, rope, *_: (rope[i], 0)),
            pl.BlockSpec((TM, LANES), lambda i, mrow, rope, *_: (rope[i], 0)),
            row_blk(LANES),
            row_blk(LANES),
            _const_spec((3, CONV_DIM)),
            _const_spec((1, CONV_DIM)),
            _const_spec((1, LANES)),
    ]
    out_specs = [row_blk(ATTN_WIDTH),
                   pl.BlockSpec((2, nkb, TK, LANES), lambda i, *_: (0, i, 0, 0)),
                   pl.BlockSpec((2, 2, nkb, V_ROWS, TK), lambda i, *_: (0, 0, i, 0, 0)),
                   ctx_blk, ctx_blk,
                   row_blk(D_INNER), row_blk(D_INNER), row_blk(2 * GN), row_blk(LANES),
                   row_blk(2 * D_MODEL)]
    out_shape = [
        jax.ShapeDtypeStruct((t, ATTN_WIDTH), BF16),
        jax.ShapeDtypeStruct((2, t // TK, TK, LANES), BF16),
        jax.ShapeDtypeStruct((2, 2, t // TK, V_ROWS, TK), BF16),
        jax.ShapeDtypeStruct((t_ctx // ctx_len, depth, ctx_len, KV_WIDTH), F32),
        jax.ShapeDtypeStruct((t_ctx // ctx_len, depth, ctx_len, KV_WIDTH), F32),
        jax.ShapeDtypeStruct((t, D_INNER), BF16),
        jax.ShapeDtypeStruct((t, D_INNER), BF16),
        jax.ShapeDtypeStruct((t, 2 * GN), BF16),
        jax.ShapeDtypeStruct((t, LANES), F32),
        jax.ShapeDtypeStruct((t, 2 * D_MODEL), BF16),
    ]
    args = [meta["mrow"], meta["rope_blk"], meta["ctx_blk"],
            x, x, x, mod, lw["norm1_w"], lw["wqkv"], lw["wz"], lw["wxbc"], lw["wdt"], lw["wg"],
            lw["q_norm_w"], lw["k_norm_w"], meta["cos"], meta["sin"], meta["keep_prev"],
            meta["keep_next"], lw["conv_w"], lw["conv_b"], lw["dt_bias"]]
    aliases = {}
    if prev_kv is not None:
        in_specs += [pl.BlockSpec(memory_space=pl.ANY)] * 2
        aliases = {len(args): 3, len(args) + 1: 4}
        args += list(prev_kv)
    grid_spec = pltpu.PrefetchScalarGridSpec(num_scalar_prefetch=3, grid=(nblk,), in_specs=in_specs,
                                             out_specs=out_specs)
    return pl.pallas_call(
        functools.partial(_in_proj_kernel, n_ctx_blocks=n_ctx_blocks, ctx_len=ctx_len),
        grid_spec=grid_spec, out_shape=out_shape, input_output_aliases=aliases,
        compiler_params=_cparams(("arbitrary",)), name="in_proj",
    )(*args)


def _attn_kernel(*refs, n_new, n_cache, tq):
    q_ref, kn_ref, vn_ref = refs[0:3]
    if n_cache:
        kc_ref, vc_ref = refs[3:5]
    o_ref, qt_ref, m_ref, acc_ref, sa_ref, mxa_ref, sb_ref, mxb_ref = refs[-8:]
    nq = 4 * tq
    zeros_half = jnp.zeros((HEAD_DIM, tq), F32)
    for hh in range(2):
        cols = []
        for jj in range(2):
            off = 2 * LANES * hh + LANES * jj
            qt = q_ref[:, off:off + LANES].astype(F32).T
            for s in range(2):
                head = qt[s * HEAD_DIM:(s + 1) * HEAD_DIM, :]
                parts = [head, zeros_half] if hh == 0 else [zeros_half, head]
                cols.append(jnp.concatenate(parts, axis=0))
        qt_ref[hh] = jnp.concatenate(cols, axis=1).astype(BF16)
        m_ref[hh] = jnp.full((1, nq), -jnp.inf, F32)
        acc_ref[hh] = jnp.zeros((V_ROWS, nq), F32)

    def scores(kblk, s_ref, mx_ref):
        for hh in range(2):
            s = _dot(kblk, qt_ref[hh])
            s_ref[hh] = s
            mx_ref[hh] = jnp.max(s, axis=0, keepdims=True)

    def consume(vt_of, s_ref, mx_ref):
        for hh in range(2):
            m = m_ref[hh]
            m_new = jnp.maximum(m, mx_ref[hh])
            alpha = jnp.exp2(m - m_new)
            p = jnp.exp2(s_ref[hh] - m_new).astype(BF16)
            acc_ref[hh] = alpha * acc_ref[hh] + _dot(vt_of(hh), p)
            m_ref[hh] = m_new

    bufs = ((sa_ref, mxa_ref), (sb_ref, mxb_ref))
    scores(kn_ref[0], *bufs[0])

    def body(j, carry):
        scores(kn_ref[2 * j + 1], *bufs[1])
        consume(lambda hh: vn_ref[hh, 2 * j], *bufs[0])
        scores(kn_ref[2 * j + 2], *bufs[0])
        consume(lambda hh: vn_ref[hh, 2 * j + 1], *bufs[1])
        return carry

    n_loop = (n_new - 1) // 2
    lax.fori_loop(0, n_loop, body, 0)
    rest = [(kn_ref, vn_ref, b) for b in range(2 * n_loop, n_new)]
    if n_cache:
        rest += [(kc_ref, vc_ref, b) for b in range(n_cache)]
    for r, (_, v_src, b) in enumerate(rest):
        if r + 1 < len(rest):
            k_nxt, _, b_nxt = rest[r + 1]
            scores(k_nxt[b_nxt], *bufs[(r + 1) % 2])
        consume(lambda hh, v_src=v_src, b=b: v_src[hh, b], *bufs[r % 2])

    for hh in range(2):
        oT = acc_ref[hh]
        oT = oT[0:HEAD_DIM, :] / oT[HEAD_DIM:HEAD_DIM + 1, :]
        for jj in range(2):
            pair = jnp.concatenate([oT[:, (2 * jj) * tq:(2 * jj + 1) * tq],
                                    oT[:, (2 * jj + 1) * tq:(2 * jj + 2) * tq]], axis=0)
            off = 2 * LANES * hh + LANES * jj
            o_ref[:, off:off + LANES] = pair.T.astype(BF16)


def _attention(q, kn, vn, cache, prev_out, nseq, seq_len, row0, tq):
    n_new = seq_len // TK
    nqb = seq_len // tq
    nq = 4 * tq
    q0 = row0 // tq
    kv0 = row0 // (n_new * TK)
    assert row0 % tq == 0 and row0 % (n_new * TK) == 0
    qspec = pl.BlockSpec((tq, 4 * LANES), lambda b, p, i: (q0 + b * nqb + i, p))
    in_specs = [
        qspec,
        pl.BlockSpec((None, n_new, TK, LANES), lambda b, p, i: (p, kv0 + b, 0, 0)),
        pl.BlockSpec((None, 2, n_new, V_ROWS, TK), lambda b, p, i: (p, 0, kv0 + b, 0, 0)),
    ]
    args = [q, kn, vn]
    n_cache = 0
    if cache is not None:
        kc, vc = cache
        n_cache = kc.shape[1] // nseq
        in_specs += [pl.BlockSpec((None, n_cache, TK, LANES), lambda b, p, i: (p, b, 0, 0)),
                     pl.BlockSpec((None, 2, n_cache, V_ROWS, TK), lambda b, p, i: (p, 0, b, 0, 0))]
        args += [kc, vc]
    aliases = {}
    kern = functools.partial(_attn_kernel, n_new=n_new, n_cache=n_cache, tq=tq)
    if prev_out is not None:
        n_in = len(args)
        in_specs.append(pl.BlockSpec(memory_space=pl.ANY))
        aliases = {n_in: 0}
        args.append(prev_out)
        inner = kern
        kern = lambda *refs: inner(*refs[:n_in], *refs[n_in + 1:])
    return pl.pallas_call(
        kern,
        grid=(nseq, 2, nqb),
        in_specs=in_specs,
        out_specs=qspec,
        out_shape=jax.ShapeDtypeStruct(q.shape, BF16),
        scratch_shapes=[pltpu.VMEM((2, LANES, nq), BF16),
                        pltpu.VMEM((2, 1, nq), F32),
                        pltpu.VMEM((2, V_ROWS, nq), F32),
                        pltpu.VMEM((2, TK, nq), F32),
                        pltpu.VMEM((2, 1, nq), F32),
                        pltpu.VMEM((2, TK, nq), F32),
                        pltpu.VMEM((2, 1, nq), F32)],
        input_output_aliases=aliases,
        compiler_params=_cparams(("arbitrary", "arbitrary", "arbitrary")),
        name="attention",
    )(*args)


def _cache_blocks(ck, cv):
    nseq, past = ck.shape[0], ck.shape[1]
    n = past // TK
    kb = ck.astype(BF16).reshape(nseq * n, TK, 2, LANES).transpose(2, 0, 1, 3)
    vb = cv.astype(BF16).reshape(nseq * n, TK, 2, 2, HEAD_DIM).transpose(2, 3, 0, 4, 1)
    ones = jnp.ones(vb.shape[:3] + (1, TK), BF16)
    zeros = jnp.zeros(vb.shape[:3] + (V_ROWS - HEAD_DIM - 1, TK), BF16)
    return kb, jnp.concatenate([vb, ones, zeros], axis=3)


def _ssd_direction(x_ref, bc_ref, dt_ref, a_lanes, expand, dvec_ref, st_ref, y_ref, d):
    q = SSD_CHUNK
    row = lax.broadcasted_iota(jnp.int32, (q, q), 0)
    col = lax.broadcasted_iota(jnp.int32, (q, q), 1)
    causal = (row >= col) if d == 0 else (row <= col)
    tril = jnp.where(row >= col, 1.0, 0.0).astype(BF16)

    dt = dt_ref[...]
    a = dt * a_lanes
    prefix = _dot_exact_lhs(tril, a)
    if d == 0:
        cs = prefix
        last = cs[q - 1:q, :]
    else:
        cs = prefix[q - 1:q, :] - prefix + a
        last = cs[0:1, :]
    cs2 = cs * LOG2E
    cst2 = (cs2 - jnp.log2(dt)).T
    wj = jnp.exp(last - cs) * dt
    wj_x = _dot(wj.astype(BF16), expand)
    elast = jnp.exp(last)

    lane = lax.broadcasted_iota(jnp.int32, (q, LANES), 1)
    lo_half = lane < SSM_HEAD_DIM
    lo_half_row = lo_half[0:1, :]
    for g in range(SSM_GROUPS):
        bg = bc_ref[:, g * D_STATE:(g + 1) * D_STATE]
        cg = bc_ref[:, GN + g * D_STATE:GN + (g + 1) * D_STATE]
        gmat = _dot_nt(cg, bg)
        gsl = slice(g * HEADS_PER_GROUP * SSM_HEAD_DIM, (g + 1) * HEADS_PER_GROUP * SSM_HEAD_DIM)
        st = st_ref[d, :, gsl]
        y_inter = _dot(cg, st.astype(BF16))
        xg = x_ref[:, gsl]
        el_parts = []
        for hp in range(HEADS_PER_GROUP // 2):
            xpair = xg[:, hp * LANES:(hp + 1) * LANES]
            l0 = d * SSM_HEADS + g * HEADS_PER_GROUP + 2 * hp
            ws, es = [], []
            for s in range(2):
                hl = l0 + s
                csb = jnp.broadcast_to(cs2[:, hl:hl + 1], (q, q))
                seg = csb - cst2[hl:hl + 1, :]
                ws.append(gmat * jnp.exp2(jnp.where(causal, seg, -jnp.inf)))
                es.append(jnp.exp2(csb))
            wcat = jnp.concatenate(ws, axis=1).astype(BF16)
            zero = jnp.zeros_like(xpair)
            xm = jnp.concatenate([jnp.where(lo_half, xpair, zero), jnp.where(lo_half, zero, xpair)],
                                 axis=0)
            e_pair = jnp.where(lo_half, es[0], es[1])
            ypair = y_inter[:, hp * LANES:(hp + 1) * LANES] * e_pair + _dot(wcat, xm)
            lo = g * HEADS_PER_GROUP * SSM_HEAD_DIM + hp * LANES
            if d == 0:
                ypair = ypair + dvec_ref[:, lo:lo + LANES] * xpair.astype(F32)
            y_ref[:, lo:lo + LANES] = ypair.astype(y_ref.dtype)
            el_parts.append(jnp.where(lo_half_row, elast[:, l0:l0 + 1], elast[:, l0 + 1:l0 + 2]))
        xw = (xg.astype(F32) * wj_x[:, gsl]).astype(BF16)
        el = jnp.concatenate(el_parts, axis=1)
        st_ref[d, :, gsl] = st * el + _dot_tn(bg, xw)


def _ssd_kernel(fblk_ref, bblk_ref, first_ref, last_ref, init_ref, s0i_ref, sfi_ref,
                xf_ref, bcf_ref, dtf_ref, xb_ref, bcb_ref, dtb_ref, a_ref, exp_ref, dvec_ref, s0_ref,
                *rest):
    yf_ref, yb_ref, sfin_ref, st_ref = rest[-4:]
    del fblk_ref, bblk_ref, s0i_ref, sfi_ref
    s = pl.program_id(0)

    @pl.when(jnp.logical_and(first_ref[s] == 1, init_ref[s] == 1))
    def _():
        st_ref[...] = s0_ref[0]

    @pl.when(jnp.logical_and(first_ref[s] == 1, init_ref[s] == 0))
    def _():
        st_ref[...] = jnp.zeros(st_ref.shape, F32)

    a_lanes = a_ref[...]
    _ssd_direction(xf_ref, bcf_ref, dtf_ref, a_lanes, exp_ref[0], dvec_ref, st_ref, yf_ref, 0)
    _ssd_direction(xb_ref, bcb_ref, dtb_ref, a_lanes, exp_ref[1], dvec_ref, st_ref, yb_ref, 1)

    @pl.when(jnp.logical_and(last_ref[s] == 1, init_ref[s] == 0))
    def _():
        sfin_ref[0] = st_ref[...]


def _ssd(xs, bc, dt, a_lanes, expand, dvec, s0, meta, n_ctx, layer, depth, prev_sfin):
    t = xs.shape[0]
    n_steps = t // SSD_CHUNK
    st_shape = (2, D_STATE, D_INNER)
    fwd = lambda s, fblk, bblk, *_: (fblk[s], 0)
    bwd = lambda s, fblk, bblk, *_: (bblk[s], 0)

    def specs(idx):
        return [pl.BlockSpec((SSD_CHUNK, D_INNER), idx),
                pl.BlockSpec((SSD_CHUNK, 2 * GN), idx),
                pl.BlockSpec((SSD_CHUNK, LANES), idx)]

    in_specs = specs(fwd) + specs(bwd) + [
        pl.BlockSpec((1, LANES), lambda s, *_: (0, 0)),
        pl.BlockSpec((2, LANES, D_INNER), lambda s, *_: (0, 0, 0)),
        pl.BlockSpec((1, D_INNER), lambda s, *_: (0, 0)),
        pl.BlockSpec((1,) + st_shape, lambda s, f, b, fi, la, ini, s0i, sfi: (s0i[s], 0, 0, 0)),
    ]
    args = [meta["ssd_fblk"], meta["ssd_bblk"], meta["ssd_first"], meta["ssd_last"], meta["ssd_init"],
            meta["ssd_s0i"], meta["ssd_sfi"], xs, bc, dt, xs, bc, dt, a_lanes, expand, dvec, s0]
    aliases = {}
    if prev_sfin is not None:
        in_specs.append(pl.BlockSpec(memory_space=pl.ANY))
        aliases = {len(args): 2}
        args.append(prev_sfin)
    grid_spec = pltpu.PrefetchScalarGridSpec(
        num_scalar_prefetch=7,
        grid=(n_steps,),
        in_specs=in_specs,
        out_specs=[pl.BlockSpec((SSD_CHUNK, D_INNER), fwd), pl.BlockSpec((SSD_CHUNK, D_INNER), bwd),
                   pl.BlockSpec((1, None) + st_shape,
                                lambda s, f, b, fi, la, ini, s0i, sfi: (sfi[s], layer, 0, 0, 0))],
        scratch_shapes=[pltpu.VMEM(st_shape, F32)],
    )
    out_shape = [jax.ShapeDtypeStruct((t, D_INNER), BF16), jax.ShapeDtypeStruct((t, D_INNER), BF16),
                 jax.ShapeDtypeStruct((n_ctx, depth) + st_shape, F32)]
    return pl.pallas_call(
        _ssd_kernel, grid_spec=grid_spec, out_shape=out_shape, input_output_aliases=aliases,
        compiler_params=_cparams(("arbitrary",)), name="ssd",
    )(*args)


def _route(sel, scores):
    neg = -jnp.inf
    rows = [sel[e:e + 1, :] for e in range(N_EXPERTS)]
    srow = [scores[e:e + 1, :] for e in range(N_EXPERTS)]
    best_score = None
    best = None
    for g in range(N_EXPERT_GROUPS):
        v = rows[g * EXPERTS_PER_GROUP:(g + 1) * EXPERTS_PER_GROUP]
        top2 = None
        for a in range(EXPERTS_PER_GROUP):
            for b in range(a + 1, EXPERTS_PER_GROUP):
                s = v[a] + v[b]
                top2 = s if top2 is None else jnp.maximum(top2, s)
        if g == 0:
            best_score, best = top2, jnp.zeros(top2.shape, jnp.int32)
        else:
            better = top2 > best_score
            best = jnp.where(better, g, best)
            best_score = jnp.where(better, top2, best_score)

    def pick(vals):
        out = vals[0]
        for g in range(1, N_EXPERT_GROUPS):
            out = jnp.where(best == g, vals[g], out)
        return out

    gsel = [pick([rows[g * EXPERTS_PER_GROUP + j] for g in range(N_EXPERT_GROUPS)])
            for j in range(EXPERTS_PER_GROUP)]
    gsc = [pick([srow[g * EXPERTS_PER_GROUP + j] for g in range(N_EXPERT_GROUPS)])
           for j in range(EXPERTS_PER_GROUP)]

    def argmax_first(vals):
        bi = jnp.zeros(vals[0].shape, jnp.int32)
        bv = vals[0]
        for j in range(1, len(vals)):
            better = vals[j] > bv
            bi = jnp.where(better, j, bi)
            bv = jnp.where(better, vals[j], bv)
        return bi

    i1 = argmax_first(gsel)
    i2 = argmax_first([jnp.where(i1 == j, neg, gsel[j]) for j in range(EXPERTS_PER_GROUP)])

    def take(vals, idx):
        out = vals[0]
        for j in range(1, len(vals)):
            out = jnp.where(idx == j, vals[j], out)
        return out

    g1 = take(gsc, i1)
    g2 = take(gsc, i2)
    tot = g1 + g2
    idx = jnp.concatenate([best * EXPERTS_PER_GROUP + i1, best * EXPERTS_PER_GROUP + i2], axis=0)
    gate = jnp.concatenate([g1 / tot, g2 / tot], axis=0)
    return idx, gate


def _merge_kernel(mrow_ref, x_ref, attn_ref, yf_ref, yb_ref, z_ref, g_ref, mod_ref,
                  wa_ref, ws_ref, wo_ref, n2w_ref, wrt_ref, rb_ref,
                  x1_ref, h2_ref, idx_ref, gate_ref):
    del mrow_ref
    gw = D_INNER // SSM_GROUPS
    hr = TM // MERGE_CHAINS

    def chain(ci):
        rs = slice(ci * hr, (ci + 1) * hr)
        attn_o = _dot(attn_ref[rs, :], wa_ref[...])
        yield
        ssm_o = None
        for g in range(SSM_GROUPS):
            sl = slice(g * gw, (g + 1) * gw)
            y = (yf_ref[rs, sl] + yb_ref[rs, sl]).astype(F32) * _silu(z_ref[rs, sl].astype(F32))
            ms = jnp.mean(y * y, axis=-1, keepdims=True)
            yn = (y * lax.rsqrt(ms + EPS)).astype(BF16)
            part = _dot(yn, ws_ref[sl, :])
            ssm_o = part if ssm_o is None else ssm_o + part
            yield
        merged = (g_ref[rs, 0:D_MODEL].astype(F32) * attn_o
                  + g_ref[rs, D_MODEL:2 * D_MODEL].astype(F32) * ssm_o)
        out = _dot(merged.astype(BF16), wo_ref[...])
        yield
        x1 = x_ref[rs, :] + mod_ref[0, 2:3, :] * out
        x1_ref[rs, :] = x1
        h2 = _modnorm(x1, n2w_ref[...], mod_ref[0, 4:5, :], mod_ref[0, 3:4, :])
        h2_ref[rs, :] = h2.astype(BF16)
        logits = lax.dot_general(wrt_ref[...], h2, (((1,), (1,)), ((), ())),
                                 preferred_element_type=F32, precision=lax.Precision.HIGHEST)
        yield
        scores = _sigmoid(logits)
        idx, gate = _route(scores + rb_ref[...], scores)
        idx_ref[:, rs] = idx
        gate_ref[:, rs] = gate

    live = [chain(ci) for ci in range(MERGE_CHAINS)]
    while live:
        for c in list(live):
            try:
                next(c)
            except StopIteration:
                live.remove(c)


def _merge(x, attn, yf, yb, z, g, mod, meta, lw, wrt, rb):
    t = x.shape[0]
    nblk = t // TM

    def row_blk(width):
        return pl.BlockSpec((TM, width), lambda i, *_: (i, 0))

    grid_spec = pltpu.PrefetchScalarGridSpec(
        num_scalar_prefetch=1,
        grid=(nblk,),
        in_specs=[row_blk(D_MODEL), row_blk(ATTN_WIDTH), row_blk(D_INNER), row_blk(D_INNER),
                  row_blk(D_INNER), row_blk(2 * D_MODEL),
                  pl.BlockSpec((1, MOD_ROWS, D_MODEL), lambda i, mrow: (mrow[i], 0, 0)),
                  _const_spec((ATTN_WIDTH, D_MODEL)), _const_spec((D_INNER, D_MODEL)),
                  _const_spec((D_MODEL, D_MODEL)), _const_spec((1, D_MODEL)),
                  _const_spec((N_EXPERTS, D_MODEL)), _const_spec((N_EXPERTS, 1))],
        out_specs=[row_blk(D_MODEL), row_blk(D_MODEL),
                   pl.BlockSpec((2, TM), lambda i, *_: (0, i)),
                   pl.BlockSpec((2, TM), lambda i, *_: (0, i))],
    )
    out_shape = [jax.ShapeDtypeStruct((t, D_MODEL), F32), jax.ShapeDtypeStruct((t, D_MODEL), BF16),
                 jax.ShapeDtypeStruct((2, t), jnp.int32), jax.ShapeDtypeStruct((2, t), F32)]
    return pl.pallas_call(
        _merge_kernel, grid_spec=grid_spec, out_shape=out_shape,
        compiler_params=_cparams(("arbitrary",)), name="merge",
    )(meta["mrow"], x, attn, yf, yb, z, g, mod, lw["w_attn_out"], lw["w_ssm_out"], lw["w_out"],
      lw["norm2_w"], wrt, rb)


def _expert_kernel(te_ref, nt_ref, chg_ref, x_ref, wg_ref, wu_ref, wd_ref, o_ref,
                   wgs_ref, wus_ref, wds_ref):
    del te_ref
    i = pl.program_id(0)

    @pl.when(chg_ref[i] == 1)
    def _():
        wgs_ref[...] = wg_ref[0].astype(BF16)
        wus_ref[...] = wu_ref[0].astype(BF16)
        wds_ref[...] = wd_ref[0].astype(BF16)

    @pl.when(i < nt_ref[0])
    def _():
        x = x_ref[...]
        hmid = _silu(_dot(x, wgs_ref[...])) * _dot(x, wus_ref[...])
        o_ref[...] = _dot(hmid.astype(BF16), wds_ref[...]).astype(o_ref.dtype)

    @pl.when(i >= nt_ref[0])
    def _():
        o_ref[...] = jnp.zeros(o_ref.shape, o_ref.dtype)


def _experts(x_sorted, tile_expert, n_tiles, wg, wu, wd, layer):
    rows = x_sorted.shape[0]
    changed = jnp.concatenate([jnp.ones((1,), jnp.int32),
                               (tile_expert[1:] != tile_expert[:-1]).astype(jnp.int32)])
    grid_spec = pltpu.PrefetchScalarGridSpec(
        num_scalar_prefetch=3,
        grid=(rows // TE,),
        in_specs=[pl.BlockSpec((TE, D_MODEL), lambda i, te, nt, chg: (i, 0)),
                  pl.BlockSpec((None, 1, D_MODEL, D_FF_EXPERT), lambda i, te, nt, chg: (layer, te[i], 0, 0)),
                  pl.BlockSpec((None, 1, D_MODEL, D_FF_EXPERT), lambda i, te, nt, chg: (layer, te[i], 0, 0)),
                  pl.BlockSpec((None, 1, D_FF_EXPERT, D_MODEL), lambda i, te, nt, chg: (layer, te[i], 0, 0))],
        out_specs=pl.BlockSpec((TE, D_MODEL), lambda i, te, nt, chg: (i, 0)),
        scratch_shapes=[pltpu.VMEM((D_MODEL, D_FF_EXPERT), BF16),
                        pltpu.VMEM((D_MODEL, D_FF_EXPERT), BF16),
                        pltpu.VMEM((D_FF_EXPERT, D_MODEL), BF16)],
    )
    return pl.pallas_call(
        _expert_kernel, grid_spec=grid_spec,
        out_shape=jax.ShapeDtypeStruct((rows, D_MODEL), BF16),
        compiler_params=_cparams(("arbitrary",)), name="experts",
    )(tile_expert, n_tiles, changed, x_sorted, wg, wu, wd)


def _combine_kernel(mrow_ref, x1_ref, y0_ref, y1_ref, gate_ref, mod_ref, fnw_ref, *out_refs,
                    n_ctx_blocks):
    del mrow_ref
    moe = (gate_ref[:, 0:1] * y0_ref[...].astype(F32) + gate_ref[:, 1:2] * y1_ref[...].astype(F32))
    x2 = x1_ref[...] + mod_ref[0, 5:6, :] * moe
    if n_ctx_blocks is None:
        out_refs[0][...] = x2
        return
    ms = jnp.mean(x2 * x2, axis=-1, keepdims=True)
    y = x2 * lax.rsqrt(ms + EPS) * fnw_ref[...]
    is_ctx = pl.program_id(0) < n_ctx_blocks

    @pl.when(is_ctx)
    def _():
        out_refs[0][...] = y

    @pl.when(jnp.logical_not(is_ctx))
    def _():
        out_refs[1][...] = y


def _combine(x1, y0, y1, gate_t, mod, meta, fnw, t_ctx):
    t = x1.shape[0]

    def row_blk(width):
        return pl.BlockSpec((TM, width), lambda i, *_: (i, 0))

    if t_ctx is None:
        n_ctx_blocks = None
        out_specs = row_blk(D_MODEL)
        out_shape = jax.ShapeDtypeStruct((t, D_MODEL), F32)
    else:
        n_ctx_blocks = t_ctx // TM
        out_specs = [pl.BlockSpec((TM, D_MODEL), lambda i, *_: (jnp.minimum(i, n_ctx_blocks - 1), 0)),
                     pl.BlockSpec((TM, D_MODEL), lambda i, *_: (jnp.maximum(i - n_ctx_blocks, 0), 0))]
        out_shape = [jax.ShapeDtypeStruct((t_ctx, D_MODEL), F32),
                     jax.ShapeDtypeStruct((t - t_ctx, D_MODEL), F32)]
    grid_spec = pltpu.PrefetchScalarGridSpec(
        num_scalar_prefetch=1,
        grid=(t // TM,),
        in_specs=[row_blk(D_MODEL), row_blk(D_MODEL), row_blk(D_MODEL), row_blk(2),
                  pl.BlockSpec((1, MOD_ROWS, D_MODEL), lambda i, mrow: (mrow[i], 0, 0)),
                  _const_spec((1, D_MODEL))],
        out_specs=out_specs,
    )
    return pl.pallas_call(
        functools.partial(_combine_kernel, n_ctx_blocks=n_ctx_blocks), grid_spec=grid_spec,
        out_shape=out_shape,
        compiler_params=_cparams(("arbitrary",)), name="combine",
    )(meta["mrow"], x1, y0, y1, gate_t, mod, fnw)


def _moe(h2, idx, gate, w_exp, layer):
    t = h2.shape[0]
    n_assign = 2 * t
    rows = n_assign + N_EXPERTS * TE
    e_flat = idx.reshape(n_assign)
    onehot = e_flat[:, None] == jnp.arange(N_EXPERTS, dtype=jnp.int32)[None, :]
    nb = n_assign // RANK_BLOCK
    oh = onehot.astype(BF16).reshape(nb, RANK_BLOCK, N_EXPERTS)
    tril = jnp.tril(jnp.ones((RANK_BLOCK, RANK_BLOCK), BF16))
    local = jnp.einsum("ij,bjk->bik", tril, oh, preferred_element_type=F32)
    bsum = local[:, -1, :]
    before = jnp.dot(jnp.tril(jnp.ones((nb, nb), F32), -1), bsum, precision=lax.Precision.HIGHEST)
    running = (local + before[:, None, :]).reshape(n_assign, N_EXPERTS)
    counts = (before[-1] + bsum[-1]).astype(jnp.int32)
    padded = ((counts + TE - 1) // TE) * TE
    pad_end = jnp.cumsum(padded)
    pad_off = pad_end - padded
    off = jnp.cumsum(counts) - counts
    pos = jnp.sum(jnp.where(onehot, running + (pad_off - 1).astype(F32)[None, :], 0.0), axis=1
                  ).astype(jnp.int32)
    order = jnp.argsort(e_flat, stable=True).astype(jnp.int32)
    p = jnp.arange(rows, dtype=jnp.int32)
    ep = jnp.minimum(jnp.sum(p[:, None] >= pad_end[None, :], axis=1), N_EXPERTS - 1).astype(jnp.int32)
    r = p - pad_off[ep]
    valid = r < counts[ep]
    src = jnp.where(valid, order[jnp.clip(off[ep] + r, 0, n_assign - 1)] % t, p % t)
    n_tiles = (pad_end[-1] // TE).astype(jnp.int32).reshape(1)
    tile_start = jnp.arange(rows // TE, dtype=jnp.int32) * TE
    tile_expert = jnp.minimum(jnp.sum(tile_start[:, None] >= pad_end[None, :], axis=1),
                              N_EXPERTS - 1).astype(jnp.int32)
    last_used = tile_expert[jnp.maximum(n_tiles[0] - 1, 0)]
    tile_expert = jnp.where(tile_start < pad_end[-1], tile_expert, last_used)
    x_sorted = h2.at[src].get(mode="promise_in_bounds")
    y_sorted = _experts(x_sorted, tile_expert, n_tiles, *w_exp, layer)
    pos2 = pos.reshape(2, t)
    return (y_sorted.at[pos2[0]].get(mode="promise_in_bounds"),
            y_sorted.at[pos2[1]].get(mode="promise_in_bounds"))


def _rope_tables(n_tokens):
    rows = n_tokens // GRID_W
    row = jnp.repeat(jnp.arange(rows, dtype=F32), GRID_W)
    col = jnp.tile(jnp.arange(GRID_W, dtype=F32), rows)
    inv = 1.0 / (ROPE_THETA ** (jnp.arange(0, ROPE_AXIS_DIM, 2, dtype=F32) / ROPE_AXIS_DIM))
    ar = row[:, None] * inv
    ac = col[:, None] * inv
    ang = jnp.concatenate([ar, ar, ac, ac], axis=-1)
    cos, sin = jnp.cos(ang), jnp.sin(ang)
    cos = jnp.concatenate([jnp.ones((TM, HEAD_DIM), F32), cos], axis=0)
    sin = jnp.concatenate([jnp.zeros((TM, HEAD_DIM), F32), sin], axis=0)
    return jnp.tile(cos, (1, 2)), jnp.tile(sin, (1, 2))


def _block_meta(n_ctx, ctx_len, n_lat, lat_len):
    ctx_blocks = n_ctx * ctx_len // TM
    lat_blocks = n_lat * lat_len // TM
    per_lat = lat_len // TM
    bi = jnp.arange(ctx_blocks + lat_blocks, dtype=jnp.int32)
    is_ctx = bi < ctx_blocks
    lat_i = bi - ctx_blocks
    mrow = jnp.where(is_ctx, 0, 1 + lat_i // per_lat).astype(jnp.int32)
    rope_blk = jnp.where(is_ctx, 0, 1 + lat_i % per_lat).astype(jnp.int32)
    cos, sin = _rope_tables(lat_len)
    ti = jnp.arange(n_ctx * ctx_len + n_lat * lat_len, dtype=jnp.int32)
    pos = jnp.where(ti < n_ctx * ctx_len, ti % ctx_len, (ti - n_ctx * ctx_len) % lat_len)
    slen = jnp.where(ti < n_ctx * ctx_len, ctx_len, lat_len)
    keep_prev = jnp.broadcast_to((pos != 0).astype(F32)[:, None], (ti.shape[0], LANES))
    keep_next = jnp.broadcast_to((pos != slen - 1).astype(F32)[:, None], (ti.shape[0], LANES))
    meta = {"mrow": mrow, "rope_blk": rope_blk, "cos": cos, "sin": sin,
            "ctx_blk": jnp.minimum(bi, ctx_blocks - 1).astype(jnp.int32),
            "keep_prev": keep_prev, "keep_next": keep_next}

    nc_ctx, nc_lat = ctx_len // SSD_CHUNK, lat_len // SSD_CHUNK
    n_ctx_steps = n_ctx * nc_ctx
    si = jnp.arange(n_ctx_steps + n_lat * nc_lat, dtype=jnp.int32)
    s_ctx = si < n_ctx_steps
    li = si - n_ctx_steps
    seq = jnp.where(s_ctx, si // nc_ctx, li // nc_lat)
    ch = jnp.where(s_ctx, si % nc_ctx, li % nc_lat)
    nc = jnp.where(s_ctx, nc_ctx, nc_lat)
    base = jnp.where(s_ctx, seq * nc_ctx, n_ctx_steps + seq * nc_lat)
    meta.update({
        "ssd_fblk": (base + ch).astype(jnp.int32),
        "ssd_bblk": (base + nc - 1 - ch).astype(jnp.int32),
        "ssd_first": (ch == 0).astype(jnp.int32),
        "ssd_last": (ch == nc - 1).astype(jnp.int32),
        "ssd_init": jnp.where(s_ctx, 0, 1).astype(jnp.int32),
        "ssd_s0i": jnp.where(s_ctx, 0, seq).astype(jnp.int32),
        "ssd_sfi": jnp.where(s_ctx, seq, n_ctx - 1).astype(jnp.int32),
    })
    return meta


def kernel(x_prompt, x_sample, cache_k, cache_v, state_ssm, c, c_ctx, norm1_w, norm2_w, w_mod, b_mod,
           w_in, q_norm_w, k_norm_w, conv_w, conv_b, a_log, dt_bias, ssm_d, ssm_norm_w, w_attn_out,
           w_ssm_out, w_out, w_router, router_bias, w_exp_gate, w_exp_up, w_exp_down, final_norm_w):
    n_ctx, ctx_len, _ = x_prompt.shape
    n_lat, lat_len, _ = x_sample.shape
    depth = w_in.shape[0]
    t_ctx = n_ctx * ctx_len
    t_lat = n_lat * lat_len
    assert t_ctx % TM == 0 and lat_len % TM == 0 and 1 + n_lat <= COND_ROWS and TM % ctx_len == 0
    assert ctx_len % TK == 0 and cache_k.shape[2] % TK == 0 and lat_len % GRID_W == 0

    meta = _block_meta(n_ctx, ctx_len, n_lat, lat_len)
    x = jnp.concatenate([x_prompt.reshape(t_ctx, D_MODEL), x_sample.reshape(t_lat, D_MODEL)], axis=0)

    cond = jnp.zeros((COND_ROWS, D_MODEL), F32).at[0].set(c_ctx).at[1:1 + n_lat].set(c)
    mod_all = _adaln(cond, w_mod, b_mod).reshape(depth, COND_ROWS, N_MOD, D_MODEL)
    mod_all = jnp.pad(mod_all, ((0, 0), (0, 0), (0, MOD_ROWS - N_MOD), (0, 0)))

    head_of_lane = jnp.arange(D_INNER, dtype=jnp.int32) // SSM_HEAD_DIM
    lane_id = jnp.arange(LANES, dtype=jnp.int32)
    expand = jnp.stack([(lane_id[:, None] == head_of_lane[None, :] + d * SSM_HEADS) for d in range(2)]
                       ).astype(BF16)
    wrt = w_router.T
    rb = router_bias.reshape(N_EXPERTS, 1)
    fnw = final_norm_w.reshape(1, D_MODEL)

    kv_ctx, s_ctx = None, None
    for l in range(depth):
        wl = w_in[l]
        lw = {
            "norm1_w": norm1_w[l].reshape(1, D_MODEL),
            "norm2_w": norm2_w[l].reshape(1, D_MODEL),
            "wqkv": wl[:, :Z_OFF].astype(BF16),
            "wz": wl[:, Z_OFF:XBC_OFF].astype(BF16),
            "wxbc": wl[:, XBC_OFF:DT_OFF].astype(BF16),
            "wdt": jnp.pad(wl[:, DT_OFF:G_OFF], ((0, 0), (0, LANES - 2 * SSM_HEADS))).astype(BF16),
            "wg": wl[:, G_OFF:].astype(BF16),
            "q_norm_w": jnp.tile(q_norm_w[l], 4).reshape(1, 2 * LANES),
            "k_norm_w": jnp.tile(k_norm_w[l], 4).reshape(1, 2 * LANES),
            "conv_w": conv_w[l],
            "conv_b": conv_b[l].reshape(1, CONV_DIM),
            "dt_bias": jnp.pad(dt_bias[l].reshape(1, 2 * SSM_HEADS), ((0, 0), (0, LANES - 2 * SSM_HEADS))),
            "ssm_d": jnp.repeat(ssm_d[l], SSM_HEAD_DIM).reshape(1, D_INNER),
            "w_attn_out": w_attn_out[l].astype(BF16),
            "w_ssm_out": (ssm_norm_w[l][:, None] * w_ssm_out[l]).astype(BF16),
            "w_out": w_out[l].astype(BF16),
        }
        mod = mod_all[l]
        q, kn, vn, k_ctx, v_ctx, z, xs, bc, dt, g = _in_proj(x, mod, meta, lw, t_ctx, ctx_len, l, depth,
                                                             kv_ctx)
        kv_ctx = (k_ctx, v_ctx)

        attn = _attention(q, kn, vn, None, None, n_ctx, ctx_len, 0, min(TQ, ctx_len))
        attn = _attention(q, kn, vn, _cache_blocks(cache_k[:, l], cache_v[:, l]), attn,
                          n_lat, lat_len, t_ctx, min(TQ, lat_len))

        a_lanes = jnp.pad(-jnp.exp(a_log[l]).reshape(1, 2 * SSM_HEADS),
                          ((0, 0), (0, LANES - 2 * SSM_HEADS)))
        s0 = state_ssm[:, l].transpose(0, 1, 4, 2, 3).reshape(n_lat, 2, D_STATE, D_INNER)
        yf, yb, s_ctx = _ssd(xs, bc, dt, a_lanes, expand, lw["ssm_d"], s0, meta, n_ctx, l, depth, s_ctx)

        x1, h2, idx, gate = _merge(x, attn, yf, yb, z, g, mod, meta, lw, wrt, rb)
        y0, y1 = _moe(h2, idx, gate, (w_exp_gate, w_exp_up, w_exp_down), l)
        if l < depth - 1:
            x = _combine(x1, y0, y1, gate.T, mod, meta, fnw, None)
        else:
            y_prompt, y_sample = _combine(x1, y0, y1, gate.T, mod, meta, fnw, t_ctx)

    y_prompt = y_prompt.reshape(n_ctx, ctx_len, D_MODEL)
    y_sample = y_sample.reshape(n_lat, lat_len, D_MODEL)
    new_k = kv_ctx[0].reshape(n_ctx, depth, ctx_len, N_KV_HEADS, HEAD_DIM)
    new_v = kv_ctx[1].reshape(n_ctx, depth, ctx_len, N_KV_HEADS, HEAD_DIM)
    new_s = s_ctx.reshape(n_ctx, depth, 2, D_STATE, SSM_HEADS, SSM_HEAD_DIM).transpose(0, 1, 2, 4, 5, 3)
    return (y_prompt, y_sample, new_k, new_v, new_s)
```

```python
import functools

import jax
import jax.numpy as jnp
from jax import lax
from jax.experimental import pallas as pl
from jax.experimental.pallas import tpu as pltpu

F32 = jnp.float32
BF16 = jnp.bfloat16

EPS = 1e-6
D_MODEL = 1024
N_HEADS = 16
N_KV_HEADS = 4
HEAD_DIM = 64
ATTN_WIDTH = N_HEADS * HEAD_DIM
KV_WIDTH = N_KV_HEADS * HEAD_DIM
GRID_W = 64
ROPE_AXIS_DIM = HEAD_DIM // 2
ROPE_THETA = 10000.0
D_INNER = 2 * D_MODEL
SSM_HEAD_DIM = 64
SSM_HEADS = D_INNER // SSM_HEAD_DIM
SSM_GROUPS = 4
HEADS_PER_GROUP = SSM_HEADS // SSM_GROUPS
D_STATE = 128
GN = SSM_GROUPS * D_STATE
CONV_DIM = D_INNER + 2 * GN
SSD_CHUNK = 128
N_EXPERTS = 16
N_EXPERT_GROUPS = 4
EXPERTS_PER_GROUP = N_EXPERTS // N_EXPERT_GROUPS
D_FF_EXPERT = 512
N_MOD = 6
MOD_ROWS = 8

LANES = 128
SUBLANES = 8
VMEM_LIMIT = 56 * 1024 * 1024

TM = 512
TQ = 512
TK = 256
TE = 512
MERGE_CHAINS = 4
RANK_BLOCK = 512
CONV_CHUNK = 1024
COND_ROWS = 16
V_ROWS = HEAD_DIM + SUBLANES
LOG2E = 1.4426950408889634
Q_SCALE = HEAD_DIM ** -0.5 * LOG2E

Q_OFF, K_OFF, V_OFF = 0, ATTN_WIDTH, ATTN_WIDTH + KV_WIDTH
Z_OFF = ATTN_WIDTH + 2 * KV_WIDTH
XBC_OFF = Z_OFF + D_INNER
DT_OFF = XBC_OFF + CONV_DIM
G_OFF = DT_OFF + 2 * SSM_HEADS
N_IN = G_OFF + 2 * D_MODEL


def _cparams(sem):
    return pltpu.CompilerParams(dimension_semantics=sem, vmem_limit_bytes=VMEM_LIMIT)


def _const_spec(shape):
    nd = len(shape)
    return pl.BlockSpec(shape, lambda *_: (0,) * nd, pipeline_mode=pl.Buffered(1))


def _dot(a, b):
    return jnp.dot(a, b, preferred_element_type=F32)


def _dot_nt(a, b):
    return lax.dot_general(a, b, (((1,), (1,)), ((), ())), preferred_element_type=F32)


def _dot_tn(a, b):
    return lax.dot_general(a, b, (((0,), (0,)), ((), ())), preferred_element_type=F32)


def _split3(a):
    a1 = a.astype(BF16)
    r = a - a1.astype(F32)
    a2 = r.astype(BF16)
    a3 = (r - a2.astype(F32)).astype(BF16)
    return a1, a2, a3


def _dot_exact_lhs(m_bf16, a):
    a1, a2, a3 = _split3(a)
    return _dot(m_bf16, a1) + _dot(m_bf16, a2) + _dot(m_bf16, a3)


def _modnorm(x, w, sc, sh):
    ms = jnp.mean(x * x, axis=-1, keepdims=True)
    return x * lax.rsqrt(ms + EPS) * (w * (1.0 + sc)) + sh


def _sigmoid(x):
    return 1.0 / (1.0 + jnp.exp(-x))


def _silu(x):
    return x * _sigmoid(x)


def _adaln_kernel(c_ref, w_ref, b_ref, o_ref):
    cs = _silu(c_ref[...])
    o_ref[0] = jnp.dot(cs, w_ref[0], preferred_element_type=F32,
                       precision=lax.Precision.HIGHEST) + b_ref[0]


def _adaln(cond, w_mod, b_mod):
    depth = w_mod.shape[0]
    nb = N_MOD
    return pl.pallas_call(
        _adaln_kernel,
        grid=(depth, nb),
        in_specs=[pl.BlockSpec((COND_ROWS, D_MODEL), lambda l, j: (0, 0)),
                  pl.BlockSpec((1, D_MODEL, D_MODEL), lambda l, j: (l, 0, j)),
                  pl.BlockSpec((1, 1, D_MODEL), lambda l, j: (l, 0, j))],
        out_specs=pl.BlockSpec((1, COND_ROWS, D_MODEL), lambda l, j: (l, 0, j)),
        out_shape=jax.ShapeDtypeStruct((depth, COND_ROWS, N_MOD * D_MODEL), F32),
        compiler_params=_cparams(("arbitrary", "arbitrary")),
        name="adaln",
    )(cond, w_mod, b_mod.reshape(depth, 1, N_MOD * D_MODEL))


def _in_proj_kernel(mrow_ref, rope_ref, ctxi_ref,
                    x_ref, xp_ref, xn_ref, mod_ref, n1w_ref,
                    wqkv_ref, wz_ref, wxbc_ref, wdt_ref, wg_ref,
                    qnw_ref, knw_ref, cos_ref, sin_ref, kprev_ref, knext_ref,
                    convw_ref, convb_ref, dtb_ref,
                    *rest, n_ctx_blocks, ctx_len):
    q_ref, kb_ref, vt_ref, kc_ref, vc_ref, z_ref, xs_ref, bc_ref, dt_ref, g_ref = rest[-10:]
    del mrow_ref, rope_ref, ctxi_ref
    i = pl.program_id(0)
    is_ctx = i < n_ctx_blocks
    sh1 = mod_ref[0, 0:1, :]
    sc1 = mod_ref[0, 1:2, :]
    nw = n1w_ref[...]
    h = _modnorm(x_ref[...], nw, sc1, sh1).astype(BF16)
    hp = _modnorm(xp_ref[...], nw, sc1, sh1).astype(BF16)
    hn = _modnorm(xn_ref[...], nw, sc1, sh1).astype(BF16)

    r = lax.broadcasted_iota(jnp.int32, (2 * LANES, 2 * LANES), 0) // HEAD_DIM
    c = lax.broadcasted_iota(jnp.int32, (2 * LANES, 2 * LANES), 1) // HEAD_DIM
    same_head = jnp.where(r == c, 1.0, 0.0).astype(BF16)
    lane = lax.broadcasted_iota(jnp.int32, (TM, LANES), 1)
    first_half = (lane % (HEAD_DIM // 2)) < (HEAD_DIM // 4)

    def head_norm_rope(t, w, scale):
        sq = t * t
        hi = sq.astype(BF16)
        lo = (sq - hi.astype(F32)).astype(BF16)
        ssum = _dot(hi, same_head) + _dot(lo, same_head)
        tn = t * lax.rsqrt(ssum * (1.0 / HEAD_DIM) + EPS) * w
        outs = []
        for s in range(2):
            a = tn[:, s * LANES:(s + 1) * LANES]
            rot = jnp.where(first_half, -pltpu.roll(a, LANES - HEAD_DIM // 4, 1),
                            pltpu.roll(a, HEAD_DIM // 4, 1))
            outs.append((a * cos_ref[...] + rot * sin_ref[...]) * scale)
        return outs

    qw = qnw_ref[...]
    kw = knw_ref[...]
    ones_rows = jnp.where(lax.broadcasted_iota(jnp.int32, (V_ROWS - HEAD_DIM, TK), 0) == 0,
                          1.0, 0.0).astype(BF16)
    cw = CONV_CHUNK
    rows = lax.broadcasted_iota(jnp.int32, (TM, cw), 0)
    kprev = jnp.concatenate([kprev_ref[...]] * (cw // LANES), axis=1)
    knext = jnp.concatenate([knext_ref[...]] * (cw // LANES), axis=1)

    def q_stage(cb):
        def mm():
            return _dot(h, wqkv_ref[:, cb * 2 * LANES:(cb + 1) * 2 * LANES])

        def fin(t):
            a, b = head_norm_rope(t, qw, Q_SCALE)
            q_ref[:, cb * 2 * LANES:cb * 2 * LANES + LANES] = a.astype(BF16)
            q_ref[:, cb * 2 * LANES + LANES:(cb + 1) * 2 * LANES] = b.astype(BF16)
        return mm, fin

    def k_fin(t):
        kpairs = head_norm_rope(t, kw, 1.0)
        for rb in range(TM // TK):
            for pr in range(2):
                kb_ref[pr, rb] = kpairs[pr][rb * TK:(rb + 1) * TK, :].astype(BF16)

        @pl.when(is_ctx)
        def _():
            for sq in range(TM // ctx_len):
                for pr in range(2):
                    kc_ref[sq, :, pr * LANES:(pr + 1) * LANES] = kpairs[pr][sq * ctx_len:(sq + 1) * ctx_len, :]

    def v_fin(v):
        for rb in range(TM // TK):
            vT = v[rb * TK:(rb + 1) * TK, :].T
            for hd in range(N_KV_HEADS):
                vt_ref[hd // 2, hd % 2, rb, 0:HEAD_DIM, :] = (
                    vT[hd * HEAD_DIM:(hd + 1) * HEAD_DIM, :].astype(BF16))
                vt_ref[hd // 2, hd % 2, rb, HEAD_DIM:V_ROWS, :] = ones_rows

        @pl.when(is_ctx)
        def _():
            for sq in range(TM // ctx_len):
                vc_ref[sq] = v[sq * ctx_len:(sq + 1) * ctx_len, :]

    def z_fin(t):
        z_ref[...] = t.astype(BF16)

    def g_fin(t):
        g_ref[...] = _sigmoid(t).astype(BF16)

    def dt_fin(t):
        dtr = t + dtb_ref[...]
        dt_ref[...] = jnp.maximum(dtr, 0.0) + jnp.log(1.0 + jnp.exp(-jnp.abs(dtr)))

    def conv_stage(cb):
        sl = slice(cb * cw, (cb + 1) * cw)

        def mm():
            return (_dot(h, wxbc_ref[:, sl]), _dot(hp, wxbc_ref[:, sl]), _dot(hn, wxbc_ref[:, sl]))

        def fin(res):
            pre, pp, pn = res
            prev_row = pp[SUBLANES - 1:SUBLANES, :]
            next_row = pn[0:1, :]
            up = jnp.where(rows == 0, prev_row, pltpu.roll(pre, 1, 0)) * kprev
            down = jnp.where(rows == TM - 1, next_row, pltpu.roll(pre, TM - 1, 0)) * knext
            y = (convb_ref[:, sl] + up * convw_ref[0:1, sl] + pre * convw_ref[1:2, sl]
                 + down * convw_ref[2:3, sl])
            y = _silu(y).astype(BF16)
            lo = cb * cw
            if lo < D_INNER:
                xs_ref[:, lo:lo + cw] = y
            else:
                bc_ref[:, lo - D_INNER:lo - D_INNER + cw] = y
        return mm, fin

    stages = [q_stage(cb) for cb in range(ATTN_WIDTH // (2 * LANES))]
    stages.append((lambda: _dot(h, wqkv_ref[:, K_OFF:K_OFF + KV_WIDTH]), k_fin))
    stages.append((lambda: _dot(h, wqkv_ref[:, V_OFF:V_OFF + KV_WIDTH]), v_fin))
    stages.append((lambda: _dot(h, wdt_ref[...]), dt_fin))
    stages += [conv_stage(cb) for cb in range(CONV_DIM // cw)]
    stages.append((lambda: _dot(h, wz_ref[...]), z_fin))
    stages.append((lambda: _dot(h, wg_ref[...]), g_fin))
    pending = None
    for mm, fin in stages:
        res = mm()
        if pending is not None:
            pending[0](pending[1])
        pending = (fin, res)
    pending[0](pending[1])


def _in_proj(x, mod, meta, lw, t_ctx, ctx_len, layer, depth, prev_kv):
    t = x.shape[0]
    nblk = t // TM
    nhalo = t // SUBLANES
    per_halo = TM // SUBLANES
    n_ctx_blocks = t_ctx // TM
    nkb = TM // TK

    def row_blk(width):
        return pl.BlockSpec((TM, width), lambda i, *_: (i, 0))

    spb = TM // ctx_len
    ctx_blk = pl.BlockSpec((spb, None, ctx_len, KV_WIDTH),
                           lambda i, mrow, rope, ctxi: (ctxi[i], layer, 0, 0))
    in_specs = [
        row_blk(D_MODEL),
            pl.BlockSpec((SUBLANES, D_MODEL),
                         lambda i, *_: (jnp.maximum(i * per_halo - 1, 0), 0)),
            pl.BlockSpec((SUBLANES, D_MODEL),
                         lambda i, *_: (jnp.minimum((i + 1) * per_halo, nhalo - 1), 0)),
            pl.BlockSpec((1, MOD_ROWS, D_MODEL), lambda i, mrow, *_: (mrow[i], 0, 0)),
            _const_spec((1, D_MODEL)),
            _const_spec((D_MODEL, ATTN_WIDTH + 2 * KV_WIDTH)),
            _const_spec((D_MODEL, D_INNER)),
            _const_spec((D_MODEL, CONV_DIM)),
            _const_spec((D_MODEL, LANES)),
            _const_spec((D_MODEL, 2 * D_MODEL)),
            _const_spec((1, 2 * LANES)),
            _const_spec((1, 2 * LANES)),
            pl.BlockSpec((TM, LANES), lambda i, mrow, rope, *_: (rope[i], 0)),
            pl.BlockSpec((TM, LANES), lambda i, mrow, rope, *_: (rope[i], 0)),
            row_blk(LANES),
            row_blk(LANES),
            _const_spec((3, CONV_DIM)),
            _const_spec((1, CONV_DIM)),
            _const_spec((1, LANES)),
    ]
    out_specs = [row_blk(ATTN_WIDTH),
                   pl.BlockSpec((2, nkb, TK, LANES), lambda i, *_: (0, i, 0, 0)),
                   pl.BlockSpec((2, 2, nkb, V_ROWS, TK), lambda i, *_: (0, 0, i, 0, 0)),
                   ctx_blk, ctx_blk,
                   row_blk(D_INNER), row_blk(D_INNER), row_blk(2 * GN), row_blk(LANES),
                   row_blk(2 * D_MODEL)]
    out_shape = [
        jax.ShapeDtypeStruct((t, ATTN_WIDTH), BF16),
        jax.ShapeDtypeStruct((2, t // TK, TK, LANES), BF16),
        jax.ShapeDtypeStruct((2, 2, t // TK, V_ROWS, TK), BF16),
        jax.ShapeDtypeStruct((t_ctx // ctx_len, depth, ctx_len, KV_WIDTH), F32),
        jax.ShapeDtypeStruct((t_ctx // ctx_len, depth, ctx_len, KV_WIDTH), F32),
        jax.ShapeDtypeStruct((t, D_INNER), BF16),
        jax.ShapeDtypeStruct((t, D_INNER), BF16),
        jax.ShapeDtypeStruct((t, 2 * GN), BF16),
        jax.ShapeDtypeStruct((t, LANES), F32),
        jax.ShapeDtypeStruct((t, 2 * D_MODEL), BF16),
    ]
    args = [meta["mrow"], meta["rope_blk"], meta["ctx_blk"],
            x, x, x, mod, lw["norm1_w"], lw["wqkv"], lw["wz"], lw["wxbc"], lw["wdt"], lw["wg"],
            lw["q_norm_w"], lw["k_norm_w"], meta["cos"], meta["sin"], meta["keep_prev"],
            meta["keep_next"], lw["conv_w"], lw["conv_b"], lw["dt_bias"]]
    aliases = {}
    if prev_kv is not None:
        in_specs += [pl.BlockSpec(memory_space=pl.ANY)] * 2
        aliases = {len(args): 3, len(args) + 1: 4}
        args += list(prev_kv)
    grid_spec = pltpu.PrefetchScalarGridSpec(num_scalar_prefetch=3, grid=(nblk,), in_specs=in_specs,
                                             out_specs=out_specs)
    return pl.pallas_call(
        functools.partial(_in_proj_kernel, n_ctx_blocks=n_ctx_blocks, ctx_len=ctx_len),
        grid_spec=grid_spec, out_shape=out_shape, input_output_aliases=aliases,
        compiler_params=_cparams(("arbitrary",)), name="in_proj",
    )(*args)


def _attn_kernel(*refs, n_new, n_cache, tq):
    q_ref, kn_ref, vn_ref = refs[0:3]
    if n_cache:
        kc_ref, vc_ref = refs[3:5]
    o_ref, qt_ref, m_ref, acc_ref, sa_ref, mxa_ref, sb_ref, mxb_ref = refs[-8:]
    nq = 4 * tq
    zeros_half = jnp.zeros((HEAD_DIM, tq), F32)
    for hh in range(2):
        cols = []
        for jj in range(2):
            off = 2 * LANES * hh + LANES * jj
            qt = q_ref[:, off:off + LANES].astype(F32).T
            for s in range(2):
                head = qt[s * HEAD_DIM:(s + 1) * HEAD_DIM, :]
                parts = [head, zeros_half] if hh == 0 else [zeros_half, head]
                cols.append(jnp.concatenate(parts, axis=0))
        qt_ref[hh] = jnp.concatenate(cols, axis=1).astype(BF16)
        m_ref[hh] = jnp.full((1, nq), -jnp.inf, F32)
        acc_ref[hh] = jnp.zeros((V_ROWS, nq), F32)

    def scores(kblk, s_ref, mx_ref):
        for hh in range(2):
            s = _dot(kblk, qt_ref[hh])
            s_ref[hh] = s
            mx_ref[hh] = jnp.max(s, axis=0, keepdims=True)

    def consume(vt_of, s_ref, mx_ref):
        for hh in range(2):
            m = m_ref[hh]
            m_new = jnp.maximum(m, mx_ref[hh])
            alpha = jnp.exp2(m - m_new)
            p = jnp.exp2(s_ref[hh] - m_new).astype(BF16)
            acc_ref[hh] = alpha * acc_ref[hh] + _dot(vt_of(hh), p)
            m_ref[hh] = m_new

    bufs = ((sa_ref, mxa_ref), (sb_ref, mxb_ref))
    scores(kn_ref[0], *bufs[0])

    def body(j, carry):
        scores(kn_ref[2 * j + 1], *bufs[1])
        consume(lambda hh: vn_ref[hh, 2 * j], *bufs[0])
        scores(kn_ref[2 * j + 2], *bufs[0])
        consume(lambda hh: vn_ref[hh, 2 * j + 1], *bufs[1])
        return carry

    n_loop = (n_new - 1) // 2
    lax.fori_loop(0, n_loop, body, 0)
    rest = [(kn_ref, vn_ref, b) for b in range(2 * n_loop, n_new)]
    if n_cache:
        rest += [(kc_ref, vc_ref, b) for b in range(n_cache)]
    for r, (_, v_src, b) in enumerate(rest):
        if r + 1 < len(rest):
            k_nxt, _, b_nxt = rest[r + 1]
            scores(k_nxt[b_nxt], *bufs[(r + 1) % 2])
        consume(lambda hh, v_src=v_src, b=b: v_src[hh, b], *bufs[r % 2])

    for hh in range(2):
        oT = acc_ref[hh]
        oT = oT[0:HEAD_DIM, :] / oT[HEAD_DIM:HEAD_DIM + 1, :]
        for jj in range(2):
            pair = jnp.concatenate([oT[:, (2 * jj) * tq:(2 * jj + 1) * tq],
                                    oT[:, (2 * jj + 1) * tq:(2 * jj + 2) * tq]], axis=0)
            off = 2 * LANES * hh + LANES * jj
            o_ref[:, off:off + LANES] = pair.T.astype(BF16)


def _attention(q, kn, vn, cache, prev_out, nseq, seq_len, row0, tq):
    n_new = seq_len // TK
    nqb = seq_len // tq
    nq = 4 * tq
    q0 = row0 // tq
    kv0 = row0 // (n_new * TK)
    assert row0 % tq == 0 and row0 % (n_new * TK) == 0
    qspec = pl.BlockSpec((tq, 4 * LANES), lambda b, p, i: (q0 + b * nqb + i, p))
    in_specs = [
        qspec,
        pl.BlockSpec((None, n_new, TK, LANES), lambda b, p, i: (p, kv0 + b, 0, 0)),
        pl.BlockSpec((None, 2, n_new, V_ROWS, TK), lambda b, p, i: (p, 0, kv0 + b, 0, 0)),
    ]
    args = [q, kn, vn]
    n_cache = 0
    if cache is not None:
        kc, vc = cache
        n_cache = kc.shape[1] // nseq
        in_specs += [pl.BlockSpec((None, n_cache, TK, LANES), lambda b, p, i: (p, b, 0, 0)),
                     pl.BlockSpec((None, 2, n_cache, V_ROWS, TK), lambda b, p, i: (p, 0, b, 0, 0))]
        args += [kc, vc]
    aliases = {}
    kern = functools.partial(_attn_kernel, n_new=n_new, n_cache=n_cache, tq=tq)
    if prev_out is not None:
        n_in = len(args)
        in_specs.append(pl.BlockSpec(memory_space=pl.ANY))
        aliases = {n_in: 0}
        args.append(prev_out)
        inner = kern
        kern = lambda *refs: inner(*refs[:n_in], *refs[n_in + 1:])
    return pl.pallas_call(
        kern,
        grid=(nseq, 2, nqb),
        in_specs=in_specs,
        out_specs=qspec,
        out_shape=jax.ShapeDtypeStruct(q.shape, BF16),
        scratch_shapes=[pltpu.VMEM((2, LANES, nq), BF16),
                        pltpu.VMEM((2, 1, nq), F32),
                        pltpu.VMEM((2, V_ROWS, nq), F32),
                        pltpu.VMEM((2, TK, nq), F32),
                        pltpu.VMEM((2, 1, nq), F32),
                        pltpu.VMEM((2, TK, nq), F32),
                        pltpu.VMEM((2, 1, nq), F32)],
        input_output_aliases=aliases,
        compiler_params=_cparams(("arbitrary", "arbitrary", "arbitrary")),
        name="attention",
    )(*args)


def _cache_blocks(ck, cv):
    nseq, past = ck.shape[0], ck.shape[1]
    n = past // TK
    kb = ck.astype(BF16).reshape(nseq * n, TK, 2, LANES).transpose(2, 0, 1, 3)
    vb = cv.astype(BF16).reshape(nseq * n, TK, 2, 2, HEAD_DIM).transpose(2, 3, 0, 4, 1)
    ones = jnp.ones(vb.shape[:3] + (1, TK), BF16)
    zeros = jnp.zeros(vb.shape[:3] + (V_ROWS - HEAD_DIM - 1, TK), BF16)
    return kb, jnp.concatenate([vb, ones, zeros], axis=3)


def _ssd_direction(x_ref, bc_ref, dt_ref, a_lanes, expand, dvec_ref, st_ref, y_ref, d):
    q = SSD_CHUNK
    row = lax.broadcasted_iota(jnp.int32, (q, q), 0)
    col = lax.broadcasted_iota(jnp.int32, (q, q), 1)
    causal = (row >= col) if d == 0 else (row <= col)
    tril = jnp.where(row >= col, 1.0, 0.0).astype(BF16)

    dt = dt_ref[...]
    a = dt * a_lanes
    prefix = _dot_exact_lhs(tril, a)
    if d == 0:
        cs = prefix
        last = cs[q - 1:q, :]
    else:
        cs = prefix[q - 1:q, :] - prefix + a
        last = cs[0:1, :]
    cs2 = cs * LOG2E
    cst2 = (cs2 - jnp.log2(dt)).T
    wj = jnp.exp(last - cs) * dt
    wj_x = _dot(wj.astype(BF16), expand)
    elast = jnp.exp(last)

    lane = lax.broadcasted_iota(jnp.int32, (q, LANES), 1)
    lo_half = lane < SSM_HEAD_DIM
    lo_half_row = lo_half[0:1, :]
    for g in range(SSM_GROUPS):
        bg = bc_ref[:, g * D_STATE:(g + 1) * D_STATE]
        cg = bc_ref[:, GN + g * D_STATE:GN + (g + 1) * D_STATE]
        gmat = _dot_nt(cg, bg)
        gsl = slice(g * HEADS_PER_GROUP * SSM_HEAD_DIM, (g + 1) * HEADS_PER_GROUP * SSM_HEAD_DIM)
        st = st_ref[d, :, gsl]
        y_inter = _dot(cg, st.astype(BF16))
        xg = x_ref[:, gsl]
        el_parts = []
        for hp in range(HEADS_PER_GROUP // 2):
            xpair = xg[:, hp * LANES:(hp + 1) * LANES]
            l0 = d * SSM_HEADS + g * HEADS_PER_GROUP + 2 * hp
            ws, es = [], []
            for s in range(2):
                hl = l0 + s
                csb = jnp.broadcast_to(cs2[:, hl:hl + 1], (q, q))
                seg = csb - cst2[hl:hl + 1, :]
                ws.append(gmat * jnp.exp2(jnp.where(causal, seg, -jnp.inf)))
                es.append(jnp.exp2(csb))
            wcat = jnp.concatenate(ws, axis=1).astype(BF16)
            zero = jnp.zeros_like(xpair)
            xm = jnp.concatenate([jnp.where(lo_half, xpair, zero), jnp.where(lo_half, zero, xpair)],
                                 axis=0)
            e_pair = jnp.where(lo_half, es[0], es[1])
            ypair = y_inter[:, hp * LANES:(hp + 1) * LANES] * e_pair + _dot(wcat, xm)
            lo = g * HEADS_PER_GROUP * SSM_HEAD_DIM + hp * LANES
            if d == 0:
                ypair = ypair + dvec_ref[:, lo:lo + LANES] * xpair.astype(F32)
            y_ref[:, lo:lo + LANES] = ypair.astype(y_ref.dtype)
            el_parts.append(jnp.where(lo_half_row, elast[:, l0:l0 + 1], elast[:, l0 + 1:l0 + 2]))
        xw = (xg.astype(F32) * wj_x[:, gsl]).astype(BF16)
        el = jnp.concatenate(el_parts, axis=1)
        st_ref[d, :, gsl] = st * el + _dot_tn(bg, xw)


def _ssd_kernel(fblk_ref, bblk_ref, first_ref, last_ref, init_ref, s0i_ref, sfi_ref,
                xf_ref, bcf_ref, dtf_ref, xb_ref, bcb_ref, dtb_ref, a_ref, exp_ref, dvec_ref, s0_ref,
                *rest):
    yf_ref, yb_ref, sfin_ref, st_ref = rest[-4:]
    del fblk_ref, bblk_ref, s0i_ref, sfi_ref
    s = pl.program_id(0)

    @pl.when(jnp.logical_and(first_ref[s] == 1, init_ref[s] == 1))
    def _():
        st_ref[...] = s0_ref[0]

    @pl.when(jnp.logical_and(first_ref[s] == 1, init_ref[s] == 0))
    def _():
        st_ref[...] = jnp.zeros(st_ref.shape, F32)

    a_lanes = a_ref[...]
    _ssd_direction(xf_ref, bcf_ref, dtf_ref, a_lanes, exp_ref[0], dvec_ref, st_ref, yf_ref, 0)
    _ssd_direction(xb_ref, bcb_ref, dtb_ref, a_lanes, exp_ref[1], dvec_ref, st_ref, yb_ref, 1)

    @pl.when(jnp.logical_and(last_ref[s] == 1, init_ref[s] == 0))
    def _():
        sfin_ref[0] = st_ref[...]


def _ssd(xs, bc, dt, a_lanes, expand, dvec, s0, meta, n_ctx, layer, depth, prev_sfin):
    t = xs.shape[0]
    n_steps = t // SSD_CHUNK
    st_shape = (2, D_STATE, D_INNER)
    fwd = lambda s, fblk, bblk, *_: (fblk[s], 0)
    bwd = lambda s, fblk, bblk, *_: (bblk[s], 0)

    def specs(idx):
        return [pl.BlockSpec((SSD_CHUNK, D_INNER), idx),
                pl.BlockSpec((SSD_CHUNK, 2 * GN), idx),
                pl.BlockSpec((SSD_CHUNK, LANES), idx)]

    in_specs = specs(fwd) + specs(bwd) + [
        pl.BlockSpec((1, LANES), lambda s, *_: (0, 0)),
        pl.BlockSpec((2, LANES, D_INNER), lambda s, *_: (0, 0, 0)),
        pl.BlockSpec((1, D_INNER), lambda s, *_: (0, 0)),
        pl.BlockSpec((1,) + st_shape, lambda s, f, b, fi, la, ini, s0i, sfi: (s0i[s], 0, 0, 0)),
    ]
    args = [meta["ssd_fblk"], meta["ssd_bblk"], meta["ssd_first"], meta["ssd_last"], meta["ssd_init"],
            meta["ssd_s0i"], meta["ssd_sfi"], xs, bc, dt, xs, bc, dt, a_lanes, expand, dvec, s0]
    aliases = {}
    if prev_sfin is not None:
        in_specs.append(pl.BlockSpec(memory_space=pl.ANY))
        aliases = {len(args): 2}
        args.append(prev_sfin)
    grid_spec = pltpu.PrefetchScalarGridSpec(
        num_scalar_prefetch=7,
        grid=(n_steps,),
        in_specs=in_specs,
        out_specs=[pl.BlockSpec((SSD_CHUNK, D_INNER), fwd), pl.BlockSpec((SSD_CHUNK, D_INNER), bwd),
                   pl.BlockSpec((1, None) + st_shape,
                                lambda s, f, b, fi, la, ini, s0i, sfi: (sfi[s], layer, 0, 0, 0))],
        scratch_shapes=[pltpu.VMEM(st_shape, F32)],
    )
    out_shape = [jax.ShapeDtypeStruct((t, D_INNER), BF16), jax.ShapeDtypeStruct((t, D_INNER), BF16),
                 jax.ShapeDtypeStruct((n_ctx, depth) + st_shape, F32)]
    return pl.pallas_call(
        _ssd_kernel, grid_spec=grid_spec, out_shape=out_shape, input_output_aliases=aliases,
        compiler_params=_cparams(("arbitrary",)), name="ssd",
    )(*args)


def _route(sel, scores):
    neg = -jnp.inf
    rows = [sel[e:e + 1, :] for e in range(N_EXPERTS)]
    srow = [scores[e:e + 1, :] for e in range(N_EXPERTS)]
    best_score = None
    best = None
    for g in range(N_EXPERT_GROUPS):
        v = rows[g * EXPERTS_PER_GROUP:(g + 1) * EXPERTS_PER_GROUP]
        top2 = None
        for a in range(EXPERTS_PER_GROUP):
            for b in range(a + 1, EXPERTS_PER_GROUP):
                s = v[a] + v[b]
                top2 = s if top2 is None else jnp.maximum(top2, s)
        if g == 0:
            best_score, best = top2, jnp.zeros(top2.shape, jnp.int32)
        else:
            better = top2 > best_score
            best = jnp.where(better, g, best)
            best_score = jnp.where(better, top2, best_score)

    def pick(vals):
        out = vals[0]
        for g in range(1, N_EXPERT_GROUPS):
            out = jnp.where(best == g, vals[g], out)
        return out

    gsel = [pick([rows[g * EXPERTS_PER_GROUP + j] for g in range(N_EXPERT_GROUPS)])
            for j in range(EXPERTS_PER_GROUP)]
    gsc = [pick([srow[g * EXPERTS_PER_GROUP + j] for g in range(N_EXPERT_GROUPS)])
           for j in range(EXPERTS_PER_GROUP)]

    def argmax_first(vals):
        bi = jnp.zeros(vals[0].shape, jnp.int32)
        bv = vals[0]
        for j in range(1, len(vals)):
            better = vals[j] > bv
            bi = jnp.where(better, j, bi)
            bv = jnp.where(better, vals[j], bv)
        return bi

    i1 = argmax_first(gsel)
    i2 = argmax_first([jnp.where(i1 == j, neg, gsel[j]) for j in range(EXPERTS_PER_GROUP)])

    def take(vals, idx):
        out = vals[0]
        for j in range(1, len(vals)):
            out = jnp.where(idx == j, vals[j], out)
        return out

    g1 = take(gsc, i1)
    g2 = take(gsc, i2)
    tot = g1 + g2
    idx = jnp.concatenate([best * EXPERTS_PER_GROUP + i1, best * EXPERTS_PER_GROUP + i2], axis=0)
    gate = jnp.concatenate([g1 / tot, g2 / tot], axis=0)
    return idx, gate


def _merge_kernel(mrow_ref, x_ref, attn_ref, yf_ref, yb_ref, z_ref, g_ref, mod_ref,
                  wa_ref, ws_ref, wo_ref, n2w_ref, wrt_ref, rb_ref,
                  x1_ref, h2_ref, idx_ref, gate_ref):
    del mrow_ref
    gw = D_INNER // SSM_GROUPS
    hr = TM // MERGE_CHAINS

    def chain(ci):
        rs = slice(ci * hr, (ci + 1) * hr)
        attn_o = _dot(attn_ref[rs, :], wa_ref[...])
        yield
        ssm_o = None
        for g in range(SSM_GROUPS):
            sl = slice(g * gw, (g + 1) * gw)
            y = (yf_ref[rs, sl] + yb_ref[rs, sl]).astype(F32) * _silu(z_ref[rs, sl].astype(F32))
            ms = jnp.mean(y * y, axis=-1, keepdims=True)
            yn = (y * lax.rsqrt(ms + EPS)).astype(BF16)
            part = _dot(yn, ws_ref[sl, :])
            ssm_o = part if ssm_o is None else ssm_o + part
            yield
        merged = (g_ref[rs, 0:D_MODEL].astype(F32) * attn_o
                  + g_ref[rs, D_MODEL:2 * D_MODEL].astype(F32) * ssm_o)
        out = _dot(merged.astype(BF16), wo_ref[...])
        yield
        x1 = x_ref[rs, :] + mod_ref[0, 2:3, :] * out
        x1_ref[rs, :] = x1
        h2 = _modnorm(x1, n2w_ref[...], mod_ref[0, 4:5, :], mod_ref[0, 3:4, :])
        h2_ref[rs, :] = h2.astype(BF16)
        logits = lax.dot_general(wrt_ref[...], h2, (((1,), (1,)), ((), ())),
                                 preferred_element_type=F32, precision=lax.Precision.HIGHEST)
        yield
        scores = _sigmoid(logits)
        idx, gate = _route(scores + rb_ref[...], scores)
        idx_ref[:, rs] = idx
        gate_ref[:, rs] = gate

    live = [chain(ci) for ci in range(MERGE_CHAINS)]
    while live:
        for c in list(live):
            try:
                next(c)
            except StopIteration:
                live.remove(c)


def _merge(x, attn, yf, yb, z, g, mod, meta, lw, wrt, rb):
    t = x.shape[0]
    nblk = t // TM

    def row_blk(width):
        return pl.BlockSpec((TM, width), lambda i, *_: (i, 0))

    grid_spec = pltpu.PrefetchScalarGridSpec(
        num_scalar_prefetch=1,
        grid=(nblk,),
        in_specs=[row_blk(D_MODEL), row_blk(ATTN_WIDTH), row_blk(D_INNER), row_blk(D_INNER),
                  row_blk(D_INNER), row_blk(2 * D_MODEL),
                  pl.BlockSpec((1, MOD_ROWS, D_MODEL), lambda i, mrow: (mrow[i], 0, 0)),
                  _const_spec((ATTN_WIDTH, D_MODEL)), _const_spec((D_INNER, D_MODEL)),
                  _const_spec((D_MODEL, D_MODEL)), _const_spec((1, D_MODEL)),
                  _const_spec((N_EXPERTS, D_MODEL)), _const_spec((N_EXPERTS, 1))],
        out_specs=[row_blk(D_MODEL), row_blk(D_MODEL),
                   pl.BlockSpec((2, TM), lambda i, *_: (0, i)),
                   pl.BlockSpec((2, TM), lambda i, *_: (0, i))],
    )
    out_shape = [jax.ShapeDtypeStruct((t, D_MODEL), F32), jax.ShapeDtypeStruct((t, D_MODEL), BF16),
                 jax.ShapeDtypeStruct((2, t), jnp.int32), jax.ShapeDtypeStruct((2, t), F32)]
    return pl.pallas_call(
        _merge_kernel, grid_spec=grid_spec, out_shape=out_shape,
        compiler_params=_cparams(("arbitrary",)), name="merge",
    )(meta["mrow"], x, attn, yf, yb, z, g, mod, lw["w_attn_out"], lw["w_ssm_out"], lw["w_out"],
      lw["norm2_w"], wrt, rb)


def _expert_kernel(te_ref, nt_ref, chg_ref, x_ref, wg_ref, wu_ref, wd_ref, o_ref,
                   wgs_ref, wus_ref, wds_ref):
    del te_ref
    i = pl.program_id(0)

    @pl.when(chg_ref[i] == 1)
    def _():
        wgs_ref[...] = wg_ref[0].astype(BF16)
        wus_ref[...] = wu_ref[0].astype(BF16)
        wds_ref[...] = wd_ref[0].astype(BF16)

    @pl.when(i < nt_ref[0])
    def _():
        x = x_ref[...]
        hmid = _silu(_dot(x, wgs_ref[...])) * _dot(x, wus_ref[...])
        o_ref[...] = _dot(hmid.astype(BF16), wds_ref[...]).astype(o_ref.dtype)

    @pl.when(i >= nt_ref[0])
    def _():
        o_ref[...] = jnp.zeros(o_ref.shape, o_ref.dtype)


def _experts(x_sorted, tile_expert, n_tiles, wg, wu, wd, layer):
    rows = x_sorted.shape[0]
    changed = jnp.concatenate([jnp.ones((1,), jnp.int32),
                               (tile_expert[1:] != tile_expert[:-1]).astype(jnp.int32)])
    grid_spec = pltpu.PrefetchScalarGridSpec(
        num_scalar_prefetch=3,
        grid=(rows // TE,),
        in_specs=[pl.BlockSpec((TE, D_MODEL), lambda i, te, nt, chg: (i, 0)),
                  pl.BlockSpec((None, 1, D_MODEL, D_FF_EXPERT), lambda i, te, nt, chg: (layer, te[i], 0, 0)),
                  pl.BlockSpec((None, 1, D_MODEL, D_FF_EXPERT), lambda i, te, nt, chg: (layer, te[i], 0, 0)),
                  pl.BlockSpec((None, 1, D_FF_EXPERT, D_MODEL), lambda i, te, nt, chg: (layer, te[i], 0, 0))],
        out_specs=pl.BlockSpec((TE, D_MODEL), lambda i, te, nt, chg: (i, 0)),
        scratch_shapes=[pltpu.VMEM((D_MODEL, D_FF_EXPERT), BF16),
                        pltpu.VMEM((D_MODEL, D_FF_EXPERT), BF16),
                        pltpu.VMEM((D_FF_EXPERT, D_MODEL), BF16)],
    )
    return pl.pallas_call(
        _expert_kernel, grid_spec=grid_spec,
        out_shape=jax.ShapeDtypeStruct((rows, D_MODEL), BF16),
        compiler_params=_cparams(("arbitrary",)), name="experts",
    )(tile_expert, n_tiles, changed, x_sorted, wg, wu, wd)


def _combine_kernel(mrow_ref, x1_ref, y0_ref, y1_ref, gate_ref, mod_ref, fnw_ref, *rest,
                    blk0, n_ctx_blocks, n_out):
    del mrow_ref
    out_refs = rest[-n_out:]
    moe = (gate_ref[:, 0:1] * y0_ref[...].astype(F32) + gate_ref[:, 1:2] * y1_ref[...].astype(F32))
    x2 = x1_ref[...] + mod_ref[0, 5:6, :] * moe
    if n_ctx_blocks is None:
        out_refs[0][...] = x2
        return
    ms = jnp.mean(x2 * x2, axis=-1, keepdims=True)
    y = x2 * lax.rsqrt(ms + EPS) * fnw_ref[...]
    if n_out == 1:
        out_refs[0][...] = y
        return
    is_ctx = blk0 + pl.program_id(0) < n_ctx_blocks

    @pl.when(is_ctx)
    def _():
        out_refs[0][...] = y

    @pl.when(jnp.logical_not(is_ctx))
    def _():
        out_refs[1][...] = y


def _combine(x1, y0, y1, gate_t, mod, meta, fnw, blk0, t_ctx, prev):
    t = x1.shape[0]
    nb = y0.shape[0] // TM
    blk = lambda width: pl.BlockSpec((TM, width), lambda i, *_: (i, 0))
    off_blk = pl.BlockSpec((TM, D_MODEL), lambda i, *_: (blk0 + i, 0))
    n_ctx_blocks = None
    if t_ctx is None:
        out_specs = [off_blk]
        out_shape = [jax.ShapeDtypeStruct((t, D_MODEL), F32)]
    else:
        n_ctx_blocks = t_ctx // TM
        lat_spec = pl.BlockSpec((TM, D_MODEL), lambda i, *_: (jnp.maximum(blk0 + i - n_ctx_blocks, 0), 0))
        lat_shape = jax.ShapeDtypeStruct((t - t_ctx, D_MODEL), F32)
        if blk0 >= n_ctx_blocks:
            out_specs, out_shape = [lat_spec], [lat_shape]
        else:
            ctx_spec = pl.BlockSpec((TM, D_MODEL),
                                    lambda i, *_: (jnp.minimum(blk0 + i, n_ctx_blocks - 1), 0))
            out_specs = [ctx_spec, lat_spec]
            out_shape = [jax.ShapeDtypeStruct((t_ctx, D_MODEL), F32), lat_shape]
    in_specs = [off_blk, blk(D_MODEL), blk(D_MODEL), blk(2),
                pl.BlockSpec((1, MOD_ROWS, D_MODEL), lambda i, mrow: (mrow[blk0 + i], 0, 0)),
                _const_spec((1, D_MODEL))]
    args = [meta["mrow"], x1, y0, y1, gate_t, mod, fnw]
    aliases = {}
    if prev is not None:
        prev = list(prev)[-len(out_specs):]
        in_specs += [pl.BlockSpec(memory_space=pl.ANY)] * len(prev)
        aliases = {len(args) + k: k for k in range(len(prev))}
        args += prev
    grid_spec = pltpu.PrefetchScalarGridSpec(num_scalar_prefetch=1, grid=(nb,), in_specs=in_specs,
                                             out_specs=out_specs)
    return pl.pallas_call(
        functools.partial(_combine_kernel, blk0=blk0, n_ctx_blocks=n_ctx_blocks, n_out=len(out_specs)),
        grid_spec=grid_spec, out_shape=out_shape, input_output_aliases=aliases,
        compiler_params=_cparams(("arbitrary",)), name="combine",
    )(*args)


def _moe(h2, row0, idx, gate, w_exp, layer):
    t = idx.shape[1]
    n_assign = 2 * t
    rows = n_assign + N_EXPERTS * TE
    e_flat = idx.reshape(n_assign)
    onehot = e_flat[:, None] == jnp.arange(N_EXPERTS, dtype=jnp.int32)[None, :]
    nb = n_assign // RANK_BLOCK
    oh = onehot.astype(BF16).reshape(nb, RANK_BLOCK, N_EXPERTS)
    tril = jnp.tril(jnp.ones((RANK_BLOCK, RANK_BLOCK), BF16))
    local = jnp.einsum("ij,bjk->bik", tril, oh, preferred_element_type=F32)
    bsum = local[:, -1, :]
    before = jnp.dot(jnp.tril(jnp.ones((nb, nb), F32), -1), bsum, precision=lax.Precision.HIGHEST)
    running = (local + before[:, None, :]).reshape(n_assign, N_EXPERTS)
    counts = (before[-1] + bsum[-1]).astype(jnp.int32)
    padded = ((counts + TE - 1) // TE) * TE
    pad_end = jnp.cumsum(padded)
    pad_off = pad_end - padded
    off = jnp.cumsum(counts) - counts
    pos = jnp.sum(jnp.where(onehot, running + (pad_off - 1).astype(F32)[None, :], 0.0), axis=1
                  ).astype(jnp.int32)
    order = jnp.argsort(e_flat, stable=True).astype(jnp.int32)
    p = jnp.arange(rows, dtype=jnp.int32)
    ep = jnp.minimum(jnp.sum(p[:, None] >= pad_end[None, :], axis=1), N_EXPERTS - 1).astype(jnp.int32)
    r = p - pad_off[ep]
    valid = r < counts[ep]
    src = jnp.where(valid, order[jnp.clip(off[ep] + r, 0, n_assign - 1)] % t, p % t)
    n_tiles = (pad_end[-1] // TE).astype(jnp.int32).reshape(1)
    tile_start = jnp.arange(rows // TE, dtype=jnp.int32) * TE
    tile_expert = jnp.minimum(jnp.sum(tile_start[:, None] >= pad_end[None, :], axis=1),
                              N_EXPERTS - 1).astype(jnp.int32)
    last_used = tile_expert[jnp.maximum(n_tiles[0] - 1, 0)]
    tile_expert = jnp.where(tile_start < pad_end[-1], tile_expert, last_used)
    x_sorted = h2.at[src + row0].get(mode="promise_in_bounds")
    y_sorted = _experts(x_sorted, tile_expert, n_tiles, *w_exp, layer)
    pos2 = pos.reshape(2, t)
    return (y_sorted.at[pos2[0]].get(mode="promise_in_bounds"),
            y_sorted.at[pos2[1]].get(mode="promise_in_bounds"))


def _rope_tables(n_tokens):
    rows = n_tokens // GRID_W
    row = jnp.repeat(jnp.arange(rows, dtype=F32), GRID_W)
    col = jnp.tile(jnp.arange(GRID_W, dtype=F32), rows)
    inv = 1.0 / (ROPE_THETA ** (jnp.arange(0, ROPE_AXIS_DIM, 2, dtype=F32) / ROPE_AXIS_DIM))
    ar = row[:, None] * inv
    ac = col[:, None] * inv
    ang = jnp.concatenate([ar, ar, ac, ac], axis=-1)
    cos, sin = jnp.cos(ang), jnp.sin(ang)
    cos = jnp.concatenate([jnp.ones((TM, HEAD_DIM), F32), cos], axis=0)
    sin = jnp.concatenate([jnp.zeros((TM, HEAD_DIM), F32), sin], axis=0)
    return jnp.tile(cos, (1, 2)), jnp.tile(sin, (1, 2))


def _block_meta(n_ctx, ctx_len, n_lat, lat_len):
    ctx_blocks = n_ctx * ctx_len // TM
    lat_blocks = n_lat * lat_len // TM
    per_lat = lat_len // TM
    bi = jnp.arange(ctx_blocks + lat_blocks, dtype=jnp.int32)
    is_ctx = bi < ctx_blocks
    lat_i = bi - ctx_blocks
    mrow = jnp.where(is_ctx, 0, 1 + lat_i // per_lat).astype(jnp.int32)
    rope_blk = jnp.where(is_ctx, 0, 1 + lat_i % per_lat).astype(jnp.int32)
    cos, sin = _rope_tables(lat_len)
    ti = jnp.arange(n_ctx * ctx_len + n_lat * lat_len, dtype=jnp.int32)
    pos = jnp.where(ti < n_ctx * ctx_len, ti % ctx_len, (ti - n_ctx * ctx_len) % lat_len)
    slen = jnp.where(ti < n_ctx * ctx_len, ctx_len, lat_len)
    keep_prev = jnp.broadcast_to((pos != 0).astype(F32)[:, None], (ti.shape[0], LANES))
    keep_next = jnp.broadcast_to((pos != slen - 1).astype(F32)[:, None], (ti.shape[0], LANES))
    meta = {"mrow": mrow, "rope_blk": rope_blk, "cos": cos, "sin": sin,
            "ctx_blk": jnp.minimum(bi, ctx_blocks - 1).astype(jnp.int32),
            "keep_prev": keep_prev, "keep_next": keep_next}

    nc_ctx, nc_lat = ctx_len // SSD_CHUNK, lat_len // SSD_CHUNK
    n_ctx_steps = n_ctx * nc_ctx
    si = jnp.arange(n_ctx_steps + n_lat * nc_lat, dtype=jnp.int32)
    s_ctx = si < n_ctx_steps
    li = si - n_ctx_steps
    seq = jnp.where(s_ctx, si // nc_ctx, li // nc_lat)
    ch = jnp.where(s_ctx, si % nc_ctx, li % nc_lat)
    nc = jnp.where(s_ctx, nc_ctx, nc_lat)
    base = jnp.where(s_ctx, seq * nc_ctx, n_ctx_steps + seq * nc_lat)
    meta.update({
        "ssd_fblk": (base + ch).astype(jnp.int32),
        "ssd_bblk": (base + nc - 1 - ch).astype(jnp.int32),
        "ssd_first": (ch == 0).astype(jnp.int32),
        "ssd_last": (ch == nc - 1).astype(jnp.int32),
        "ssd_init": jnp.where(s_ctx, 0, 1).astype(jnp.int32),
        "ssd_s0i": jnp.where(s_ctx, 0, seq).astype(jnp.int32),
        "ssd_sfi": jnp.where(s_ctx, seq, n_ctx - 1).astype(jnp.int32),
    })
    return meta


def kernel(x_prompt, x_sample, cache_k, cache_v, state_ssm, c, c_ctx, norm1_w, norm2_w, w_mod, b_mod,
           w_in, q_norm_w, k_norm_w, conv_w, conv_b, a_log, dt_bias, ssm_d, ssm_norm_w, w_attn_out,
           w_ssm_out, w_out, w_router, router_bias, w_exp_gate, w_exp_up, w_exp_down, final_norm_w):
    n_ctx, ctx_len, _ = x_prompt.shape
    n_lat, lat_len, _ = x_sample.shape
    depth = w_in.shape[0]
    t_ctx = n_ctx * ctx_len
    t_lat = n_lat * lat_len
    assert t_ctx % TM == 0 and lat_len % TM == 0 and 1 + n_lat <= COND_ROWS and TM % ctx_len == 0
    assert ctx_len % TK == 0 and cache_k.shape[2] % TK == 0 and lat_len % GRID_W == 0

    meta = _block_meta(n_ctx, ctx_len, n_lat, lat_len)
    x = jnp.concatenate([x_prompt.reshape(t_ctx, D_MODEL), x_sample.reshape(t_lat, D_MODEL)], axis=0)

    cond = jnp.zeros((COND_ROWS, D_MODEL), F32).at[0].set(c_ctx).at[1:1 + n_lat].set(c)
    mod_all = _adaln(cond, w_mod, b_mod).reshape(depth, COND_ROWS, N_MOD, D_MODEL)
    mod_all = jnp.pad(mod_all, ((0, 0), (0, 0), (0, MOD_ROWS - N_MOD), (0, 0)))

    head_of_lane = jnp.arange(D_INNER, dtype=jnp.int32) // SSM_HEAD_DIM
    lane_id = jnp.arange(LANES, dtype=jnp.int32)
    expand = jnp.stack([(lane_id[:, None] == head_of_lane[None, :] + d * SSM_HEADS) for d in range(2)]
                       ).astype(BF16)
    wrt = w_router.T
    rb = router_bias.reshape(N_EXPERTS, 1)
    fnw = final_norm_w.reshape(1, D_MODEL)

    kv_ctx, s_ctx = None, None
    for l in range(depth):
        wl = w_in[l]
        lw = {
            "norm1_w": norm1_w[l].reshape(1, D_MODEL),
            "norm2_w": norm2_w[l].reshape(1, D_MODEL),
            "wqkv": wl[:, :Z_OFF].astype(BF16),
            "wz": wl[:, Z_OFF:XBC_OFF].astype(BF16),
            "wxbc": wl[:, XBC_OFF:DT_OFF].astype(BF16),
            "wdt": jnp.pad(wl[:, DT_OFF:G_OFF], ((0, 0), (0, LANES - 2 * SSM_HEADS))).astype(BF16),
            "wg": wl[:, G_OFF:].astype(BF16),
            "q_norm_w": jnp.tile(q_norm_w[l], 4).reshape(1, 2 * LANES),
            "k_norm_w": jnp.tile(k_norm_w[l], 4).reshape(1, 2 * LANES),
            "conv_w": conv_w[l],
            "conv_b": conv_b[l].reshape(1, CONV_DIM),
            "dt_bias": jnp.pad(dt_bias[l].reshape(1, 2 * SSM_HEADS), ((0, 0), (0, LANES - 2 * SSM_HEADS))),
            "ssm_d": jnp.repeat(ssm_d[l], SSM_HEAD_DIM).reshape(1, D_INNER),
            "w_attn_out": w_attn_out[l].astype(BF16),
            "w_ssm_out": (ssm_norm_w[l][:, None] * w_ssm_out[l]).astype(BF16),
            "w_out": w_out[l].astype(BF16),
        }
        mod = mod_all[l]
        q, kn, vn, k_ctx, v_ctx, z, xs, bc, dt, g = _in_proj(x, mod, meta, lw, t_ctx, ctx_len, l, depth,
                                                             kv_ctx)
        kv_ctx = (k_ctx, v_ctx)

        attn = _attention(q, kn, vn, None, None, n_ctx, ctx_len, 0, min(TQ, ctx_len))
        attn = _attention(q, kn, vn, _cache_blocks(cache_k[:, l], cache_v[:, l]), attn,
                          n_lat, lat_len, t_ctx, min(TQ, lat_len))

        a_lanes = jnp.pad(-jnp.exp(a_log[l]).reshape(1, 2 * SSM_HEADS),
                          ((0, 0), (0, LANES - 2 * SSM_HEADS)))
        s0 = state_ssm[:, l].transpose(0, 1, 4, 2, 3).reshape(n_lat, 2, D_STATE, D_INNER)
        yf, yb, s_ctx = _ssd(xs, bc, dt, a_lanes, expand, lw["ssm_d"], s0, meta, n_ctx, l, depth, s_ctx)

        x1, h2, idx, gate = _merge(x, attn, yf, yb, z, g, mod, meta, lw, wrt, rb)
        t_a = (x1.shape[0] // TM // 2) * TM
        w_exp = (w_exp_gate, w_exp_up, w_exp_down)
        ya = _moe(h2, 0, idx[:, :t_a], gate[:, :t_a], w_exp, l)
        yb2 = _moe(h2, t_a, idx[:, t_a:], gate[:, t_a:], w_exp, l)
        last = l == depth - 1
        outs = _combine(x1, ya[0], ya[1], gate[:, :t_a].T, mod, meta, fnw, 0,
                        t_ctx if last else None, None)
        outs_b = _combine(x1, yb2[0], yb2[1], gate[:, t_a:].T, mod, meta, fnw, t_a // TM,
                          t_ctx if last else None, outs)
        if last:
            y_prompt = outs_b[0] if len(outs_b) == 2 else outs[0]
            y_sample = outs_b[-1]
        else:
            x = outs_b[0]

    y_prompt = y_prompt.reshape(n_ctx, ctx_len, D_MODEL)
    y_sample = y_sample.reshape(n_lat, lat_len, D_MODEL)
    new_k = kv_ctx[0].reshape(n_ctx, depth, ctx_len, N_KV_HEADS, HEAD_DIM)
    new_v = kv_ctx[1].reshape(n_ctx, depth, ctx_len, N_KV_HEADS, HEAD_DIM)
    new_s = s_ctx.reshape(n_ctx, depth, 2, D_STATE, SSM_HEADS, SSM_HEAD_DIM).transpose(0, 1, 2, 4, 5, 3)
    return (y_prompt, y_sample, new_k, new_v, new_s)
```

```python
import functools

import jax
import jax.numpy as jnp
from jax import lax
from jax.experimental import pallas as pl
from jax.experimental.pallas import tpu as pltpu

F32 = jnp.float32
BF16 = jnp.bfloat16

EPS = 1e-6
D_MODEL = 1024
N_HEADS = 16
N_KV_HEADS = 4
HEAD_DIM = 64
ATTN_WIDTH = N_HEADS * HEAD_DIM
KV_WIDTH = N_KV_HEADS * HEAD_DIM
GRID_W = 64
ROPE_AXIS_DIM = HEAD_DIM // 2
ROPE_THETA = 10000.0
D_INNER = 2 * D_MODEL
SSM_HEAD_DIM = 64
SSM_HEADS = D_INNER // SSM_HEAD_DIM
SSM_GROUPS = 4
HEADS_PER_GROUP = SSM_HEADS // SSM_GROUPS
D_STATE = 128
GN = SSM_GROUPS * D_STATE
CONV_DIM = D_INNER + 2 * GN
SSD_CHUNK = 128
N_EXPERTS = 16
N_EXPERT_GROUPS = 4
EXPERTS_PER_GROUP = N_EXPERTS // N_EXPERT_GROUPS
D_FF_EXPERT = 512
N_MOD = 6
MOD_ROWS = 8

LANES = 128
SUBLANES = 8
VMEM_LIMIT = 56 * 1024 * 1024

TM = 512
TQ = 512
TK = 256
ATTN_UNIT = 256
TE = 512
MERGE_CHAINS = 4
RANK_BLOCK = 512
CONV_CHUNK = 1024
COND_ROWS = 16
V_ROWS = HEAD_DIM + SUBLANES
LOG2E = 1.4426950408889634
Q_SCALE = HEAD_DIM ** -0.5 * LOG2E

Q_OFF, K_OFF, V_OFF = 0, ATTN_WIDTH, ATTN_WIDTH + KV_WIDTH
Z_OFF = ATTN_WIDTH + 2 * KV_WIDTH
XBC_OFF = Z_OFF + D_INNER
DT_OFF = XBC_OFF + CONV_DIM
G_OFF = DT_OFF + 2 * SSM_HEADS
N_IN = G_OFF + 2 * D_MODEL


def _cparams(sem):
    return pltpu.CompilerParams(dimension_semantics=sem, vmem_limit_bytes=VMEM_LIMIT)


def _const_spec(shape):
    nd = len(shape)
    return pl.BlockSpec(shape, lambda *_: (0,) * nd, pipeline_mode=pl.Buffered(1))


def _dot(a, b):
    return jnp.dot(a, b, preferred_element_type=F32)


def _dot_nt(a, b):
    return lax.dot_general(a, b, (((1,), (1,)), ((), ())), preferred_element_type=F32)


def _dot_tn(a, b):
    return lax.dot_general(a, b, (((0,), (0,)), ((), ())), preferred_element_type=F32)


def _split3(a):
    a1 = a.astype(BF16)
    r = a - a1.astype(F32)
    a2 = r.astype(BF16)
    a3 = (r - a2.astype(F32)).astype(BF16)
    return a1, a2, a3


def _dot_exact_lhs(m_bf16, a):
    a1, a2, a3 = _split3(a)
    return _dot(m_bf16, a1) + _dot(m_bf16, a2) + _dot(m_bf16, a3)


def _modnorm(x, w, sc, sh):
    ms = jnp.mean(x * x, axis=-1, keepdims=True)
    return x * lax.rsqrt(ms + EPS) * (w * (1.0 + sc)) + sh


def _sigmoid(x):
    return 1.0 / (1.0 + jnp.exp(-x))


def _silu(x):
    return x * _sigmoid(x)


def _adaln_kernel(c_ref, w_ref, b_ref, o_ref):
    cs = _silu(c_ref[...])
    o_ref[0] = jnp.dot(cs, w_ref[0], preferred_element_type=F32,
                       precision=lax.Precision.HIGHEST) + b_ref[0]


def _adaln(cond, w_mod, b_mod):
    depth = w_mod.shape[0]
    nb = N_MOD
    return pl.pallas_call(
        _adaln_kernel,
        grid=(depth, nb),
        in_specs=[pl.BlockSpec((COND_ROWS, D_MODEL), lambda l, j: (0, 0)),
                  pl.BlockSpec((1, D_MODEL, D_MODEL), lambda l, j: (l, 0, j)),
                  pl.BlockSpec((1, 1, D_MODEL), lambda l, j: (l, 0, j))],
        out_specs=pl.BlockSpec((1, COND_ROWS, D_MODEL), lambda l, j: (l, 0, j)),
        out_shape=jax.ShapeDtypeStruct((depth, COND_ROWS, N_MOD * D_MODEL), F32),
        compiler_params=_cparams(("arbitrary", "arbitrary")),
        name="adaln",
    )(cond, w_mod, b_mod.reshape(depth, 1, N_MOD * D_MODEL))


def _in_proj_kernel(mrow_ref, rope_ref, ctxi_ref,
                    x_ref, xp_ref, xn_ref, mod_ref, n1w_ref,
                    wqkv_ref, wz_ref, wxbc_ref, wdt_ref, wg_ref,
                    qnw_ref, knw_ref, cos_ref, sin_ref, kprev_ref, knext_ref,
                    convw_ref, convb_ref, dtb_ref,
                    *rest, n_ctx_blocks, ctx_len):
    q_ref, kb_ref, vt_ref, kc_ref, vc_ref, z_ref, xs_ref, bc_ref, dt_ref, g_ref = rest[-10:]
    del mrow_ref, rope_ref, ctxi_ref
    i = pl.program_id(0)
    is_ctx = i < n_ctx_blocks
    sh1 = mod_ref[0, 0:1, :]
    sc1 = mod_ref[0, 1:2, :]
    nw = n1w_ref[...]
    h = _modnorm(x_ref[...], nw, sc1, sh1).astype(BF16)
    hp = _modnorm(xp_ref[...], nw, sc1, sh1).astype(BF16)
    hn = _modnorm(xn_ref[...], nw, sc1, sh1).astype(BF16)

    r = lax.broadcasted_iota(jnp.int32, (2 * LANES, 2 * LANES), 0) // HEAD_DIM
    c = lax.broadcasted_iota(jnp.int32, (2 * LANES, 2 * LANES), 1) // HEAD_DIM
    same_head = jnp.where(r == c, 1.0, 0.0).astype(BF16)
    lane = lax.broadcasted_iota(jnp.int32, (TM, LANES), 1)
    first_half = (lane % (HEAD_DIM // 2)) < (HEAD_DIM // 4)

    def head_norm_rope(t, w, scale):
        sq = t * t
        hi = sq.astype(BF16)
        lo = (sq - hi.astype(F32)).astype(BF16)
        ssum = _dot(hi, same_head) + _dot(lo, same_head)
        tn = t * lax.rsqrt(ssum * (1.0 / HEAD_DIM) + EPS) * w
        outs = []
        for s in range(2):
            a = tn[:, s * LANES:(s + 1) * LANES]
            rot = jnp.where(first_half, -pltpu.roll(a, LANES - HEAD_DIM // 4, 1),
                            pltpu.roll(a, HEAD_DIM // 4, 1))
            outs.append((a * cos_ref[...] + rot * sin_ref[...]) * scale)
        return outs

    qw = qnw_ref[...]
    kw = knw_ref[...]
    ones_rows = jnp.where(lax.broadcasted_iota(jnp.int32, (V_ROWS - HEAD_DIM, TK), 0) == 0,
                          1.0, 0.0).astype(BF16)
    cw = CONV_CHUNK
    rows = lax.broadcasted_iota(jnp.int32, (TM, cw), 0)
    kprev = jnp.concatenate([kprev_ref[...]] * (cw // LANES), axis=1)
    knext = jnp.concatenate([knext_ref[...]] * (cw // LANES), axis=1)

    def q_stage(cb):
        def mm():
            return _dot(h, wqkv_ref[:, cb * 2 * LANES:(cb + 1) * 2 * LANES])

        def fin(t):
            a, b = head_norm_rope(t, qw, Q_SCALE)
            q_ref[:, cb * 2 * LANES:cb * 2 * LANES + LANES] = a.astype(BF16)
            q_ref[:, cb * 2 * LANES + LANES:(cb + 1) * 2 * LANES] = b.astype(BF16)
        return mm, fin

    def k_fin(t):
        kpairs = head_norm_rope(t, kw, 1.0)
        for rb in range(TM // TK):
            for pr in range(2):
                kb_ref[pr, rb] = kpairs[pr][rb * TK:(rb + 1) * TK, :].astype(BF16)

        @pl.when(is_ctx)
        def _():
            for sq in range(TM // ctx_len):
                for pr in range(2):
                    kc_ref[sq, :, pr * LANES:(pr + 1) * LANES] = kpairs[pr][sq * ctx_len:(sq + 1) * ctx_len, :]

    def v_fin(v):
        for rb in range(TM // TK):
            vT = v[rb * TK:(rb + 1) * TK, :].T
            for hd in range(N_KV_HEADS):
                vt_ref[hd // 2, hd % 2, rb, 0:HEAD_DIM, :] = (
                    vT[hd * HEAD_DIM:(hd + 1) * HEAD_DIM, :].astype(BF16))
                vt_ref[hd // 2, hd % 2, rb, HEAD_DIM:V_ROWS, :] = ones_rows

        @pl.when(is_ctx)
        def _():
            for sq in range(TM // ctx_len):
                vc_ref[sq] = v[sq * ctx_len:(sq + 1) * ctx_len, :]

    def z_fin(t):
        z_ref[...] = t.astype(BF16)

    def g_fin(t):
        g_ref[...] = _sigmoid(t).astype(BF16)

    def dt_fin(t):
        dtr = t + dtb_ref[...]
        dt_ref[...] = jnp.maximum(dtr, 0.0) + jnp.log(1.0 + jnp.exp(-jnp.abs(dtr)))

    def conv_stage(cb):
        sl = slice(cb * cw, (cb + 1) * cw)

        def mm():
            return (_dot(h, wxbc_ref[:, sl]), _dot(hp, wxbc_ref[:, sl]), _dot(hn, wxbc_ref[:, sl]))

        def fin(res):
            pre, pp, pn = res
            prev_row = pp[SUBLANES - 1:SUBLANES, :]
            next_row = pn[0:1, :]
            up = jnp.where(rows == 0, prev_row, pltpu.roll(pre, 1, 0)) * kprev
            down = jnp.where(rows == TM - 1, next_row, pltpu.roll(pre, TM - 1, 0)) * knext
            y = (convb_ref[:, sl] + up * convw_ref[0:1, sl] + pre * convw_ref[1:2, sl]
                 + down * convw_ref[2:3, sl])
            y = _silu(y).astype(BF16)
            lo = cb * cw
            if lo < D_INNER:
                xs_ref[:, lo:lo + cw] = y
            else:
                bc_ref[:, lo - D_INNER:lo - D_INNER + cw] = y
        return mm, fin

    stages = [q_stage(cb) for cb in range(ATTN_WIDTH // (2 * LANES))]
    stages.append((lambda: _dot(h, wqkv_ref[:, K_OFF:K_OFF + KV_WIDTH]), k_fin))
    stages.append((lambda: _dot(h, wqkv_ref[:, V_OFF:V_OFF + KV_WIDTH]), v_fin))
    stages.append((lambda: _dot(h, wdt_ref[...]), dt_fin))
    stages += [conv_stage(cb) for cb in range(CONV_DIM // cw)]
    stages.append((lambda: _dot(h, wz_ref[...]), z_fin))
    stages.append((lambda: _dot(h, wg_ref[...]), g_fin))
    pending = None
    for mm, fin in stages:
        res = mm()
        if pending is not None:
            pending[0](pending[1])
        pending = (fin, res)
    pending[0](pending[1])


def _in_proj(x, mod, meta, lw, t_ctx, ctx_len, layer, depth, prev_kv):
    t = x.shape[0]
    nblk = t // TM
    nhalo = t // SUBLANES
    per_halo = TM // SUBLANES
    n_ctx_blocks = t_ctx // TM
    nkb = TM // TK

    def row_blk(width):
        return pl.BlockSpec((TM, width), lambda i, *_: (i, 0))

    spb = TM // ctx_len
    ctx_blk = pl.BlockSpec((spb, None, ctx_len, KV_WIDTH),
                           lambda i, mrow, rope, ctxi: (ctxi[i], layer, 0, 0))
    in_specs = [
        row_blk(D_MODEL),
            pl.BlockSpec((SUBLANES, D_MODEL),
                         lambda i, *_: (jnp.maximum(i * per_halo - 1, 0), 0)),
            pl.BlockSpec((SUBLANES, D_MODEL),
                         lambda i, *_: (jnp.minimum((i + 1) * per_halo, nhalo - 1), 0)),
            pl.BlockSpec((1, MOD_ROWS, D_MODEL), lambda i, mrow, *_: (mrow[i], 0, 0)),
            _const_spec((1, D_MODEL)),
            _const_spec((D_MODEL, ATTN_WIDTH + 2 * KV_WIDTH)),
            _const_spec((D_MODEL, D_INNER)),
            _const_spec((D_MODEL, CONV_DIM)),
            _const_spec((D_MODEL, LANES)),
            _const_spec((D_MODEL, 2 * D_MODEL)),
            _const_spec((1, 2 * LANES)),
            _const_spec((1, 2 * LANES)),
            pl.BlockSpec((TM, LANES), lambda i, mrow, rope, *_: (rope[i], 0)),
            pl.BlockSpec((TM, LANES), lambda i, mrow, rope, *_: (rope[i], 0)),
            row_blk(LANES),
            row_blk(LANES),
            _const_spec((3, CONV_DIM)),
            _const_spec((1, CONV_DIM)),
            _const_spec((1, LANES)),
    ]
    out_specs = [row_blk(ATTN_WIDTH),
                   pl.BlockSpec((2, nkb, TK, LANES), lambda i, *_: (0, i, 0, 0)),
                   pl.BlockSpec((2, 2, nkb, V_ROWS, TK), lambda i, *_: (0, 0, i, 0, 0)),
                   ctx_blk, ctx_blk,
                   row_blk(D_INNER), row_blk(D_INNER), row_blk(2 * GN), row_blk(LANES),
                   row_blk(2 * D_MODEL)]
    out_shape = [
        jax.ShapeDtypeStruct((t, ATTN_WIDTH), BF16),
        jax.ShapeDtypeStruct((2, t // TK, TK, LANES), BF16),
        jax.ShapeDtypeStruct((2, 2, t // TK, V_ROWS, TK), BF16),
        jax.ShapeDtypeStruct((t_ctx // ctx_len, depth, ctx_len, KV_WIDTH), F32),
        jax.ShapeDtypeStruct((t_ctx // ctx_len, depth, ctx_len, KV_WIDTH), F32),
        jax.ShapeDtypeStruct((t, D_INNER), BF16),
        jax.ShapeDtypeStruct((t, D_INNER), BF16),
        jax.ShapeDtypeStruct((t, 2 * GN), BF16),
        jax.ShapeDtypeStruct((t, LANES), F32),
        jax.ShapeDtypeStruct((t, 2 * D_MODEL), BF16),
    ]
    args = [meta["mrow"], meta["rope_blk"], meta["ctx_blk"],
            x, x, x, mod, lw["norm1_w"], lw["wqkv"], lw["wz"], lw["wxbc"], lw["wdt"], lw["wg"],
            lw["q_norm_w"], lw["k_norm_w"], meta["cos"], meta["sin"], meta["keep_prev"],
            meta["keep_next"], lw["conv_w"], lw["conv_b"], lw["dt_bias"]]
    aliases = {}
    if prev_kv is not None:
        in_specs += [pl.BlockSpec(memory_space=pl.ANY)] * 2
        aliases = {len(args): 3, len(args) + 1: 4}
        args += list(prev_kv)
    grid_spec = pltpu.PrefetchScalarGridSpec(num_scalar_prefetch=3, grid=(nblk,), in_specs=in_specs,
                                             out_specs=out_specs)
    return pl.pallas_call(
        functools.partial(_in_proj_kernel, n_ctx_blocks=n_ctx_blocks, ctx_len=ctx_len),
        grid_spec=grid_spec, out_shape=out_shape, input_output_aliases=aliases,
        compiler_params=_cparams(("arbitrary",)), name="in_proj",
    )(*args)


def _attn_kernel(*refs, n_new, n_cache, tq):
    q_ref, kn_ref, vn_ref = refs[0:3]
    if n_cache:
        kc_ref, vc_ref = refs[3:5]
    o_ref, qt_ref, m_ref, acc_ref, sa_ref, mxa_ref, sb_ref, mxb_ref = refs[-8:]
    nq = 4 * tq
    zeros_half = jnp.zeros((HEAD_DIM, tq), F32)
    for hh in range(2):
        cols = []
        for jj in range(2):
            off = 2 * LANES * hh + LANES * jj
            qt = q_ref[:, off:off + LANES].astype(F32).T
            for s in range(2):
                head = qt[s * HEAD_DIM:(s + 1) * HEAD_DIM, :]
                parts = [head, zeros_half] if hh == 0 else [zeros_half, head]
                cols.append(jnp.concatenate(parts, axis=0))
        qt_ref[hh] = jnp.concatenate(cols, axis=1).astype(BF16)
        m_ref[hh] = jnp.full((1, nq), -jnp.inf, F32)
        acc_ref[hh] = jnp.zeros((V_ROWS, nq), F32)

    units = [(hh, slice(c * ATTN_UNIT, (c + 1) * ATTN_UNIT))
             for hh in range(2) for c in range(nq // ATTN_UNIT)]

    def scores(kblk, s_ref, mx_ref, hh, cs):
        s = _dot(kblk, qt_ref[hh, :, cs])
        s_ref[hh, :, cs] = s
        mx_ref[hh, :, cs] = jnp.max(s, axis=0, keepdims=True)

    def consume(vt, s_ref, mx_ref, hh, cs):
        m = m_ref[hh, :, cs]
        m_new = jnp.maximum(m, mx_ref[hh, :, cs])
        alpha = jnp.exp2(m - m_new)
        p = jnp.exp2(s_ref[hh, :, cs] - m_new).astype(BF16)
        acc_ref[hh, :, cs] = alpha * acc_ref[hh, :, cs] + _dot(vt, p)
        m_ref[hh, :, cs] = m_new

    def step(k_nxt, nxt, v_cur, cur):
        for hh, cs in units:
            if k_nxt is not None:
                scores(k_nxt, *nxt, hh, cs)
            consume(v_cur[hh], *cur, hh, cs)

    bufs = ((sa_ref, mxa_ref), (sb_ref, mxb_ref))
    for hh, cs in units:
        scores(kn_ref[0], *bufs[0], hh, cs)

    def body(j, carry):
        step(kn_ref[2 * j + 1], bufs[1], (vn_ref[0, 2 * j], vn_ref[1, 2 * j]), bufs[0])
        step(kn_ref[2 * j + 2], bufs[0], (vn_ref[0, 2 * j + 1], vn_ref[1, 2 * j + 1]), bufs[1])
        return carry

    n_loop = (n_new - 1) // 2
    lax.fori_loop(0, n_loop, body, 0)
    rest = [(kn_ref, vn_ref, b) for b in range(2 * n_loop, n_new)]
    if n_cache:
        rest += [(kc_ref, vc_ref, b) for b in range(n_cache)]
    for r, (_, v_src, b) in enumerate(rest):
        k_nxt = None
        if r + 1 < len(rest):
            k_src, _, b_nxt = rest[r + 1]
            k_nxt = k_src[b_nxt]
        step(k_nxt, bufs[(r + 1) % 2], (v_src[0, b], v_src[1, b]), bufs[r % 2])

    for hh in range(2):
        oT = acc_ref[hh]
        oT = oT[0:HEAD_DIM, :] / oT[HEAD_DIM:HEAD_DIM + 1, :]
        for jj in range(2):
            pair = jnp.concatenate([oT[:, (2 * jj) * tq:(2 * jj + 1) * tq],
                                    oT[:, (2 * jj + 1) * tq:(2 * jj + 2) * tq]], axis=0)
            off = 2 * LANES * hh + LANES * jj
            o_ref[:, off:off + LANES] = pair.T.astype(BF16)


def _attention(q, kn, vn, cache, prev_out, nseq, seq_len, row0, tq):
    n_new = seq_len // TK
    nqb = seq_len // tq
    nq = 4 * tq
    q0 = row0 // tq
    kv0 = row0 // (n_new * TK)
    assert row0 % tq == 0 and row0 % (n_new * TK) == 0
    qspec = pl.BlockSpec((tq, 4 * LANES), lambda b, p, i: (q0 + b * nqb + i, p))
    in_specs = [
        qspec,
        pl.BlockSpec((None, n_new, TK, LANES), lambda b, p, i: (p, kv0 + b, 0, 0)),
        pl.BlockSpec((None, 2, n_new, V_ROWS, TK), lambda b, p, i: (p, 0, kv0 + b, 0, 0)),
    ]
    args = [q, kn, vn]
    n_cache = 0
    if cache is not None:
        kc, vc = cache
        n_cache = kc.shape[1] // nseq
        in_specs += [pl.BlockSpec((None, n_cache, TK, LANES), lambda b, p, i: (p, b, 0, 0)),
                     pl.BlockSpec((None, 2, n_cache, V_ROWS, TK), lambda b, p, i: (p, 0, b, 0, 0))]
        args += [kc, vc]
    aliases = {}
    kern = functools.partial(_attn_kernel, n_new=n_new, n_cache=n_cache, tq=tq)
    if prev_out is not None:
        n_in = len(args)
        in_specs.append(pl.BlockSpec(memory_space=pl.ANY))
        aliases = {n_in: 0}
        args.append(prev_out)
        inner = kern
        kern = lambda *refs: inner(*refs[:n_in], *refs[n_in + 1:])
    return pl.pallas_call(
        kern,
        grid=(nseq, 2, nqb),
        in_specs=in_specs,
        out_specs=qspec,
        out_shape=jax.ShapeDtypeStruct(q.shape, BF16),
        scratch_shapes=[pltpu.VMEM((2, LANES, nq), BF16),
                        pltpu.VMEM((2, 1, nq), F32),
                        pltpu.VMEM((2, V_ROWS, nq), F32),
                        pltpu.VMEM((2, TK, nq), F32),
                        pltpu.VMEM((2, 1, nq), F32),
                        pltpu.VMEM((2, TK, nq), F32),
                        pltpu.VMEM((2, 1, nq), F32)],
        input_output_aliases=aliases,
        compiler_params=_cparams(("arbitrary", "arbitrary", "arbitrary")),
        name="attention",
    )(*args)


def _cache_blocks(ck, cv):
    nseq, past = ck.shape[0], ck.shape[1]
    n = past // TK
    kb = ck.astype(BF16).reshape(nseq * n, TK, 2, LANES).transpose(2, 0, 1, 3)
    vb = cv.astype(BF16).reshape(nseq * n, TK, 2, 2, HEAD_DIM).transpose(2, 3, 0, 4, 1)
    ones = jnp.ones(vb.shape[:3] + (1, TK), BF16)
    zeros = jnp.zeros(vb.shape[:3] + (V_ROWS - HEAD_DIM - 1, TK), BF16)
    return kb, jnp.concatenate([vb, ones, zeros], axis=3)


def _ssd_direction(x_ref, bc_ref, dt_ref, a_lanes, expand, dvec_ref, st_ref, y_ref, d):
    q = SSD_CHUNK
    row = lax.broadcasted_iota(jnp.int32, (q, q), 0)
    col = lax.broadcasted_iota(jnp.int32, (q, q), 1)
    causal = (row >= col) if d == 0 else (row <= col)
    tril = jnp.where(row >= col, 1.0, 0.0).astype(BF16)

    dt = dt_ref[...]
    a = dt * a_lanes
    prefix = _dot_exact_lhs(tril, a)
    if d == 0:
        cs = prefix
        last = cs[q - 1:q, :]
    else:
        cs = prefix[q - 1:q, :] - prefix + a
        last = cs[0:1, :]
    cs2 = cs * LOG2E
    cst2 = (cs2 - jnp.log2(dt)).T
    wj = jnp.exp(last - cs) * dt
    wj_x = _dot(wj.astype(BF16), expand)
    elast = jnp.exp(last)

    lane = lax.broadcasted_iota(jnp.int32, (q, LANES), 1)
    lo_half = lane < SSM_HEAD_DIM
    lo_half_row = lo_half[0:1, :]
    for g in range(SSM_GROUPS):
        bg = bc_ref[:, g * D_STATE:(g + 1) * D_STATE]
        cg = bc_ref[:, GN + g * D_STATE:GN + (g + 1) * D_STATE]
        gmat = _dot_nt(cg, bg)
        gsl = slice(g * HEADS_PER_GROUP * SSM_HEAD_DIM, (g + 1) * HEADS_PER_GROUP * SSM_HEAD_DIM)
        st = st_ref[d, :, gsl]
        y_inter = _dot(cg, st.astype(BF16))
        xg = x_ref[:, gsl]
        el_parts = []
        for hp in range(HEADS_PER_GROUP // 2):
            xpair = xg[:, hp * LANES:(hp + 1) * LANES]
            l0 = d * SSM_HEADS + g * HEADS_PER_GROUP + 2 * hp
            ws, es = [], []
            for s in range(2):
                hl = l0 + s
                csb = jnp.broadcast_to(cs2[:, hl:hl + 1], (q, q))
                seg = csb - cst2[hl:hl + 1, :]
                ws.append(gmat * jnp.exp2(jnp.where(causal, seg, -jnp.inf)))
                es.append(jnp.exp2(csb))
            wcat = jnp.concatenate(ws, axis=1).astype(BF16)
            zero = jnp.zeros_like(xpair)
            xm = jnp.concatenate([jnp.where(lo_half, xpair, zero), jnp.where(lo_half, zero, xpair)],
                                 axis=0)
            e_pair = jnp.where(lo_half, es[0], es[1])
            ypair = y_inter[:, hp * LANES:(hp + 1) * LANES] * e_pair + _dot(wcat, xm)
            lo = g * HEADS_PER_GROUP * SSM_HEAD_DIM + hp * LANES
            if d == 0:
                ypair = ypair + dvec_ref[:, lo:lo + LANES] * xpair.astype(F32)
            y_ref[:, lo:lo + LANES] = ypair.astype(y_ref.dtype)
            el_parts.append(jnp.where(lo_half_row, elast[:, l0:l0 + 1], elast[:, l0 + 1:l0 + 2]))
        xw = (xg.astype(F32) * wj_x[:, gsl]).astype(BF16)
        el = jnp.concatenate(el_parts, axis=1)
        st_ref[d, :, gsl] = st * el + _dot_tn(bg, xw)


def _ssd_kernel(fblk_ref, bblk_ref, first_ref, last_ref, init_ref, s0i_ref, sfi_ref,
                xf_ref, bcf_ref, dtf_ref, xb_ref, bcb_ref, dtb_ref, a_ref, exp_ref, dvec_ref, s0_ref,
                *rest):
    yf_ref, yb_ref, sfin_ref, st_ref = rest[-4:]
    del fblk_ref, bblk_ref, s0i_ref, sfi_ref
    s = pl.program_id(0)

    @pl.when(jnp.logical_and(first_ref[s] == 1, init_ref[s] == 1))
    def _():
        st_ref[...] = s0_ref[0]

    @pl.when(jnp.logical_and(first_ref[s] == 1, init_ref[s] == 0))
    def _():
        st_ref[...] = jnp.zeros(st_ref.shape, F32)

    a_lanes = a_ref[...]
    _ssd_direction(xf_ref, bcf_ref, dtf_ref, a_lanes, exp_ref[0], dvec_ref, st_ref, yf_ref, 0)
    _ssd_direction(xb_ref, bcb_ref, dtb_ref, a_lanes, exp_ref[1], dvec_ref, st_ref, yb_ref, 1)

    @pl.when(jnp.logical_and(last_ref[s] == 1, init_ref[s] == 0))
    def _():
        sfin_ref[0] = st_ref[...]


def _ssd(xs, bc, dt, a_lanes, expand, dvec, s0, meta, n_ctx, layer, depth, prev_sfin):
    t = xs.shape[0]
    n_steps = t // SSD_CHUNK
    st_shape = (2, D_STATE, D_INNER)
    fwd = lambda s, fblk, bblk, *_: (fblk[s], 0)
    bwd = lambda s, fblk, bblk, *_: (bblk[s], 0)

    def specs(idx):
        return [pl.BlockSpec((SSD_CHUNK, D_INNER), idx),
                pl.BlockSpec((SSD_CHUNK, 2 * GN), idx),
                pl.BlockSpec((SSD_CHUNK, LANES), idx)]

    in_specs = specs(fwd) + specs(bwd) + [
        pl.BlockSpec((1, LANES), lambda s, *_: (0, 0)),
        pl.BlockSpec((2, LANES, D_INNER), lambda s, *_: (0, 0, 0)),
        pl.BlockSpec((1, D_INNER), lambda s, *_: (0, 0)),
        pl.BlockSpec((1,) + st_shape, lambda s, f, b, fi, la, ini, s0i, sfi: (s0i[s], 0, 0, 0)),
    ]
    args = [meta["ssd_fblk"], meta["ssd_bblk"], meta["ssd_first"], meta["ssd_last"], meta["ssd_init"],
            meta["ssd_s0i"], meta["ssd_sfi"], xs, bc, dt, xs, bc, dt, a_lanes, expand, dvec, s0]
    aliases = {}
    if prev_sfin is not None:
        in_specs.append(pl.BlockSpec(memory_space=pl.ANY))
        aliases = {len(args): 2}
        args.append(prev_sfin)
    grid_spec = pltpu.PrefetchScalarGridSpec(
        num_scalar_prefetch=7,
        grid=(n_steps,),
        in_specs=in_specs,
        out_specs=[pl.BlockSpec((SSD_CHUNK, D_INNER), fwd), pl.BlockSpec((SSD_CHUNK, D_INNER), bwd),
                   pl.BlockSpec((1, None) + st_shape,
                                lambda s, f, b, fi, la, ini, s0i, sfi: (sfi[s], layer, 0, 0, 0))],
        scratch_shapes=[pltpu.VMEM(st_shape, F32)],
    )
    out_shape = [jax.ShapeDtypeStruct((t, D_INNER), BF16), jax.ShapeDtypeStruct((t, D_INNER), BF16),
                 jax.ShapeDtypeStruct((n_ctx, depth) + st_shape, F32)]
    return pl.pallas_call(
        _ssd_kernel, grid_spec=grid_spec, out_shape=out_shape, input_output_aliases=aliases,
        compiler_params=_cparams(("arbitrary",)), name="ssd",
    )(*args)


def _route(sel, scores):
    neg = -jnp.inf
    rows = [sel[e:e + 1, :] for e in range(N_EXPERTS)]
    srow = [scores[e:e + 1, :] for e in range(N_EXPERTS)]
    best_score = None
    best = None
    for g in range(N_EXPERT_GROUPS):
        v = rows[g * EXPERTS_PER_GROUP:(g + 1) * EXPERTS_PER_GROUP]
        top2 = None
        for a in range(EXPERTS_PER_GROUP):
            for b in range(a + 1, EXPERTS_PER_GROUP):
                s = v[a] + v[b]
                top2 = s if top2 is None else jnp.maximum(top2, s)
        if g == 0:
            best_score, best = top2, jnp.zeros(top2.shape, jnp.int32)
        else:
            better = top2 > best_score
            best = jnp.where(better, g, best)
            best_score = jnp.where(better, top2, best_score)

    def pick(vals):
        out = vals[0]
        for g in range(1, N_EXPERT_GROUPS):
            out = jnp.where(best == g, vals[g], out)
        return out

    gsel = [pick([rows[g * EXPERTS_PER_GROUP + j] for g in range(N_EXPERT_GROUPS)])
            for j in range(EXPERTS_PER_GROUP)]
    gsc = [pick([srow[g * EXPERTS_PER_GROUP + j] for g in range(N_EXPERT_GROUPS)])
           for j in range(EXPERTS_PER_GROUP)]

    def argmax_first(vals):
        bi = jnp.zeros(vals[0].shape, jnp.int32)
        bv = vals[0]
        for j in range(1, len(vals)):
            better = vals[j] > bv
            bi = jnp.where(better, j, bi)
            bv = jnp.where(better, vals[j], bv)
        return bi

    i1 = argmax_first(gsel)
    i2 = argmax_first([jnp.where(i1 == j, neg, gsel[j]) for j in range(EXPERTS_PER_GROUP)])

    def take(vals, idx):
        out = vals[0]
        for j in range(1, len(vals)):
            out = jnp.where(idx == j, vals[j], out)
        return out

    g1 = take(gsc, i1)
    g2 = take(gsc, i2)
    tot = g1 + g2
    idx = jnp.concatenate([best * EXPERTS_PER_GROUP + i1, best * EXPERTS_PER_GROUP + i2], axis=0)
    gate = jnp.concatenate([g1 / tot, g2 / tot], axis=0)
    return idx, gate


def _merge_kernel(mrow_ref, x_ref, attn_ref, yf_ref, yb_ref, z_ref, g_ref, mod_ref,
                  wa_ref, ws_ref, wo_ref, n2w_ref, wrt_ref, rb_ref,
                  x1_ref, h2_ref, idx_ref, gate_ref):
    del mrow_ref
    gw = D_INNER // SSM_GROUPS
    hr = TM // MERGE_CHAINS

    def chain(ci):
        rs = slice(ci * hr, (ci + 1) * hr)
        attn_o = _dot(attn_ref[rs, :], wa_ref[...])
        yield
        ssm_o = None
        for g in range(SSM_GROUPS):
            sl = slice(g * gw, (g + 1) * gw)
            y = (yf_ref[rs, sl] + yb_ref[rs, sl]).astype(F32) * _silu(z_ref[rs, sl].astype(F32))
            ms = jnp.mean(y * y, axis=-1, keepdims=True)
            yn = (y * lax.rsqrt(ms + EPS)).astype(BF16)
            part = _dot(yn, ws_ref[sl, :])
            ssm_o = part if ssm_o is None else ssm_o + part
            yield
        merged = (g_ref[rs, 0:D_MODEL].astype(F32) * attn_o
                  + g_ref[rs, D_MODEL:2 * D_MODEL].astype(F32) * ssm_o)
        out = _dot(merged.astype(BF16), wo_ref[...])
        yield
        x1 = x_ref[rs, :] + mod_ref[0, 2:3, :] * out
        x1_ref[rs, :] = x1
        h2 = _modnorm(x1, n2w_ref[...], mod_ref[0, 4:5, :], mod_ref[0, 3:4, :])
        h2_ref[rs, :] = h2.astype(BF16)
        logits = lax.dot_general(wrt_ref[...], h2, (((1,), (1,)), ((), ())),
                                 preferred_element_type=F32, precision=lax.Precision.HIGHEST)
        yield
        scores = _sigmoid(logits)
        idx, gate = _route(scores + rb_ref[...], scores)
        idx_ref[:, rs] = idx
        gate_ref[:, rs] = gate

    live = [chain(ci) for ci in range(MERGE_CHAINS)]
    while live:
        for c in list(live):
            try:
                next(c)
            except StopIteration:
                live.remove(c)


def _merge(x, attn, yf, yb, z, g, mod, meta, lw, wrt, rb):
    t = x.shape[0]
    nblk = t // TM

    def row_blk(width):
        return pl.BlockSpec((TM, width), lambda i, *_: (i, 0))

    grid_spec = pltpu.PrefetchScalarGridSpec(
        num_scalar_prefetch=1,
        grid=(nblk,),
        in_specs=[row_blk(D_MODEL), row_blk(ATTN_WIDTH), row_blk(D_INNER), row_blk(D_INNER),
                  row_blk(D_INNER), row_blk(2 * D_MODEL),
                  pl.BlockSpec((1, MOD_ROWS, D_MODEL), lambda i, mrow: (mrow[i], 0, 0)),
                  _const_spec((ATTN_WIDTH, D_MODEL)), _const_spec((D_INNER, D_MODEL)),
                  _const_spec((D_MODEL, D_MODEL)), _const_spec((1, D_MODEL)),
                  _const_spec((N_EXPERTS, D_MODEL)), _const_spec((N_EXPERTS, 1))],
        out_specs=[row_blk(D_MODEL), row_blk(D_MODEL),
                   pl.BlockSpec((2, TM), lambda i, *_: (0, i)),
                   pl.BlockSpec((2, TM), lambda i, *_: (0, i))],
    )
    out_shape = [jax.ShapeDtypeStruct((t, D_MODEL), F32), jax.ShapeDtypeStruct((t, D_MODEL), BF16),
                 jax.ShapeDtypeStruct((2, t), jnp.int32), jax.ShapeDtypeStruct((2, t), F32)]
    return pl.pallas_call(
        _merge_kernel, grid_spec=grid_spec, out_shape=out_shape,
        compiler_params=_cparams(("arbitrary",)), name="merge",
    )(meta["mrow"], x, attn, yf, yb, z, g, mod, lw["w_attn_out"], lw["w_ssm_out"], lw["w_out"],
      lw["norm2_w"], wrt, rb)


def _expert_kernel(te_ref, nt_ref, chg_ref, x_ref, wg_ref, wu_ref, wd_ref, o_ref,
                   wgs_ref, wus_ref, wds_ref):
    del te_ref
    i = pl.program_id(0)

    @pl.when(chg_ref[i] == 1)
    def _():
        wgs_ref[...] = wg_ref[0].astype(BF16)
        wus_ref[...] = wu_ref[0].astype(BF16)
        wds_ref[...] = wd_ref[0].astype(BF16)

    @pl.when(i < nt_ref[0])
    def _():
        x = x_ref[...]
        hmid = _silu(_dot(x, wgs_ref[...])) * _dot(x, wus_ref[...])
        o_ref[...] = _dot(hmid.astype(BF16), wds_ref[...]).astype(o_ref.dtype)

    @pl.when(i >= nt_ref[0])
    def _():
        o_ref[...] = jnp.zeros(o_ref.shape, o_ref.dtype)


def _experts(x_sorted, tile_expert, n_tiles, wg, wu, wd, layer):
    rows = x_sorted.shape[0]
    changed = jnp.concatenate([jnp.ones((1,), jnp.int32),
                               (tile_expert[1:] != tile_expert[:-1]).astype(jnp.int32)])
    grid_spec = pltpu.PrefetchScalarGridSpec(
        num_scalar_prefetch=3,
        grid=(rows // TE,),
        in_specs=[pl.BlockSpec((TE, D_MODEL), lambda i, te, nt, chg: (i, 0)),
                  pl.BlockSpec((None, 1, D_MODEL, D_FF_EXPERT), lambda i, te, nt, chg: (layer, te[i], 0, 0)),
                  pl.BlockSpec((None, 1, D_MODEL, D_FF_EXPERT), lambda i, te, nt, chg: (layer, te[i], 0, 0)),
                  pl.BlockSpec((None, 1, D_FF_EXPERT, D_MODEL), lambda i, te, nt, chg: (layer, te[i], 0, 0))],
        out_specs=pl.BlockSpec((TE, D_MODEL), lambda i, te, nt, chg: (i, 0)),
        scratch_shapes=[pltpu.VMEM((D_MODEL, D_FF_EXPERT), BF16),
                        pltpu.VMEM((D_MODEL, D_FF_EXPERT), BF16),
                        pltpu.VMEM((D_FF_EXPERT, D_MODEL), BF16)],
    )
    return pl.pallas_call(
        _expert_kernel, grid_spec=grid_spec,
        out_shape=jax.ShapeDtypeStruct((rows, D_MODEL), BF16),
        compiler_params=_cparams(("arbitrary",)), name="experts",
    )(tile_expert, n_tiles, changed, x_sorted, wg, wu, wd)


def _combine_kernel(mrow_ref, x1_ref, y0_ref, y1_ref, gate_ref, mod_ref, fnw_ref, *out_refs,
                    n_ctx_blocks):
    del mrow_ref
    moe = (gate_ref[:, 0:1] * y0_ref[...].astype(F32) + gate_ref[:, 1:2] * y1_ref[...].astype(F32))
    x2 = x1_ref[...] + mod_ref[0, 5:6, :] * moe
    if n_ctx_blocks is None:
        out_refs[0][...] = x2
        return
    ms = jnp.mean(x2 * x2, axis=-1, keepdims=True)
    y = x2 * lax.rsqrt(ms + EPS) * fnw_ref[...]
    is_ctx = pl.program_id(0) < n_ctx_blocks

    @pl.when(is_ctx)
    def _():
        out_refs[0][...] = y

    @pl.when(jnp.logical_not(is_ctx))
    def _():
        out_refs[1][...] = y


def _combine(x1, y0, y1, gate_t, mod, meta, fnw, t_ctx):
    t = x1.shape[0]

    def row_blk(width):
        return pl.BlockSpec((TM, width), lambda i, *_: (i, 0))

    if t_ctx is None:
        n_ctx_blocks = None
        out_specs = row_blk(D_MODEL)
        out_shape = jax.ShapeDtypeStruct((t, D_MODEL), F32)
    else:
        n_ctx_blocks = t_ctx // TM
        out_specs = [pl.BlockSpec((TM, D_MODEL), lambda i, *_: (jnp.minimum(i, n_ctx_blocks - 1), 0)),
                     pl.BlockSpec((TM, D_MODEL), lambda i, *_: (jnp.maximum(i - n_ctx_blocks, 0), 0))]
        out_shape = [jax.ShapeDtypeStruct((t_ctx, D_MODEL), F32),
                     jax.ShapeDtypeStruct((t - t_ctx, D_MODEL), F32)]
    grid_spec = pltpu.PrefetchScalarGridSpec(
        num_scalar_prefetch=1,
        grid=(t // TM,),
        in_specs=[row_blk(D_MODEL), row_blk(D_MODEL), row_blk(D_MODEL), row_blk(2),
                  pl.BlockSpec((1, MOD_ROWS, D_MODEL), lambda i, mrow: (mrow[i], 0, 0)),
                  _const_spec((1, D_MODEL))],
        out_specs=out_specs,
    )
    return pl.pallas_call(
        functools.partial(_combine_kernel, n_ctx_blocks=n_ctx_blocks), grid_spec=grid_spec,
        out_shape=out_shape,
        compiler_params=_cparams(("arbitrary",)), name="combine",
    )(meta["mrow"], x1, y0, y1, gate_t, mod, fnw)


def _moe(h2, idx, gate, w_exp, layer):
    t = h2.shape[0]
    n_assign = 2 * t
    rows = n_assign + N_EXPERTS * TE
    e_flat = idx.reshape(n_assign)
    onehot = e_flat[:, None] == jnp.arange(N_EXPERTS, dtype=jnp.int32)[None, :]
    nb = n_assign // RANK_BLOCK
    oh = onehot.astype(BF16).reshape(nb, RANK_BLOCK, N_EXPERTS)
    tril = jnp.tril(jnp.ones((RANK_BLOCK, RANK_BLOCK), BF16))
    local = jnp.einsum("ij,bjk->bik", tril, oh, preferred_element_type=F32)
    bsum = local[:, -1, :]
    before = jnp.dot(jnp.tril(jnp.ones((nb, nb), F32), -1), bsum, precision=lax.Precision.HIGHEST)
    running = (local + before[:, None, :]).reshape(n_assign, N_EXPERTS)
    counts = (before[-1] + bsum[-1]).astype(jnp.int32)
    padded = ((counts + TE - 1) // TE) * TE
    pad_end = jnp.cumsum(padded)
    pad_off = pad_end - padded
    off = jnp.cumsum(counts) - counts
    pos = jnp.sum(jnp.where(onehot, running + (pad_off - 1).astype(F32)[None, :], 0.0), axis=1
                  ).astype(jnp.int32)
    order = jnp.argsort(e_flat, stable=True).astype(jnp.int32)
    p = jnp.arange(rows, dtype=jnp.int32)
    ep = jnp.minimum(jnp.sum(p[:, None] >= pad_end[None, :], axis=1), N_EXPERTS - 1).astype(jnp.int32)
    r = p - pad_off[ep]
    valid = r < counts[ep]
    src = jnp.where(valid, order[jnp.clip(off[ep] + r, 0, n_assign - 1)] % t, p % t)
    n_tiles = (pad_end[-1] // TE).astype(jnp.int32).reshape(1)
    tile_start = jnp.arange(rows // TE, dtype=jnp.int32) * TE
    tile_expert = jnp.minimum(jnp.sum(tile_start[:, None] >= pad_end[None, :], axis=1),
                              N_EXPERTS - 1).astype(jnp.int32)
    last_used = tile_expert[jnp.maximum(n_tiles[0] - 1, 0)]
    tile_expert = jnp.where(tile_start < pad_end[-1], tile_expert, last_used)
    x_sorted = h2.at[src].get(mode="promise_in_bounds")
    y_sorted = _experts(x_sorted, tile_expert, n_tiles, *w_exp, layer)
    pos2 = pos.reshape(2, t)
    return (y_sorted.at[pos2[0]].get(mode="promise_in_bounds"),
            y_sorted.at[pos2[1]].get(mode="promise_in_bounds"))


def _rope_tables(n_tokens):
    rows = n_tokens // GRID_W
    row = jnp.repeat(jnp.arange(rows, dtype=F32), GRID_W)
    col = jnp.tile(jnp.arange(GRID_W, dtype=F32), rows)
    inv = 1.0 / (ROPE_THETA ** (jnp.arange(0, ROPE_AXIS_DIM, 2, dtype=F32) / ROPE_AXIS_DIM))
    ar = row[:, None] * inv
    ac = col[:, None] * inv
    ang = jnp.concatenate([ar, ar, ac, ac], axis=-1)
    cos, sin = jnp.cos(ang), jnp.sin(ang)
    cos = jnp.concatenate([jnp.ones((TM, HEAD_DIM), F32), cos], axis=0)
    sin = jnp.concatenate([jnp.zeros((TM, HEAD_DIM), F32), sin], axis=0)
    return jnp.tile(cos, (1, 2)), jnp.tile(sin, (1, 2))


def _block_meta(n_ctx, ctx_len, n_lat, lat_len):
    ctx_blocks = n_ctx * ctx_len // TM
    lat_blocks = n_lat * lat_len // TM
    per_lat = lat_len // TM
    bi = jnp.arange(ctx_blocks + lat_blocks, dtype=jnp.int32)
    is_ctx = bi < ctx_blocks
    lat_i = bi - ctx_blocks
    mrow = jnp.where(is_ctx, 0, 1 + lat_i // per_lat).astype(jnp.int32)
    rope_blk = jnp.where(is_ctx, 0, 1 + lat_i % per_lat).astype(jnp.int32)
    cos, sin = _rope_tables(lat_len)
    ti = jnp.arange(n_ctx * ctx_len + n_lat * lat_len, dtype=jnp.int32)
    pos = jnp.where(ti < n_ctx * ctx_len, ti % ctx_len, (ti - n_ctx * ctx_len) % lat_len)
    slen = jnp.where(ti < n_ctx * ctx_len, ctx_len, lat_len)
    keep_prev = jnp.broadcast_to((pos != 0).astype(F32)[:, None], (ti.shape[0], LANES))
    keep_next = jnp.broadcast_to((pos != slen - 1).astype(F32)[:, None], (ti.shape[0], LANES))
    meta = {"mrow": mrow, "rope_blk": rope_blk, "cos": cos, "sin": sin,
            "ctx_blk": jnp.minimum(bi, ctx_blocks - 1).astype(jnp.int32),
            "keep_prev": keep_prev, "keep_next": keep_next}

    nc_ctx, nc_lat = ctx_len // SSD_CHUNK, lat_len // SSD_CHUNK
    n_ctx_steps = n_ctx * nc_ctx
    si = jnp.arange(n_ctx_steps + n_lat * nc_lat, dtype=jnp.int32)
    s_ctx = si < n_ctx_steps
    li = si - n_ctx_steps
    seq = jnp.where(s_ctx, si // nc_ctx, li // nc_lat)
    ch = jnp.where(s_ctx, si % nc_ctx, li % nc_lat)
    nc = jnp.where(s_ctx, nc_ctx, nc_lat)
    base = jnp.where(s_ctx, seq * nc_ctx, n_ctx_steps + seq * nc_lat)
    meta.update({
        "ssd_fblk": (base + ch).astype(jnp.int32),
        "ssd_bblk": (base + nc - 1 - ch).astype(jnp.int32),
        "ssd_first": (ch == 0).astype(jnp.int32),
        "ssd_last": (ch == nc - 1).astype(jnp.int32),
        "ssd_init": jnp.where(s_ctx, 0, 1).astype(jnp.int32),
        "ssd_s0i": jnp.where(s_ctx, 0, seq).astype(jnp.int32),
        "ssd_sfi": jnp.where(s_ctx, seq, n_ctx - 1).astype(jnp.int32),
    })
    return meta


def kernel(x_prompt, x_sample, cache_k, cache_v, state_ssm, c, c_ctx, norm1_w, norm2_w, w_mod, b_mod,
           w_in, q_norm_w, k_norm_w, conv_w, conv_b, a_log, dt_bias, ssm_d, ssm_norm_w, w_attn_out,
           w_ssm_out, w_out, w_router, router_bias, w_exp_gate, w_exp_up, w_exp_down, final_norm_w):
    n_ctx, ctx_len, _ = x_prompt.shape
    n_lat, lat_len, _ = x_sample.shape
    depth = w_in.shape[0]
    t_ctx = n_ctx * ctx_len
    t_lat = n_lat * lat_len
    assert t_ctx % TM == 0 and lat_len % TM == 0 and 1 + n_lat <= COND_ROWS and TM % ctx_len == 0
    assert ctx_len % TK == 0 and cache_k.shape[2] % TK == 0 and lat_len % GRID_W == 0

    meta = _block_meta(n_ctx, ctx_len, n_lat, lat_len)
    x = jnp.concatenate([x_prompt.reshape(t_ctx, D_MODEL), x_sample.reshape(t_lat, D_MODEL)], axis=0)

    cond = jnp.zeros((COND_ROWS, D_MODEL), F32).at[0].set(c_ctx).at[1:1 + n_lat].set(c)
    mod_all = _adaln(cond, w_mod, b_mod).reshape(depth, COND_ROWS, N_MOD, D_MODEL)
    mod_all = jnp.pad(mod_all, ((0, 0), (0, 0), (0, MOD_ROWS - N_MOD), (0, 0)))

    head_of_lane = jnp.arange(D_INNER, dtype=jnp.int32) // SSM_HEAD_DIM
    lane_id = jnp.arange(LANES, dtype=jnp.int32)
    expand = jnp.stack([(lane_id[:, None] == head_of_lane[None, :] + d * SSM_HEADS) for d in range(2)]
                       ).astype(BF16)
    wrt = w_router.T
    rb = router_bias.reshape(N_EXPERTS, 1)
    fnw = final_norm_w.reshape(1, D_MODEL)

    kv_ctx, s_ctx = None, None
    for l in range(depth):
        wl = w_in[l]
        lw = {
            "norm1_w": norm1_w[l].reshape(1, D_MODEL),
            "norm2_w": norm2_w[l].reshape(1, D_MODEL),
            "wqkv": wl[:, :Z_OFF].astype(BF16),
            "wz": wl[:, Z_OFF:XBC_OFF].astype(BF16),
            "wxbc": wl[:, XBC_OFF:DT_OFF].astype(BF16),
            "wdt": jnp.pad(wl[:, DT_OFF:G_OFF], ((0, 0), (0, LANES - 2 * SSM_HEADS))).astype(BF16),
            "wg": wl[:, G_OFF:].astype(BF16),
            "q_norm_w": jnp.tile(q_norm_w[l], 4).reshape(1, 2 * LANES),
            "k_norm_w": jnp.tile(k_norm_w[l], 4).reshape(1, 2 * LANES),
            "conv_w": conv_w[l],
            "conv_b": conv_b[l].reshape(1, CONV_DIM),
            "dt_bias": jnp.pad(dt_bias[l].reshape(1, 2 * SSM_HEADS), ((0, 0), (0, LANES - 2 * SSM_HEADS))),
            "ssm_d": jnp.repeat(ssm_d[l], SSM_HEAD_DIM).reshape(1, D_INNER),
            "w_attn_out": w_attn_out[l].astype(BF16),
            "w_ssm_out": (ssm_norm_w[l][:, None] * w_ssm_out[l]).astype(BF16),
            "w_out": w_out[l].astype(BF16),
        }
        mod = mod_all[l]
        q, kn, vn, k_ctx, v_ctx, z, xs, bc, dt, g = _in_proj(x, mod, meta, lw, t_ctx, ctx_len, l, depth,
                                                             kv_ctx)
        kv_ctx = (k_ctx, v_ctx)

        attn = _attention(q, kn, vn, None, None, n_ctx, ctx_len, 0, min(TQ, ctx_len))
        attn = _attention(q, kn, vn, _cache_blocks(cache_k[:, l], cache_v[:, l]), attn,
                          n_lat, lat_len, t_ctx, min(TQ, lat_len))

        a_lanes = jnp.pad(-jnp.exp(a_log[l]).reshape(1, 2 * SSM_HEADS),
                          ((0, 0), (0, LANES - 2 * SSM_HEADS)))
        s0 = state_ssm[:, l].transpose(0, 1, 4, 2, 3).reshape(n_lat, 2, D_STATE, D_INNER)
        yf, yb, s_ctx = _ssd(xs, bc, dt, a_lanes, expand, lw["ssm_d"], s0, meta, n_ctx, l, depth, s_ctx)

        x1, h2, idx, gate = _merge(x, attn, yf, yb, z, g, mod, meta, lw, wrt, rb)
        y0, y1 = _moe(h2, idx, gate, (w_exp_gate, w_exp_up, w_exp_down), l)
        if l < depth - 1:
            x = _combine(x1, y0, y1, gate.T, mod, meta, fnw, None)
        else:
            y_prompt, y_sample = _combine(x1, y0, y1, gate.T, mod, meta, fnw, t_ctx)

    y_prompt = y_prompt.reshape(n_ctx, ctx_len, D_MODEL)
    y_sample = y_sample.reshape(n_lat, lat_len, D_MODEL)
    new_k = kv_ctx[0].reshape(n_ctx, depth, ctx_len, N_KV_HEADS, HEAD_DIM)
    new_v = kv_ctx[1].reshape(n_ctx, depth, ctx_len, N_KV_HEADS, HEAD_DIM)
    new_s = s_ctx.reshape(n_ctx, depth, 2, D_STATE, SSM_HEADS, SSM_HEAD_DIM).transpose(0, 1, 2, 4, 5, 3)
    return (y_prompt, y_sample, new_k, new_v, new_s)
```

```python
import functools

import jax
import jax.numpy as jnp
from jax import lax
from jax.experimental import pallas as pl
from jax.experimental.pallas import tpu as pltpu

F32 = jnp.float32
BF16 = jnp.bfloat16

EPS = 1e-6
D_MODEL = 1024
N_HEADS = 16
N_KV_HEADS = 4
HEAD_DIM = 64
ATTN_WIDTH = N_HEADS * HEAD_DIM
KV_WIDTH = N_KV_HEADS * HEAD_DIM
GRID_W = 64
ROPE_AXIS_DIM = HEAD_DIM // 2
ROPE_THETA = 10000.0
D_INNER = 2 * D_MODEL
SSM_HEAD_DIM = 64
SSM_HEADS = D_INNER // SSM_HEAD_DIM
SSM_GROUPS = 4
HEADS_PER_GROUP = SSM_HEADS // SSM_GROUPS
D_STATE = 128
GN = SSM_GROUPS * D_STATE
CONV_DIM = D_INNER + 2 * GN
SSD_CHUNK = 128
N_EXPERTS = 16
N_EXPERT_GROUPS = 4
EXPERTS_PER_GROUP = N_EXPERTS // N_EXPERT_GROUPS
D_FF_EXPERT = 512
N_MOD = 6
MOD_ROWS = 8

LANES = 128
SUBLANES = 8
VMEM_LIMIT = 56 * 1024 * 1024

TM = 512
TQ = 1024
TK = 256
ATTN_UNIT = 256
TE = 512
MERGE_CHAINS = 4
RANK_BLOCK = 512
CONV_CHUNK = 1024
COND_ROWS = 16
V_ROWS = HEAD_DIM + SUBLANES
LOG2E = 1.4426950408889634
Q_SCALE = HEAD_DIM ** -0.5 * LOG2E

Q_OFF, K_OFF, V_OFF = 0, ATTN_WIDTH, ATTN_WIDTH + KV_WIDTH
Z_OFF = ATTN_WIDTH + 2 * KV_WIDTH
XBC_OFF = Z_OFF + D_INNER
DT_OFF = XBC_OFF + CONV_DIM
G_OFF = DT_OFF + 2 * SSM_HEADS
N_IN = G_OFF + 2 * D_MODEL


def _cparams(sem):
    return pltpu.CompilerParams(dimension_semantics=sem, vmem_limit_bytes=VMEM_LIMIT)


def _const_spec(shape):
    nd = len(shape)
    return pl.BlockSpec(shape, lambda *_: (0,) * nd, pipeline_mode=pl.Buffered(1))


def _dot(a, b):
    return jnp.dot(a, b, preferred_element_type=F32)


def _dot_nt(a, b):
    return lax.dot_general(a, b, (((1,), (1,)), ((), ())), preferred_element_type=F32)


def _dot_tn(a, b):
    return lax.dot_general(a, b, (((0,), (0,)), ((), ())), preferred_element_type=F32)


def _split3(a):
    a1 = a.astype(BF16)
    r = a - a1.astype(F32)
    a2 = r.astype(BF16)
    a3 = (r - a2.astype(F32)).astype(BF16)
    return a1, a2, a3


def _dot_exact_lhs(m_bf16, a):
    a1, a2, a3 = _split3(a)
    return _dot(m_bf16, a1) + _dot(m_bf16, a2) + _dot(m_bf16, a3)


def _modnorm(x, w, sc, sh):
    ms = jnp.mean(x * x, axis=-1, keepdims=True)
    return x * lax.rsqrt(ms + EPS) * (w * (1.0 + sc)) + sh


def _sigmoid(x):
    return 1.0 / (1.0 + jnp.exp(-x))


def _silu(x):
    return x * _sigmoid(x)


def _adaln_kernel(c_ref, w_ref, b_ref, o_ref):
    cs = _silu(c_ref[...])
    o_ref[0] = jnp.dot(cs, w_ref[0], preferred_element_type=F32,
                       precision=lax.Precision.HIGHEST) + b_ref[0]


def _adaln(cond, w_mod, b_mod):
    depth = w_mod.shape[0]
    nb = N_MOD
    return pl.pallas_call(
        _adaln_kernel,
        grid=(depth, nb),
        in_specs=[pl.BlockSpec((COND_ROWS, D_MODEL), lambda l, j: (0, 0)),
                  pl.BlockSpec((1, D_MODEL, D_MODEL), lambda l, j: (l, 0, j)),
                  pl.BlockSpec((1, 1, D_MODEL), lambda l, j: (l, 0, j))],
        out_specs=pl.BlockSpec((1, COND_ROWS, D_MODEL), lambda l, j: (l, 0, j)),
        out_shape=jax.ShapeDtypeStruct((depth, COND_ROWS, N_MOD * D_MODEL), F32),
        compiler_params=_cparams(("arbitrary", "arbitrary")),
        name="adaln",
    )(cond, w_mod, b_mod.reshape(depth, 1, N_MOD * D_MODEL))


def _in_proj_kernel(mrow_ref, rope_ref, ctxi_ref,
                    x_ref, xp_ref, xn_ref, mod_ref, n1w_ref,
                    wqkv_ref, wz_ref, wxbc_ref, wdt_ref, wg_ref,
                    qnw_ref, knw_ref, cos_ref, sin_ref, kprev_ref, knext_ref,
                    convw_ref, convb_ref, dtb_ref,
                    *rest, n_ctx_blocks, ctx_len):
    q_ref, kb_ref, vt_ref, kc_ref, vc_ref, z_ref, xs_ref, bc_ref, dt_ref, g_ref = rest[-10:]
    del mrow_ref, rope_ref, ctxi_ref
    i = pl.program_id(0)
    is_ctx = i < n_ctx_blocks
    sh1 = mod_ref[0, 0:1, :]
    sc1 = mod_ref[0, 1:2, :]
    nw = n1w_ref[...]
    h = _modnorm(x_ref[...], nw, sc1, sh1).astype(BF16)
    hp = _modnorm(xp_ref[...], nw, sc1, sh1).astype(BF16)
    hn = _modnorm(xn_ref[...], nw, sc1, sh1).astype(BF16)

    r = lax.broadcasted_iota(jnp.int32, (2 * LANES, 2 * LANES), 0) // HEAD_DIM
    c = lax.broadcasted_iota(jnp.int32, (2 * LANES, 2 * LANES), 1) // HEAD_DIM
    same_head = jnp.where(r == c, 1.0, 0.0).astype(BF16)
    lane = lax.broadcasted_iota(jnp.int32, (TM, LANES), 1)
    first_half = (lane % (HEAD_DIM // 2)) < (HEAD_DIM // 4)

    def head_norm_rope(t, w, scale):
        sq = t * t
        hi = sq.astype(BF16)
        lo = (sq - hi.astype(F32)).astype(BF16)
        ssum = _dot(hi, same_head) + _dot(lo, same_head)
        tn = t * lax.rsqrt(ssum * (1.0 / HEAD_DIM) + EPS) * w
        outs = []
        for s in range(2):
            a = tn[:, s * LANES:(s + 1) * LANES]
            rot = jnp.where(first_half, -pltpu.roll(a, LANES - HEAD_DIM // 4, 1),
                            pltpu.roll(a, HEAD_DIM // 4, 1))
            outs.append((a * cos_ref[...] + rot * sin_ref[...]) * scale)
        return outs

    qw = qnw_ref[...]
    kw = knw_ref[...]
    ones_rows = jnp.where(lax.broadcasted_iota(jnp.int32, (V_ROWS - HEAD_DIM, TK), 0) == 0,
                          1.0, 0.0).astype(BF16)
    cw = CONV_CHUNK
    rows = lax.broadcasted_iota(jnp.int32, (TM, cw), 0)
    kprev = jnp.concatenate([kprev_ref[...]] * (cw // LANES), axis=1)
    knext = jnp.concatenate([knext_ref[...]] * (cw // LANES), axis=1)

    def q_stage(cb):
        def mm():
            return _dot(h, wqkv_ref[:, cb * 2 * LANES:(cb + 1) * 2 * LANES])

        def fin(t):
            a, b = head_norm_rope(t, qw, Q_SCALE)
            q_ref[:, cb * 2 * LANES:cb * 2 * LANES + LANES] = a.astype(BF16)
            q_ref[:, cb * 2 * LANES + LANES:(cb + 1) * 2 * LANES] = b.astype(BF16)
        return mm, fin

    def k_fin(t):
        kpairs = head_norm_rope(t, kw, 1.0)
        for rb in range(TM // TK):
            for pr in range(2):
                kb_ref[pr, rb] = kpairs[pr][rb * TK:(rb + 1) * TK, :].astype(BF16)

        @pl.when(is_ctx)
        def _():
            for sq in range(TM // ctx_len):
                for pr in range(2):
                    kc_ref[sq, :, pr * LANES:(pr + 1) * LANES] = kpairs[pr][sq * ctx_len:(sq + 1) * ctx_len, :]

    def v_fin(v):
        for rb in range(TM // TK):
            vT = v[rb * TK:(rb + 1) * TK, :].T
            for hd in range(N_KV_HEADS):
                vt_ref[hd // 2, hd % 2, rb, 0:HEAD_DIM, :] = (
                    vT[hd * HEAD_DIM:(hd + 1) * HEAD_DIM, :].astype(BF16))
                vt_ref[hd // 2, hd % 2, rb, HEAD_DIM:V_ROWS, :] = ones_rows

        @pl.when(is_ctx)
        def _():
            for sq in range(TM // ctx_len):
                vc_ref[sq] = v[sq * ctx_len:(sq + 1) * ctx_len, :]

    def z_fin(t):
        z_ref[...] = t.astype(BF16)

    def g_fin(t):
        g_ref[...] = _sigmoid(t).astype(BF16)

    def dt_fin(t):
        dtr = t + dtb_ref[...]
        dt_ref[...] = jnp.maximum(dtr, 0.0) + jnp.log(1.0 + jnp.exp(-jnp.abs(dtr)))

    def conv_stage(cb):
        sl = slice(cb * cw, (cb + 1) * cw)

        def mm():
            return (_dot(h, wxbc_ref[:, sl]), _dot(hp, wxbc_ref[:, sl]), _dot(hn, wxbc_ref[:, sl]))

        def fin(res):
            pre, pp, pn = res
            prev_row = pp[SUBLANES - 1:SUBLANES, :]
            next_row = pn[0:1, :]
            up = jnp.where(rows == 0, prev_row, pltpu.roll(pre, 1, 0)) * kprev
            down = jnp.where(rows == TM - 1, next_row, pltpu.roll(pre, TM - 1, 0)) * knext
            y = (convb_ref[:, sl] + up * convw_ref[0:1, sl] + pre * convw_ref[1:2, sl]
                 + down * convw_ref[2:3, sl])
            y = _silu(y).astype(BF16)
            lo = cb * cw
            if lo < D_INNER:
                xs_ref[:, lo:lo + cw] = y
            else:
                bc_ref[:, lo - D_INNER:lo - D_INNER + cw] = y
        return mm, fin

    stages = [q_stage(cb) for cb in range(ATTN_WIDTH // (2 * LANES))]
    stages.append((lambda: _dot(h, wqkv_ref[:, K_OFF:K_OFF + KV_WIDTH]), k_fin))
    stages.append((lambda: _dot(h, wqkv_ref[:, V_OFF:V_OFF + KV_WIDTH]), v_fin))
    stages.append((lambda: _dot(h, wdt_ref[...]), dt_fin))
    stages += [conv_stage(cb) for cb in range(CONV_DIM // cw)]
    stages.append((lambda: _dot(h, wz_ref[...]), z_fin))
    stages.append((lambda: _dot(h, wg_ref[...]), g_fin))
    pending = None
    for mm, fin in stages:
        res = mm()
        if pending is not None:
            pending[0](pending[1])
        pending = (fin, res)
    pending[0](pending[1])


def _in_proj(x, mod, meta, lw, t_ctx, ctx_len, layer, depth, prev_kv):
    t = x.shape[0]
    nblk = t // TM
    nhalo = t // SUBLANES
    per_halo = TM // SUBLANES
    n_ctx_blocks = t_ctx // TM
    nkb = TM // TK

    def row_blk(width):
        return pl.BlockSpec((TM, width), lambda i, *_: (i, 0))

    spb = TM // ctx_len
    ctx_blk = pl.BlockSpec((spb, None, ctx_len, KV_WIDTH),
                           lambda i, mrow, rope, ctxi: (ctxi[i], layer, 0, 0))
    in_specs = [
        row_blk(D_MODEL),
            pl.BlockSpec((SUBLANES, D_MODEL),
                         lambda i, *_: (jnp.maximum(i * per_halo - 1, 0), 0)),
            pl.BlockSpec((SUBLANES, D_MODEL),
                         lambda i, *_: (jnp.minimum((i + 1) * per_halo, nhalo - 1), 0)),
            pl.BlockSpec((1, MOD_ROWS, D_MODEL), lambda i, mrow, *_: (mrow[i], 0, 0)),
            _const_spec((1, D_MODEL)),
            _const_spec((D_MODEL, ATTN_WIDTH + 2 * KV_WIDTH)),
            _const_spec((D_MODEL, D_INNER)),
            _const_spec((D_MODEL, CONV_DIM)),
            _const_spec((D_MODEL, LANES)),
            _const_spec((D_MODEL, 2 * D_MODEL)),
            _const_spec((1, 2 * LANES)),
            _const_spec((1, 2 * LANES)),
            pl.BlockSpec((TM, LANES), lambda i, mrow, rope, *_: (rope[i], 0)),
            pl.BlockSpec((TM, LANES), lambda i, mrow, rope, *_: (rope[i], 0)),
            row_blk(LANES),
            row_blk(LANES),
            _const_spec((3, CONV_DIM)),
            _const_spec((1, CONV_DIM)),
            _const_spec((1, LANES)),
    ]
    out_specs = [row_blk(ATTN_WIDTH),
                   pl.BlockSpec((2, nkb, TK, LANES), lambda i, *_: (0, i, 0, 0)),
                   pl.BlockSpec((2, 2, nkb, V_ROWS, TK), lambda i, *_: (0, 0, i, 0, 0)),
                   ctx_blk, ctx_blk,
                   row_blk(D_INNER), row_blk(D_INNER), row_blk(2 * GN), row_blk(LANES),
                   row_blk(2 * D_MODEL)]
    out_shape = [
        jax.ShapeDtypeStruct((t, ATTN_WIDTH), BF16),
        jax.ShapeDtypeStruct((2, t // TK, TK, LANES), BF16),
        jax.ShapeDtypeStruct((2, 2, t // TK, V_ROWS, TK), BF16),
        jax.ShapeDtypeStruct((t_ctx // ctx_len, depth, ctx_len, KV_WIDTH), F32),
        jax.ShapeDtypeStruct((t_ctx // ctx_len, depth, ctx_len, KV_WIDTH), F32),
        jax.ShapeDtypeStruct((t, D_INNER), BF16),
        jax.ShapeDtypeStruct((t, D_INNER), BF16),
        jax.ShapeDtypeStruct((t, 2 * GN), BF16),
        jax.ShapeDtypeStruct((t, LANES), F32),
        jax.ShapeDtypeStruct((t, 2 * D_MODEL), BF16),
    ]
    args = [meta["mrow"], meta["rope_blk"], meta["ctx_blk"],
            x, x, x, mod, lw["norm1_w"], lw["wqkv"], lw["wz"], lw["wxbc"], lw["wdt"], lw["wg"],
            lw["q_norm_w"], lw["k_norm_w"], meta["cos"], meta["sin"], meta["keep_prev"],
            meta["keep_next"], lw["conv_w"], lw["conv_b"], lw["dt_bias"]]
    aliases = {}
    if prev_kv is not None:
        in_specs += [pl.BlockSpec(memory_space=pl.ANY)] * 2
        aliases = {len(args): 3, len(args) + 1: 4}
        args += list(prev_kv)
    grid_spec = pltpu.PrefetchScalarGridSpec(num_scalar_prefetch=3, grid=(nblk,), in_specs=in_specs,
                                             out_specs=out_specs)
    return pl.pallas_call(
        functools.partial(_in_proj_kernel, n_ctx_blocks=n_ctx_blocks, ctx_len=ctx_len),
        grid_spec=grid_spec, out_shape=out_shape, input_output_aliases=aliases,
        compiler_params=_cparams(("arbitrary",)), name="in_proj",
    )(*args)


def _attn_kernel(*refs, n_new, n_cache, tq):
    q_ref, kn_ref, vn_ref = refs[0:3]
    if n_cache:
        kc_ref, vc_ref = refs[3:5]
    o_ref, qt_ref, m_ref, acc_ref, sa_ref, mxa_ref, sb_ref, mxb_ref = refs[-8:]
    nq = 4 * tq
    zeros_half = jnp.zeros((HEAD_DIM, tq), F32)
    for hh in range(2):
        cols = []
        for jj in range(2):
            off = 2 * LANES * hh + LANES * jj
            qt = q_ref[:, off:off + LANES].astype(F32).T
            for s in range(2):
                head = qt[s * HEAD_DIM:(s + 1) * HEAD_DIM, :]
                parts = [head, zeros_half] if hh == 0 else [zeros_half, head]
                cols.append(jnp.concatenate(parts, axis=0))
        qt_ref[hh] = jnp.concatenate(cols, axis=1).astype(BF16)
        m_ref[hh] = jnp.full((1, nq), -jnp.inf, F32)
        acc_ref[hh] = jnp.zeros((V_ROWS, nq), F32)

    units = [(hh, slice(c * ATTN_UNIT, (c + 1) * ATTN_UNIT))
             for hh in range(2) for c in range(nq // ATTN_UNIT)]

    def scores(kblk, s_ref, mx_ref, hh, cs):
        s = _dot(kblk, qt_ref[hh, :, cs])
        s_ref[hh, :, cs] = s
        mx_ref[hh, :, cs] = jnp.max(s, axis=0, keepdims=True)

    def consume(vt, s_ref, mx_ref, hh, cs):
        m = m_ref[hh, :, cs]
        m_new = jnp.maximum(m, mx_ref[hh, :, cs])
        alpha = jnp.exp2(m - m_new)
        p = jnp.exp2(s_ref[hh, :, cs] - m_new).astype(BF16)
        acc_ref[hh, :, cs] = alpha * acc_ref[hh, :, cs] + _dot(vt, p)
        m_ref[hh, :, cs] = m_new

    def step(k_nxt, nxt, v_cur, cur):
        for hh, cs in units:
            if k_nxt is not None:
                scores(k_nxt, *nxt, hh, cs)
            consume(v_cur[hh], *cur, hh, cs)

    bufs = ((sa_ref, mxa_ref), (sb_ref, mxb_ref))
    for hh, cs in units:
        scores(kn_ref[0], *bufs[0], hh, cs)

    def body(j, carry):
        step(kn_ref[2 * j + 1], bufs[1], (vn_ref[0, 2 * j], vn_ref[1, 2 * j]), bufs[0])
        step(kn_ref[2 * j + 2], bufs[0], (vn_ref[0, 2 * j + 1], vn_ref[1, 2 * j + 1]), bufs[1])
        return carry

    n_loop = (n_new - 1) // 2
    lax.fori_loop(0, n_loop, body, 0)
    rest = [(kn_ref, vn_ref, b) for b in range(2 * n_loop, n_new)]
    if n_cache:
        rest += [(kc_ref, vc_ref, b) for b in range(n_cache)]
    for r, (_, v_src, b) in enumerate(rest):
        k_nxt = None
        if r + 1 < len(rest):
            k_src, _, b_nxt = rest[r + 1]
            k_nxt = k_src[b_nxt]
        step(k_nxt, bufs[(r + 1) % 2], (v_src[0, b], v_src[1, b]), bufs[r % 2])

    for hh in range(2):
        oT = acc_ref[hh]
        oT = oT[0:HEAD_DIM, :] / oT[HEAD_DIM:HEAD_DIM + 1, :]
        for jj in range(2):
            pair = jnp.concatenate([oT[:, (2 * jj) * tq:(2 * jj + 1) * tq],
                                    oT[:, (2 * jj + 1) * tq:(2 * jj + 2) * tq]], axis=0)
            off = 2 * LANES * hh + LANES * jj
            o_ref[:, off:off + LANES] = pair.T.astype(BF16)


def _attention(q, kn, vn, cache, prev_out, nseq, seq_len, row0, tq):
    n_new = seq_len // TK
    nqb = seq_len // tq
    nq = 4 * tq
    q0 = row0 // tq
    kv0 = row0 // (n_new * TK)
    assert row0 % tq == 0 and row0 % (n_new * TK) == 0
    qspec = pl.BlockSpec((tq, 4 * LANES), lambda b, p, i: (q0 + b * nqb + i, p))
    in_specs = [
        qspec,
        pl.BlockSpec((None, n_new, TK, LANES), lambda b, p, i: (p, kv0 + b, 0, 0)),
        pl.BlockSpec((None, 2, n_new, V_ROWS, TK), lambda b, p, i: (p, 0, kv0 + b, 0, 0)),
    ]
    args = [q, kn, vn]
    n_cache = 0
    if cache is not None:
        kc, vc = cache
        n_cache = kc.shape[1] // nseq
        in_specs += [pl.BlockSpec((None, n_cache, TK, LANES), lambda b, p, i: (p, b, 0, 0)),
                     pl.BlockSpec((None, 2, n_cache, V_ROWS, TK), lambda b, p, i: (p, 0, b, 0, 0))]
        args += [kc, vc]
    aliases = {}
    kern = functools.partial(_attn_kernel, n_new=n_new, n_cache=n_cache, tq=tq)
    if prev_out is not None:
        n_in = len(args)
        in_specs.append(pl.BlockSpec(memory_space=pl.ANY))
        aliases = {n_in: 0}
        args.append(prev_out)
        inner = kern
        kern = lambda *refs: inner(*refs[:n_in], *refs[n_in + 1:])
    return pl.pallas_call(
        kern,
        grid=(nseq, 2, nqb),
        in_specs=in_specs,
        out_specs=qspec,
        out_shape=jax.ShapeDtypeStruct(q.shape, BF16),
        scratch_shapes=[pltpu.VMEM((2, LANES, nq), BF16),
                        pltpu.VMEM((2, 1, nq), F32),
                        pltpu.VMEM((2, V_ROWS, nq), F32),
                        pltpu.VMEM((2, TK, nq), F32),
                        pltpu.VMEM((2, 1, nq), F32),
                        pltpu.VMEM((2, TK, nq), F32),
                        pltpu.VMEM((2, 1, nq), F32)],
        input_output_aliases=aliases,
        compiler_params=_cparams(("arbitrary", "arbitrary", "arbitrary")),
        name="attention",
    )(*args)


def _cache_blocks(ck, cv):
    nseq, past = ck.shape[0], ck.shape[1]
    n = past // TK
    kb = ck.astype(BF16).reshape(nseq * n, TK, 2, LANES).transpose(2, 0, 1, 3)
    vb = cv.astype(BF16).reshape(nseq * n, TK, 2, 2, HEAD_DIM).transpose(2, 3, 0, 4, 1)
    ones = jnp.ones(vb.shape[:3] + (1, TK), BF16)
    zeros = jnp.zeros(vb.shape[:3] + (V_ROWS - HEAD_DIM - 1, TK), BF16)
    return kb, jnp.concatenate([vb, ones, zeros], axis=3)


def _ssd_direction(x_ref, bc_ref, dt_ref, a_lanes, expand, dvec_ref, st_ref, y_ref, d):
    q = SSD_CHUNK
    row = lax.broadcasted_iota(jnp.int32, (q, q), 0)
    col = lax.broadcasted_iota(jnp.int32, (q, q), 1)
    causal = (row >= col) if d == 0 else (row <= col)
    tril = jnp.where(row >= col, 1.0, 0.0).astype(BF16)

    dt = dt_ref[...]
    a = dt * a_lanes
    prefix = _dot_exact_lhs(tril, a)
    if d == 0:
        cs = prefix
        last = cs[q - 1:q, :]
    else:
        cs = prefix[q - 1:q, :] - prefix + a
        last = cs[0:1, :]
    cs2 = cs * LOG2E
    cst2 = (cs2 - jnp.log2(dt)).T
    wj = jnp.exp(last - cs) * dt
    wj_x = _dot(wj.astype(BF16), expand)
    elast = jnp.exp(last)

    lane = lax.broadcasted_iota(jnp.int32, (q, LANES), 1)
    lo_half = lane < SSM_HEAD_DIM
    lo_half_row = lo_half[0:1, :]
    for g in range(SSM_GROUPS):
        bg = bc_ref[:, g * D_STATE:(g + 1) * D_STATE]
        cg = bc_ref[:, GN + g * D_STATE:GN + (g + 1) * D_STATE]
        gmat = _dot_nt(cg, bg)
        gsl = slice(g * HEADS_PER_GROUP * SSM_HEAD_DIM, (g + 1) * HEADS_PER_GROUP * SSM_HEAD_DIM)
        st = st_ref[d, :, gsl]
        y_inter = _dot(cg, st.astype(BF16))
        xg = x_ref[:, gsl]
        el_parts = []
        for hp in range(HEADS_PER_GROUP // 2):
            xpair = xg[:, hp * LANES:(hp + 1) * LANES]
            l0 = d * SSM_HEADS + g * HEADS_PER_GROUP + 2 * hp
            ws, es = [], []
            for s in range(2):
                hl = l0 + s
                csb = jnp.broadcast_to(cs2[:, hl:hl + 1], (q, q))
                seg = csb - cst2[hl:hl + 1, :]
                ws.append(gmat * jnp.exp2(jnp.where(causal, seg, -jnp.inf)))
                es.append(jnp.exp2(csb))
            wcat = jnp.concatenate(ws, axis=1).astype(BF16)
            zero = jnp.zeros_like(xpair)
            xm = jnp.concatenate([jnp.where(lo_half, xpair, zero), jnp.where(lo_half, zero, xpair)],
                                 axis=0)
            e_pair = jnp.where(lo_half, es[0], es[1])
            ypair = y_inter[:, hp * LANES:(hp + 1) * LANES] * e_pair + _dot(wcat, xm)
            lo = g * HEADS_PER_GROUP * SSM_HEAD_DIM + hp * LANES
            if d == 0:
                ypair = ypair + dvec_ref[:, lo:lo + LANES] * xpair.astype(F32)
            y_ref[:, lo:lo + LANES] = ypair.astype(y_ref.dtype)
            el_parts.append(jnp.where(lo_half_row, elast[:, l0:l0 + 1], elast[:, l0 + 1:l0 + 2]))
        xw = (xg.astype(F32) * wj_x[:, gsl]).astype(BF16)
        el = jnp.concatenate(el_parts, axis=1)
        st_ref[d, :, gsl] = st * el + _dot_tn(bg, xw)


def _ssd_kernel(fblk_ref, bblk_ref, first_ref, last_ref, init_ref, s0i_ref, sfi_ref,
                xf_ref, bcf_ref, dtf_ref, xb_ref, bcb_ref, dtb_ref, a_ref, exp_ref, dvec_ref, s0_ref,
                *rest):
    yf_ref, yb_ref, sfin_ref, st_ref = rest[-4:]
    del fblk_ref, bblk_ref, s0i_ref, sfi_ref
    s = pl.program_id(0)

    @pl.when(jnp.logical_and(first_ref[s] == 1, init_ref[s] == 1))
    def _():
        st_ref[...] = s0_ref[0]

    @pl.when(jnp.logical_and(first_ref[s] == 1, init_ref[s] == 0))
    def _():
        st_ref[...] = jnp.zeros(st_ref.shape, F32)

    a_lanes = a_ref[...]
    _ssd_direction(xf_ref, bcf_ref, dtf_ref, a_lanes, exp_ref[0], dvec_ref, st_ref, yf_ref, 0)
    _ssd_direction(xb_ref, bcb_ref, dtb_ref, a_lanes, exp_ref[1], dvec_ref, st_ref, yb_ref, 1)

    @pl.when(jnp.logical_and(last_ref[s] == 1, init_ref[s] == 0))
    def _():
        sfin_ref[0] = st_ref[...]


def _ssd(xs, bc, dt, a_lanes, expand, dvec, s0, meta, n_ctx, layer, depth, prev_sfin):
    t = xs.shape[0]
    n_steps = t // SSD_CHUNK
    st_shape = (2, D_STATE, D_INNER)
    fwd = lambda s, fblk, bblk, *_: (fblk[s], 0)
    bwd = lambda s, fblk, bblk, *_: (bblk[s], 0)

    def specs(idx):
        return [pl.BlockSpec((SSD_CHUNK, D_INNER), idx),
                pl.BlockSpec((SSD_CHUNK, 2 * GN), idx),
                pl.BlockSpec((SSD_CHUNK, LANES), idx)]

    in_specs = specs(fwd) + specs(bwd) + [
        pl.BlockSpec((1, LANES), lambda s, *_: (0, 0)),
        pl.BlockSpec((2, LANES, D_INNER), lambda s, *_: (0, 0, 0)),
        pl.BlockSpec((1, D_INNER), lambda s, *_: (0, 0)),
        pl.BlockSpec((1,) + st_shape, lambda s, f, b, fi, la, ini, s0i, sfi: (s0i[s], 0, 0, 0)),
    ]
    args = [meta["ssd_fblk"], meta["ssd_bblk"], meta["ssd_first"], meta["ssd_last"], meta["ssd_init"],
            meta["ssd_s0i"], meta["ssd_sfi"], xs, bc, dt, xs, bc, dt, a_lanes, expand, dvec, s0]
    aliases = {}
    if prev_sfin is not None:
        in_specs.append(pl.BlockSpec(memory_space=pl.ANY))
        aliases = {len(args): 2}
        args.append(prev_sfin)
    grid_spec = pltpu.PrefetchScalarGridSpec(
        num_scalar_prefetch=7,
        grid=(n_steps,),
        in_specs=in_specs,
        out_specs=[pl.BlockSpec((SSD_CHUNK, D_INNER), fwd), pl.BlockSpec((SSD_CHUNK, D_INNER), bwd),
                   pl.BlockSpec((1, None) + st_shape,
                                lambda s, f, b, fi, la, ini, s0i, sfi: (sfi[s], layer, 0, 0, 0))],
        scratch_shapes=[pltpu.VMEM(st_shape, F32)],
    )
    out_shape = [jax.ShapeDtypeStruct((t, D_INNER), BF16), jax.ShapeDtypeStruct((t, D_INNER), BF16),
                 jax.ShapeDtypeStruct((n_ctx, depth) + st_shape, F32)]
    return pl.pallas_call(
        _ssd_kernel, grid_spec=grid_spec, out_shape=out_shape, input_output_aliases=aliases,
        compiler_params=_cparams(("arbitrary",)), name="ssd",
    )(*args)


def _route(sel, scores):
    neg = -jnp.inf
    rows = [sel[e:e + 1, :] for e in range(N_EXPERTS)]
    srow = [scores[e:e + 1, :] for e in range(N_EXPERTS)]
    best_score = None
    best = None
    for g in range(N_EXPERT_GROUPS):
        v = rows[g * EXPERTS_PER_GROUP:(g + 1) * EXPERTS_PER_GROUP]
        top2 = None
        for a in range(EXPERTS_PER_GROUP):
            for b in range(a + 1, EXPERTS_PER_GROUP):
                s = v[a] + v[b]
                top2 = s if top2 is None else jnp.maximum(top2, s)
        if g == 0:
            best_score, best = top2, jnp.zeros(top2.shape, jnp.int32)
        else:
            better = top2 > best_score
            best = jnp.where(better, g, best)
            best_score = jnp.where(better, top2, best_score)

    def pick(vals):
        out = vals[0]
        for g in range(1, N_EXPERT_GROUPS):
            out = jnp.where(best == g, vals[g], out)
        return out

    gsel = [pick([rows[g * EXPERTS_PER_GROUP + j] for g in range(N_EXPERT_GROUPS)])
            for j in range(EXPERTS_PER_GROUP)]
    gsc = [pick([srow[g * EXPERTS_PER_GROUP + j] for g in range(N_EXPERT_GROUPS)])
           for j in range(EXPERTS_PER_GROUP)]

    def argmax_first(vals):
        bi = jnp.zeros(vals[0].shape, jnp.int32)
        bv = vals[0]
        for j in range(1, len(vals)):
            better = vals[j] > bv
            bi = jnp.where(better, j, bi)
            bv = jnp.where(better, vals[j], bv)
        return bi

    i1 = argmax_first(gsel)
    i2 = argmax_first([jnp.where(i1 == j, neg, gsel[j]) for j in range(EXPERTS_PER_GROUP)])

    def take(vals, idx):
        out = vals[0]
        for j in range(1, len(vals)):
            out = jnp.where(idx == j, vals[j], out)
        return out

    g1 = take(gsc, i1)
    g2 = take(gsc, i2)
    tot = g1 + g2
    idx = jnp.concatenate([best * EXPERTS_PER_GROUP + i1, best * EXPERTS_PER_GROUP + i2], axis=0)
    gate = jnp.concatenate([g1 / tot, g2 / tot], axis=0)
    return idx, gate


def _merge_kernel(mrow_ref, x_ref, attn_ref, yf_ref, yb_ref, z_ref, g_ref, mod_ref,
                  wa_ref, ws_ref, wo_ref, n2w_ref, wrt_ref, rb_ref,
                  x1_ref, h2_ref, idx_ref, gate_ref):
    del mrow_ref
    gw = D_INNER // SSM_GROUPS
    hr = TM // MERGE_CHAINS

    def chain(ci):
        rs = slice(ci * hr, (ci + 1) * hr)
        attn_o = _dot(attn_ref[rs, :], wa_ref[...])
        yield
        ssm_o = None
        for g in range(SSM_GROUPS):
            sl = slice(g * gw, (g + 1) * gw)
            y = (yf_ref[rs, sl] + yb_ref[rs, sl]).astype(F32) * _silu(z_ref[rs, sl].astype(F32))
            ms = jnp.mean(y * y, axis=-1, keepdims=True)
            yn = (y * lax.rsqrt(ms + EPS)).astype(BF16)
            part = _dot(yn, ws_ref[sl, :])
            ssm_o = part if ssm_o is None else ssm_o + part
            yield
        merged = (g_ref[rs, 0:D_MODEL].astype(F32) * attn_o
                  + g_ref[rs, D_MODEL:2 * D_MODEL].astype(F32) * ssm_o)
        out = _dot(merged.astype(BF16), wo_ref[...])
        yield
        x1 = x_ref[rs, :] + mod_ref[0, 2:3, :] * out
        x1_ref[rs, :] = x1
        h2 = _modnorm(x1, n2w_ref[...], mod_ref[0, 4:5, :], mod_ref[0, 3:4, :])
        h2_ref[rs, :] = h2.astype(BF16)
        logits = lax.dot_general(wrt_ref[...], h2, (((1,), (1,)), ((), ())),
                                 preferred_element_type=F32, precision=lax.Precision.HIGHEST)
        yield
        scores = _sigmoid(logits)
        idx, gate = _route(scores + rb_ref[...], scores)
        idx_ref[:, rs] = idx
        gate_ref[:, rs] = gate

    live = [chain(ci) for ci in range(MERGE_CHAINS)]
    while live:
        for c in list(live):
            try:
                next(c)
            except StopIteration:
                live.remove(c)


def _merge(x, attn, yf, yb, z, g, mod, meta, lw, wrt, rb):
    t = x.shape[0]
    nblk = t // TM

    def row_blk(width):
        return pl.BlockSpec((TM, width), lambda i, *_: (i, 0))

    grid_spec = pltpu.PrefetchScalarGridSpec(
        num_scalar_prefetch=1,
        grid=(nblk,),
        in_specs=[row_blk(D_MODEL), row_blk(ATTN_WIDTH), row_blk(D_INNER), row_blk(D_INNER),
                  row_blk(D_INNER), row_blk(2 * D_MODEL),
                  pl.BlockSpec((1, MOD_ROWS, D_MODEL), lambda i, mrow: (mrow[i], 0, 0)),
                  _const_spec((ATTN_WIDTH, D_MODEL)), _const_spec((D_INNER, D_MODEL)),
                  _const_spec((D_MODEL, D_MODEL)), _const_spec((1, D_MODEL)),
                  _const_spec((N_EXPERTS, D_MODEL)), _const_spec((N_EXPERTS, 1))],
        out_specs=[row_blk(D_MODEL), row_blk(D_MODEL),
                   pl.BlockSpec((2, TM), lambda i, *_: (0, i)),
                   pl.BlockSpec((2, TM), lambda i, *_: (0, i))],
    )
    out_shape = [jax.ShapeDtypeStruct((t, D_MODEL), F32), jax.ShapeDtypeStruct((t, D_MODEL), BF16),
                 jax.ShapeDtypeStruct((2, t), jnp.int32), jax.ShapeDtypeStruct((2, t), F32)]
    return pl.pallas_call(
        _merge_kernel, grid_spec=grid_spec, out_shape=out_shape,
        compiler_params=_cparams(("arbitrary",)), name="merge",
    )(meta["mrow"], x, attn, yf, yb, z, g, mod, lw["w_attn_out"], lw["w_ssm_out"], lw["w_out"],
      lw["norm2_w"], wrt, rb)


def _expert_kernel(te_ref, nt_ref, chg_ref, x_ref, wg_ref, wu_ref, wd_ref, o_ref,
                   wgs_ref, wus_ref, wds_ref):
    del te_ref
    i = pl.program_id(0)

    @pl.when(chg_ref[i] == 1)
    def _():
        wgs_ref[...] = wg_ref[0].astype(BF16)
        wus_ref[...] = wu_ref[0].astype(BF16)
        wds_ref[...] = wd_ref[0].astype(BF16)

    @pl.when(i < nt_ref[0])
    def _():
        x = x_ref[...]
        hmid = _silu(_dot(x, wgs_ref[...])) * _dot(x, wus_ref[...])
        o_ref[...] = _dot(hmid.astype(BF16), wds_ref[...]).astype(o_ref.dtype)

    @pl.when(i >= nt_ref[0])
    def _():
        o_ref[...] = jnp.zeros(o_ref.shape, o_ref.dtype)


def _experts(x_sorted, tile_expert, n_tiles, wg, wu, wd, layer):
    rows = x_sorted.shape[0]
    changed = jnp.concatenate([jnp.ones((1,), jnp.int32),
                               (tile_expert[1:] != tile_expert[:-1]).astype(jnp.int32)])
    grid_spec = pltpu.PrefetchScalarGridSpec(
        num_scalar_prefetch=3,
        grid=(rows // TE,),
        in_specs=[pl.BlockSpec((TE, D_MODEL), lambda i, te, nt, chg: (i, 0)),
                  pl.BlockSpec((None, 1, D_MODEL, D_FF_EXPERT), lambda i, te, nt, chg: (layer, te[i], 0, 0)),
                  pl.BlockSpec((None, 1, D_MODEL, D_FF_EXPERT), lambda i, te, nt, chg: (layer, te[i], 0, 0)),
                  pl.BlockSpec((None, 1, D_FF_EXPERT, D_MODEL), lambda i, te, nt, chg: (layer, te[i], 0, 0))],
        out_specs=pl.BlockSpec((TE, D_MODEL), lambda i, te, nt, chg: (i, 0)),
        scratch_shapes=[pltpu.VMEM((D_MODEL, D_FF_EXPERT), BF16),
                        pltpu.VMEM((D_MODEL, D_FF_EXPERT), BF16),
                        pltpu.VMEM((D_FF_EXPERT, D_MODEL), BF16)],
    )
    return pl.pallas_call(
        _expert_kernel, grid_spec=grid_spec,
        out_shape=jax.ShapeDtypeStruct((rows, D_MODEL), BF16),
        compiler_params=_cparams(("arbitrary",)), name="experts",
    )(tile_expert, n_tiles, changed, x_sorted, wg, wu, wd)


def _combine_kernel(mrow_ref, x1_ref, y0_ref, y1_ref, gate_ref, mod_ref, fnw_ref, *out_refs,
                    n_ctx_blocks):
    del mrow_ref
    moe = (gate_ref[:, 0:1] * y0_ref[...].astype(F32) + gate_ref[:, 1:2] * y1_ref[...].astype(F32))
    x2 = x1_ref[...] + mod_ref[0, 5:6, :] * moe
    if n_ctx_blocks is None:
        out_refs[0][...] = x2
        return
    ms = jnp.mean(x2 * x2, axis=-1, keepdims=True)
    y = x2 * lax.rsqrt(ms + EPS) * fnw_ref[...]
    is_ctx = pl.program_id(0) < n_ctx_blocks

    @pl.when(is_ctx)
    def _():
        out_refs[0][...] = y

    @pl.when(jnp.logical_not(is_ctx))
    def _():
        out_refs[1][...] = y


def _combine(x1, y0, y1, gate_t, mod, meta, fnw, t_ctx):
    t = x1.shape[0]

    def row_blk(width):
        return pl.BlockSpec((TM, width), lambda i, *_: (i, 0))

    if t_ctx is None:
        n_ctx_blocks = None
        out_specs = row_blk(D_MODEL)
        out_shape = jax.ShapeDtypeStruct((t, D_MODEL), F32)
    else:
        n_ctx_blocks = t_ctx // TM
        out_specs = [pl.BlockSpec((TM, D_MODEL), lambda i, *_: (jnp.minimum(i, n_ctx_blocks - 1), 0)),
                     pl.BlockSpec((TM, D_MODEL), lambda i, *_: (jnp.maximum(i - n_ctx_blocks, 0), 0))]
        out_shape = [jax.ShapeDtypeStruct((t_ctx, D_MODEL), F32),
                     jax.ShapeDtypeStruct((t - t_ctx, D_MODEL), F32)]
    grid_spec = pltpu.PrefetchScalarGridSpec(
        num_scalar_prefetch=1,
        grid=(t // TM,),
        in_specs=[row_blk(D_MODEL), row_blk(D_MODEL), row_blk(D_MODEL), row_blk(2),
                  pl.BlockSpec((1, MOD_ROWS, D_MODEL), lambda i, mrow: (mrow[i], 0, 0)),
                  _const_spec((1, D_MODEL))],
        out_specs=out_specs,
    )
    return pl.pallas_call(
        functools.partial(_combine_kernel, n_ctx_blocks=n_ctx_blocks), grid_spec=grid_spec,
        out_shape=out_shape,
        compiler_params=_cparams(("arbitrary",)), name="combine",
    )(meta["mrow"], x1, y0, y1, gate_t, mod, fnw)


def _moe(h2, idx, gate, w_exp, layer):
    t = h2.shape[0]
    n_assign = 2 * t
    rows = n_assign + N_EXPERTS * TE
    e_flat = idx.reshape(n_assign)
    onehot = e_flat[:, None] == jnp.arange(N_EXPERTS, dtype=jnp.int32)[None, :]
    nb = n_assign // RANK_BLOCK
    oh = onehot.astype(BF16).reshape(nb, RANK_BLOCK, N_EXPERTS)
    tril = jnp.tril(jnp.ones((RANK_BLOCK, RANK_BLOCK), BF16))
    local = jnp.einsum("ij,bjk->bik", tril, oh, preferred_element_type=F32)
    bsum = local[:, -1, :]
    before = jnp.dot(jnp.tril(jnp.ones((nb, nb), F32), -1), bsum, precision=lax.Precision.HIGHEST)
    running = (local + before[:, None, :]).reshape(n_assign, N_EXPERTS)
    counts = (before[-1] + bsum[-1]).astype(jnp.int32)
    padded = ((counts + TE - 1) // TE) * TE
    pad_end = jnp.cumsum(padded)
    pad_off = pad_end - padded
    off = jnp.cumsum(counts) - counts
    pos = jnp.sum(jnp.where(onehot, running + (pad_off - 1).astype(F32)[None, :], 0.0), axis=1
                  ).astype(jnp.int32)
    order = jnp.argsort(e_flat, stable=True).astype(jnp.int32)
    p = jnp.arange(rows, dtype=jnp.int32)
    ep = jnp.minimum(jnp.sum(p[:, None] >= pad_end[None, :], axis=1), N_EXPERTS - 1).astype(jnp.int32)
    r = p - pad_off[ep]
    valid = r < counts[ep]
    src = jnp.where(valid, order[jnp.clip(off[ep] + r, 0, n_assign - 1)] % t, p % t)
    n_tiles = (pad_end[-1] // TE).astype(jnp.int32).reshape(1)
    tile_start = jnp.arange(rows // TE, dtype=jnp.int32) * TE
    tile_expert = jnp.minimum(jnp.sum(tile_start[:, None] >= pad_end[None, :], axis=1),
                              N_EXPERTS - 1).astype(jnp.int32)
    last_used = tile_expert[jnp.maximum(n_tiles[0] - 1, 0)]
    tile_expert = jnp.where(tile_start < pad_end[-1], tile_expert, last_used)
    x_sorted = h2.at[src].get(mode="promise_in_bounds")
    y_sorted = _experts(x_sorted, tile_expert, n_tiles, *w_exp, layer)
    pos2 = pos.reshape(2, t)
    return (y_sorted.at[pos2[0]].get(mode="promise_in_bounds"),
            y_sorted.at[pos2[1]].get(mode="promise_in_bounds"))


def _rope_tables(n_tokens):
    rows = n_tokens // GRID_W
    row = jnp.repeat(jnp.arange(rows, dtype=F32), GRID_W)
    col = jnp.tile(jnp.arange(GRID_W, dtype=F32), rows)
    inv = 1.0 / (ROPE_THETA ** (jnp.arange(0, ROPE_AXIS_DIM, 2, dtype=F32) / ROPE_AXIS_DIM))
    ar = row[:, None] * inv
    ac = col[:, None] * inv
    ang = jnp.concatenate([ar, ar, ac, ac], axis=-1)
    cos, sin = jnp.cos(ang), jnp.sin(ang)
    cos = jnp.concatenate([jnp.ones((TM, HEAD_DIM), F32), cos], axis=0)
    sin = jnp.concatenate([jnp.zeros((TM, HEAD_DIM), F32), sin], axis=0)
    return jnp.tile(cos, (1, 2)), jnp.tile(sin, (1, 2))


def _block_meta(n_ctx, ctx_len, n_lat, lat_len):
    ctx_blocks = n_ctx * ctx_len // TM
    lat_blocks = n_lat * lat_len // TM
    per_lat = lat_len // TM
    bi = jnp.arange(ctx_blocks + lat_blocks, dtype=jnp.int32)
    is_ctx = bi < ctx_blocks
    lat_i = bi - ctx_blocks
    mrow = jnp.where(is_ctx, 0, 1 + lat_i // per_lat).astype(jnp.int32)
    rope_blk = jnp.where(is_ctx, 0, 1 + lat_i % per_lat).astype(jnp.int32)
    cos, sin = _rope_tables(lat_len)
    ti = jnp.arange(n_ctx * ctx_len + n_lat * lat_len, dtype=jnp.int32)
    pos = jnp.where(ti < n_ctx * ctx_len, ti % ctx_len, (ti - n_ctx * ctx_len) % lat_len)
    slen = jnp.where(ti < n_ctx * ctx_len, ctx_len, lat_len)
    keep_prev = jnp.broadcast_to((pos != 0).astype(F32)[:, None], (ti.shape[0], LANES))
    keep_next = jnp.broadcast_to((pos != slen - 1).astype(F32)[:, None], (ti.shape[0], LANES))
    meta = {"mrow": mrow, "rope_blk": rope_blk, "cos": cos, "sin": sin,
            "ctx_blk": jnp.minimum(bi, ctx_blocks - 1).astype(jnp.int32),
            "keep_prev": keep_prev, "keep_next": keep_next}

    nc_ctx, nc_lat = ctx_len // SSD_CHUNK, lat_len // SSD_CHUNK
    n_ctx_steps = n_ctx * nc_ctx
    si = jnp.arange(n_ctx_steps + n_lat * nc_lat, dtype=jnp.int32)
    s_ctx = si < n_ctx_steps
    li = si - n_ctx_steps
    seq = jnp.where(s_ctx, si // nc_ctx, li // nc_lat)
    ch = jnp.where(s_ctx, si % nc_ctx, li % nc_lat)
    nc = jnp.where(s_ctx, nc_ctx, nc_lat)
    base = jnp.where(s_ctx, seq * nc_ctx, n_ctx_steps + seq * nc_lat)
    meta.update({
        "ssd_fblk": (base + ch).astype(jnp.int32),
        "ssd_bblk": (base + nc - 1 - ch).astype(jnp.int32),
        "ssd_first": (ch == 0).astype(jnp.int32),
        "ssd_last": (ch == nc - 1).astype(jnp.int32),
        "ssd_init": jnp.where(s_ctx, 0, 1).astype(jnp.int32),
        "ssd_s0i": jnp.where(s_ctx, 0, seq).astype(jnp.int32),
        "ssd_sfi": jnp.where(s_ctx, seq, n_ctx - 1).astype(jnp.int32),
    })
    return meta


def kernel(x_prompt, x_sample, cache_k, cache_v, state_ssm, c, c_ctx, norm1_w, norm2_w, w_mod, b_mod,
           w_in, q_norm_w, k_norm_w, conv_w, conv_b, a_log, dt_bias, ssm_d, ssm_norm_w, w_attn_out,
           w_ssm_out, w_out, w_router, router_bias, w_exp_gate, w_exp_up, w_exp_down, final_norm_w):
    n_ctx, ctx_len, _ = x_prompt.shape
    n_lat, lat_len, _ = x_sample.shape
    depth = w_in.shape[0]
    t_ctx = n_ctx * ctx_len
    t_lat = n_lat * lat_len
    assert t_ctx % TM == 0 and lat_len % TM == 0 and 1 + n_lat <= COND_ROWS and TM % ctx_len == 0
    assert ctx_len % TK == 0 and cache_k.shape[2] % TK == 0 and lat_len % GRID_W == 0

    meta = _block_meta(n_ctx, ctx_len, n_lat, lat_len)
    x = jnp.concatenate([x_prompt.reshape(t_ctx, D_MODEL), x_sample.reshape(t_lat, D_MODEL)], axis=0)

    cond = jnp.zeros((COND_ROWS, D_MODEL), F32).at[0].set(c_ctx).at[1:1 + n_lat].set(c)
    mod_all = _adaln(cond, w_mod, b_mod).reshape(depth, COND_ROWS, N_MOD, D_MODEL)
    mod_all = jnp.pad(mod_all, ((0, 0), (0, 0), (0, MOD_ROWS - N_MOD), (0, 0)))

    head_of_lane = jnp.arange(D_INNER, dtype=jnp.int32) // SSM_HEAD_DIM
    lane_id = jnp.arange(LANES, dtype=jnp.int32)
    expand = jnp.stack([(lane_id[:, None] == head_of_lane[None, :] + d * SSM_HEADS) for d in range(2)]
                       ).astype(BF16)
    wrt = w_router.T
    rb = router_bias.reshape(N_EXPERTS, 1)
    fnw = final_norm_w.reshape(1, D_MODEL)

    kv_ctx, s_ctx = None, None
    for l in range(depth):
        wl = w_in[l]
        lw = {
            "norm1_w": norm1_w[l].reshape(1, D_MODEL),
            "norm2_w": norm2_w[l].reshape(1, D_MODEL),
            "wqkv": wl[:, :Z_OFF].astype(BF16),
            "wz": wl[:, Z_OFF:XBC_OFF].astype(BF16),
            "wxbc": wl[:, XBC_OFF:DT_OFF].astype(BF16),
            "wdt": jnp.pad(wl[:, DT_OFF:G_OFF], ((0, 0), (0, LANES - 2 * SSM_HEADS))).astype(BF16),
            "wg": wl[:, G_OFF:].astype(BF16),
            "q_norm_w": jnp.tile(q_norm_w[l], 4).reshape(1, 2 * LANES),
            "k_norm_w": jnp.tile(k_norm_w[l], 4).reshape(1, 2 * LANES),
            "conv_w": conv_w[l],
            "conv_b": conv_b[l].reshape(1, CONV_DIM),
            "dt_bias": jnp.pad(dt_bias[l].reshape(1, 2 * SSM_HEADS), ((0, 0), (0, LANES - 2 * SSM_HEADS))),
            "ssm_d": jnp.repeat(ssm_d[l], SSM_HEAD_DIM).reshape(1, D_INNER),
            "w_attn_out": w_attn_out[l].astype(BF16),
            "w_ssm_out": (ssm_norm_w[l][:, None] * w_ssm_out[l]).astype(BF16),
            "w_out": w_out[l].astype(BF16),
        }
        mod = mod_all[l]
        q, kn, vn, k_ctx, v_ctx, z, xs, bc, dt, g = _in_proj(x, mod, meta, lw, t_ctx, ctx_len, l, depth,
                                                             kv_ctx)
        kv_ctx = (k_ctx, v_ctx)

        attn = _attention(q, kn, vn, None, None, n_ctx, ctx_len, 0, min(TQ, ctx_len))
        attn = _attention(q, kn, vn, _cache_blocks(cache_k[:, l], cache_v[:, l]), attn,
                          n_lat, lat_len, t_ctx, min(TQ, lat_len))

        a_lanes = jnp.pad(-jnp.exp(a_log[l]).reshape(1, 2 * SSM_HEADS),
                          ((0, 0), (0, LANES - 2 * SSM_HEADS)))
        s0 = state_ssm[:, l].transpose(0, 1, 4, 2, 3).reshape(n_lat, 2, D_STATE, D_INNER)
        yf, yb, s_ctx = _ssd(xs, bc, dt, a_lanes, expand, lw["ssm_d"], s0, meta, n_ctx, l, depth, s_ctx)

        x1, h2, idx, gate = _merge(x, attn, yf, yb, z, g, mod, meta, lw, wrt, rb)
        y0, y1 = _moe(h2, idx, gate, (w_exp_gate, w_exp_up, w_exp_down), l)
        if l < depth - 1:
            x = _combine(x1, y0, y1, gate.T, mod, meta, fnw, None)
        else:
            y_prompt, y_sample = _combine(x1, y0, y1, gate.T, mod, meta, fnw, t_ctx)

    y_prompt = y_prompt.reshape(n_ctx, ctx_len, D_MODEL)
    y_sample = y_sample.reshape(n_lat, lat_len, D_MODEL)
    new_k = kv_ctx[0].reshape(n_ctx, depth, ctx_len, N_KV_HEADS, HEAD_DIM)
    new_v = kv_ctx[1].reshape(n_ctx, depth, ctx_len, N_KV_HEADS, HEAD_DIM)
    new_s = s_ctx.reshape(n_ctx, depth, 2, D_STATE, SSM_HEADS, SSM_HEAD_DIM).transpose(0, 1, 2, 4, 5, 3)
    return (y_prompt, y_sample, new_k, new_v, new_s)
```

```python
import functools

import jax
import jax.numpy as jnp
from jax import lax
from jax.experimental import pallas as pl
from jax.experimental.pallas import tpu as pltpu

F32 = jnp.float32
BF16 = jnp.bfloat16

EPS = 1e-6
D_MODEL = 1024
N_HEADS = 16
N_KV_HEADS = 4
HEAD_DIM = 64
ATTN_WIDTH = N_HEADS * HEAD_DIM
KV_WIDTH = N_KV_HEADS * HEAD_DIM
GRID_W = 64
ROPE_AXIS_DIM = HEAD_DIM // 2
ROPE_THETA = 10000.0
D_INNER = 2 * D_MODEL
SSM_HEAD_DIM = 64
SSM_HEADS = D_INNER // SSM_HEAD_DIM
SSM_GROUPS = 4
HEADS_PER_GROUP = SSM_HEADS // SSM_GROUPS
D_STATE = 128
GN = SSM_GROUPS * D_STATE
CONV_DIM = D_INNER + 2 * GN
SSD_CHUNK = 128
N_EXPERTS = 16
N_EXPERT_GROUPS = 4
EXPERTS_PER_GROUP = N_EXPERTS // N_EXPERT_GROUPS
D_FF_EXPERT = 512
N_MOD = 6
MOD_ROWS = 8

LANES = 128
SUBLANES = 8
VMEM_LIMIT = 56 * 1024 * 1024

TM = 512
TQ = 2048
TK = 256
ATTN_UNIT = 256
TE = 512
MERGE_CHAINS = 4
RANK_BLOCK = 512
CONV_CHUNK = 1024
COND_ROWS = 16
V_ROWS = HEAD_DIM + SUBLANES
LOG2E = 1.4426950408889634
Q_SCALE = HEAD_DIM ** -0.5 * LOG2E

Q_OFF, K_OFF, V_OFF = 0, ATTN_WIDTH, ATTN_WIDTH + KV_WIDTH
Z_OFF = ATTN_WIDTH + 2 * KV_WIDTH
XBC_OFF = Z_OFF + D_INNER
DT_OFF = XBC_OFF + CONV_DIM
G_OFF = DT_OFF + 2 * SSM_HEADS
N_IN = G_OFF + 2 * D_MODEL


def _cparams(sem):
    return pltpu.CompilerParams(dimension_semantics=sem, vmem_limit_bytes=VMEM_LIMIT)


def _const_spec(shape):
    nd = len(shape)
    return pl.BlockSpec(shape, lambda *_: (0,) * nd, pipeline_mode=pl.Buffered(1))


def _dot(a, b):
    return jnp.dot(a, b, preferred_element_type=F32)


def _dot_nt(a, b):
    return lax.dot_general(a, b, (((1,), (1,)), ((), ())), preferred_element_type=F32)


def _dot_tn(a, b):
    return lax.dot_general(a, b, (((0,), (0,)), ((), ())), preferred_element_type=F32)


def _split3(a):
    a1 = a.astype(BF16)
    r = a - a1.astype(F32)
    a2 = r.astype(BF16)
    a3 = (r - a2.astype(F32)).astype(BF16)
    return a1, a2, a3


def _dot_exact_lhs(m_bf16, a):
    a1, a2, a3 = _split3(a)
    return _dot(m_bf16, a1) + _dot(m_bf16, a2) + _dot(m_bf16, a3)


def _modnorm(x, w, sc, sh):
    ms = jnp.mean(x * x, axis=-1, keepdims=True)
    return x * lax.rsqrt(ms + EPS) * (w * (1.0 + sc)) + sh


def _sigmoid(x):
    return 1.0 / (1.0 + jnp.exp(-x))


def _silu(x):
    return x * _sigmoid(x)


def _adaln_kernel(c_ref, w_ref, b_ref, o_ref):
    cs = _silu(c_ref[...])
    o_ref[0] = jnp.dot(cs, w_ref[0], preferred_element_type=F32,
                       precision=lax.Precision.HIGHEST) + b_ref[0]


def _adaln(cond, w_mod, b_mod):
    depth = w_mod.shape[0]
    nb = N_MOD
    return pl.pallas_call(
        _adaln_kernel,
        grid=(depth, nb),
        in_specs=[pl.BlockSpec((COND_ROWS, D_MODEL), lambda l, j: (0, 0)),
                  pl.BlockSpec((1, D_MODEL, D_MODEL), lambda l, j: (l, 0, j)),
                  pl.BlockSpec((1, 1, D_MODEL), lambda l, j: (l, 0, j))],
        out_specs=pl.BlockSpec((1, COND_ROWS, D_MODEL), lambda l, j: (l, 0, j)),
        out_shape=jax.ShapeDtypeStruct((depth, COND_ROWS, N_MOD * D_MODEL), F32),
        compiler_params=_cparams(("arbitrary", "arbitrary")),
        name="adaln",
    )(cond, w_mod, b_mod.reshape(depth, 1, N_MOD * D_MODEL))


def _in_proj_kernel(mrow_ref, rope_ref, ctxi_ref,
                    x_ref, xp_ref, xn_ref, mod_ref, n1w_ref,
                    wqkv_ref, wz_ref, wxbc_ref, wdt_ref, wg_ref,
                    qnw_ref, knw_ref, cos_ref, sin_ref, kprev_ref, knext_ref,
                    convw_ref, convb_ref, dtb_ref,
                    *rest, n_ctx_blocks, ctx_len):
    q_ref, kb_ref, vt_ref, kc_ref, vc_ref, z_ref, xs_ref, bc_ref, dt_ref, g_ref = rest[-10:]
    del mrow_ref, rope_ref, ctxi_ref
    i = pl.program_id(0)
    is_ctx = i < n_ctx_blocks
    sh1 = mod_ref[0, 0:1, :]
    sc1 = mod_ref[0, 1:2, :]
    nw = n1w_ref[...]
    h = _modnorm(x_ref[...], nw, sc1, sh1).astype(BF16)
    hp = _modnorm(xp_ref[...], nw, sc1, sh1).astype(BF16)
    hn = _modnorm(xn_ref[...], nw, sc1, sh1).astype(BF16)

    r = lax.broadcasted_iota(jnp.int32, (2 * LANES, 2 * LANES), 0) // HEAD_DIM
    c = lax.broadcasted_iota(jnp.int32, (2 * LANES, 2 * LANES), 1) // HEAD_DIM
    same_head = jnp.where(r == c, 1.0, 0.0).astype(BF16)
    lane = lax.broadcasted_iota(jnp.int32, (TM, LANES), 1)
    first_half = (lane % (HEAD_DIM // 2)) < (HEAD_DIM // 4)

    def head_norm_rope(t, w, scale):
        sq = t * t
        hi = sq.astype(BF16)
        lo = (sq - hi.astype(F32)).astype(BF16)
        ssum = _dot(hi, same_head) + _dot(lo, same_head)
        tn = t * lax.rsqrt(ssum * (1.0 / HEAD_DIM) + EPS) * w
        outs = []
        for s in range(2):
            a = tn[:, s * LANES:(s + 1) * LANES]
            rot = jnp.where(first_half, -pltpu.roll(a, LANES - HEAD_DIM // 4, 1),
                            pltpu.roll(a, HEAD_DIM // 4, 1))
            outs.append((a * cos_ref[...] + rot * sin_ref[...]) * scale)
        return outs

    qw = qnw_ref[...]
    kw = knw_ref[...]
    ones_rows = jnp.where(lax.broadcasted_iota(jnp.int32, (V_ROWS - HEAD_DIM, TK), 0) == 0,
                          1.0, 0.0).astype(BF16)
    cw = CONV_CHUNK
    rows = lax.broadcasted_iota(jnp.int32, (TM, cw), 0)
    kprev = jnp.concatenate([kprev_ref[...]] * (cw // LANES), axis=1)
    knext = jnp.concatenate([knext_ref[...]] * (cw // LANES), axis=1)

    def q_stage(cb):
        def mm():
            return _dot(h, wqkv_ref[:, cb * 2 * LANES:(cb + 1) * 2 * LANES])

        def fin(t):
            a, b = head_norm_rope(t, qw, Q_SCALE)
            q_ref[:, cb * 2 * LANES:cb * 2 * LANES + LANES] = a.astype(BF16)
            q_ref[:, cb * 2 * LANES + LANES:(cb + 1) * 2 * LANES] = b.astype(BF16)
        return mm, fin

    def k_fin(t):
        kpairs = head_norm_rope(t, kw, 1.0)
        for rb in range(TM // TK):
            for pr in range(2):
                kb_ref[pr, rb] = kpairs[pr][rb * TK:(rb + 1) * TK, :].astype(BF16)

        @pl.when(is_ctx)
        def _():
            for sq in range(TM // ctx_len):
                for pr in range(2):
                    kc_ref[sq, :, pr * LANES:(pr + 1) * LANES] = kpairs[pr][sq * ctx_len:(sq + 1) * ctx_len, :]

    def v_fin(v):
        for rb in range(TM // TK):
            vT = v[rb * TK:(rb + 1) * TK, :].T
            for hd in range(N_KV_HEADS):
                vt_ref[hd // 2, hd % 2, rb, 0:HEAD_DIM, :] = (
                    vT[hd * HEAD_DIM:(hd + 1) * HEAD_DIM, :].astype(BF16))
                vt_ref[hd // 2, hd % 2, rb, HEAD_DIM:V_ROWS, :] = ones_rows

        @pl.when(is_ctx)
        def _():
            for sq in range(TM // ctx_len):
                vc_ref[sq] = v[sq * ctx_len:(sq + 1) * ctx_len, :]

    def z_fin(t):
        z_ref[...] = t.astype(BF16)

    def g_fin(t):
        g_ref[...] = _sigmoid(t).astype(BF16)

    def dt_fin(t):
        dtr = t + dtb_ref[...]
        dt_ref[...] = jnp.maximum(dtr, 0.0) + jnp.log(1.0 + jnp.exp(-jnp.abs(dtr)))

    def conv_stage(cb):
        sl = slice(cb * cw, (cb + 1) * cw)

        def mm():
            return (_dot(h, wxbc_ref[:, sl]), _dot(hp, wxbc_ref[:, sl]), _dot(hn, wxbc_ref[:, sl]))

        def fin(res):
            pre, pp, pn = res
            prev_row = pp[SUBLANES - 1:SUBLANES, :]
            next_row = pn[0:1, :]
            up = jnp.where(rows == 0, prev_row, pltpu.roll(pre, 1, 0)) * kprev
            down = jnp.where(rows == TM - 1, next_row, pltpu.roll(pre, TM - 1, 0)) * knext
            y = (convb_ref[:, sl] + up * convw_ref[0:1, sl] + pre * convw_ref[1:2, sl]
                 + down * convw_ref[2:3, sl])
            y = _silu(y).astype(BF16)
            lo = cb * cw
            if lo < D_INNER:
                xs_ref[:, lo:lo + cw] = y
            else:
                bc_ref[:, lo - D_INNER:lo - D_INNER + cw] = y
        return mm, fin

    stages = [q_stage(cb) for cb in range(ATTN_WIDTH // (2 * LANES))]
    stages.append((lambda: _dot(h, wqkv_ref[:, K_OFF:K_OFF + KV_WIDTH]), k_fin))
    stages.append((lambda: _dot(h, wqkv_ref[:, V_OFF:V_OFF + KV_WIDTH]), v_fin))
    stages.append((lambda: _dot(h, wdt_ref[...]), dt_fin))
    stages += [conv_stage(cb) for cb in range(CONV_DIM // cw)]
    stages.append((lambda: _dot(h, wz_ref[...]), z_fin))
    stages.append((lambda: _dot(h, wg_ref[...]), g_fin))
    pending = None
    for mm, fin in stages:
        res = mm()
        if pending is not None:
            pending[0](pending[1])
        pending = (fin, res)
    pending[0](pending[1])


def _in_proj(x, mod, meta, lw, t_ctx, ctx_len, layer, depth, prev_kv):
    t = x.shape[0]
    nblk = t // TM
    nhalo = t // SUBLANES
    per_halo = TM // SUBLANES
    n_ctx_blocks = t_ctx // TM
    nkb = TM // TK

    def row_blk(width):
        return pl.BlockSpec((TM, width), lambda i, *_: (i, 0))

    spb = TM // ctx_len
    ctx_blk = pl.BlockSpec((spb, None, ctx_len, KV_WIDTH),
                           lambda i, mrow, rope, ctxi: (ctxi[i], layer, 0, 0))
    in_specs = [
        row_blk(D_MODEL),
            pl.BlockSpec((SUBLANES, D_MODEL),
                         lambda i, *_: (jnp.maximum(i * per_halo - 1, 0), 0)),
            pl.BlockSpec((SUBLANES, D_MODEL),
                         lambda i, *_: (jnp.minimum((i + 1) * per_halo, nhalo - 1), 0)),
            pl.BlockSpec((1, MOD_ROWS, D_MODEL), lambda i, mrow, *_: (mrow[i], 0, 0)),
            _const_spec((1, D_MODEL)),
            _const_spec((D_MODEL, ATTN_WIDTH + 2 * KV_WIDTH)),
            _const_spec((D_MODEL, D_INNER)),
            _const_spec((D_MODEL, CONV_DIM)),
            _const_spec((D_MODEL, LANES)),
            _const_spec((D_MODEL, 2 * D_MODEL)),
            _const_spec((1, 2 * LANES)),
            _const_spec((1, 2 * LANES)),
            pl.BlockSpec((TM, LANES), lambda i, mrow, rope, *_: (rope[i], 0)),
            pl.BlockSpec((TM, LANES), lambda i, mrow, rope, *_: (rope[i], 0)),
            row_blk(LANES),
            row_blk(LANES),
            _const_spec((3, CONV_DIM)),
            _const_spec((1, CONV_DIM)),
            _const_spec((1, LANES)),
    ]
    out_specs = [row_blk(ATTN_WIDTH),
                   pl.BlockSpec((2, nkb, TK, LANES), lambda i, *_: (0, i, 0, 0)),
                   pl.BlockSpec((2, 2, nkb, V_ROWS, TK), lambda i, *_: (0, 0, i, 0, 0)),
                   ctx_blk, ctx_blk,
                   row_blk(D_INNER), row_blk(D_INNER), row_blk(2 * GN), row_blk(LANES),
                   row_blk(2 * D_MODEL)]
    out_shape = [
        jax.ShapeDtypeStruct((t, ATTN_WIDTH), BF16),
        jax.ShapeDtypeStruct((2, t // TK, TK, LANES), BF16),
        jax.ShapeDtypeStruct((2, 2, t // TK, V_ROWS, TK), BF16),
        jax.ShapeDtypeStruct((t_ctx // ctx_len, depth, ctx_len, KV_WIDTH), F32),
        jax.ShapeDtypeStruct((t_ctx // ctx_len, depth, ctx_len, KV_WIDTH), F32),
        jax.ShapeDtypeStruct((t, D_INNER), BF16),
        jax.ShapeDtypeStruct((t, D_INNER), BF16),
        jax.ShapeDtypeStruct((t, 2 * GN), BF16),
        jax.ShapeDtypeStruct((t, LANES), F32),
        jax.ShapeDtypeStruct((t, 2 * D_MODEL), BF16),
    ]
    args = [meta["mrow"], meta["rope_blk"], meta["ctx_blk"],
            x, x, x, mod, lw["norm1_w"], lw["wqkv"], lw["wz"], lw["wxbc"], lw["wdt"], lw["wg"],
            lw["q_norm_w"], lw["k_norm_w"], meta["cos"], meta["sin"], meta["keep_prev"],
            meta["keep_next"], lw["conv_w"], lw["conv_b"], lw["dt_bias"]]
    aliases = {}
    if prev_kv is not None:
        in_specs += [pl.BlockSpec(memory_space=pl.ANY)] * 2
        aliases = {len(args): 3, len(args) + 1: 4}
        args += list(prev_kv)
    grid_spec = pltpu.PrefetchScalarGridSpec(num_scalar_prefetch=3, grid=(nblk,), in_specs=in_specs,
                                             out_specs=out_specs)
    return pl.pallas_call(
        functools.partial(_in_proj_kernel, n_ctx_blocks=n_ctx_blocks, ctx_len=ctx_len),
        grid_spec=grid_spec, out_shape=out_shape, input_output_aliases=aliases,
        compiler_params=_cparams(("arbitrary",)), name="in_proj",
    )(*args)


def _attn_kernel(*refs, n_new, n_cache, tq):
    q_ref, kn_ref, vn_ref = refs[0:3]
    if n_cache:
        kc_ref, vc_ref = refs[3:5]
    o_ref, qt_ref, m_ref, acc_ref, sa_ref, mxa_ref, sb_ref, mxb_ref = refs[-8:]
    nq = 4 * tq
    zeros_half = jnp.zeros((HEAD_DIM, tq), F32)
    for hh in range(2):
        cols = []
        for jj in range(2):
            off = 2 * LANES * hh + LANES * jj
            qt = q_ref[:, off:off + LANES].astype(F32).T
            for s in range(2):
                head = qt[s * HEAD_DIM:(s + 1) * HEAD_DIM, :]
                parts = [head, zeros_half] if hh == 0 else [zeros_half, head]
                cols.append(jnp.concatenate(parts, axis=0))
        qt_ref[hh] = jnp.concatenate(cols, axis=1).astype(BF16)
        m_ref[hh] = jnp.full((1, nq), -jnp.inf, F32)
        acc_ref[hh] = jnp.zeros((V_ROWS, nq), F32)

    units = [(hh, slice(c * ATTN_UNIT, (c + 1) * ATTN_UNIT))
             for hh in range(2) for c in range(nq // ATTN_UNIT)]

    def scores(kblk, s_ref, mx_ref, hh, cs):
        s = _dot(kblk, qt_ref[hh, :, cs])
        s_ref[hh, :, cs] = s
        mx_ref[hh, :, cs] = jnp.max(s, axis=0, keepdims=True)

    def consume(vt, s_ref, mx_ref, hh, cs):
        m = m_ref[hh, :, cs]
        m_new = jnp.maximum(m, mx_ref[hh, :, cs])
        alpha = jnp.exp2(m - m_new)
        p = jnp.exp2(s_ref[hh, :, cs] - m_new).astype(BF16)
        acc_ref[hh, :, cs] = alpha * acc_ref[hh, :, cs] + _dot(vt, p)
        m_ref[hh, :, cs] = m_new

    def step(k_nxt, nxt, v_cur, cur):
        for hh, cs in units:
            if k_nxt is not None:
                scores(k_nxt, *nxt, hh, cs)
            consume(v_cur[hh], *cur, hh, cs)

    bufs = ((sa_ref, mxa_ref), (sb_ref, mxb_ref))
    for hh, cs in units:
        scores(kn_ref[0], *bufs[0], hh, cs)

    def body(j, carry):
        step(kn_ref[2 * j + 1], bufs[1], (vn_ref[0, 2 * j], vn_ref[1, 2 * j]), bufs[0])
        step(kn_ref[2 * j + 2], bufs[0], (vn_ref[0, 2 * j + 1], vn_ref[1, 2 * j + 1]), bufs[1])
        return carry

    n_loop = (n_new - 1) // 2
    lax.fori_loop(0, n_loop, body, 0)
    rest = [(kn_ref, vn_ref, b) for b in range(2 * n_loop, n_new)]
    if n_cache:
        rest += [(kc_ref, vc_ref, b) for b in range(n_cache)]
    for r, (_, v_src, b) in enumerate(rest):
        k_nxt = None
        if r + 1 < len(rest):
            k_src, _, b_nxt = rest[r + 1]
            k_nxt = k_src[b_nxt]
        step(k_nxt, bufs[(r + 1) % 2], (v_src[0, b], v_src[1, b]), bufs[r % 2])

    for hh in range(2):
        oT = acc_ref[hh]
        oT = oT[0:HEAD_DIM, :] / oT[HEAD_DIM:HEAD_DIM + 1, :]
        for jj in range(2):
            pair = jnp.concatenate([oT[:, (2 * jj) * tq:(2 * jj + 1) * tq],
                                    oT[:, (2 * jj + 1) * tq:(2 * jj + 2) * tq]], axis=0)
            off = 2 * LANES * hh + LANES * jj
            o_ref[:, off:off + LANES] = pair.T.astype(BF16)


def _attention(q, kn, vn, cache, prev_out, nseq, seq_len, row0, tq):
    n_new = seq_len // TK
    nqb = seq_len // tq
    nq = 4 * tq
    q0 = row0 // tq
    kv0 = row0 // (n_new * TK)
    assert row0 % tq == 0 and row0 % (n_new * TK) == 0
    qspec = pl.BlockSpec((tq, 4 * LANES), lambda b, p, i: (q0 + b * nqb + i, p))
    in_specs = [
        qspec,
        pl.BlockSpec((None, n_new, TK, LANES), lambda b, p, i: (p, kv0 + b, 0, 0)),
        pl.BlockSpec((None, 2, n_new, V_ROWS, TK), lambda b, p, i: (p, 0, kv0 + b, 0, 0)),
    ]
    args = [q, kn, vn]
    n_cache = 0
    if cache is not None:
        kc, vc = cache
        n_cache = kc.shape[1] // nseq
        in_specs += [pl.BlockSpec((None, n_cache, TK, LANES), lambda b, p, i: (p, b, 0, 0)),
                     pl.BlockSpec((None, 2, n_cache, V_ROWS, TK), lambda b, p, i: (p, 0, b, 0, 0))]
        args += [kc, vc]
    aliases = {}
    kern = functools.partial(_attn_kernel, n_new=n_new, n_cache=n_cache, tq=tq)
    if prev_out is not None:
        n_in = len(args)
        in_specs.append(pl.BlockSpec(memory_space=pl.ANY))
        aliases = {n_in: 0}
        args.append(prev_out)
        inner = kern
        kern = lambda *refs: inner(*refs[:n_in], *refs[n_in + 1:])
    return pl.pallas_call(
        kern,
        grid=(nseq, 2, nqb),
        in_specs=in_specs,
        out_specs=qspec,
        out_shape=jax.ShapeDtypeStruct(q.shape, BF16),
        scratch_shapes=[pltpu.VMEM((2, LANES, nq), BF16),
                        pltpu.VMEM((2, 1, nq), F32),
                        pltpu.VMEM((2, V_ROWS, nq), F32),
                        pltpu.VMEM((2, TK, nq), F32),
                        pltpu.VMEM((2, 1, nq), F32),
                        pltpu.VMEM((2, TK, nq), F32),
                        pltpu.VMEM((2, 1, nq), F32)],
        input_output_aliases=aliases,
        compiler_params=_cparams(("arbitrary", "arbitrary", "arbitrary")),
        name="attention",
    )(*args)


def _cache_blocks(ck, cv):
    nseq, past = ck.shape[0], ck.shape[1]
    n = past // TK
    kb = ck.astype(BF16).reshape(nseq * n, TK, 2, LANES).transpose(2, 0, 1, 3)
    vb = cv.astype(BF16).reshape(nseq * n, TK, 2, 2, HEAD_DIM).transpose(2, 3, 0, 4, 1)
    ones = jnp.ones(vb.shape[:3] + (1, TK), BF16)
    zeros = jnp.zeros(vb.shape[:3] + (V_ROWS - HEAD_DIM - 1, TK), BF16)
    return kb, jnp.concatenate([vb, ones, zeros], axis=3)


def _ssd_direction(x_ref, bc_ref, dt_ref, a_lanes, expand, dvec_ref, st_ref, y_ref, d):
    q = SSD_CHUNK
    row = lax.broadcasted_iota(jnp.int32, (q, q), 0)
    col = lax.broadcasted_iota(jnp.int32, (q, q), 1)
    causal = (row >= col) if d == 0 else (row <= col)
    tril = jnp.where(row >= col, 1.0, 0.0).astype(BF16)

    dt = dt_ref[...]
    a = dt * a_lanes
    prefix = _dot_exact_lhs(tril, a)
    if d == 0:
        cs = prefix
        last = cs[q - 1:q, :]
    else:
        cs = prefix[q - 1:q, :] - prefix + a
        last = cs[0:1, :]
    cs2 = cs * LOG2E
    cst2 = (cs2 - jnp.log2(dt)).T
    wj = jnp.exp(last - cs) * dt
    wj_x = _dot(wj.astype(BF16), expand)
    elast = jnp.exp(last)

    lane = lax.broadcasted_iota(jnp.int32, (q, LANES), 1)
    lo_half = lane < SSM_HEAD_DIM
    lo_half_row = lo_half[0:1, :]
    for g in range(SSM_GROUPS):
        bg = bc_ref[:, g * D_STATE:(g + 1) * D_STATE]
        cg = bc_ref[:, GN + g * D_STATE:GN + (g + 1) * D_STATE]
        gmat = _dot_nt(cg, bg)
        gsl = slice(g * HEADS_PER_GROUP * SSM_HEAD_DIM, (g + 1) * HEADS_PER_GROUP * SSM_HEAD_DIM)
        st = st_ref[d, :, gsl]
        y_inter = _dot(cg, st.astype(BF16))
        xg = x_ref[:, gsl]
        el_parts = []
        for hp in range(HEADS_PER_GROUP // 2):
            xpair = xg[:, hp * LANES:(hp + 1) * LANES]
            l0 = d * SSM_HEADS + g * HEADS_PER_GROUP + 2 * hp
            ws, es = [], []
            for s in range(2):
                hl = l0 + s
                csb = jnp.broadcast_to(cs2[:, hl:hl + 1], (q, q))
                seg = csb - cst2[hl:hl + 1, :]
                ws.append(gmat * jnp.exp2(jnp.where(causal, seg, -jnp.inf)))
                es.append(jnp.exp2(csb))
            wcat = jnp.concatenate(ws, axis=1).astype(BF16)
            zero = jnp.zeros_like(xpair)
            xm = jnp.concatenate([jnp.where(lo_half, xpair, zero), jnp.where(lo_half, zero, xpair)],
                                 axis=0)
            e_pair = jnp.where(lo_half, es[0], es[1])
            ypair = y_inter[:, hp * LANES:(hp + 1) * LANES] * e_pair + _dot(wcat, xm)
            lo = g * HEADS_PER_GROUP * SSM_HEAD_DIM + hp * LANES
            if d == 0:
                ypair = ypair + dvec_ref[:, lo:lo + LANES] * xpair.astype(F32)
            y_ref[:, lo:lo + LANES] = ypair.astype(y_ref.dtype)
            el_parts.append(jnp.where(lo_half_row, elast[:, l0:l0 + 1], elast[:, l0 + 1:l0 + 2]))
        xw = (xg.astype(F32) * wj_x[:, gsl]).astype(BF16)
        el = jnp.concatenate(el_parts, axis=1)
        st_ref[d, :, gsl] = st * el + _dot_tn(bg, xw)


def _ssd_kernel(fblk_ref, bblk_ref, first_ref, last_ref, init_ref, s0i_ref, sfi_ref,
                xf_ref, bcf_ref, dtf_ref, xb_ref, bcb_ref, dtb_ref, a_ref, exp_ref, dvec_ref, s0_ref,
                *rest):
    yf_ref, yb_ref, sfin_ref, st_ref = rest[-4:]
    del fblk_ref, bblk_ref, s0i_ref, sfi_ref
    s = pl.program_id(0)

    @pl.when(jnp.logical_and(first_ref[s] == 1, init_ref[s] == 1))
    def _():
        st_ref[...] = s0_ref[0]

    @pl.when(jnp.logical_and(first_ref[s] == 1, init_ref[s] == 0))
    def _():
        st_ref[...] = jnp.zeros(st_ref.shape, F32)

    a_lanes = a_ref[...]
    _ssd_direction(xf_ref, bcf_ref, dtf_ref, a_lanes, exp_ref[0], dvec_ref, st_ref, yf_ref, 0)
    _ssd_direction(xb_ref, bcb_ref, dtb_ref, a_lanes, exp_ref[1], dvec_ref, st_ref, yb_ref, 1)

    @pl.when(jnp.logical_and(last_ref[s] == 1, init_ref[s] == 0))
    def _():
        sfin_ref[0] = st_ref[...]


def _ssd(xs, bc, dt, a_lanes, expand, dvec, s0, meta, n_ctx, layer, depth, prev_sfin):
    t = xs.shape[0]
    n_steps = t // SSD_CHUNK
    st_shape = (2, D_STATE, D_INNER)
    fwd = lambda s, fblk, bblk, *_: (fblk[s], 0)
    bwd = lambda s, fblk, bblk, *_: (bblk[s], 0)

    def specs(idx):
        return [pl.BlockSpec((SSD_CHUNK, D_INNER), idx),
                pl.BlockSpec((SSD_CHUNK, 2 * GN), idx),
                pl.BlockSpec((SSD_CHUNK, LANES), idx)]

    in_specs = specs(fwd) + specs(bwd) + [
        pl.BlockSpec((1, LANES), lambda s, *_: (0, 0)),
        pl.BlockSpec((2, LANES, D_INNER), lambda s, *_: (0, 0, 0)),
        pl.BlockSpec((1, D_INNER), lambda s, *_: (0, 0)),
        pl.BlockSpec((1,) + st_shape, lambda s, f, b, fi, la, ini, s0i, sfi: (s0i[s], 0, 0, 0)),
    ]
    args = [meta["ssd_fblk"], meta["ssd_bblk"], meta["ssd_first"], meta["ssd_last"], meta["ssd_init"],
            meta["ssd_s0i"], meta["ssd_sfi"], xs, bc, dt, xs, bc, dt, a_lanes, expand, dvec, s0]
    aliases = {}
    if prev_sfin is not None:
        in_specs.append(pl.BlockSpec(memory_space=pl.ANY))
        aliases = {len(args): 2}
        args.append(prev_sfin)
    grid_spec = pltpu.PrefetchScalarGridSpec(
        num_scalar_prefetch=7,
        grid=(n_steps,),
        in_specs=in_specs,
        out_specs=[pl.BlockSpec((SSD_CHUNK, D_INNER), fwd), pl.BlockSpec((SSD_CHUNK, D_INNER), bwd),
                   pl.BlockSpec((1, None) + st_shape,
                                lambda s, f, b, fi, la, ini, s0i, sfi: (sfi[s], layer, 0, 0, 0))],
        scratch_shapes=[pltpu.VMEM(st_shape, F32)],
    )
    out_shape = [jax.ShapeDtypeStruct((t, D_INNER), BF16), jax.ShapeDtypeStruct((t, D_INNER), BF16),
                 jax.ShapeDtypeStruct((n_ctx, depth) + st_shape, F32)]
    return pl.pallas_call(
        _ssd_kernel, grid_spec=grid_spec, out_shape=out_shape, input_output_aliases=aliases,
        compiler_params=_cparams(("arbitrary",)), name="ssd",
    )(*args)


def _route(sel, scores):
    neg = -jnp.inf
    rows = [sel[e:e + 1, :] for e in range(N_EXPERTS)]
    srow = [scores[e:e + 1, :] for e in range(N_EXPERTS)]
    best_score = None
    best = None
    for g in range(N_EXPERT_GROUPS):
        v = rows[g * EXPERTS_PER_GROUP:(g + 1) * EXPERTS_PER_GROUP]
        top2 = None
        for a in range(EXPERTS_PER_GROUP):
            for b in range(a + 1, EXPERTS_PER_GROUP):
                s = v[a] + v[b]
                top2 = s if top2 is None else jnp.maximum(top2, s)
        if g == 0:
            best_score, best = top2, jnp.zeros(top2.shape, jnp.int32)
        else:
            better = top2 > best_score
            best = jnp.where(better, g, best)
            best_score = jnp.where(better, top2, best_score)

    def pick(vals):
        out = vals[0]
        for g in range(1, N_EXPERT_GROUPS):
            out = jnp.where(best == g, vals[g], out)
        return out

    gsel = [pick([rows[g * EXPERTS_PER_GROUP + j] for g in range(N_EXPERT_GROUPS)])
            for j in range(EXPERTS_PER_GROUP)]
    gsc = [pick([srow[g * EXPERTS_PER_GROUP + j] for g in range(N_EXPERT_GROUPS)])
           for j in range(EXPERTS_PER_GROUP)]

    def argmax_first(vals):
        bi = jnp.zeros(vals[0].shape, jnp.int32)
        bv = vals[0]
        for j in range(1, len(vals)):
            better = vals[j] > bv
            bi = jnp.where(better, j, bi)
            bv = jnp.where(better, vals[j], bv)
        return bi

    i1 = argmax_first(gsel)
    i2 = argmax_first([jnp.where(i1 == j, neg, gsel[j]) for j in range(EXPERTS_PER_GROUP)])

    def take(vals, idx):
        out = vals[0]
        for j in range(1, len(vals)):
            out = jnp.where(idx == j, vals[j], out)
        return out

    g1 = take(gsc, i1)
    g2 = take(gsc, i2)
    tot = g1 + g2
    idx = jnp.concatenate([best * EXPERTS_PER_GROUP + i1, best * EXPERTS_PER_GROUP + i2], axis=0)
    gate = jnp.concatenate([g1 / tot, g2 / tot], axis=0)
    return idx, gate


def _merge_kernel(mrow_ref, x_ref, attn_ref, yf_ref, yb_ref, z_ref, g_ref, mod_ref,
                  wa_ref, ws_ref, wo_ref, n2w_ref, wrt_ref, rb_ref,
                  x1_ref, h2_ref, idx_ref, gate_ref):
    del mrow_ref
    gw = D_INNER // SSM_GROUPS
    hr = TM // MERGE_CHAINS

    def chain(ci):
        rs = slice(ci * hr, (ci + 1) * hr)
        attn_o = _dot(attn_ref[rs, :], wa_ref[...])
        yield
        ssm_o = None
        for g in range(SSM_GROUPS):
            sl = slice(g * gw, (g + 1) * gw)
            y = (yf_ref[rs, sl] + yb_ref[rs, sl]).astype(F32) * _silu(z_ref[rs, sl].astype(F32))
            ms = jnp.mean(y * y, axis=-1, keepdims=True)
            yn = (y * lax.rsqrt(ms + EPS)).astype(BF16)
            part = _dot(yn, ws_ref[sl, :])
            ssm_o = part if ssm_o is None else ssm_o + part
            yield
        merged = (g_ref[rs, 0:D_MODEL].astype(F32) * attn_o
                  + g_ref[rs, D_MODEL:2 * D_MODEL].astype(F32) * ssm_o)
        out = _dot(merged.astype(BF16), wo_ref[...])
        yield
        x1 = x_ref[rs, :] + mod_ref[0, 2:3, :] * out
        x1_ref[rs, :] = x1
        h2 = _modnorm(x1, n2w_ref[...], mod_ref[0, 4:5, :], mod_ref[0, 3:4, :])
        h2_ref[rs, :] = h2.astype(BF16)
        logits = lax.dot_general(wrt_ref[...], h2, (((1,), (1,)), ((), ())),
                                 preferred_element_type=F32, precision=lax.Precision.HIGHEST)
        yield
        scores = _sigmoid(logits)
        idx, gate = _route(scores + rb_ref[...], scores)
        idx_ref[:, rs] = idx
        gate_ref[:, rs] = gate

    live = [chain(ci) for ci in range(MERGE_CHAINS)]
    while live:
        for c in list(live):
            try:
                next(c)
            except StopIteration:
                live.remove(c)


def _merge(x, attn, yf, yb, z, g, mod, meta, lw, wrt, rb):
    t = x.shape[0]
    nblk = t // TM

    def row_blk(width):
        return pl.BlockSpec((TM, width), lambda i, *_: (i, 0))

    grid_spec = pltpu.PrefetchScalarGridSpec(
        num_scalar_prefetch=1,
        grid=(nblk,),
        in_specs=[row_blk(D_MODEL), row_blk(ATTN_WIDTH), row_blk(D_INNER), row_blk(D_INNER),
                  row_blk(D_INNER), row_blk(2 * D_MODEL),
                  pl.BlockSpec((1, MOD_ROWS, D_MODEL), lambda i, mrow: (mrow[i], 0, 0)),
                  _const_spec((ATTN_WIDTH, D_MODEL)), _const_spec((D_INNER, D_MODEL)),
                  _const_spec((D_MODEL, D_MODEL)), _const_spec((1, D_MODEL)),
                  _const_spec((N_EXPERTS, D_MODEL)), _const_spec((N_EXPERTS, 1))],
        out_specs=[row_blk(D_MODEL), row_blk(D_MODEL),
                   pl.BlockSpec((2, TM), lambda i, *_: (0, i)),
                   pl.BlockSpec((2, TM), lambda i, *_: (0, i))],
    )
    out_shape = [jax.ShapeDtypeStruct((t, D_MODEL), F32), jax.ShapeDtypeStruct((t, D_MODEL), BF16),
                 jax.ShapeDtypeStruct((2, t), jnp.int32), jax.ShapeDtypeStruct((2, t), F32)]
    return pl.pallas_call(
        _merge_kernel, grid_spec=grid_spec, out_shape=out_shape,
        compiler_params=_cparams(("arbitrary",)), name="merge",
    )(meta["mrow"], x, attn, yf, yb, z, g, mod, lw["w_attn_out"], lw["w_ssm_out"], lw["w_out"],
      lw["norm2_w"], wrt, rb)


def _expert_kernel(te_ref, nt_ref, chg_ref, x_ref, wg_ref, wu_ref, wd_ref, o_ref,
                   wgs_ref, wus_ref, wds_ref):
    del te_ref
    i = pl.program_id(0)

    @pl.when(chg_ref[i] == 1)
    def _():
        wgs_ref[...] = wg_ref[0].astype(BF16)
        wus_ref[...] = wu_ref[0].astype(BF16)
        wds_ref[...] = wd_ref[0].astype(BF16)

    @pl.when(i < nt_ref[0])
    def _():
        x = x_ref[...]
        hmid = _silu(_dot(x, wgs_ref[...])) * _dot(x, wus_ref[...])
        o_ref[...] = _dot(hmid.astype(BF16), wds_ref[...]).astype(o_ref.dtype)

    @pl.when(i >= nt_ref[0])
    def _():
        o_ref[...] = jnp.zeros(o_ref.shape, o_ref.dtype)


def _experts(x_sorted, tile_expert, n_tiles, wg, wu, wd, layer):
    rows = x_sorted.shape[0]
    changed = jnp.concatenate([jnp.ones((1,), jnp.int32),
                               (tile_expert[1:] != tile_expert[:-1]).astype(jnp.int32)])
    grid_spec = pltpu.PrefetchScalarGridSpec(
        num_scalar_prefetch=3,
        grid=(rows // TE,),
        in_specs=[pl.BlockSpec((TE, D_MODEL), lambda i, te, nt, chg: (i, 0)),
                  pl.BlockSpec((None, 1, D_MODEL, D_FF_EXPERT), lambda i, te, nt, chg: (layer, te[i], 0, 0)),
                  pl.BlockSpec((None, 1, D_MODEL, D_FF_EXPERT), lambda i, te, nt, chg: (layer, te[i], 0, 0)),
                  pl.BlockSpec((None, 1, D_FF_EXPERT, D_MODEL), lambda i, te, nt, chg: (layer, te[i], 0, 0))],
        out_specs=pl.BlockSpec((TE, D_MODEL), lambda i, te, nt, chg: (i, 0)),
        scratch_shapes=[pltpu.VMEM((D_MODEL, D_FF_EXPERT), BF16),
                        pltpu.VMEM((D_MODEL, D_FF_EXPERT), BF16),
                        pltpu.VMEM((D_FF_EXPERT, D_MODEL), BF16)],
    )
    return pl.pallas_call(
        _expert_kernel, grid_spec=grid_spec,
        out_shape=jax.ShapeDtypeStruct((rows, D_MODEL), BF16),
        compiler_params=_cparams(("arbitrary",)), name="experts",
    )(tile_expert, n_tiles, changed, x_sorted, wg, wu, wd)


def _combine_kernel(mrow_ref, x1_ref, y0_ref, y1_ref, gate_ref, mod_ref, fnw_ref, *out_refs,
                    n_ctx_blocks):
    del mrow_ref
    moe = (gate_ref[:, 0:1] * y0_ref[...].astype(F32) + gate_ref[:, 1:2] * y1_ref[...].astype(F32))
    x2 = x1_ref[...] + mod_ref[0, 5:6, :] * moe
    if n_ctx_blocks is None:
        out_refs[0][...] = x2
        return
    ms = jnp.mean(x2 * x2, axis=-1, keepdims=True)
    y = x2 * lax.rsqrt(ms + EPS) * fnw_ref[...]
    is_ctx = pl.program_id(0) < n_ctx_blocks

    @pl.when(is_ctx)
    def _():
        out_refs[0][...] = y

    @pl.when(jnp.logical_not(is_ctx))
    def _():
        out_refs[1][...] = y


def _combine(x1, y0, y1, gate_t, mod, meta, fnw, t_ctx):
    t = x1.shape[0]

    def row_blk(width):
        return pl.BlockSpec((TM, width), lambda i, *_: (i, 0))

    if t_ctx is None:
        n_ctx_blocks = None
        out_specs = row_blk(D_MODEL)
        out_shape = jax.ShapeDtypeStruct((t, D_MODEL), F32)
    else:
        n_ctx_blocks = t_ctx // TM
        out_specs = [pl.BlockSpec((TM, D_MODEL), lambda i, *_: (jnp.minimum(i, n_ctx_blocks - 1), 0)),
                     pl.BlockSpec((TM, D_MODEL), lambda i, *_: (jnp.maximum(i - n_ctx_blocks, 0), 0))]
        out_shape = [jax.ShapeDtypeStruct((t_ctx, D_MODEL), F32),
                     jax.ShapeDtypeStruct((t - t_ctx, D_MODEL), F32)]
    grid_spec = pltpu.PrefetchScalarGridSpec(
        num_scalar_prefetch=1,
        grid=(t // TM,),
        in_specs=[row_blk(D_MODEL), row_blk(D_MODEL), row_blk(D_MODEL), row_blk(2),
                  pl.BlockSpec((1, MOD_ROWS, D_MODEL), lambda i, mrow: (mrow[i], 0, 0)),
                  _const_spec((1, D_MODEL))],
        out_specs=out_specs,
    )
    return pl.pallas_call(
        functools.partial(_combine_kernel, n_ctx_blocks=n_ctx_blocks), grid_spec=grid_spec,
        out_shape=out_shape,
        compiler_params=_cparams(("arbitrary",)), name="combine",
    )(meta["mrow"], x1, y0, y1, gate_t, mod, fnw)


def _moe(h2, idx, gate, w_exp, layer):
    t = h2.shape[0]
    n_assign = 2 * t
    rows = n_assign + N_EXPERTS * TE
    e_flat = idx.reshape(n_assign)
    onehot = e_flat[:, None] == jnp.arange(N_EXPERTS, dtype=jnp.int32)[None, :]
    nb = n_assign // RANK_BLOCK
    oh = onehot.astype(BF16).reshape(nb, RANK_BLOCK, N_EXPERTS)
    tril = jnp.tril(jnp.ones((RANK_BLOCK, RANK_BLOCK), BF16))
    local = jnp.einsum("ij,bjk->bik", tril, oh, preferred_element_type=F32)
    bsum = local[:, -1, :]
    before = jnp.dot(jnp.tril(jnp.ones((nb, nb), F32), -1), bsum, precision=lax.Precision.HIGHEST)
    running = (local + before[:, None, :]).reshape(n_assign, N_EXPERTS)
    counts = (before[-1] + bsum[-1]).astype(jnp.int32)
    padded = ((counts + TE - 1) // TE) * TE
    pad_end = jnp.cumsum(padded)
    pad_off = pad_end - padded
    off = jnp.cumsum(counts) - counts
    pos = jnp.sum(jnp.where(onehot, running + (pad_off - 1).astype(F32)[None, :], 0.0), axis=1
                  ).astype(jnp.int32)
    order = jnp.argsort(e_flat, stable=True).astype(jnp.int32)
    p = jnp.arange(rows, dtype=jnp.int32)
    ep = jnp.minimum(jnp.sum(p[:, None] >= pad_end[None, :], axis=1), N_EXPERTS - 1).astype(jnp.int32)
    r = p - pad_off[ep]
    valid = r < counts[ep]
    src = jnp.where(valid, order[jnp.clip(off[ep] + r, 0, n_assign - 1)] % t, p % t)
    n_tiles = (pad_end[-1] // TE).astype(jnp.int32).reshape(1)
    tile_start = jnp.arange(rows // TE, dtype=jnp.int32) * TE
    tile_expert = jnp.minimum(jnp.sum(tile_start[:, None] >= pad_end[None, :], axis=1),
                              N_EXPERTS - 1).astype(jnp.int32)
    last_used = tile_expert[jnp.maximum(n_tiles[0] - 1, 0)]
    tile_expert = jnp.where(tile_start < pad_end[-1], tile_expert, last_used)
    x_sorted = h2.at[src].get(mode="promise_in_bounds")
    y_sorted = _experts(x_sorted, tile_expert, n_tiles, *w_exp, layer)
    pos2 = pos.reshape(2, t)
    return (y_sorted.at[pos2[0]].get(mode="promise_in_bounds"),
            y_sorted.at[pos2[1]].get(mode="promise_in_bounds"))


def _rope_tables(n_tokens):
    rows = n_tokens // GRID_W
    row = jnp.repeat(jnp.arange(rows, dtype=F32), GRID_W)
    col = jnp.tile(jnp.arange(GRID_W, dtype=F32), rows)
    inv = 1.0 / (ROPE_THETA ** (jnp.arange(0, ROPE_AXIS_DIM, 2, dtype=F32) / ROPE_AXIS_DIM))
    ar = row[:, None] * inv
    ac = col[:, None] * inv
    ang = jnp.concatenate([ar, ar, ac, ac], axis=-1)
    cos, sin = jnp.cos(ang), jnp.sin(ang)
    cos = jnp.concatenate([jnp.ones((TM, HEAD_DIM), F32), cos], axis=0)
    sin = jnp.concatenate([jnp.zeros((TM, HEAD_DIM), F32), sin], axis=0)
    return jnp.tile(cos, (1, 2)), jnp.tile(sin, (1, 2))


def _block_meta(n_ctx, ctx_len, n_lat, lat_len):
    ctx_blocks = n_ctx * ctx_len // TM
    lat_blocks = n_lat * lat_len // TM
    per_lat = lat_len // TM
    bi = jnp.arange(ctx_blocks + lat_blocks, dtype=jnp.int32)
    is_ctx = bi < ctx_blocks
    lat_i = bi - ctx_blocks
    mrow = jnp.where(is_ctx, 0, 1 + lat_i // per_lat).astype(jnp.int32)
    rope_blk = jnp.where(is_ctx, 0, 1 + lat_i % per_lat).astype(jnp.int32)
    cos, sin = _rope_tables(lat_len)
    ti = jnp.arange(n_ctx * ctx_len + n_lat * lat_len, dtype=jnp.int32)
    pos = jnp.where(ti < n_ctx * ctx_len, ti % ctx_len, (ti - n_ctx * ctx_len) % lat_len)
    slen = jnp.where(ti < n_ctx * ctx_len, ctx_len, lat_len)
    keep_prev = jnp.broadcast_to((pos != 0).astype(F32)[:, None], (ti.shape[0], LANES))
    keep_next = jnp.broadcast_to((pos != slen - 1).astype(F32)[:, None], (ti.shape[0], LANES))
    meta = {"mrow": mrow, "rope_blk": rope_blk, "cos": cos, "sin": sin,
            "ctx_blk": jnp.minimum(bi, ctx_blocks - 1).astype(jnp.int32),
            "keep_prev": keep_prev, "keep_next": keep_next}

    nc_ctx, nc_lat = ctx_len // SSD_CHUNK, lat_len // SSD_CHUNK
    n_ctx_steps = n_ctx * nc_ctx
    si = jnp.arange(n_ctx_steps + n_lat * nc_lat, dtype=jnp.int32)
    s_ctx = si < n_ctx_steps
    li = si - n_ctx_steps
    seq = jnp.where(s_ctx, si // nc_ctx, li // nc_lat)
    ch = jnp.where(s_ctx, si % nc_ctx, li % nc_lat)
    nc = jnp.where(s_ctx, nc_ctx, nc_lat)
    base = jnp.where(s_ctx, seq * nc_ctx, n_ctx_steps + seq * nc_lat)
    meta.update({
        "ssd_fblk": (base + ch).astype(jnp.int32),
        "ssd_bblk": (base + nc - 1 - ch).astype(jnp.int32),
        "ssd_first": (ch == 0).astype(jnp.int32),
        "ssd_last": (ch == nc - 1).astype(jnp.int32),
        "ssd_init": jnp.where(s_ctx, 0, 1).astype(jnp.int32),
        "ssd_s0i": jnp.where(s_ctx, 0, seq).astype(jnp.int32),
        "ssd_sfi": jnp.where(s_ctx, seq, n_ctx - 1).astype(jnp.int32),
    })
    return meta


def kernel(x_prompt, x_sample, cache_k, cache_v, state_ssm, c, c_ctx, norm1_w, norm2_w, w_mod, b_mod,
           w_in, q_norm_w, k_norm_w, conv_w, conv_b, a_log, dt_bias, ssm_d, ssm_norm_w, w_attn_out,
           w_ssm_out, w_out, w_router, router_bias, w_exp_gate, w_exp_up, w_exp_down, final_norm_w):
    n_ctx, ctx_len, _ = x_prompt.shape
    n_lat, lat_len, _ = x_sample.shape
    depth = w_in.shape[0]
    t_ctx = n_ctx * ctx_len
    t_lat = n_lat * lat_len
    assert t_ctx % TM == 0 and lat_len % TM == 0 and 1 + n_lat <= COND_ROWS and TM % ctx_len == 0
    assert ctx_len % TK == 0 and cache_k.shape[2] % TK == 0 and lat_len % GRID_W == 0

    meta = _block_meta(n_ctx, ctx_len, n_lat, lat_len)
    x = jnp.concatenate([x_prompt.reshape(t_ctx, D_MODEL), x_sample.reshape(t_lat, D_MODEL)], axis=0)

    cond = jnp.zeros((COND_ROWS, D_MODEL), F32).at[0].set(c_ctx).at[1:1 + n_lat].set(c)
    mod_all = _adaln(cond, w_mod, b_mod).reshape(depth, COND_ROWS, N_MOD, D_MODEL)
    mod_all = jnp.pad(mod_all, ((0, 0), (0, 0), (0, MOD_ROWS - N_MOD), (0, 0)))

    head_of_lane = jnp.arange(D_INNER, dtype=jnp.int32) // SSM_HEAD_DIM
    lane_id = jnp.arange(LANES, dtype=jnp.int32)
    expand = jnp.stack([(lane_id[:, None] == head_of_lane[None, :] + d * SSM_HEADS) for d in range(2)]
                       ).astype(BF16)
    wrt = w_router.T
    rb = router_bias.reshape(N_EXPERTS, 1)
    fnw = final_norm_w.reshape(1, D_MODEL)

    kv_ctx, s_ctx = None, None
    for l in range(depth):
        wl = w_in[l]
        lw = {
            "norm1_w": norm1_w[l].reshape(1, D_MODEL),
            "norm2_w": norm2_w[l].reshape(1, D_MODEL),
            "wqkv": wl[:, :Z_OFF].astype(BF16),
            "wz": wl[:, Z_OFF:XBC_OFF].astype(BF16),
            "wxbc": wl[:, XBC_OFF:DT_OFF].astype(BF16),
            "wdt": jnp.pad(wl[:, DT_OFF:G_OFF], ((0, 0), (0, LANES - 2 * SSM_HEADS))).astype(BF16),
            "wg": wl[:, G_OFF:].astype(BF16),
            "q_norm_w": jnp.tile(q_norm_w[l], 4).reshape(1, 2 * LANES),
            "k_norm_w": jnp.tile(k_norm_w[l], 4).reshape(1, 2 * LANES),
            "conv_w": conv_w[l],
            "conv_b": conv_b[l].reshape(1, CONV_DIM),
            "dt_bias": jnp.pad(dt_bias[l].reshape(1, 2 * SSM_HEADS), ((0, 0), (0, LANES - 2 * SSM_HEADS))),
            "ssm_d": jnp.repeat(ssm_d[l], SSM_HEAD_DIM).reshape(1, D_INNER),
            "w_attn_out": w_attn_out[l].astype(BF16),
            "w_ssm_out": (ssm_norm_w[l][:, None] * w_ssm_out[l]).astype(BF16),
            "w_out": w_out[l].astype(BF16),
        }
        mod = mod_all[l]
        q, kn, vn, k_ctx, v_ctx, z, xs, bc, dt, g = _in_proj(x, mod, meta, lw, t_ctx, ctx_len, l, depth,
                                                             kv_ctx)
        kv_ctx = (k_ctx, v_ctx)

        attn = _attention(q, kn, vn, None, None, n_ctx, ctx_len, 0, min(TQ, ctx_len))
        attn = _attention(q, kn, vn, _cache_blocks(cache_k[:, l], cache_v[:, l]), attn,
                          n_lat, lat_len, t_ctx, min(TQ, lat_len))

        a_lanes = jnp.pad(-jnp.exp(a_log[l]).reshape(1, 2 * SSM_HEADS),
                          ((0, 0), (0, LANES - 2 * SSM_HEADS)))
        s0 = state_ssm[:, l].transpose(0, 1, 4, 2, 3).reshape(n_lat, 2, D_STATE, D_INNER)
        yf, yb, s_ctx = _ssd(xs, bc, dt, a_lanes, expand, lw["ssm_d"], s0, meta, n_ctx, l, depth, s_ctx)

        x1, h2, idx, gate = _merge(x, attn, yf, yb, z, g, mod, meta, lw, wrt, rb)
        y0, y1 = _moe(h2, idx, gate, (w_exp_gate, w_exp_up, w_exp_down), l)
        if l < depth - 1:
            x = _combine(x1, y0, y1, gate.T, mod, meta, fnw, None)
        else:
            y_prompt, y_sample = _combine(x1, y0, y1, gate.T, mod, meta, fnw, t_ctx)

    y_prompt = y_prompt.reshape(n_ctx, ctx_len, D_MODEL)
    y_sample = y_sample.reshape(n_lat, lat_len, D_MODEL)
    new_k = kv_ctx[0].reshape(n_ctx, depth, ctx_len, N_KV_HEADS, HEAD_DIM)
    new_v = kv_ctx[1].reshape(n_ctx, depth, ctx_len, N_KV_HEADS, HEAD_DIM)
    new_s = s_ctx.reshape(n_ctx, depth, 2, D_STATE, SSM_HEADS, SSM_HEAD_DIM).transpose(0, 1, 2, 4, 5, 3)
    return (y_prompt, y_sample, new_k, new_v, new_s)
```

```python
import functools

import jax
import jax.numpy as jnp
from jax import lax
from jax.experimental import pallas as pl
from jax.experimental.pallas import tpu as pltpu

F32 = jnp.float32
BF16 = jnp.bfloat16

EPS = 1e-6
D_MODEL = 1024
N_HEADS = 16
N_KV_HEADS = 4
HEAD_DIM = 64
ATTN_WIDTH = N_HEADS * HEAD_DIM
KV_WIDTH = N_KV_HEADS * HEAD_DIM
GRID_W = 64
ROPE_AXIS_DIM = HEAD_DIM // 2
ROPE_THETA = 10000.0
D_INNER = 2 * D_MODEL
SSM_HEAD_DIM = 64
SSM_HEADS = D_INNER // SSM_HEAD_DIM
SSM_GROUPS = 4
HEADS_PER_GROUP = SSM_HEADS // SSM_GROUPS
D_STATE = 128
GN = SSM_GROUPS * D_STATE
CONV_DIM = D_INNER + 2 * GN
SSD_CHUNK = 128
N_EXPERTS = 16
N_EXPERT_GROUPS = 4
EXPERTS_PER_GROUP = N_EXPERTS // N_EXPERT_GROUPS
D_FF_EXPERT = 512
N_MOD = 6
MOD_ROWS = 8

LANES = 128
SUBLANES = 8
VMEM_LIMIT = 56 * 1024 * 1024

TM = 512
TQ = 2048
TK = 256
ATTN_UNIT = 256
TE = 512
MERGE_CHAINS = 4
RANK_BLOCK = 512
CONV_CHUNK = 1024
COND_ROWS = 16
V_ROWS = HEAD_DIM + SUBLANES
LOG2E = 1.4426950408889634
Q_SCALE = HEAD_DIM ** -0.5 * LOG2E

Q_OFF, K_OFF, V_OFF = 0, ATTN_WIDTH, ATTN_WIDTH + KV_WIDTH
Z_OFF = ATTN_WIDTH + 2 * KV_WIDTH
XBC_OFF = Z_OFF + D_INNER
DT_OFF = XBC_OFF + CONV_DIM
G_OFF = DT_OFF + 2 * SSM_HEADS
N_IN = G_OFF + 2 * D_MODEL


def _cparams(sem):
    return pltpu.CompilerParams(dimension_semantics=sem, vmem_limit_bytes=VMEM_LIMIT)


def _const_spec(shape):
    nd = len(shape)
    return pl.BlockSpec(shape, lambda *_: (0,) * nd, pipeline_mode=pl.Buffered(1))


def _dot(a, b):
    return jnp.dot(a, b, preferred_element_type=F32)


def _dot_nt(a, b):
    return lax.dot_general(a, b, (((1,), (1,)), ((), ())), preferred_element_type=F32)


def _dot_tn(a, b):
    return lax.dot_general(a, b, (((0,), (0,)), ((), ())), preferred_element_type=F32)


def _split3(a):
    a1 = a.astype(BF16)
    r = a - a1.astype(F32)
    a2 = r.astype(BF16)
    a3 = (r - a2.astype(F32)).astype(BF16)
    return a1, a2, a3


def _dot_exact_lhs(m_bf16, a):
    a1, a2, a3 = _split3(a)
    return _dot(m_bf16, a1) + _dot(m_bf16, a2) + _dot(m_bf16, a3)


def _modnorm(x, w, sc, sh):
    ms = jnp.mean(x * x, axis=-1, keepdims=True)
    return x * lax.rsqrt(ms + EPS) * (w * (1.0 + sc)) + sh


def _sigmoid(x):
    return 1.0 / (1.0 + jnp.exp(-x))


def _silu(x):
    return x * _sigmoid(x)


def _adaln_kernel(c_ref, w_ref, b_ref, o_ref):
    cs = _silu(c_ref[...])
    o_ref[0] = jnp.dot(cs, w_ref[0], preferred_element_type=F32,
                       precision=lax.Precision.HIGHEST) + b_ref[0]


def _adaln(cond, w_mod, b_mod):
    depth = w_mod.shape[0]
    nb = N_MOD
    return pl.pallas_call(
        _adaln_kernel,
        grid=(depth, nb),
        in_specs=[pl.BlockSpec((COND_ROWS, D_MODEL), lambda l, j: (0, 0)),
                  pl.BlockSpec((1, D_MODEL, D_MODEL), lambda l, j: (l, 0, j)),
                  pl.BlockSpec((1, 1, D_MODEL), lambda l, j: (l, 0, j))],
        out_specs=pl.BlockSpec((1, COND_ROWS, D_MODEL), lambda l, j: (l, 0, j)),
        out_shape=jax.ShapeDtypeStruct((depth, COND_ROWS, N_MOD * D_MODEL), F32),
        compiler_params=_cparams(("arbitrary", "arbitrary")),
        name="adaln",
    )(cond, w_mod, b_mod.reshape(depth, 1, N_MOD * D_MODEL))


def _in_proj_kernel(mrow_ref, rope_ref, ctxi_ref,
                    x_ref, xp_ref, xn_ref, mod_ref, n1w_ref,
                    wqkv_ref, wz_ref, wxbc_ref, wdt_ref, wg_ref,
                    qnw_ref, knw_ref, cos_ref, sin_ref, kprev_ref, knext_ref,
                    convw_ref, convb_ref, dtb_ref,
                    *rest, n_ctx_blocks, ctx_len):
    q_ref, kb_ref, vt_ref, kc_ref, vc_ref, z_ref, xs_ref, bc_ref, dt_ref, g_ref = rest[-10:]
    del mrow_ref, rope_ref, ctxi_ref
    i = pl.program_id(0)
    is_ctx = i < n_ctx_blocks
    sh1 = mod_ref[0, 0:1, :]
    sc1 = mod_ref[0, 1:2, :]
    nw = n1w_ref[...]
    h = _modnorm(x_ref[...], nw, sc1, sh1).astype(BF16)
    hp = _modnorm(xp_ref[...], nw, sc1, sh1).astype(BF16)
    hn = _modnorm(xn_ref[...], nw, sc1, sh1).astype(BF16)

    r = lax.broadcasted_iota(jnp.int32, (2 * LANES, 2 * LANES), 0) // HEAD_DIM
    c = lax.broadcasted_iota(jnp.int32, (2 * LANES, 2 * LANES), 1) // HEAD_DIM
    same_head = jnp.where(r == c, 1.0, 0.0).astype(BF16)
    lane = lax.broadcasted_iota(jnp.int32, (TM, LANES), 1)
    first_half = (lane % (HEAD_DIM // 2)) < (HEAD_DIM // 4)

    def head_norm_rope(t, w, scale):
        sq = t * t
        hi = sq.astype(BF16)
        lo = (sq - hi.astype(F32)).astype(BF16)
        ssum = _dot(hi, same_head) + _dot(lo, same_head)
        tn = t * lax.rsqrt(ssum * (1.0 / HEAD_DIM) + EPS) * w
        outs = []
        for s in range(2):
            a = tn[:, s * LANES:(s + 1) * LANES]
            rot = jnp.where(first_half, -pltpu.roll(a, LANES - HEAD_DIM // 4, 1),
                            pltpu.roll(a, HEAD_DIM // 4, 1))
            outs.append((a * cos_ref[...] + rot * sin_ref[...]) * scale)
        return outs

    qw = qnw_ref[...]
    kw = knw_ref[...]
    ones_rows = jnp.where(lax.broadcasted_iota(jnp.int32, (V_ROWS - HEAD_DIM, TK), 0) == 0,
                          1.0, 0.0).astype(BF16)
    cw = CONV_CHUNK
    rows = lax.broadcasted_iota(jnp.int32, (TM, cw), 0)
    kprev = jnp.concatenate([kprev_ref[...]] * (cw // LANES), axis=1)
    knext = jnp.concatenate([knext_ref[...]] * (cw // LANES), axis=1)

    def q_stage(cb):
        def mm():
            return _dot(h, wqkv_ref[:, cb * 2 * LANES:(cb + 1) * 2 * LANES])

        def fin(t):
            a, b = head_norm_rope(t, qw, Q_SCALE)
            q_ref[:, cb * 2 * LANES:cb * 2 * LANES + LANES] = a.astype(BF16)
            q_ref[:, cb * 2 * LANES + LANES:(cb + 1) * 2 * LANES] = b.astype(BF16)
        return mm, fin

    def k_fin(t):
        kpairs = head_norm_rope(t, kw, 1.0)
        for rb in range(TM // TK):
            for pr in range(2):
                kb_ref[pr, rb] = kpairs[pr][rb * TK:(rb + 1) * TK, :].astype(BF16)

        @pl.when(is_ctx)
        def _():
            for sq in range(TM // ctx_len):
                for pr in range(2):
                    kc_ref[sq, :, pr * LANES:(pr + 1) * LANES] = kpairs[pr][sq * ctx_len:(sq + 1) * ctx_len, :]

    def v_fin(v):
        for rb in range(TM // TK):
            vT = v[rb * TK:(rb + 1) * TK, :].T
            for hd in range(N_KV_HEADS):
                vt_ref[hd // 2, hd % 2, rb, 0:HEAD_DIM, :] = (
                    vT[hd * HEAD_DIM:(hd + 1) * HEAD_DIM, :].astype(BF16))
                vt_ref[hd // 2, hd % 2, rb, HEAD_DIM:V_ROWS, :] = ones_rows

        @pl.when(is_ctx)
        def _():
            for sq in range(TM // ctx_len):
                vc_ref[sq] = v[sq * ctx_len:(sq + 1) * ctx_len, :]

    def z_fin(t):
        z_ref[...] = t.astype(BF16)

    def g_fin(t):
        g_ref[...] = _sigmoid(t).astype(BF16)

    def dt_fin(t):
        dtr = t + dtb_ref[...]
        dt_ref[...] = jnp.maximum(dtr, 0.0) + jnp.log(1.0 + jnp.exp(-jnp.abs(dtr)))

    def conv_stage(cb):
        sl = slice(cb * cw, (cb + 1) * cw)

        def mm():
            return (_dot(h, wxbc_ref[:, sl]), _dot(hp, wxbc_ref[:, sl]), _dot(hn, wxbc_ref[:, sl]))

        def fin(res):
            pre, pp, pn = res
            prev_row = pp[SUBLANES - 1:SUBLANES, :]
            next_row = pn[0:1, :]
            up = jnp.where(rows == 0, prev_row, pltpu.roll(pre, 1, 0)) * kprev
            down = jnp.where(rows == TM - 1, next_row, pltpu.roll(pre, TM - 1, 0)) * knext
            y = (convb_ref[:, sl] + up * convw_ref[0:1, sl] + pre * convw_ref[1:2, sl]
                 + down * convw_ref[2:3, sl])
            y = _silu(y).astype(BF16)
            lo = cb * cw
            if lo < D_INNER:
                xs_ref[:, lo:lo + cw] = y
            else:
                bc_ref[:, lo - D_INNER:lo - D_INNER + cw] = y
        return mm, fin

    stages = [q_stage(cb) for cb in range(ATTN_WIDTH // (2 * LANES))]
    stages.append((lambda: _dot(h, wqkv_ref[:, K_OFF:K_OFF + KV_WIDTH]), k_fin))
    stages.append((lambda: _dot(h, wqkv_ref[:, V_OFF:V_OFF + KV_WIDTH]), v_fin))
    stages.append((lambda: _dot(h, wdt_ref[...]), dt_fin))
    stages += [conv_stage(cb) for cb in range(CONV_DIM // cw)]
    stages.append((lambda: _dot(h, wz_ref[...]), z_fin))
    stages.append((lambda: _dot(h, wg_ref[...]), g_fin))
    pending = None
    for mm, fin in stages:
        res = mm()
        if pending is not None:
            pending[0](pending[1])
        pending = (fin, res)
    pending[0](pending[1])


def _in_proj(x, mod, meta, lw, t_ctx, ctx_len, layer, depth, prev_kv):
    t = x.shape[0]
    nblk = t // TM
    nhalo = t // SUBLANES
    per_halo = TM // SUBLANES
    n_ctx_blocks = t_ctx // TM
    nkb = TM // TK

    def row_blk(width):
        return pl.BlockSpec((TM, width), lambda i, *_: (i, 0))

    spb = TM // ctx_len
    ctx_blk = pl.BlockSpec((spb, None, ctx_len, KV_WIDTH),
                           lambda i, mrow, rope, ctxi: (ctxi[i], layer, 0, 0))
    in_specs = [
        row_blk(D_MODEL),
            pl.BlockSpec((SUBLANES, D_MODEL),
                         lambda i, *_: (jnp.maximum(i * per_halo - 1, 0), 0)),
            pl.BlockSpec((SUBLANES, D_MODEL),
                         lambda i, *_: (jnp.minimum((i + 1) * per_halo, nhalo - 1), 0)),
            pl.BlockSpec((1, MOD_ROWS, D_MODEL), lambda i, mrow, *_: (mrow[i], 0, 0)),
            _const_spec((1, D_MODEL)),
            _const_spec((D_MODEL, ATTN_WIDTH + 2 * KV_WIDTH)),
            _const_spec((D_MODEL, D_INNER)),
            _const_spec((D_MODEL, CONV_DIM)),
            _const_spec((D_MODEL, LANES)),
            _const_spec((D_MODEL, 2 * D_MODEL)),
            _const_spec((1, 2 * LANES)),
            _const_spec((1, 2 * LANES)),
            pl.BlockSpec((TM, LANES), lambda i, mrow, rope, *_: (rope[i], 0)),
            pl.BlockSpec((TM, LANES), lambda i, mrow, rope, *_: (rope[i], 0)),
            row_blk(LANES),
            row_blk(LANES),
            _const_spec((3, CONV_DIM)),
            _const_spec((1, CONV_DIM)),
            _const_spec((1, LANES)),
    ]
    out_specs = [row_blk(ATTN_WIDTH),
                   pl.BlockSpec((2, nkb, TK, LANES), lambda i, *_: (0, i, 0, 0)),
                   pl.BlockSpec((2, 2, nkb, V_ROWS, TK), lambda i, *_: (0, 0, i, 0, 0)),
                   ctx_blk, ctx_blk,
                   row_blk(D_INNER), row_blk(D_INNER), row_blk(2 * GN), row_blk(LANES),
                   row_blk(2 * D_MODEL)]
    out_shape = [
        jax.ShapeDtypeStruct((t, ATTN_WIDTH), BF16),
        jax.ShapeDtypeStruct((2, t // TK, TK, LANES), BF16),
        jax.ShapeDtypeStruct((2, 2, t // TK, V_ROWS, TK), BF16),
        jax.ShapeDtypeStruct((t_ctx // ctx_len, depth, ctx_len, KV_WIDTH), F32),
        jax.ShapeDtypeStruct((t_ctx // ctx_len, depth, ctx_len, KV_WIDTH), F32),
        jax.ShapeDtypeStruct((t, D_INNER), BF16),
        jax.ShapeDtypeStruct((t, D_INNER), BF16),
        jax.ShapeDtypeStruct((t, 2 * GN), BF16),
        jax.ShapeDtypeStruct((t, LANES), F32),
        jax.ShapeDtypeStruct((t, 2 * D_MODEL), BF16),
    ]
    args = [meta["mrow"], meta["rope_blk"], meta["ctx_blk"],
            x, x, x, mod, lw["norm1_w"], lw["wqkv"], lw["wz"], lw["wxbc"], lw["wdt"], lw["wg"],
            lw["q_norm_w"], lw["k_norm_w"], meta["cos"], meta["sin"], meta["keep_prev"],
            meta["keep_next"], lw["conv_w"], lw["conv_b"], lw["dt_bias"]]
    aliases = {}
    if prev_kv is not None:
        in_specs += [pl.BlockSpec(memory_space=pl.ANY)] * 2
        aliases = {len(args): 3, len(args) + 1: 4}
        args += list(prev_kv)
    grid_spec = pltpu.PrefetchScalarGridSpec(num_scalar_prefetch=3, grid=(nblk,), in_specs=in_specs,
                                             out_specs=out_specs)
    return pl.pallas_call(
        functools.partial(_in_proj_kernel, n_ctx_blocks=n_ctx_blocks, ctx_len=ctx_len),
        grid_spec=grid_spec, out_shape=out_shape, input_output_aliases=aliases,
        compiler_params=_cparams(("arbitrary",)), name="in_proj",
    )(*args)


def _attn_kernel(*refs, n_new, n_cache, tq):
    q_ref, kn_ref, vn_ref = refs[0:3]
    if n_cache:
        kc_ref, vc_ref = refs[3:5]
    o_ref, qt_ref, m_ref, acc_ref, sa_ref, mxa_ref, sb_ref, mxb_ref = refs[-8:]
    nq = 4 * tq
    zeros_half = jnp.zeros((HEAD_DIM, tq), F32)
    for hh in range(2):
        cols = []
        for jj in range(2):
            off = 2 * LANES * hh + LANES * jj
            qt = q_ref[:, off:off + LANES].astype(F32).T
            for s in range(2):
                head = qt[s * HEAD_DIM:(s + 1) * HEAD_DIM, :]
                parts = [head, zeros_half] if hh == 0 else [zeros_half, head]
                cols.append(jnp.concatenate(parts, axis=0))
        qt_ref[hh] = jnp.concatenate(cols, axis=1).astype(BF16)
        m_ref[hh] = jnp.full((1, nq), -jnp.inf, F32)
        acc_ref[hh] = jnp.zeros((V_ROWS, nq), F32)

    units = [(hh, slice(c * ATTN_UNIT, (c + 1) * ATTN_UNIT))
             for hh in range(2) for c in range(nq // ATTN_UNIT)]

    def scores(kblk, s_ref, mx_ref, hh, cs):
        s = _dot(kblk, qt_ref[hh, :, cs])
        s_ref[hh, :, cs] = s
        mx_ref[hh, :, cs] = jnp.max(s, axis=0, keepdims=True)

    def consume(vt, s_ref, mx_ref, hh, cs):
        m = m_ref[hh, :, cs]
        m_new = jnp.maximum(m, mx_ref[hh, :, cs])
        alpha = jnp.exp2(m - m_new)
        p = jnp.exp2(s_ref[hh, :, cs] - m_new).astype(BF16)
        acc_ref[hh, :, cs] = alpha * acc_ref[hh, :, cs] + _dot(vt, p)
        m_ref[hh, :, cs] = m_new

    def step(k_nxt, nxt, v_cur, cur):
        for hh, cs in units:
            if k_nxt is not None:
                scores(k_nxt, *nxt, hh, cs)
            consume(v_cur[hh], *cur, hh, cs)

    bufs = ((sa_ref, mxa_ref), (sb_ref, mxb_ref))
    for hh, cs in units:
        scores(kn_ref[0], *bufs[0], hh, cs)

    def body(j, carry):
        step(kn_ref[2 * j + 1], bufs[1], (vn_ref[0, 2 * j], vn_ref[1, 2 * j]), bufs[0])
        step(kn_ref[2 * j + 2], bufs[0], (vn_ref[0, 2 * j + 1], vn_ref[1, 2 * j + 1]), bufs[1])
        return carry

    n_loop = (n_new - 1) // 2
    lax.fori_loop(0, n_loop, body, 0)
    rest = [(kn_ref, vn_ref, b) for b in range(2 * n_loop, n_new)]
    if n_cache:
        rest += [(kc_ref, vc_ref, b) for b in range(n_cache)]
    for r, (_, v_src, b) in enumerate(rest):
        k_nxt = None
        if r + 1 < len(rest):
            k_src, _, b_nxt = rest[r + 1]
            k_nxt = k_src[b_nxt]
        step(k_nxt, bufs[(r + 1) % 2], (v_src[0, b], v_src[1, b]), bufs[r % 2])

    for hh in range(2):
        oT = acc_ref[hh]
        oT = oT[0:HEAD_DIM, :] / oT[HEAD_DIM:HEAD_DIM + 1, :]
        for jj in range(2):
            pair = jnp.concatenate([oT[:, (2 * jj) * tq:(2 * jj + 1) * tq],
                                    oT[:, (2 * jj + 1) * tq:(2 * jj + 2) * tq]], axis=0)
            off = 2 * LANES * hh + LANES * jj
            o_ref[:, off:off + LANES] = pair.T.astype(BF16)


def _attention(q, kn, vn, cache, prev_out, nseq, seq_len, row0, tq):
    n_new = seq_len // TK
    nqb = seq_len // tq
    nq = 4 * tq
    q0 = row0 // tq
    kv0 = row0 // (n_new * TK)
    assert row0 % tq == 0 and row0 % (n_new * TK) == 0
    qspec = pl.BlockSpec((tq, 4 * LANES), lambda b, p, i: (q0 + b * nqb + i, p))
    in_specs = [
        qspec,
        pl.BlockSpec((None, n_new, TK, LANES), lambda b, p, i: (p, kv0 + b, 0, 0)),
        pl.BlockSpec((None, 2, n_new, V_ROWS, TK), lambda b, p, i: (p, 0, kv0 + b, 0, 0)),
    ]
    args = [q, kn, vn]
    n_cache = 0
    if cache is not None:
        kc, vc = cache
        n_cache = kc.shape[1] // nseq
        in_specs += [pl.BlockSpec((None, n_cache, TK, LANES), lambda b, p, i: (p, b, 0, 0)),
                     pl.BlockSpec((None, 2, n_cache, V_ROWS, TK), lambda b, p, i: (p, 0, b, 0, 0))]
        args += [kc, vc]
    aliases = {}
    kern = functools.partial(_attn_kernel, n_new=n_new, n_cache=n_cache, tq=tq)
    if prev_out is not None:
        n_in = len(args)
        in_specs.append(pl.BlockSpec(memory_space=pl.ANY))
        aliases = {n_in: 0}
        args.append(prev_out)
        inner = kern
        kern = lambda *refs: inner(*refs[:n_in], *refs[n_in + 1:])
    return pl.pallas_call(
        kern,
        grid=(nseq, 2, nqb),
        in_specs=in_specs,
        out_specs=qspec,
        out_shape=jax.ShapeDtypeStruct(q.shape, BF16),
        scratch_shapes=[pltpu.VMEM((2, LANES, nq), BF16),
                        pltpu.VMEM((2, 1, nq), F32),
                        pltpu.VMEM((2, V_ROWS, nq), F32),
                        pltpu.VMEM((2, TK, nq), F32),
                        pltpu.VMEM((2, 1, nq), F32),
                        pltpu.VMEM((2, TK, nq), F32),
                        pltpu.VMEM((2, 1, nq), F32)],
        input_output_aliases=aliases,
        compiler_params=_cparams(("arbitrary", "arbitrary", "arbitrary")),
        name="attention",
    )(*args)


def _cache_blocks(ck, cv):
    nseq, past = ck.shape[0], ck.shape[1]
    n = past // TK
    kb = ck.astype(BF16).reshape(nseq * n, TK, 2, LANES).transpose(2, 0, 1, 3)
    vb = cv.astype(BF16).reshape(nseq * n, TK, 2, 2, HEAD_DIM).transpose(2, 3, 0, 4, 1)
    ones = jnp.ones(vb.shape[:3] + (1, TK), BF16)
    zeros = jnp.zeros(vb.shape[:3] + (V_ROWS - HEAD_DIM - 1, TK), BF16)
    return kb, jnp.concatenate([vb, ones, zeros], axis=3)


def _ssd_direction(x_ref, bc_ref, dt_ref, a_lanes, expand, dvec_ref, st_ref, y_ref, d):
    q = SSD_CHUNK
    row = lax.broadcasted_iota(jnp.int32, (q, q), 0)
    col = lax.broadcasted_iota(jnp.int32, (q, q), 1)
    causal = (row >= col) if d == 0 else (row <= col)
    tril = jnp.where(row >= col, 1.0, 0.0).astype(BF16)

    dt = dt_ref[...]
    a = dt * a_lanes
    prefix = _dot_exact_lhs(tril, a)
    if d == 0:
        cs = prefix
        last = cs[q - 1:q, :]
    else:
        cs = prefix[q - 1:q, :] - prefix + a
        last = cs[0:1, :]
    cs2 = cs * LOG2E
    cst2 = (cs2 - jnp.log2(dt)).T
    wj = jnp.exp(last - cs) * dt
    wj_x = _dot(wj.astype(BF16), expand)
    elast = jnp.exp(last)

    lane = lax.broadcasted_iota(jnp.int32, (q, LANES), 1)
    lo_half = lane < SSM_HEAD_DIM
    lo_half_row = lo_half[0:1, :]
    for g in range(SSM_GROUPS):
        bg = bc_ref[:, g * D_STATE:(g + 1) * D_STATE]
        cg = bc_ref[:, GN + g * D_STATE:GN + (g + 1) * D_STATE]
        gmat = _dot_nt(cg, bg)
        gsl = slice(g * HEADS_PER_GROUP * SSM_HEAD_DIM, (g + 1) * HEADS_PER_GROUP * SSM_HEAD_DIM)
        st = st_ref[d, :, gsl]
        y_inter = _dot(cg, st.astype(BF16))
        xg = x_ref[:, gsl]
        el_parts = []
        for hp in range(HEADS_PER_GROUP // 2):
            xpair = xg[:, hp * LANES:(hp + 1) * LANES]
            l0 = d * SSM_HEADS + g * HEADS_PER_GROUP + 2 * hp
            ws, es = [], []
            for s in range(2):
                hl = l0 + s
                csb = jnp.broadcast_to(cs2[:, hl:hl + 1], (q, q))
                seg = csb - cst2[hl:hl + 1, :]
                ws.append(gmat * jnp.exp2(jnp.where(causal, seg, -jnp.inf)))
                es.append(jnp.exp2(csb))
            wcat = jnp.concatenate(ws, axis=1).astype(BF16)
            zero = jnp.zeros_like(xpair)
            xm = jnp.concatenate([jnp.where(lo_half, xpair, zero), jnp.where(lo_half, zero, xpair)],
                                 axis=0)
            e_pair = jnp.where(lo_half, es[0], es[1])
            ypair = y_inter[:, hp * LANES:(hp + 1) * LANES] * e_pair + _dot(wcat, xm)
            lo = g * HEADS_PER_GROUP * SSM_HEAD_DIM + hp * LANES
            if d == 0:
                ypair = ypair + dvec_ref[:, lo:lo + LANES] * xpair.astype(F32)
            y_ref[:, lo:lo + LANES] = ypair.astype(y_ref.dtype)
            el_parts.append(jnp.where(lo_half_row, elast[:, l0:l0 + 1], elast[:, l0 + 1:l0 + 2]))
        xw = (xg.astype(F32) * wj_x[:, gsl]).astype(BF16)
        el = jnp.concatenate(el_parts, axis=1)
        st_ref[d, :, gsl] = st * el + _dot_tn(bg, xw)


def _ssd_kernel(fblk_ref, bblk_ref, first_ref, last_ref, init_ref, s0i_ref, sfi_ref,
                xf_ref, bcf_ref, dtf_ref, xb_ref, bcb_ref, dtb_ref, a_ref, exp_ref, dvec_ref, s0_ref,
                *rest):
    yf_ref, yb_ref, sfin_ref, st_ref = rest[-4:]
    del fblk_ref, bblk_ref, s0i_ref, sfi_ref
    s = pl.program_id(0)

    @pl.when(jnp.logical_and(first_ref[s] == 1, init_ref[s] == 1))
    def _():
        st_ref[...] = s0_ref[0]

    @pl.when(jnp.logical_and(first_ref[s] == 1, init_ref[s] == 0))
    def _():
        st_ref[...] = jnp.zeros(st_ref.shape, F32)

    a_lanes = a_ref[...]
    _ssd_direction(xf_ref, bcf_ref, dtf_ref, a_lanes, exp_ref[0], dvec_ref, st_ref, yf_ref, 0)
    _ssd_direction(xb_ref, bcb_ref, dtb_ref, a_lanes, exp_ref[1], dvec_ref, st_ref, yb_ref, 1)

    @pl.when(jnp.logical_and(last_ref[s] == 1, init_ref[s] == 0))
    def _():
        sfin_ref[0] = st_ref[...]


def _ssd(xs, bc, dt, a_lanes, expand, dvec, s0, meta, n_ctx, layer, depth, prev_sfin):
    t = xs.shape[0]
    n_steps = t // SSD_CHUNK
    st_shape = (2, D_STATE, D_INNER)
    fwd = lambda s, fblk, bblk, *_: (fblk[s], 0)
    bwd = lambda s, fblk, bblk, *_: (bblk[s], 0)

    def specs(idx):
        return [pl.BlockSpec((SSD_CHUNK, D_INNER), idx),
                pl.BlockSpec((SSD_CHUNK, 2 * GN), idx),
                pl.BlockSpec((SSD_CHUNK, LANES), idx)]

    in_specs = specs(fwd) + specs(bwd) + [
        pl.BlockSpec((1, LANES), lambda s, *_: (0, 0)),
        pl.BlockSpec((2, LANES, D_INNER), lambda s, *_: (0, 0, 0)),
        pl.BlockSpec((1, D_INNER), lambda s, *_: (0, 0)),
        pl.BlockSpec((1,) + st_shape, lambda s, f, b, fi, la, ini, s0i, sfi: (s0i[s], 0, 0, 0)),
    ]
    args = [meta["ssd_fblk"], meta["ssd_bblk"], meta["ssd_first"], meta["ssd_last"], meta["ssd_init"],
            meta["ssd_s0i"], meta["ssd_sfi"], xs, bc, dt, xs, bc, dt, a_lanes, expand, dvec, s0]
    aliases = {}
    if prev_sfin is not None:
        in_specs.append(pl.BlockSpec(memory_space=pl.ANY))
        aliases = {len(args): 2}
        args.append(prev_sfin)
    grid_spec = pltpu.PrefetchScalarGridSpec(
        num_scalar_prefetch=7,
        grid=(n_steps,),
        in_specs=in_specs,
        out_specs=[pl.BlockSpec((SSD_CHUNK, D_INNER), fwd), pl.BlockSpec((SSD_CHUNK, D_INNER), bwd),
                   pl.BlockSpec((1, None) + st_shape,
                                lambda s, f, b, fi, la, ini, s0i, sfi: (sfi[s], layer, 0, 0, 0))],
        scratch_shapes=[pltpu.VMEM(st_shape, F32)],
    )
    out_shape = [jax.ShapeDtypeStruct((t, D_INNER), BF16), jax.ShapeDtypeStruct((t, D_INNER), BF16),
                 jax.ShapeDtypeStruct((n_ctx, depth) + st_shape, F32)]
    return pl.pallas_call(
        _ssd_kernel, grid_spec=grid_spec, out_shape=out_shape, input_output_aliases=aliases,
        compiler_params=_cparams(("arbitrary",)), name="ssd",
    )(*args)


def _route(sel, scores):
    neg = -jnp.inf
    rows = [sel[e:e + 1, :] for e in range(N_EXPERTS)]
    srow = [scores[e:e + 1, :] for e in range(N_EXPERTS)]
    best_score = None
    best = None
    for g in range(N_EXPERT_GROUPS):
        v = rows[g * EXPERTS_PER_GROUP:(g + 1) * EXPERTS_PER_GROUP]
        top2 = None
        for a in range(EXPERTS_PER_GROUP):
            for b in range(a + 1, EXPERTS_PER_GROUP):
                s = v[a] + v[b]
                top2 = s if top2 is None else jnp.maximum(top2, s)
        if g == 0:
            best_score, best = top2, jnp.zeros(top2.shape, jnp.int32)
        else:
            better = top2 > best_score
            best = jnp.where(better, g, best)
            best_score = jnp.where(better, top2, best_score)

    def pick(vals):
        out = vals[0]
        for g in range(1, N_EXPERT_GROUPS):
            out = jnp.where(best == g, vals[g], out)
        return out

    gsel = [pick([rows[g * EXPERTS_PER_GROUP + j] for g in range(N_EXPERT_GROUPS)])
            for j in range(EXPERTS_PER_GROUP)]
    gsc = [pick([srow[g * EXPERTS_PER_GROUP + j] for g in range(N_EXPERT_GROUPS)])
           for j in range(EXPERTS_PER_GROUP)]

    def argmax_first(vals):
        bi = jnp.zeros(vals[0].shape, jnp.int32)
        bv = vals[0]
        for j in range(1, len(vals)):
            better = vals[j] > bv
            bi = jnp.where(better, j, bi)
            bv = jnp.where(better, vals[j], bv)
        return bi

    i1 = argmax_first(gsel)
    i2 = argmax_first([jnp.where(i1 == j, neg, gsel[j]) for j in range(EXPERTS_PER_GROUP)])

    def take(vals, idx):
        out = vals[0]
        for j in range(1, len(vals)):
            out = jnp.where(idx == j, vals[j], out)
        return out

    g1 = take(gsc, i1)
    g2 = take(gsc, i2)
    tot = g1 + g2
    idx = jnp.concatenate([best * EXPERTS_PER_GROUP + i1, best * EXPERTS_PER_GROUP + i2], axis=0)
    gate = jnp.concatenate([g1 / tot, g2 / tot], axis=0)
    return idx, gate


def _merge_kernel(mrow_ref, x_ref, attn_ref, yf_ref, yb_ref, z_ref, g_ref, mod_ref,
                  wa_ref, ws_ref, wo_ref, n2w_ref, wrt_ref, rb_ref,
                  x1_ref, h2_ref, idx_ref, gate_ref):
    del mrow_ref
    gw = D_INNER // SSM_GROUPS
    hr = TM // MERGE_CHAINS

    def chain(ci):
        rs = slice(ci * hr, (ci + 1) * hr)
        attn_o = _dot(attn_ref[rs, :], wa_ref[...])
        yield
        ssm_o = None
        for g in range(SSM_GROUPS):
            sl = slice(g * gw, (g + 1) * gw)
            y = (yf_ref[rs, sl] + yb_ref[rs, sl]).astype(F32) * _silu(z_ref[rs, sl].astype(F32))
            ms = jnp.mean(y * y, axis=-1, keepdims=True)
            yn = (y * lax.rsqrt(ms + EPS)).astype(BF16)
            part = _dot(yn, ws_ref[sl, :])
            ssm_o = part if ssm_o is None else ssm_o + part
            yield
        merged = (g_ref[rs, 0:D_MODEL].astype(F32) * attn_o
                  + g_ref[rs, D_MODEL:2 * D_MODEL].astype(F32) * ssm_o)
        out = _dot(merged.astype(BF16), wo_ref[...])
        yield
        x1 = x_ref[rs, :] + mod_ref[0, 2:3, :] * out
        x1_ref[rs, :] = x1
        h2 = _modnorm(x1, n2w_ref[...], mod_ref[0, 4:5, :], mod_ref[0, 3:4, :])
        h2_ref[rs, :] = h2.astype(BF16)
        logits = lax.dot_general(wrt_ref[...], h2, (((1,), (1,)), ((), ())),
                                 preferred_element_type=F32, precision=lax.Precision.HIGHEST)
        yield
        scores = _sigmoid(logits)
        idx, gate = _route(scores + rb_ref[...], scores)
        idx_ref[:, rs] = idx
        gate_ref[:, rs] = gate

    live = [chain(ci) for ci in range(MERGE_CHAINS)]
    while live:
        for c in list(live):
            try:
                next(c)
            except StopIteration:
                live.remove(c)


def _merge(x, attn, yf, yb, z, g, mod, meta, lw, wrt, rb):
    t = x.shape[0]
    nblk = t // TM

    def row_blk(width):
        return pl.BlockSpec((TM, width), lambda i, *_: (i, 0))

    grid_spec = pltpu.PrefetchScalarGridSpec(
        num_scalar_prefetch=1,
        grid=(nblk,),
        in_specs=[row_blk(D_MODEL), row_blk(ATTN_WIDTH), row_blk(D_INNER), row_blk(D_INNER),
                  row_blk(D_INNER), row_blk(2 * D_MODEL),
                  pl.BlockSpec((1, MOD_ROWS, D_MODEL), lambda i, mrow: (mrow[i], 0, 0)),
                  _const_spec((ATTN_WIDTH, D_MODEL)), _const_spec((D_INNER, D_MODEL)),
                  _const_spec((D_MODEL, D_MODEL)), _const_spec((1, D_MODEL)),
                  _const_spec((N_EXPERTS, D_MODEL)), _const_spec((N_EXPERTS, 1))],
        out_specs=[row_blk(D_MODEL), row_blk(D_MODEL),
                   pl.BlockSpec((2, TM), lambda i, *_: (0, i)),
                   pl.BlockSpec((2, TM), lambda i, *_: (0, i))],
    )
    out_shape = [jax.ShapeDtypeStruct((t, D_MODEL), F32), jax.ShapeDtypeStruct((t, D_MODEL), BF16),
                 jax.ShapeDtypeStruct((2, t), jnp.int32), jax.ShapeDtypeStruct((2, t), F32)]
    return pl.pallas_call(
        _merge_kernel, grid_spec=grid_spec, out_shape=out_shape,
        compiler_params=_cparams(("arbitrary",)), name="merge",
    )(meta["mrow"], x, attn, yf, yb, z, g, mod, lw["w_attn_out"], lw["w_ssm_out"], lw["w_out"],
      lw["norm2_w"], wrt, rb)


def _expert_kernel(te_ref, nt_ref, chg_ref, x_ref, wg_ref, wu_ref, wd_ref, o_ref,
                   wgs_ref, wus_ref, wds_ref):
    del te_ref
    i = pl.program_id(0)

    @pl.when(chg_ref[i] == 1)
    def _():
        wgs_ref[...] = wg_ref[0].astype(BF16)
        wus_ref[...] = wu_ref[0].astype(BF16)
        wds_ref[...] = wd_ref[0].astype(BF16)

    @pl.when(i < nt_ref[0])
    def _():
        x = x_ref[...]
        hmid = _silu(_dot(x, wgs_ref[...])) * _dot(x, wus_ref[...])
        o_ref[...] = _dot(hmid.astype(BF16), wds_ref[...]).astype(o_ref.dtype)

    @pl.when(i >= nt_ref[0])
    def _():
        o_ref[...] = jnp.zeros(o_ref.shape, o_ref.dtype)


def _experts(x_sorted, tile_expert, n_tiles, wg, wu, wd, layer):
    rows = x_sorted.shape[0]
    changed = jnp.concatenate([jnp.ones((1,), jnp.int32),
                               (tile_expert[1:] != tile_expert[:-1]).astype(jnp.int32)])
    grid_spec = pltpu.PrefetchScalarGridSpec(
        num_scalar_prefetch=3,
        grid=(rows // TE,),
        in_specs=[pl.BlockSpec((TE, D_MODEL), lambda i, te, nt, chg: (i, 0)),
                  pl.BlockSpec((None, 1, D_MODEL, D_FF_EXPERT), lambda i, te, nt, chg: (layer, te[i], 0, 0)),
                  pl.BlockSpec((None, 1, D_MODEL, D_FF_EXPERT), lambda i, te, nt, chg: (layer, te[i], 0, 0)),
                  pl.BlockSpec((None, 1, D_FF_EXPERT, D_MODEL), lambda i, te, nt, chg: (layer, te[i], 0, 0))],
        out_specs=pl.BlockSpec((TE, D_MODEL), lambda i, te, nt, chg: (i, 0)),
        scratch_shapes=[pltpu.VMEM((D_MODEL, D_FF_EXPERT), BF16),
                        pltpu.VMEM((D_MODEL, D_FF_EXPERT), BF16),
                        pltpu.VMEM((D_FF_EXPERT, D_MODEL), BF16)],
    )
    return pl.pallas_call(
        _expert_kernel, grid_spec=grid_spec,
        out_shape=jax.ShapeDtypeStruct((rows, D_MODEL), BF16),
        compiler_params=_cparams(("arbitrary",)), name="experts",
    )(tile_expert, n_tiles, changed, x_sorted, wg, wu, wd)


def _combine_kernel(mrow_ref, x1_ref, y0_ref, y1_ref, gate_ref, mod_ref, fnw_ref, *out_refs,
                    n_ctx_blocks):
    del mrow_ref
    moe = (gate_ref[:, 0:1] * y0_ref[...].astype(F32) + gate_ref[:, 1:2] * y1_ref[...].astype(F32))
    x2 = x1_ref[...] + mod_ref[0, 5:6, :] * moe
    if n_ctx_blocks is None:
        out_refs[0][...] = x2
        return
    ms = jnp.mean(x2 * x2, axis=-1, keepdims=True)
    y = x2 * lax.rsqrt(ms + EPS) * fnw_ref[...]
    is_ctx = pl.program_id(0) < n_ctx_blocks

    @pl.when(is_ctx)
    def _():
        out_refs[0][...] = y

    @pl.when(jnp.logical_not(is_ctx))
    def _():
        out_refs[1][...] = y


def _combine(x1, y0, y1, gate_t, mod, meta, fnw, t_ctx):
    t = x1.shape[0]

    def row_blk(width):
        return pl.BlockSpec((TM, width), lambda i, *_: (i, 0))

    if t_ctx is None:
        n_ctx_blocks = None
        out_specs = row_blk(D_MODEL)
        out_shape = jax.ShapeDtypeStruct((t, D_MODEL), F32)
    else:
        n_ctx_blocks = t_ctx // TM
        out_specs = [pl.BlockSpec((TM, D_MODEL), lambda i, *_: (jnp.minimum(i, n_ctx_blocks - 1), 0)),
                     pl.BlockSpec((TM, D_MODEL), lambda i, *_: (jnp.maximum(i - n_ctx_blocks, 0), 0))]
        out_shape = [jax.ShapeDtypeStruct((t_ctx, D_MODEL), F32),
                     jax.ShapeDtypeStruct((t - t_ctx, D_MODEL), F32)]
    grid_spec = pltpu.PrefetchScalarGridSpec(
        num_scalar_prefetch=1,
        grid=(t // TM,),
        in_specs=[row_blk(D_MODEL), row_blk(D_MODEL), row_blk(D_MODEL), row_blk(2),
                  pl.BlockSpec((1, MOD_ROWS, D_MODEL), lambda i, mrow: (mrow[i], 0, 0)),
                  _const_spec((1, D_MODEL))],
        out_specs=out_specs,
    )
    return pl.pallas_call(
        functools.partial(_combine_kernel, n_ctx_blocks=n_ctx_blocks), grid_spec=grid_spec,
        out_shape=out_shape,
        compiler_params=_cparams(("arbitrary",)), name="combine",
    )(meta["mrow"], x1, y0, y1, gate_t, mod, fnw)


def _moe(h2, idx, gate, w_exp, layer):
    t = h2.shape[0]
    n_assign = 2 * t
    rows = n_assign + N_EXPERTS * TE
    e_flat = idx.reshape(n_assign)
    onehot = e_flat[:, None] == jnp.arange(N_EXPERTS, dtype=jnp.int32)[None, :]
    nb = n_assign // RANK_BLOCK
    oh = onehot.astype(BF16).reshape(nb, RANK_BLOCK, N_EXPERTS)
    tril = jnp.tril(jnp.ones((RANK_BLOCK, RANK_BLOCK), BF16))
    local = jnp.einsum("ij,bjk->bik", tril, oh, preferred_element_type=F32)
    bsum = local[:, -1, :]
    before = jnp.dot(jnp.tril(jnp.ones((nb, nb), F32), -1), bsum, precision=lax.Precision.HIGHEST)
    running = (local + before[:, None, :]).reshape(n_assign, N_EXPERTS)
    counts = (before[-1] + bsum[-1]).astype(jnp.int32)
    padded = ((counts + TE - 1) // TE) * TE
    pad_end = jnp.cumsum(padded)
    pad_off = pad_end - padded
    pos = jnp.sum(jnp.where(onehot, running + (pad_off - 1).astype(F32)[None, :], 0.0), axis=1
                  ).astype(jnp.int32)
    tok = jnp.arange(n_assign, dtype=jnp.int32) % t
    src = (jnp.arange(rows, dtype=jnp.int32) % t).at[pos].set(
        tok, mode="promise_in_bounds", unique_indices=True)
    n_tiles = (pad_end[-1] // TE).astype(jnp.int32).reshape(1)
    tile_start = jnp.arange(rows // TE, dtype=jnp.int32) * TE
    tile_expert = jnp.minimum(jnp.sum(tile_start[:, None] >= pad_end[None, :], axis=1),
                              N_EXPERTS - 1).astype(jnp.int32)
    last_used = tile_expert[jnp.maximum(n_tiles[0] - 1, 0)]
    tile_expert = jnp.where(tile_start < pad_end[-1], tile_expert, last_used)
    x_sorted = h2.at[src].get(mode="promise_in_bounds")
    y_sorted = _experts(x_sorted, tile_expert, n_tiles, *w_exp, layer)
    pos2 = pos.reshape(2, t)
    return (y_sorted.at[pos2[0]].get(mode="promise_in_bounds"),
            y_sorted.at[pos2[1]].get(mode="promise_in_bounds"))


def _rope_tables(n_tokens):
    rows = n_tokens // GRID_W
    row = jnp.repeat(jnp.arange(rows, dtype=F32), GRID_W)
    col = jnp.tile(jnp.arange(GRID_W, dtype=F32), rows)
    inv = 1.0 / (ROPE_THETA ** (jnp.arange(0, ROPE_AXIS_DIM, 2, dtype=F32) / ROPE_AXIS_DIM))
    ar = row[:, None] * inv
    ac = col[:, None] * inv
    ang = jnp.concatenate([ar, ar, ac, ac], axis=-1)
    cos, sin = jnp.cos(ang), jnp.sin(ang)
    cos = jnp.concatenate([jnp.ones((TM, HEAD_DIM), F32), cos], axis=0)
    sin = jnp.concatenate([jnp.zeros((TM, HEAD_DIM), F32), sin], axis=0)
    return jnp.tile(cos, (1, 2)), jnp.tile(sin, (1, 2))


def _block_meta(n_ctx, ctx_len, n_lat, lat_len):
    ctx_blocks = n_ctx * ctx_len // TM
    lat_blocks = n_lat * lat_len // TM
    per_lat = lat_len // TM
    bi = jnp.arange(ctx_blocks + lat_blocks, dtype=jnp.int32)
    is_ctx = bi < ctx_blocks
    lat_i = bi - ctx_blocks
    mrow = jnp.where(is_ctx, 0, 1 + lat_i // per_lat).astype(jnp.int32)
    rope_blk = jnp.where(is_ctx, 0, 1 + lat_i % per_lat).astype(jnp.int32)
    cos, sin = _rope_tables(lat_len)
    ti = jnp.arange(n_ctx * ctx_len + n_lat * lat_len, dtype=jnp.int32)
    pos = jnp.where(ti < n_ctx * ctx_len, ti % ctx_len, (ti - n_ctx * ctx_len) % lat_len)
    slen = jnp.where(ti < n_ctx * ctx_len, ctx_len, lat_len)
    keep_prev = jnp.broadcast_to((pos != 0).astype(F32)[:, None], (ti.shape[0], LANES))
    keep_next = jnp.broadcast_to((pos != slen - 1).astype(F32)[:, None], (ti.shape[0], LANES))
    meta = {"mrow": mrow, "rope_blk": rope_blk, "cos": cos, "sin": sin,
            "ctx_blk": jnp.minimum(bi, ctx_blocks - 1).astype(jnp.int32),
            "keep_prev": keep_prev, "keep_next": keep_next}

    nc_ctx, nc_lat = ctx_len // SSD_CHUNK, lat_len // SSD_CHUNK
    n_ctx_steps = n_ctx * nc_ctx
    si = jnp.arange(n_ctx_steps + n_lat * nc_lat, dtype=jnp.int32)
    s_ctx = si < n_ctx_steps
    li = si - n_ctx_steps
    seq = jnp.where(s_ctx, si // nc_ctx, li // nc_lat)
    ch = jnp.where(s_ctx, si % nc_ctx, li % nc_lat)
    nc = jnp.where(s_ctx, nc_ctx, nc_lat)
    base = jnp.where(s_ctx, seq * nc_ctx, n_ctx_steps + seq * nc_lat)
    meta.update({
        "ssd_fblk": (base + ch).astype(jnp.int32),
        "ssd_bblk": (base + nc - 1 - ch).astype(jnp.int32),
        "ssd_first": (ch == 0).astype(jnp.int32),
        "ssd_last": (ch == nc - 1).astype(jnp.int32),
        "ssd_init": jnp.where(s_ctx, 0, 1).astype(jnp.int32),
        "ssd_s0i": jnp.where(s_ctx, 0, seq).astype(jnp.int32),
        "ssd_sfi": jnp.where(s_ctx, seq, n_ctx - 1).astype(jnp.int32),
    })
    return meta


def kernel(x_prompt, x_sample, cache_k, cache_v, state_ssm, c, c_ctx, norm1_w, norm2_w, w_mod, b_mod,
           w_in, q_norm_w, k_norm_w, conv_w, conv_b, a_log, dt_bias, ssm_d, ssm_norm_w, w_attn_out,
           w_ssm_out, w_out, w_router, router_bias, w_exp_gate, w_exp_up, w_exp_down, final_norm_w):
    n_ctx, ctx_len, _ = x_prompt.shape
    n_lat, lat_len, _ = x_sample.shape
    depth = w_in.shape[0]
    t_ctx = n_ctx * ctx_len
    t_lat = n_lat * lat_len
    assert t_ctx % TM == 0 and lat_len % TM == 0 and 1 + n_lat <= COND_ROWS and TM % ctx_len == 0
    assert ctx_len % TK == 0 and cache_k.shape[2] % TK == 0 and lat_len % GRID_W == 0

    meta = _block_meta(n_ctx, ctx_len, n_lat, lat_len)
    x = jnp.concatenate([x_prompt.reshape(t_ctx, D_MODEL), x_sample.reshape(t_lat, D_MODEL)], axis=0)

    cond = jnp.zeros((COND_ROWS, D_MODEL), F32).at[0].set(c_ctx).at[1:1 + n_lat].set(c)
    mod_all = _adaln(cond, w_mod, b_mod).reshape(depth, COND_ROWS, N_MOD, D_MODEL)
    mod_all = jnp.pad(mod_all, ((0, 0), (0, 0), (0, MOD_ROWS - N_MOD), (0, 0)))

    head_of_lane = jnp.arange(D_INNER, dtype=jnp.int32) // SSM_HEAD_DIM
    lane_id = jnp.arange(LANES, dtype=jnp.int32)
    expand = jnp.stack([(lane_id[:, None] == head_of_lane[None, :] + d * SSM_HEADS) for d in range(2)]
                       ).astype(BF16)
    wrt = w_router.T
    rb = router_bias.reshape(N_EXPERTS, 1)
    fnw = final_norm_w.reshape(1, D_MODEL)

    kv_ctx, s_ctx = None, None
    for l in range(depth):
        wl = w_in[l]
        lw = {
            "norm1_w": norm1_w[l].reshape(1, D_MODEL),
            "norm2_w": norm2_w[l].reshape(1, D_MODEL),
            "wqkv": wl[:, :Z_OFF].astype(BF16),
            "wz": wl[:, Z_OFF:XBC_OFF].astype(BF16),
            "wxbc": wl[:, XBC_OFF:DT_OFF].astype(BF16),
            "wdt": jnp.pad(wl[:, DT_OFF:G_OFF], ((0, 0), (0, LANES - 2 * SSM_HEADS))).astype(BF16),
            "wg": wl[:, G_OFF:].astype(BF16),
            "q_norm_w": jnp.tile(q_norm_w[l], 4).reshape(1, 2 * LANES),
            "k_norm_w": jnp.tile(k_norm_w[l], 4).reshape(1, 2 * LANES),
            "conv_w": conv_w[l],
            "conv_b": conv_b[l].reshape(1, CONV_DIM),
            "dt_bias": jnp.pad(dt_bias[l].reshape(1, 2 * SSM_HEADS), ((0, 0), (0, LANES - 2 * SSM_HEADS))),
            "ssm_d": jnp.repeat(ssm_d[l], SSM_HEAD_DIM).reshape(1, D_INNER),
            "w_attn_out": w_attn_out[l].astype(BF16),
            "w_ssm_out": (ssm_norm_w[l][:, None] * w_ssm_out[l]).astype(BF16),
            "w_out": w_out[l].astype(BF16),
        }
        mod = mod_all[l]
        q, kn, vn, k_ctx, v_ctx, z, xs, bc, dt, g = _in_proj(x, mod, meta, lw, t_ctx, ctx_len, l, depth,
                                                             kv_ctx)
        kv_ctx = (k_ctx, v_ctx)

        attn = _attention(q, kn, vn, None, None, n_ctx, ctx_len, 0, min(TQ, ctx_len))
        attn = _attention(q, kn, vn, _cache_blocks(cache_k[:, l], cache_v[:, l]), attn,
                          n_lat, lat_len, t_ctx, min(TQ, lat_len))

        a_lanes = jnp.pad(-jnp.exp(a_log[l]).reshape(1, 2 * SSM_HEADS),
                          ((0, 0), (0, LANES - 2 * SSM_HEADS)))
        s0 = state_ssm[:, l].transpose(0, 1, 4, 2, 3).reshape(n_lat, 2, D_STATE, D_INNER)
        yf, yb, s_ctx = _ssd(xs, bc, dt, a_lanes, expand, lw["ssm_d"], s0, meta, n_ctx, l, depth, s_ctx)

        x1, h2, idx, gate = _merge(x, attn, yf, yb, z, g, mod, meta, lw, wrt, rb)
        y0, y1 = _moe(h2, idx, gate, (w_exp_gate, w_exp_up, w_exp_down), l)
        if l < depth - 1:
            x = _combine(x1, y0, y1, gate.T, mod, meta, fnw, None)
        else:
            y_prompt, y_sample = _combine(x1, y0, y1, gate.T, mod, meta, fnw, t_ctx)

    y_prompt = y_prompt.reshape(n_ctx, ctx_len, D_MODEL)
    y_sample = y_sample.reshape(n_lat, lat_len, D_MODEL)
    new_k = kv_ctx[0].reshape(n_ctx, depth, ctx_len, N_KV_HEADS, HEAD_DIM)
    new_v = kv_ctx[1].reshape(n_ctx, depth, ctx_len, N_KV_HEADS, HEAD_DIM)
    new_s = s_ctx.reshape(n_ctx, depth, 2, D_STATE, SSM_HEADS, SSM_HEAD_DIM).transpose(0, 1, 2, 4, 5, 3)
    return (y_prompt, y_sample, new_k, new_v, new_s)
```
